```python
import jax, jax.numpy as jnp
from jax import lax
import numpy as np

D_MODEL = 2048
BATCH = 2
SEQ = 4096
DEPTH = 1

HEAD_DIM = 128
MOBA_HEADS = 8
MOBA_BLOCK = 256
MOBA_TOPK = 3
DSA_HEADS = 8
IDX_HEADS = 16
IDX_DIM = 64
DSA_TOPK_MAX = 256
D_FF = 4 * D_MODEL
ROPE_THETA = 10000.0
Q_CHUNK = 32
RMS_EPS = 1e-6
MOBA_W = MOBA_HEADS * HEAD_DIM
DSA_W = DSA_HEADS * HEAD_DIM
IN_SIZES = (3 * MOBA_W, 3 * DSA_W, IDX_HEADS * IDX_DIM, IDX_DIM, IDX_HEADS, D_MODEL, D_MODEL)
D_IN = 3 * MOBA_W + 3 * DSA_W + IDX_HEADS * IDX_DIM + IDX_DIM + IDX_HEADS + 2 * D_MODEL

kernel_name = 'hybrid_moba_dsa_gated_block'


def _rmsnorm(x, g):
    xf = x.astype(jnp.float32)
    y = xf * lax.rsqrt(jnp.mean(xf * xf, axis=-1, keepdims=True) + RMS_EPS)
    return (y * g.astype(jnp.float32)).astype(x.dtype)


def _rope(x):
    T, dh = x.shape[1], x.shape[-1]
    half = dh // 2
    inv_freq = jnp.power(ROPE_THETA, -jnp.arange(half, dtype=jnp.float32) / half)
    ang = jnp.arange(T, dtype=jnp.float32)[:, None] * inv_freq[None, :]
    cos = jnp.cos(ang)[None, :, None, :]
    sin = jnp.sin(ang)[None, :, None, :]
    xf = x.astype(jnp.float32)
    x1, x2 = xf[..., :half], xf[..., half:]
    return jnp.concatenate([x1 * cos - x2 * sin, x2 * cos + x1 * sin], axis=-1).astype(x.dtype)


def _moba_attention(q, k, v):
    B, T, H, D = q.shape
    scale = D ** -0.5
    nb = -(-T // MOBA_BLOCK)
    pad = nb * MOBA_BLOCK - T
    kpad = jnp.pad(k, ((0, 0), (0, pad), (0, 0), (0, 0)))
    vpad = jnp.pad(v, ((0, 0), (0, pad), (0, 0), (0, 0)))
    kb = kpad.reshape(B, nb, MOBA_BLOCK, H, D)
    vb = vpad.reshape(B, nb, MOBA_BLOCK, H, D)
    k_mean = jnp.mean(kb.astype(jnp.float32), axis=2)
    gate = jnp.einsum('bthd,bnhd->bthn', q.astype(jnp.float32), k_mean)
    qblk = jnp.arange(T) // MOBA_BLOCK
    past = jnp.arange(nb)[None, :] < qblk[:, None]
    gate = jnp.where(past[None, :, None, :], gate, -jnp.inf)
    n_sel = max(1, min(MOBA_TOPK, nb - 1))
    _, sel_idx = lax.top_k(gate, n_sel)
    sel_ok = sel_idx < qblk[None, :, None, None]
    kbt = jnp.transpose(kb, (0, 3, 1, 2, 4))
    vbt = jnp.transpose(vb, (0, 3, 1, 2, 4))
    b_ix = jnp.arange(B)[:, None, None, None]
    h_ix = jnp.arange(H)[None, None, :, None]

    def chunk(start):
        qc = lax.dynamic_slice_in_dim(q, start, Q_CHUNK, axis=1)
        idx = lax.dynamic_slice_in_dim(sel_idx, start, Q_CHUNK, axis=1)
        ok = lax.dynamic_slice_in_dim(sel_ok, start, Q_CHUNK, axis=1)
        kg = kbt[b_ix, h_ix, idx]
        vg = vbt[b_ix, h_ix, idx]
        s_sel = jnp.einsum('bchd,bchsjd->bchsj', qc, kg).astype(jnp.float32) * scale
        s_sel = jnp.where(ok[..., None], s_sel, -jnp.inf).reshape(B, Q_CHUNK, H, n_sel * MOBA_BLOCK)
        blk_start = (start // MOBA_BLOCK) * MOBA_BLOCK
        ko = lax.dynamic_slice_in_dim(kpad, blk_start, MOBA_BLOCK, axis=1)
        vo = lax.dynamic_slice_in_dim(vpad, blk_start, MOBA_BLOCK, axis=1)
        s_own = jnp.einsum('bchd,bjhd->bchj', qc, ko).astype(jnp.float32) * scale
        qpos = start + jnp.arange(Q_CHUNK)
        kpos = blk_start + jnp.arange(MOBA_BLOCK)
        causal = kpos[None, :] <= qpos[:, None]
        s_own = jnp.where(causal[None, :, None, :], s_own, -jnp.inf)
        p = jax.nn.softmax(jnp.concatenate([s_sel, s_own], axis=-1), axis=-1).astype(v.dtype)
        p_sel = p[..., :n_sel * MOBA_BLOCK].reshape(B, Q_CHUNK, H, n_sel, MOBA_BLOCK)
        p_own = p[..., n_sel * MOBA_BLOCK:]
        return (jnp.einsum('bchsj,bchsjd->bchd', p_sel, vg)
                + jnp.einsum('bchj,bjhd->bchd', p_own, vo))

    starts = jnp.arange(T // Q_CHUNK, dtype=jnp.int32) * Q_CHUNK
    outs = lax.map(chunk, starts)
    return jnp.moveaxis(outs, 0, 1).reshape(B, T, H, D)


def _dsa_attention(q, k, v, q_idx, k_idx, w_idx):
    B, T, H, D = q.shape
    scale = D ** -0.5
    idx_scale = (IDX_DIM ** -0.5) * (IDX_HEADS ** -0.5)
    topk = min(DSA_TOPK_MAX, T // 4)
    kidx_f = k_idx.astype(jnp.float32)
    b_ix = jnp.arange(B)[:, None, None]

    def chunk(start):
        qc = lax.dynamic_slice_in_dim(q, start, Q_CHUNK, axis=1)
        qi = lax.dynamic_slice_in_dim(q_idx, start, Q_CHUNK, axis=1).astype(jnp.float32)
        wi = lax.dynamic_slice_in_dim(w_idx, start, Q_CHUNK, axis=1).astype(jnp.float32)
        rel = jax.nn.relu(jnp.einsum('bchi,bsi->bchs', qi, kidx_f))
        score = jnp.einsum('bch,bchs->bcs', wi, rel) * idx_scale
        qpos = start + jnp.arange(Q_CHUNK)
        admissible = jnp.arange(T)[None, :] <= qpos[:, None]
        score = jnp.where(admissible[None], score, -jnp.inf)
        _, idx = lax.top_k(score, topk)
        ok = idx <= qpos[None, :, None]
        kg = k[b_ix, idx]
        vg = v[b_ix, idx]
        s = jnp.einsum('bchd,bckhd->bchk', qc, kg).astype(jnp.float32) * scale
        s = jnp.where(ok[:, :, None, :], s, -jnp.inf)
        p = jax.nn.softmax(s, axis=-1).astype(v.dtype)
        return jnp.einsum('bchk,bckhd->bchd', p, vg)

    starts = jnp.arange(T // Q_CHUNK, dtype=jnp.int32) * Q_CHUNK
    outs = lax.map(chunk, starts)
    return jnp.moveaxis(outs, 0, 1).reshape(B, T, H, D)


def setup_inputs(seed: int = 0) -> dict:
    key = jax.random.key(seed)
    ks = jax.random.split(key, 16)
    f32 = jnp.float32

    def nrm(k, shape, fan_in, mult=1.0):
        return jax.random.normal(k, shape, f32) * (mult * fan_in ** -0.5)

    def gain(k):
        return 1.0 + 0.02 * jax.random.normal(k, (DEPTH, D_MODEL), f32)

    return {
        'x': jax.random.normal(ks[0], (BATCH, SEQ, D_MODEL), f32),
        'c': jax.random.normal(ks[1], (BATCH, D_MODEL), f32),
        'w_ada': nrm(ks[2], (DEPTH, D_MODEL, 6 * D_MODEL), D_MODEL),
        'b_ada': 0.02 * jax.random.normal(ks[3], (DEPTH, 6 * D_MODEL), f32),
        'g_pre_mix': gain(ks[4]),
        'g_post_mix': gain(ks[5]),
        'w_in': nrm(ks[6], (DEPTH, D_MODEL, D_IN), D_MODEL),
        'w_moba_out': nrm(ks[7], (DEPTH, MOBA_W, D_MODEL), MOBA_W),
        'w_dsa_out': nrm(ks[8], (DEPTH, DSA_W, D_MODEL), DSA_W),
        'w_o': nrm(ks[9], (DEPTH, D_MODEL, D_MODEL), D_MODEL),
        'g_pre_ffn': gain(ks[10]),
        'g_post_ffn': gain(ks[11]),
        'w_ff1': nrm(ks[12], (DEPTH, D_MODEL, D_FF), D_MODEL),
        'w_ff2': nrm(ks[13], (DEPTH, D_FF, D_MODEL), D_FF),
    }


def reference(x, c, w_ada, b_ada, g_pre_mix, g_post_mix, w_in, w_moba_out, w_dsa_out, w_o,
              g_pre_ffn, g_post_ffn, w_ff1, w_ff2):
    B, T, _ = x.shape
    cs = jax.nn.silu(c)
    split_pts = [int(v) for v in np.cumsum(IN_SIZES)[:-1]]
    for l in range(DEPTH):
        mod = cs @ w_ada[l] + b_ada[l]
        sh1, sc1, gt1, sh2, sc2, gt2 = [m[:, None, :] for m in jnp.split(mod, 6, axis=-1)]

        h = _rmsnorm(x, g_pre_mix[l]) * (1.0 + sc1) + sh1
        proj = h @ w_in[l]
        p_moba, p_dsa, p_qi, p_ki, p_wi, ga, gb = jnp.split(proj, split_pts, axis=-1)
        qa, ka, va = [t.reshape(B, T, MOBA_HEADS, HEAD_DIM) for t in jnp.split(p_moba, 3, axis=-1)]
        qb, kb, vb = [t.reshape(B, T, DSA_HEADS, HEAD_DIM) for t in jnp.split(p_dsa, 3, axis=-1)]
        q_idx = _rope(p_qi.reshape(B, T, IDX_HEADS, IDX_DIM))
        k_idx = _rope(p_ki[:, :, None, :])[:, :, 0, :]
        ya = _moba_attention(_rope(qa), _rope(ka), va).reshape(B, T, MOBA_W) @ w_moba_out[l]
        yb = _dsa_attention(_rope(qb), _rope(kb), vb, q_idx, k_idx, p_wi).reshape(B, T, DSA_W) @ w_dsa_out[l]
        y = (jax.nn.sigmoid(ga) * ya + jax.nn.sigmoid(gb) * yb) @ w_o[l]
        x = x + gt1 * _rmsnorm(y, g_post_mix[l])

        h = _rmsnorm(x, g_pre_ffn[l]) * (1.0 + sc2) + sh2
        f = jnp.square(jax.nn.relu(h @ w_ff1[l])) @ w_ff2[l]
        x = x + gt2 * _rmsnorm(f, g_post_ffn[l])
    return x
```

```python
import functools

import jax
import jax.numpy as jnp
from jax import lax
from jax.experimental import pallas as pl
from jax.experimental.pallas import tpu as pltpu

HEAD_DIM = 128
MOBA_HEADS = 8
MOBA_BLOCK = 256
MOBA_TOPK = 3
DSA_HEADS = 8
IDX_HEADS = 16
IDX_DIM = 64
DSA_TOPK_MAX = 256
ROPE_THETA = 10000.0
RMS_EPS = 1e-6

MOBA_W = MOBA_HEADS * HEAD_DIM
DSA_W = DSA_HEADS * HEAD_DIM
IDX_W = IDX_HEADS * IDX_DIM
LANES = 128
MAX_BLOCKS = LANES // MOBA_HEADS
KEY_TILE = 256
MASK_BIAS = -30000.0
NEG_BIG = -1e30
INT_MIN = -(2 ** 31)
KEY_OF_NEG_INF = -2139095041
VMEM_LIMIT = 56 * 1024 * 1024

F32 = jnp.float32
BF16 = jnp.bfloat16
_NT = (((1,), (1,)), ((), ()))


def _params(sem):
    return pltpu.CompilerParams(dimension_semantics=sem, vmem_limit_bytes=VMEM_LIMIT)


def _tile(n, pref):
    if n <= pref:
        return n
    t = pref - pref % LANES
    while t >= LANES:
        if n % t == 0:
            return t
        t -= LANES
    return n


def _rms(x):
    return x * lax.rsqrt(jnp.mean(x * x, axis=-1, keepdims=True) + RMS_EPS)


def _ada_kernel(c_ref, w_ref, b_ref, o_ref):
    c = c_ref[...]
    cs = (c * jax.nn.sigmoid(c)).astype(BF16)
    o_ref[...] = jnp.dot(cs, w_ref[...].astype(BF16), preferred_element_type=F32) + b_ref[...]


def _ada(c_pad, w, b):
    rows, d = c_pad.shape
    n = w.shape[1]
    tn = _tile(n, 1024)
    return pl.pallas_call(
        _ada_kernel,
        grid=(n // tn,),
        in_specs=[pl.BlockSpec((rows, d), lambda j: (0, 0)),
                  pl.BlockSpec((d, tn), lambda j: (0, j)),
                  pl.BlockSpec((1, tn), lambda j: (0, j))],
        out_specs=pl.BlockSpec((rows, tn), lambda j: (0, j)),
        out_shape=jax.ShapeDtypeStruct((rows, n), F32),
        compiler_params=_params(("parallel",)),
        name="ada_mod",
    )(c_pad, w, b)


def _norm_mod_kernel(x_ref, g_ref, sc_ref, sh_ref, o_ref):
    y = _rms(x_ref[...])
    o_ref[...] = ((y * g_ref[...]) * (1.0 + sc_ref[...]) + sh_ref[...]).astype(o_ref.dtype)


def _norm_mod(x, g, sc, sh):
    b, t, d = x.shape
    tt = _tile(t, 512)
    vec = pl.BlockSpec((None, 1, d), lambda bi, ti: (bi, 0, 0))
    return pl.pallas_call(
        _norm_mod_kernel,
        grid=(b, t // tt),
        in_specs=[pl.BlockSpec((None, tt, d), lambda bi, ti: (bi, ti, 0)),
                  pl.BlockSpec((1, d), lambda bi, ti: (0, 0)), vec, vec],
        out_specs=pl.BlockSpec((None, tt, d), lambda bi, ti: (bi, ti, 0)),
        out_shape=jax.ShapeDtypeStruct((b, t, d), BF16),
        compiler_params=_params(("parallel", "parallel")),
        name="norm_mod",
    )(x, g, sc, sh)


def _mm_plain_kernel(h_ref, w_ref, o_ref):
    o_ref[...] = jnp.dot(h_ref[...], w_ref[...], preferred_element_type=F32).astype(o_ref.dtype)


def _mm_sigmoid_kernel(h_ref, w_ref, o_ref):
    acc = jnp.dot(h_ref[...], w_ref[...], preferred_element_type=F32)
    o_ref[...] = jax.nn.sigmoid(acc).astype(o_ref.dtype)


def _mm_rope128_kernel(h_ref, w_ref, cos_ref, sin_ref, o_ref):
    acc = jnp.dot(h_ref[...], w_ref[...], preferred_element_type=F32)
    cos = cos_ref[...]
    sin = sin_ref[...]
    for g in range(acc.shape[1] // LANES):
        y = acc[:, g * LANES:(g + 1) * LANES]
        o_ref[:, g * LANES:(g + 1) * LANES] = (
            y * cos + pltpu.roll(y, LANES // 2, 1) * sin).astype(o_ref.dtype)


def _rope64(y, cos, sin_a, sin_b):
    return y * cos + pltpu.roll(y, LANES - 32, 1) * sin_a + pltpu.roll(y, 32, 1) * sin_b


def _mm_rope64_kernel(h_ref, w_ref, cos_ref, sa_ref, sb_ref, o_ref):
    acc = jnp.dot(h_ref[...], w_ref[...], preferred_element_type=F32)
    cos, sa, sb = cos_ref[...], sa_ref[...], sb_ref[...]
    for g in range(acc.shape[1] // LANES):
        y = acc[:, g * LANES:(g + 1) * LANES]
        o_ref[:, g * LANES:(g + 1) * LANES] = _rope64(y, cos, sa, sb).astype(o_ref.dtype)


def _mm_kw_kernel(h_ref, w_ref, cos_ref, sa_ref, sb_ref, klo_ref, khi_ref, wi_ref):
    y = jnp.dot(h_ref[...], w_ref[...], preferred_element_type=F32)
    r = _rope64(y, cos_ref[...], sa_ref[...], sb_ref[...])
    lane = lax.broadcasted_iota(jnp.int32, y.shape, 1)
    klo = jnp.where(lane < IDX_DIM, r, 0.0)
    klo_ref[...] = klo.astype(BF16)
    khi_ref[...] = pltpu.roll(klo, IDX_DIM, 1).astype(BF16)
    wi_ref[...] = y


def _proj(body, h, w, tables, out_dtypes, tn_pref=512, name="proj"):
    m, k = h.shape
    n = w.shape[1]
    tm = _tile(m, 512)
    tn = _tile(n, tn_pref)
    t = tables[0].shape[0] if tables else m
    tm = _tile(t, tm) if tables else tm
    nt = t // tm
    tab_spec = pl.BlockSpec((tm, LANES), lambda i, j: (i % nt, 0))
    outs = [jax.ShapeDtypeStruct((m, n), dt) for dt in out_dtypes]
    res = pl.pallas_call(
        body,
        grid=(m // tm, n // tn),
        in_specs=[pl.BlockSpec((tm, k), lambda i, j: (i, 0)),
                  pl.BlockSpec((k, tn), lambda i, j: (0, j))] + [tab_spec] * len(tables),
        out_specs=[pl.BlockSpec((tm, tn), lambda i, j: (i, j)) for _ in outs],
        out_shape=outs,
        compiler_params=_params(("parallel", "parallel")),
        name=name,
    )(h, w, *tables)
    return res


def _moba_kmean_kernel(k_ref, o_ref, km_ref, *, nb):
    km_ref[...] = jnp.zeros(km_ref.shape, F32)
    for j in range(nb):
        blk = k_ref[j * MOBA_BLOCK:(j + 1) * MOBA_BLOCK, :].astype(F32)
        km_ref[j:j + 1, :] = jnp.sum(blk, axis=0, keepdims=True) * (1.0 / MOBA_BLOCK)
    km = km_ref[...]
    tiled = jnp.concatenate([km] * MOBA_HEADS, axis=0)
    r = lax.shift_right_logical(lax.broadcasted_iota(jnp.int32, tiled.shape, 0), 4)
    c = lax.shift_right_logical(lax.broadcasted_iota(jnp.int32, tiled.shape, 1), 7)
    o_ref[...] = jnp.where(r == c, tiled, 0.0).astype(o_ref.dtype)


def _moba_kmean(qk3, nb):
    b, t, _ = qk3.shape
    return pl.pallas_call(
        functools.partial(_moba_kmean_kernel, nb=nb),
        grid=(b,),
        in_specs=[pl.BlockSpec((None, t, MOBA_W), lambda bi: (bi, 0, 1))],
        out_specs=pl.BlockSpec((None, LANES, MOBA_W), lambda bi: (bi, 0, 0)),
        out_shape=jax.ShapeDtypeStruct((b, LANES, MOBA_W), BF16),
        scratch_shapes=[pltpu.VMEM((MAX_BLOCKS, MOBA_W), F32)],
        compiler_params=_params(("parallel",)),
        name="moba_kmean",
    )(qk3)


def _moba_select_kernel(q_ref, kmt_ref, o_ref, *, n_sel):
    i = pl.program_id(1)
    g = lax.dot_general(q_ref[...], kmt_ref[...], _NT, preferred_element_type=F32)
    lane = lax.broadcasted_iota(jnp.int32, g.shape, 1)
    j = lane & (MAX_BLOCKS - 1)
    past = j < i
    gm = jnp.where(past, g, -jnp.inf)
    rank = jnp.zeros(g.shape, F32)
    for s in range(1, MAX_BLOCKS):
        wrap = (j + s) >= MAX_BLOCKS
        other = jnp.where(wrap, pltpu.roll(gm, MAX_BLOCKS - s, 1), pltpu.roll(gm, LANES - s, 1))
        beats = (other > gm) | ((other == gm) & wrap)
        rank = rank + jnp.where(beats, 1.0, 0.0)
    sel = past & (rank < n_sel)
    bias = jnp.where(sel, 0.0, MASK_BIAS)
    for h in range(MOBA_HEADS):
        shift = (LANES - MAX_BLOCKS * h) % LANES
        bh = bias if shift == 0 else pltpu.roll(bias, shift, 1)
        o_ref[h] = jnp.where(lane < MAX_BLOCKS, bh, 0.0).astype(o_ref.dtype)


def _moba_select(qk3, kmt, nb, n_sel):
    b, t, _ = qk3.shape
    return pl.pallas_call(
        functools.partial(_moba_select_kernel, n_sel=n_sel),
        grid=(b, nb),
        in_specs=[pl.BlockSpec((None, MOBA_BLOCK, MOBA_W), lambda bi, i: (bi, i, 0)),
                  pl.BlockSpec((None, LANES, MOBA_W), lambda bi, i: (bi, 0, 0))],
        out_specs=pl.BlockSpec((None, MOBA_HEADS, MOBA_BLOCK, LANES), lambda bi, i: (bi, 0, i, 0)),
        out_shape=jax.ShapeDtypeStruct((b, MOBA_HEADS, t, LANES), BF16),
        compiler_params=_params(("parallel", "parallel")),
        name="moba_select",
    )(qk3, kmt)


def _softmax_step(s, m, l, acc, v):
    m_new = jnp.maximum(m, jnp.max(s, axis=1, keepdims=True))
    alpha = jnp.exp(m - m_new)
    p = jnp.exp(s - m_new)
    l = alpha * l + jnp.sum(p, axis=1, keepdims=True)
    acc = alpha * acc + jnp.dot(p.astype(BF16), v, preferred_element_type=F32)
    return m_new, l, acc


def _moba_attn_kernel(q_ref, k_ref, v_ref, sb_ref, o_ref, *, scale):
    i = pl.program_id(2)
    q = q_ref[...]
    own = pl.multiple_of(i * MOBA_BLOCK, MOBA_BLOCK)
    s = lax.dot_general(q, k_ref[pl.ds(own, MOBA_BLOCK), :], _NT, preferred_element_type=F32) * scale
    row = lax.broadcasted_iota(jnp.int32, s.shape, 0)
    col = lax.broadcasted_iota(jnp.int32, s.shape, 1)
    s = jnp.where(col <= row, s, -jnp.inf)
    m = jnp.max(s, axis=1, keepdims=True)
    p = jnp.exp(s - m)
    l = jnp.sum(p, axis=1, keepdims=True)
    acc = jnp.dot(p.astype(BF16), v_ref[pl.ds(own, MOBA_BLOCK), :], preferred_element_type=F32)

    q_aug = jnp.concatenate([q, sb_ref[...]], axis=1)
    lane = lax.broadcasted_iota(jnp.int32, (MOBA_BLOCK, LANES), 1)

    def body(j, carry):
        off = pl.multiple_of(j * MOBA_BLOCK, MOBA_BLOCK)
        onehot = jnp.where(lane == j, 1.0, 0.0).astype(BF16)
        k_aug = jnp.concatenate([k_ref[pl.ds(off, MOBA_BLOCK), :], onehot], axis=1)
        sj = lax.dot_general(q_aug, k_aug, _NT, preferred_element_type=F32) * scale
        return _softmax_step(sj, *carry, v_ref[pl.ds(off, MOBA_BLOCK), :])

    m, l, acc = lax.fori_loop(0, i, body, (m, l, acc))
    o_ref[...] = (acc / l).astype(o_ref.dtype)


def _moba_attn(qk3, v3, selb, nb):
    b, t, _ = qk3.shape
    scale = HEAD_DIM ** -0.5
    return pl.pallas_call(
        functools.partial(_moba_attn_kernel, scale=scale),
        grid=(b, MOBA_HEADS, nb),
        in_specs=[pl.BlockSpec((None, MOBA_BLOCK, HEAD_DIM), lambda bi, h, i: (bi, i, h)),
                  pl.BlockSpec((None, t, HEAD_DIM), lambda bi, h, i: (bi, 0, MOBA_HEADS + h)),
                  pl.BlockSpec((None, t, HEAD_DIM), lambda bi, h, i: (bi, 0, h)),
                  pl.BlockSpec((None, None, MOBA_BLOCK, LANES), lambda bi, h, i: (bi, h, i, 0))],
        out_specs=pl.BlockSpec((None, MOBA_BLOCK, HEAD_DIM), lambda bi, h, i: (bi, i, h)),
        out_shape=jax.ShapeDtypeStruct((b, t, MOBA_W), BF16),
        compiler_params=_params(("parallel", "parallel", "arbitrary")),
        name="moba_attn",
    )(qk3, qk3, v3, selb)


def _key_to_float(t):
    return lax.bitcast_convert_type(jnp.where(t >= 0, t, t ^ 0x7FFFFFFF), F32)


def _dsa_kernel(qi_ref, klo_ref, khi_ref, wi_ref, q_ref, k_ref, v_ref, o_ref,
                sc_ref, wb_ref, jl_ref, *, topk, scale, idx_scale, tq, seq_bits):
    i = pl.program_id(1)
    nk = i + 1
    shape = (tq, KEY_TILE)
    row = lax.broadcasted_iota(jnp.int32, shape, 0)
    col = lax.broadcasted_iota(jnp.int32, shape, 1)
    qpos = i * tq + row

    w = wi_ref[...]
    for h in range(IDX_HEADS):
        wb_ref[h] = jnp.broadcast_to(w[:, IDX_DIM + h:IDX_DIM + h + 1], shape)

    def score_body(kt, carry):
        off = pl.multiple_of(kt * KEY_TILE, KEY_TILE)
        klo = klo_ref[pl.ds(off, KEY_TILE), :]
        khi = khi_ref[pl.ds(off, KEY_TILE), :]
        acc = jnp.zeros(shape, F32)
        for p in range(IDX_HEADS // 2):
            qp = qi_ref[:, p * LANES:(p + 1) * LANES]
            s0 = lax.dot_general(qp, klo, _NT, preferred_element_type=F32)
            s1 = lax.dot_general(qp, khi, _NT, preferred_element_type=F32)
            acc = acc + jnp.maximum(s0, 0.0) * wb_ref[2 * p] + jnp.maximum(s1, 0.0) * wb_ref[2 * p + 1]
        kpos = kt * KEY_TILE + col
        sc_ref[kt] = jnp.where(kpos <= qpos, acc * idx_scale, -jnp.inf)
        return carry

    lax.fori_loop(0, nk, score_body, 0)

    def count(pred):
        def body(kt, acc):
            return acc + jnp.where(pred(sc_ref[kt], kt * KEY_TILE + col), 1.0, 0.0)
        return jnp.sum(lax.fori_loop(0, nk, body, jnp.zeros(shape, F32)), axis=1, keepdims=True)

    need_select = nk * tq > topk

    @pl.when(jnp.logical_not(need_select))
    def _():
        def body(kt, carry):
            sc_ref[kt] = jnp.where(kt * KEY_TILE + col <= qpos, 0.0, NEG_BIG)
            return carry
        lax.fori_loop(0, nk, body, 0)

    @pl.when(need_select)
    def _():
        kf = jnp.float32(topk)
        zero = jnp.zeros((tq, 1), F32)
        c0 = count(lambda s, kp: s >= jnp.broadcast_to(zero, shape))
        t0 = jnp.where(c0 >= kf, 0, INT_MIN).astype(jnp.int32)

        def bit_body(b, t):
            cand = t | lax.shift_left(jnp.int32(1), 30 - b)
            cb = jnp.broadcast_to(_key_to_float(cand), shape)
            c = count(lambda s, kp: s >= cb)
            return jnp.where(c >= kf, cand, t)

        t = lax.fori_loop(0, 31, bit_body, t0)
        t = jnp.maximum(t, KEY_OF_NEG_INF)
        thr = jnp.broadcast_to(_key_to_float(t), shape)

        need = kf - count(lambda s, kp: s > thr)
        n_eq = count(lambda s, kp: s == thr)
        jl_ref[...] = jnp.full(shape, 2 ** seq_bits, jnp.int32)

        @pl.when(jnp.max(n_eq - need) > 0.0)
        def _():
            def idx_body(b, c):
                cand = c | lax.shift_left(jnp.int32(1), seq_bits - 1 - b)
                cb = jnp.broadcast_to(cand, shape)
                g = count(lambda s, kp: (s == thr) & (kp < cb))
                return jnp.where(g < need, cand, c)
            c = lax.fori_loop(0, seq_bits, idx_body, jnp.zeros((tq, 1), jnp.int32))
            jl_ref[...] = jnp.broadcast_to(c, shape)

        def bias_body(kt, carry):
            s = sc_ref[kt]
            kpos = kt * KEY_TILE + col
            sel = ((s > thr) | ((s == thr) & (kpos <= jl_ref[...]))) & (kpos <= qpos)
            sc_ref[kt] = jnp.where(sel, 0.0, NEG_BIG)
            return carry
        lax.fori_loop(0, nk, bias_body, 0)

    for h in range(DSA_HEADS):
        hs = slice(h * HEAD_DIM, (h + 1) * HEAD_DIM)
        qh = q_ref[:, hs]

        def body(kt, carry, hs=hs, qh=qh):
            off = pl.multiple_of(kt * KEY_TILE, KEY_TILE)
            s = lax.dot_general(qh, k_ref[pl.ds(off, KEY_TILE), hs], _NT,
                                preferred_element_type=F32) * scale + sc_ref[kt]
            return _softmax_step(s, *carry, v_ref[pl.ds(off, KEY_TILE), hs])

        init = (jnp.full((tq, 1), NEG_BIG, F32), jnp.zeros((tq, 1), F32), jnp.zeros((tq, HEAD_DIM), F32))
        m, l, acc = lax.fori_loop(0, nk, body, init)
        o_ref[:, hs] = (acc / l).astype(o_ref.dtype)


def _dsa_attn(qi3, klo3, khi3, wi3, qk3, v3, topk):
    b, t, _ = qk3.shape
    tq = KEY_TILE
    nq = t // tq
    seq_bits = max(1, (t - 1).bit_length())
    one = pl.Buffered(1)
    kern = functools.partial(
        _dsa_kernel, topk=topk, scale=HEAD_DIM ** -0.5,
        idx_scale=(IDX_DIM ** -0.5) * (IDX_HEADS ** -0.5), tq=tq, seq_bits=seq_bits)
    return pl.pallas_call(
        kern,
        grid=(b, nq),
        in_specs=[pl.BlockSpec((None, tq, IDX_W), lambda bi, i: (bi, i, 0)),
                  pl.BlockSpec((None, t, LANES), lambda bi, i: (bi, 0, 0), pipeline_mode=one),
                  pl.BlockSpec((None, t, LANES), lambda bi, i: (bi, 0, 0), pipeline_mode=one),
                  pl.BlockSpec((None, tq, LANES), lambda bi, i: (bi, i, 0)),
                  pl.BlockSpec((None, tq, DSA_W), lambda bi, i: (bi, i, 2)),
                  pl.BlockSpec((None, t, DSA_W), lambda bi, i: (bi, 0, 3), pipeline_mode=one),
                  pl.BlockSpec((None, t, DSA_W), lambda bi, i: (bi, 0, 1), pipeline_mode=one)],
        out_specs=pl.BlockSpec((None, tq, DSA_W), lambda bi, i: (bi, i, 0)),
        out_shape=jax.ShapeDtypeStruct((b, t, DSA_W), BF16),
        scratch_shapes=[pltpu.VMEM((nq, tq, KEY_TILE), F32),
                        pltpu.VMEM((IDX_HEADS, tq, KEY_TILE), F32),
                        pltpu.VMEM((tq, KEY_TILE), jnp.int32)],
        compiler_params=_params(("parallel", "arbitrary")),
        name="dsa_attn",
    )(qi3, klo3, khi3, wi3, qk3, qk3, v3)


def _mix_out_kernel(oa_ref, ob_ref, sg_ref, wa_ref, wb_ref, wo_ref, x_ref, gt_ref, gpost_ref,
                    gpre_ref, sc_ref, sh_ref, x1_ref, h2_ref, *, d):
    ya = jnp.dot(oa_ref[...], wa_ref[...], preferred_element_type=F32)
    yb = jnp.dot(ob_ref[...], wb_ref[...], preferred_element_type=F32)
    z = sg_ref[:, :d].astype(F32) * ya + sg_ref[:, d:].astype(F32) * yb
    y = jnp.dot(z.astype(BF16), wo_ref[...], preferred_element_type=F32)
    x1 = x_ref[...] + gt_ref[...] * (_rms(y) * gpost_ref[...])
    x1_ref[...] = x1
    h2_ref[...] = ((_rms(x1) * gpre_ref[...]) * (1.0 + sc_ref[...]) + sh_ref[...]).astype(h2_ref.dtype)


def _mix_out(oa3, ob3, sg3, wa, wb, wo, x, gt1, g_post, g_pre2, sc2, sh2):
    b, t, d = x.shape
    tm = _tile(t, 256)
    one = pl.Buffered(1)
    row = lambda w: pl.BlockSpec((None, tm, w), lambda bi, i: (bi, i, 0))
    full = lambda a: pl.BlockSpec(a.shape, lambda bi, i: (0, 0), pipeline_mode=one)
    vec = pl.BlockSpec((None, 1, d), lambda bi, i: (bi, 0, 0))
    gain = pl.BlockSpec((1, d), lambda bi, i: (0, 0))
    return pl.pallas_call(
        functools.partial(_mix_out_kernel, d=d),
        grid=(b, t // tm),
        in_specs=[row(MOBA_W), row(DSA_W), row(2 * d), full(wa), full(wb), full(wo), row(d),
                  vec, gain, gain, vec, vec],
        out_specs=[row(d), row(d)],
        out_shape=[jax.ShapeDtypeStruct((b, t, d), F32), jax.ShapeDtypeStruct((b, t, d), BF16)],
        compiler_params=_params(("parallel", "parallel")),
        name="mix_out",
    )(oa3, ob3, sg3, wa, wb, wo, x, gt1, g_post, g_pre2, sc2, sh2)


def _ffn_kernel(h_ref, w1_ref, w2_ref, x1_ref, gt_ref, g_ref, o_ref, acc_ref):
    j = pl.program_id(2)

    @pl.when(j == 0)
    def _():
        acc_ref[...] = jnp.zeros(acc_ref.shape, F32)

    u = jnp.dot(h_ref[...], w1_ref[...], preferred_element_type=F32)
    u = jnp.square(jnp.maximum(u, 0.0)).astype(BF16)
    acc_ref[...] += jnp.dot(u, w2_ref[...], preferred_element_type=F32)

    @pl.when(j == pl.num_programs(2) - 1)
    def _():
        o_ref[...] = x1_ref[...] + gt_ref[...] * (_rms(acc_ref[...]) * g_ref[...])


def _ffn(h2, w1, w2, x1, gt2, g_post):
    b, t, d = x1.shape
    ff = w1.shape[1]
    tm = _tile(t, 512)
    tf = _tile(ff, 512)
    row = lambda: pl.BlockSpec((None, tm, d), lambda bi, i, j: (bi, i, 0))
    return pl.pallas_call(
        _ffn_kernel,
        grid=(b, t // tm, ff // tf),
        in_specs=[row(),
                  pl.BlockSpec((d, tf), lambda bi, i, j: (0, j)),
                  pl.BlockSpec((tf, d), lambda bi, i, j: (j, 0)),
                  row(),
                  pl.BlockSpec((None, 1, d), lambda bi, i, j: (bi, 0, 0)),
                  pl.BlockSpec((1, d), lambda bi, i, j: (0, 0))],
        out_specs=row(),
        out_shape=jax.ShapeDtypeStruct((b, t, d), F32),
        scratch_shapes=[pltpu.VMEM((tm, d), F32)],
        compiler_params=_params(("parallel", "parallel", "arbitrary")),
        name="ffn",
    )(h2, w1, w2, x1, gt2, g_post)


def _rope_tables(t):
    pos = jnp.arange(t, dtype=F32)[:, None]
    lane = jnp.arange(LANES)[None, :]

    def cos_sin(half):
        inv_freq = jnp.power(ROPE_THETA, -jnp.arange(half, dtype=F32) / half)
        ang = pos * inv_freq[None, :]
        reps = LANES // half
        return jnp.tile(jnp.cos(ang), (1, reps)), jnp.tile(jnp.sin(ang), (1, reps))

    cos128, sin128 = cos_sin(HEAD_DIM // 2)
    sin128 = jnp.where(lane < HEAD_DIM // 2, -sin128, sin128)
    cos64, sin64 = cos_sin(IDX_DIM // 2)
    low = (lane % IDX_DIM) < IDX_DIM // 2
    sa64 = jnp.where(low, -sin64, 0.0)
    sb64 = jnp.where(low, 0.0, sin64)
    is_key = lane < IDX_DIM
    kw = (jnp.where(is_key, cos64, 1.0), jnp.where(is_key, sa64, 0.0), jnp.where(is_key, sb64, 0.0))
    return (cos128, sin128), (cos64, sa64, sb64), kw


def kernel(x, c, w_ada, b_ada, g_pre_mix, g_post_mix, w_in, w_moba_out, w_dsa_out, w_o,
           g_pre_ffn, g_post_ffn, w_ff1, w_ff2):
    b, t, d = x.shape
    m = b * t
    nb = t // MOBA_BLOCK
    assert t % MOBA_BLOCK == 0 and nb <= MAX_BLOCKS and d % LANES == 0 and b <= 16
    n_sel = max(1, min(MOBA_TOPK, nb - 1))
    topk = min(DSA_TOPK_MAX, t // 4)
    rope128, rope64, rope_kw = _rope_tables(t)
    c_pad = jnp.zeros((16, d), F32).at[:b].set(c)

    o_dsa, o_qi, o_kw, o_g = 3 * MOBA_W, 3 * MOBA_W + 3 * DSA_W, 3 * MOBA_W + 3 * DSA_W + IDX_W, \
        3 * MOBA_W + 3 * DSA_W + IDX_W + IDX_DIM + IDX_HEADS

    for l in range(w_ada.shape[0]):
        mod = _ada(c_pad, w_ada[l], b_ada[l][None, :])[:b]
        sh1, sc1, gt1, sh2, sc2, gt2 = [v[:, None, :] for v in jnp.split(mod, 6, axis=-1)]

        wl = w_in[l]
        w_qk = jnp.concatenate([wl[:, :2 * MOBA_W], wl[:, o_dsa:o_dsa + 2 * DSA_W]], axis=1).astype(BF16)
        w_v = jnp.concatenate([wl[:, 2 * MOBA_W:o_dsa], wl[:, o_dsa + 2 * DSA_W:o_qi]], axis=1).astype(BF16)
        w_qi = wl[:, o_qi:o_kw].astype(BF16)
        w_kw = jnp.pad(wl[:, o_kw:o_g], ((0, 0), (0, LANES - IDX_DIM - IDX_HEADS))).astype(BF16)
        w_g = wl[:, o_g:].astype(BF16)

        h = _norm_mod(x, g_pre_mix[l][None, :], sc1, sh1).reshape(m, d)
        (qk,) = _proj(_mm_rope128_kernel, h, w_qk, rope128, [BF16], name="proj_qk")
        (v,) = _proj(_mm_plain_kernel, h, w_v, (), [BF16], name="proj_v")
        (qi,) = _proj(_mm_rope64_kernel, h, w_qi, rope64, [BF16], name="proj_qi")
        klo, khi, wi = _proj(_mm_kw_kernel, h, w_kw, rope_kw, [BF16, BF16, F32], name="proj_kw")
        (sg,) = _proj(_mm_sigmoid_kernel, h, w_g, (), [BF16], name="proj_gate")

        qk3 = qk.reshape(b, t, -1)
        v3 = v.reshape(b, t, -1)
        kmt = _moba_kmean(qk3, nb)
        selb = _moba_select(qk3, kmt, nb, n_sel)
        oa = _moba_attn(qk3, v3, selb, nb)
        ob = _dsa_attn(qi.reshape(b, t, -1), klo.reshape(b, t, -1), khi.reshape(b, t, -1),
                       wi.reshape(b, t, -1), qk3, v3, topk)

        x, h2 = _mix_out(oa, ob, sg.reshape(b, t, -1), w_moba_out[l].astype(BF16),
                         w_dsa_out[l].astype(BF16), w_o[l].astype(BF16), x, gt1,
                         g_post_mix[l][None, :], g_pre_ffn[l][None, :], sc2, sh2)
        x = _ffn(h2, w_ff1[l].astype(BF16), w_ff2[l].astype(BF16), x, gt2, g_post_ffn[l][None, :])
    return x
```

```python
import functools

import jax
import jax.numpy as jnp
from jax import lax
from jax.experimental import pallas as pl
from jax.experimental.pallas import tpu as pltpu

HEAD_DIM = 128
MOBA_HEADS = 8
MOBA_BLOCK = 256
MOBA_TOPK = 3
DSA_HEADS = 8
IDX_HEADS = 16
IDX_DIM = 64
DSA_TOPK_MAX = 256
ROPE_THETA = 10000.0
RMS_EPS = 1e-6

MOBA_W = MOBA_HEADS * HEAD_DIM
DSA_W = DSA_HEADS * HEAD_DIM
IDX_W = IDX_HEADS * IDX_DIM
LANES = 128
MAX_BLOCKS = LANES // MOBA_HEADS
KEY_TILE = 256
SUB_TILE = 128
LOG2_E = 1.4426950408889634
MASK_BIAS = -30000.0
NEG_BIG = -1e30
INT_MIN = -(2 ** 31)
KEY_OF_NEG_INF = -2139095041
VMEM_LIMIT = 56 * 1024 * 1024

F32 = jnp.float32
BF16 = jnp.bfloat16
_NT = (((1,), (1,)), ((), ()))


def _params(sem):
    return pltpu.CompilerParams(dimension_semantics=sem, vmem_limit_bytes=VMEM_LIMIT)


def _tile(n, pref):
    if n <= pref:
        return n
    t = pref - pref % LANES
    while t >= LANES:
        if n % t == 0:
            return t
        t -= LANES
    return n


def _rms(x):
    return x * lax.rsqrt(jnp.mean(x * x, axis=-1, keepdims=True) + RMS_EPS)


def _ada_kernel(c_ref, w_ref, b_ref, o_ref):
    c = c_ref[...]
    cs = (c * jax.nn.sigmoid(c)).astype(BF16)
    o_ref[...] = jnp.dot(cs, w_ref[...].astype(BF16), preferred_element_type=F32) + b_ref[...]


def _ada(c_pad, w, b):
    rows, d = c_pad.shape
    n = w.shape[1]
    tn = _tile(n, 1024)
    return pl.pallas_call(
        _ada_kernel,
        grid=(n // tn,),
        in_specs=[pl.BlockSpec((rows, d), lambda j: (0, 0)),
                  pl.BlockSpec((d, tn), lambda j: (0, j)),
                  pl.BlockSpec((1, tn), lambda j: (0, j))],
        out_specs=pl.BlockSpec((rows, tn), lambda j: (0, j)),
        out_shape=jax.ShapeDtypeStruct((rows, n), F32),
        compiler_params=_params(("parallel",)),
        name="ada_mod",
    )(c_pad, w, b)


def _norm_mod_kernel(x_ref, g_ref, sc_ref, sh_ref, o_ref):
    y = _rms(x_ref[...])
    o_ref[...] = ((y * g_ref[...]) * (1.0 + sc_ref[...]) + sh_ref[...]).astype(o_ref.dtype)


def _norm_mod(x, g, sc, sh):
    b, t, d = x.shape
    tt = _tile(t, 512)
    vec = pl.BlockSpec((None, 1, d), lambda bi, ti: (bi, 0, 0))
    return pl.pallas_call(
        _norm_mod_kernel,
        grid=(b, t // tt),
        in_specs=[pl.BlockSpec((None, tt, d), lambda bi, ti: (bi, ti, 0)),
                  pl.BlockSpec((1, d), lambda bi, ti: (0, 0)), vec, vec],
        out_specs=pl.BlockSpec((None, tt, d), lambda bi, ti: (bi, ti, 0)),
        out_shape=jax.ShapeDtypeStruct((b, t, d), BF16),
        compiler_params=_params(("parallel", "parallel")),
        name="norm_mod",
    )(x, g, sc, sh)


def _mm_sigmoid_kernel(h_ref, w_ref, o_ref):
    acc = jnp.dot(h_ref[...], w_ref[...], preferred_element_type=F32)
    o_ref[...] = jax.nn.sigmoid(acc).astype(o_ref.dtype)


def _mm_rope128_kernel(h_ref, w_ref, cos_ref, sin_ref, o_ref):
    acc = jnp.dot(h_ref[...], w_ref[...], preferred_element_type=F32)
    cos = cos_ref[...]
    sin = sin_ref[...]
    for g in range(acc.shape[1] // LANES):
        y = acc[:, g * LANES:(g + 1) * LANES]
        o_ref[:, g * LANES:(g + 1) * LANES] = (
            y * cos + pltpu.roll(y, LANES // 2, 1) * sin).astype(o_ref.dtype)


def _rope64(y, cos, sin_a, sin_b):
    return y * cos + pltpu.roll(y, LANES - 32, 1) * sin_a + pltpu.roll(y, 32, 1) * sin_b


def _mm_rope64_kernel(h_ref, w_ref, cos_ref, sa_ref, sb_ref, o_ref):
    acc = jnp.dot(h_ref[...], w_ref[...], preferred_element_type=F32)
    cos, sa, sb = cos_ref[...], sa_ref[...], sb_ref[...]
    for g in range(acc.shape[1] // LANES):
        y = acc[:, g * LANES:(g + 1) * LANES]
        o_ref[:, g * LANES:(g + 1) * LANES] = _rope64(y, cos, sa, sb).astype(o_ref.dtype)


def _mm_kw_kernel(h_ref, w_ref, wt_ref, cos_ref, sa_ref, sb_ref, klo_ref, khi_ref, wit_ref):
    h = h_ref[...]
    y = jnp.dot(h, w_ref[...], preferred_element_type=F32)
    r = _rope64(y, cos_ref[...], sa_ref[...], sb_ref[...])
    lane = lax.broadcasted_iota(jnp.int32, y.shape, 1)
    klo = jnp.where(lane < IDX_DIM, r, 0.0)
    klo_ref[...] = klo.astype(BF16)
    khi_ref[...] = pltpu.roll(klo, IDX_DIM, 1).astype(BF16)
    yt = lax.dot_general(wt_ref[...], h, _NT, preferred_element_type=F32)
    for c in range(wit_ref.shape[0]):
        wit_ref[c] = yt[:, c * KEY_TILE:(c + 1) * KEY_TILE]


def _proj_kw(h, w, wt, tables, t):
    m, k = h.shape
    tm = _tile(t, 512)
    nt = t // tm
    tab = pl.BlockSpec((tm, LANES), lambda i: (i % nt, 0))
    row = pl.BlockSpec((tm, LANES), lambda i: (i, 0))
    return pl.pallas_call(
        _mm_kw_kernel,
        grid=(m // tm,),
        in_specs=[pl.BlockSpec((tm, k), lambda i: (i, 0)),
                  pl.BlockSpec((k, LANES), lambda i: (0, 0)),
                  pl.BlockSpec((LANES, k), lambda i: (0, 0)), tab, tab, tab],
        out_specs=[row, row, pl.BlockSpec((tm // KEY_TILE, LANES, KEY_TILE), lambda i: (i, 0, 0))],
        out_shape=[jax.ShapeDtypeStruct((m, LANES), BF16), jax.ShapeDtypeStruct((m, LANES), BF16),
                   jax.ShapeDtypeStruct((m // KEY_TILE, LANES, KEY_TILE), F32)],
        compiler_params=_params(("parallel",)),
        name="proj_kw",
    )(h, w, wt, *tables)


def _mm_t_kernel(wt_ref, h_ref, o_ref):
    r = lax.dot_general(wt_ref[...], h_ref[...], _NT, preferred_element_type=F32)
    for c in range(o_ref.shape[0]):
        o_ref[c] = r[:, c * KEY_TILE:(c + 1) * KEY_TILE].astype(o_ref.dtype)


def _proj_t(h, wt, name):
    m, k = h.shape
    n = wt.shape[0]
    tm = _tile(m, 512)
    tn = _tile(n, 512)
    return pl.pallas_call(
        _mm_t_kernel,
        grid=(m // tm, n // tn),
        in_specs=[pl.BlockSpec((tn, k), lambda i, j: (j, 0)),
                  pl.BlockSpec((tm, k), lambda i, j: (i, 0))],
        out_specs=pl.BlockSpec((tm // KEY_TILE, tn, KEY_TILE), lambda i, j: (i, j, 0)),
        out_shape=jax.ShapeDtypeStruct((m // KEY_TILE, n, KEY_TILE), BF16),
        compiler_params=_params(("parallel", "parallel")),
        name=name,
    )(wt, h)


def _proj(body, h, w, tables, out_dtypes, tn_pref=512, name="proj"):
    m, k = h.shape
    n = w.shape[1]
    tm = _tile(m, 512)
    tn = _tile(n, tn_pref)
    t = tables[0].shape[0] if tables else m
    tm = _tile(t, tm) if tables else tm
    nt = t // tm
    tab_spec = pl.BlockSpec((tm, LANES), lambda i, j: (i % nt, 0))
    outs = [jax.ShapeDtypeStruct((m, n), dt) for dt in out_dtypes]
    res = pl.pallas_call(
        body,
        grid=(m // tm, n // tn),
        in_specs=[pl.BlockSpec((tm, k), lambda i, j: (i, 0)),
                  pl.BlockSpec((k, tn), lambda i, j: (0, j))] + [tab_spec] * len(tables),
        out_specs=[pl.BlockSpec((tm, tn), lambda i, j: (i, j)) for _ in outs],
        out_shape=outs,
        compiler_params=_params(("parallel", "parallel")),
        name=name,
    )(h, w, *tables)
    return res


def _moba_kmean_kernel(k_ref, o_ref, km_ref, *, nb):
    km_ref[...] = jnp.zeros(km_ref.shape, F32)
    for j in range(nb):
        blk = k_ref[j * MOBA_BLOCK:(j + 1) * MOBA_BLOCK, :].astype(F32)
        km_ref[j:j + 1, :] = jnp.sum(blk, axis=0, keepdims=True) * (1.0 / MOBA_BLOCK)
    km = km_ref[...]
    tiled = jnp.concatenate([km] * MOBA_HEADS, axis=0)
    r = lax.shift_right_logical(lax.broadcasted_iota(jnp.int32, tiled.shape, 0), 4)
    c = lax.shift_right_logical(lax.broadcasted_iota(jnp.int32, tiled.shape, 1), 7)
    o_ref[...] = jnp.where(r == c, tiled, 0.0).astype(o_ref.dtype)


def _moba_kmean(qk3, nb):
    b, t, _ = qk3.shape
    return pl.pallas_call(
        functools.partial(_moba_kmean_kernel, nb=nb),
        grid=(b,),
        in_specs=[pl.BlockSpec((None, t, MOBA_W), lambda bi: (bi, 0, 1))],
        out_specs=pl.BlockSpec((None, LANES, MOBA_W), lambda bi: (bi, 0, 0)),
        out_shape=jax.ShapeDtypeStruct((b, LANES, MOBA_W), BF16),
        scratch_shapes=[pltpu.VMEM((MAX_BLOCKS, MOBA_W), F32)],
        compiler_params=_params(("parallel",)),
        name="moba_kmean",
    )(qk3)


def _moba_select_kernel(q_ref, kmt_ref, o_ref, *, n_sel):
    i = pl.program_id(1)
    g = lax.dot_general(q_ref[...], kmt_ref[...], _NT, preferred_element_type=F32)
    lane = lax.broadcasted_iota(jnp.int32, g.shape, 1)
    j = lane & (MAX_BLOCKS - 1)
    past = j < i
    gm = jnp.where(past, g, -jnp.inf)
    rank = jnp.zeros(g.shape, F32)
    for s in range(1, MAX_BLOCKS):
        wrap = (j + s) >= MAX_BLOCKS
        other = jnp.where(wrap, pltpu.roll(gm, MAX_BLOCKS - s, 1), pltpu.roll(gm, LANES - s, 1))
        beats = (other > gm) | ((other == gm) & wrap)
        rank = rank + jnp.where(beats, 1.0, 0.0)
    sel = past & (rank < n_sel)
    bias = jnp.where(sel, 0.0, MASK_BIAS)
    for h in range(MOBA_HEADS):
        shift = (LANES - MAX_BLOCKS * h) % LANES
        bh = bias if shift == 0 else pltpu.roll(bias, shift, 1)
        o_ref[h] = jnp.where(lane < MAX_BLOCKS, bh, 0.0).astype(o_ref.dtype)


def _moba_select(qk3, kmt, nb, n_sel):
    b, t, _ = qk3.shape
    return pl.pallas_call(
        functools.partial(_moba_select_kernel, n_sel=n_sel),
        grid=(b, nb),
        in_specs=[pl.BlockSpec((None, MOBA_BLOCK, MOBA_W), lambda bi, i: (bi, i, 0)),
                  pl.BlockSpec((None, LANES, MOBA_W), lambda bi, i: (bi, 0, 0))],
        out_specs=pl.BlockSpec((None, MOBA_HEADS, MOBA_BLOCK, LANES), lambda bi, i: (bi, 0, i, 0)),
        out_shape=jax.ShapeDtypeStruct((b, MOBA_HEADS, t, LANES), BF16),
        compiler_params=_params(("parallel", "parallel")),
        name="moba_select",
    )(qk3, kmt)


def _flash_step(st, m, l, acc_t, v_t):
    m_new = jnp.maximum(m, jnp.max(st, axis=0, keepdims=True))
    alpha = jnp.exp2(m - m_new)
    p = jnp.exp2(st - m_new)
    l = alpha * l + jnp.sum(p, axis=0, keepdims=True)
    acc_t = alpha * acc_t + jnp.dot(v_t, p.astype(BF16), preferred_element_type=F32)
    return m_new, l, acc_t


def _transpose_bf16(x):
    return x.astype(F32).T.astype(BF16)


def _flash_finish(o_ref, l_ref, acc_ref, heads):
    for h in range(heads):
        hs = slice(h * HEAD_DIM, (h + 1) * HEAD_DIM)
        o_ref[:, hs] = (acc_ref[h] / l_ref[h:h + 1, :]).T.astype(o_ref.dtype)


def _moba_attn_kernel(q_ref, k_ref, vt_ref, sb_ref, o_ref, m_ref, l_ref, acc_ref, qa_ref, *, scale):
    i = pl.program_id(1)
    own = pl.multiple_of(i * MOBA_BLOCK, MOBA_BLOCK)
    c2 = scale * LOG2_E
    sub_shape = (SUB_TILE, MOBA_BLOCK)
    kidx = lax.broadcasted_iota(jnp.int32, sub_shape, 0)
    qidx = lax.broadcasted_iota(jnp.int32, sub_shape, 1)
    for h in range(MOBA_HEADS):
        hs = slice(h * HEAD_DIM, (h + 1) * HEAD_DIM)
        qa_ref[h, :HEAD_DIM, :] = _transpose_bf16(q_ref[:, hs])
        qa_ref[h, HEAD_DIM:, :] = _transpose_bf16(sb_ref[h])
        m = jnp.full((1, MOBA_BLOCK), NEG_BIG, F32)
        l = jnp.zeros((1, MOBA_BLOCK), F32)
        acc = jnp.zeros((HEAD_DIM, MOBA_BLOCK), F32)
        for u in range(MOBA_BLOCK // SUB_TILE):
            st = jnp.dot(k_ref[pl.ds(own + u * SUB_TILE, SUB_TILE), hs], qa_ref[h, :HEAD_DIM, :],
                         preferred_element_type=F32) * c2
            st = jnp.where(kidx + u * SUB_TILE <= qidx, st, -jnp.inf)
            m, l, acc = _flash_step(st, m, l, acc, vt_ref[i, hs, u * SUB_TILE:(u + 1) * SUB_TILE])
        m_ref[h:h + 1, :] = m
        l_ref[h:h + 1, :] = l
        acc_ref[h] = acc

    lane = lax.broadcasted_iota(jnp.int32, (SUB_TILE, LANES), 1)

    def body(j, carry):
        off = pl.multiple_of(j * MOBA_BLOCK, MOBA_BLOCK)
        onehot = jnp.where(lane == j, 1.0, 0.0).astype(BF16)
        for h in range(MOBA_HEADS):
            hs = slice(h * HEAD_DIM, (h + 1) * HEAD_DIM)
            m, l, acc = m_ref[h:h + 1, :], l_ref[h:h + 1, :], acc_ref[h]
            for u in range(MOBA_BLOCK // SUB_TILE):
                k_aug = jnp.concatenate([k_ref[pl.ds(off + u * SUB_TILE, SUB_TILE), hs], onehot], axis=1)
                st = jnp.dot(k_aug, qa_ref[h], preferred_element_type=F32) * c2
                m, l, acc = _flash_step(st, m, l, acc, vt_ref[j, hs, u * SUB_TILE:(u + 1) * SUB_TILE])
            m_ref[h:h + 1, :] = m
            l_ref[h:h + 1, :] = l
            acc_ref[h] = acc
        return carry

    lax.fori_loop(0, i, body, 0)
    _flash_finish(o_ref, l_ref, acc_ref, MOBA_HEADS)


def _moba_attn(qk3, vt3, selb, nb):
    b, t, _ = qk3.shape
    one = pl.Buffered(1)
    return pl.pallas_call(
        functools.partial(_moba_attn_kernel, scale=HEAD_DIM ** -0.5),
        grid=(b, nb),
        in_specs=[pl.BlockSpec((None, MOBA_BLOCK, MOBA_W), lambda bi, i: (bi, i, 0)),
                  pl.BlockSpec((None, t, MOBA_W), lambda bi, i: (bi, 0, 1), pipeline_mode=one),
                  pl.BlockSpec((nb, MOBA_W, KEY_TILE), lambda bi, i: (bi, 0, 0), pipeline_mode=one),
                  pl.BlockSpec((None, MOBA_HEADS, MOBA_BLOCK, LANES), lambda bi, i: (bi, 0, i, 0))],
        out_specs=pl.BlockSpec((None, MOBA_BLOCK, MOBA_W), lambda bi, i: (bi, i, 0)),
        out_shape=jax.ShapeDtypeStruct((b, t, MOBA_W), BF16),
        scratch_shapes=[pltpu.VMEM((MOBA_HEADS, MOBA_BLOCK), F32),
                        pltpu.VMEM((MOBA_HEADS, MOBA_BLOCK), F32),
                        pltpu.VMEM((MOBA_HEADS, HEAD_DIM, MOBA_BLOCK), F32),
                        pltpu.VMEM((MOBA_HEADS, HEAD_DIM + LANES, MOBA_BLOCK), BF16)],
        compiler_params=_params(("parallel", "arbitrary")),
        name="moba_attn",
    )(qk3, qk3, vt3, selb)


def _key_to_float(t):
    return lax.bitcast_convert_type(jnp.where(t >= 0, t, t ^ 0x7FFFFFFF), F32)


def _dsa_kernel(qi_ref, klo_ref, khi_ref, wt_ref, q_ref, k_ref, vt_ref, o_ref,
                sc_ref, jl_ref, m_ref, l_ref, acc_ref, qt_ref, qit_ref, *, topk, scale, idx_scale, tq, seq_bits):
    i = pl.program_id(1)
    nk = i + 1
    shape = (KEY_TILE, tq)
    krow = lax.broadcasted_iota(jnp.int32, shape, 0)
    qpos = i * tq + lax.broadcasted_iota(jnp.int32, shape, 1)

    wt = wt_ref[...]
    for p in range(IDX_HEADS // 2):
        qit_ref[p] = _transpose_bf16(qi_ref[:, p * LANES:(p + 1) * LANES])
    for h in range(DSA_HEADS):
        qt_ref[h] = _transpose_bf16(q_ref[:, h * HEAD_DIM:(h + 1) * HEAD_DIM])

    def score_body(kt, carry):
        off = pl.multiple_of(kt * KEY_TILE, KEY_TILE)
        keys = jnp.concatenate([klo_ref[pl.ds(off, KEY_TILE), :], khi_ref[pl.ds(off, KEY_TILE), :]], axis=0)
        acc = jnp.zeros(shape, F32)
        for p in range(IDX_HEADS // 2):
            s = jnp.dot(keys, qit_ref[p], preferred_element_type=F32)
            acc = acc + (jnp.maximum(s[:KEY_TILE], 0.0) * wt[2 * p:2 * p + 1, :]
                         + jnp.maximum(s[KEY_TILE:], 0.0) * wt[2 * p + 1:2 * p + 2, :])
        kpos = kt * KEY_TILE + krow
        sc_ref[kt] = jnp.where(kpos <= qpos, acc * idx_scale, -jnp.inf)
        return carry

    lax.fori_loop(0, nk, score_body, 0)

    def count(pred):
        def body(kt, acc):
            hit = jnp.where(pred(sc_ref[kt], kt * KEY_TILE + krow), 1.0, 0.0)
            return acc + jnp.sum(hit.reshape(KEY_TILE // 32, 32, tq), axis=0)
        part = lax.fori_loop(0, nk, body, jnp.zeros((32, tq), F32))
        return jnp.sum(part, axis=0, keepdims=True)

    need_select = nk * tq > topk

    @pl.when(jnp.logical_not(need_select))
    def _():
        def body(kt, carry):
            sc_ref[kt] = jnp.where(kt * KEY_TILE + krow <= qpos, 0.0, NEG_BIG)
            return carry
        lax.fori_loop(0, nk, body, 0)

    @pl.when(need_select)
    def _():
        kf = jnp.float32(topk)
        zero = jnp.zeros((1, tq), F32)
        c0 = count(lambda s, kp: s >= zero)
        t0 = jnp.where(c0 >= kf, 0, INT_MIN).astype(jnp.int32)

        def bit_body(b, t):
            cand = t | lax.shift_left(jnp.int32(1), 30 - b)
            cf = _key_to_float(cand)
            c = count(lambda s, kp: s >= cf)
            return jnp.where(c >= kf, cand, t)

        t = lax.fori_loop(0, 31, bit_body, t0)
        thr = _key_to_float(jnp.maximum(t, KEY_OF_NEG_INF))

        need = kf - count(lambda s, kp: s > thr)
        n_eq = count(lambda s, kp: s == thr)
        jl_ref[...] = jnp.full(jl_ref.shape, 2 ** seq_bits, jnp.int32)

        @pl.when(jnp.max(n_eq - need) > 0.0)
        def _():
            def idx_body(b, c):
                cand = c | lax.shift_left(jnp.int32(1), seq_bits - 1 - b)
                g = count(lambda s, kp: (s == thr) & (kp < cand))
                return jnp.where(g < need, cand, c)
            c = lax.fori_loop(0, seq_bits, idx_body, jnp.zeros((1, tq), jnp.int32))
            jl_ref[...] = jnp.broadcast_to(c, jl_ref.shape)

        def bias_body(kt, carry):
            s = sc_ref[kt]
            kpos = kt * KEY_TILE + krow
            sel = ((s > thr) | ((s == thr) & (kpos <= jl_ref[0:1, :]))) & (kpos <= qpos)
            sc_ref[kt] = jnp.where(sel, 0.0, NEG_BIG)
            return carry
        lax.fori_loop(0, nk, bias_body, 0)

    m_ref[...] = jnp.full(m_ref.shape, NEG_BIG, F32)
    l_ref[...] = jnp.zeros(l_ref.shape, F32)
    acc_ref[...] = jnp.zeros(acc_ref.shape, F32)

    c2 = scale * LOG2_E

    def attn_body(kt, carry):
        off = pl.multiple_of(kt * KEY_TILE, KEY_TILE)
        for h in range(DSA_HEADS):
            hs = slice(h * HEAD_DIM, (h + 1) * HEAD_DIM)
            m, l, acc = m_ref[h:h + 1, :], l_ref[h:h + 1, :], acc_ref[h]
            for u in range(KEY_TILE // SUB_TILE):
                us = slice(u * SUB_TILE, (u + 1) * SUB_TILE)
                st = jnp.dot(k_ref[pl.ds(off + u * SUB_TILE, SUB_TILE), hs], qt_ref[h],
                             preferred_element_type=F32) * c2 + sc_ref[kt, us, :]
                m, l, acc = _flash_step(st, m, l, acc, vt_ref[kt, hs, us])
            m_ref[h:h + 1, :] = m
            l_ref[h:h + 1, :] = l
            acc_ref[h] = acc
        return carry

    lax.fori_loop(0, nk, attn_body, 0)
    _flash_finish(o_ref, l_ref, acc_ref, DSA_HEADS)


def _dsa_attn(qi3, klo3, khi3, wit, qk3, vt3, topk):
    b, t, _ = qk3.shape
    tq = KEY_TILE
    nq = t // tq
    seq_bits = max(1, (t - 1).bit_length())
    one = pl.Buffered(1)
    kern = functools.partial(
        _dsa_kernel, topk=topk, scale=HEAD_DIM ** -0.5,
        idx_scale=(IDX_DIM ** -0.5) * (IDX_HEADS ** -0.5), tq=tq, seq_bits=seq_bits)
    return pl.pallas_call(
        kern,
        grid=(b, nq),
        in_specs=[pl.BlockSpec((None, tq, IDX_W), lambda bi, i: (bi, i, 0)),
                  pl.BlockSpec((None, t, LANES), lambda bi, i: (bi, 0, 0), pipeline_mode=one),
                  pl.BlockSpec((None, t, LANES), lambda bi, i: (bi, 0, 0), pipeline_mode=one),
                  pl.BlockSpec((None, IDX_HEADS, tq), lambda bi, i: (bi * nq + i, IDX_DIM // IDX_HEADS, 0)),
                  pl.BlockSpec((None, tq, DSA_W), lambda bi, i: (bi, i, 2)),
                  pl.BlockSpec((None, t, DSA_W), lambda bi, i: (bi, 0, 3), pipeline_mode=one),
                  pl.BlockSpec((nq, DSA_W, KEY_TILE), lambda bi, i: (bi, 1, 0), pipeline_mode=one)],
        out_specs=pl.BlockSpec((None, tq, DSA_W), lambda bi, i: (bi, i, 0)),
        out_shape=jax.ShapeDtypeStruct((b, t, DSA_W), BF16),
        scratch_shapes=[pltpu.VMEM((nq, KEY_TILE, tq), F32),
                        pltpu.VMEM((8, tq), jnp.int32),
                        pltpu.VMEM((DSA_HEADS, tq), F32),
                        pltpu.VMEM((DSA_HEADS, tq), F32),
                        pltpu.VMEM((DSA_HEADS, HEAD_DIM, tq), F32),
                        pltpu.VMEM((DSA_HEADS, HEAD_DIM, tq), BF16),
                        pltpu.VMEM((IDX_HEADS // 2, LANES, tq), BF16)],
        compiler_params=_params(("parallel", "arbitrary")),
        name="dsa_attn",
    )(qi3, klo3, khi3, wit, qk3, qk3, vt3)


def _mix_out_kernel(oa_ref, ob_ref, sg_ref, wa_ref, wb_ref, wo_ref, x_ref, gt_ref, gpost_ref,
                    gpre_ref, sc_ref, sh_ref, x1_ref, h2_ref, *, d):
    ya = jnp.dot(oa_ref[...], wa_ref[...], preferred_element_type=F32)
    yb = jnp.dot(ob_ref[...], wb_ref[...], preferred_element_type=F32)
    z = sg_ref[:, :d].astype(F32) * ya + sg_ref[:, d:].astype(F32) * yb
    y = jnp.dot(z.astype(BF16), wo_ref[...], preferred_element_type=F32)
    x1 = x_ref[...] + gt_ref[...] * (_rms(y) * gpost_ref[...])
    x1_ref[...] = x1
    h2_ref[...] = ((_rms(x1) * gpre_ref[...]) * (1.0 + sc_ref[...]) + sh_ref[...]).astype(h2_ref.dtype)


def _mix_out(oa3, ob3, sg3, wa, wb, wo, x, gt1, g_post, g_pre2, sc2, sh2):
    b, t, d = x.shape
    tm = _tile(t, 256)
    one = pl.Buffered(1)
    row = lambda w: pl.BlockSpec((None, tm, w), lambda bi, i: (bi, i, 0))
    full = lambda a: pl.BlockSpec(a.shape, lambda bi, i: (0, 0), pipeline_mode=one)
    vec = pl.BlockSpec((None, 1, d), lambda bi, i: (bi, 0, 0))
    gain = pl.BlockSpec((1, d), lambda bi, i: (0, 0))
    return pl.pallas_call(
        functools.partial(_mix_out_kernel, d=d),
        grid=(b, t // tm),
        in_specs=[row(MOBA_W), row(DSA_W), row(2 * d), full(wa), full(wb), full(wo), row(d),
                  vec, gain, gain, vec, vec],
        out_specs=[row(d), row(d)],
        out_shape=[jax.ShapeDtypeStruct((b, t, d), F32), jax.ShapeDtypeStruct((b, t, d), BF16)],
        compiler_params=_params(("parallel", "parallel")),
        name="mix_out",
    )(oa3, ob3, sg3, wa, wb, wo, x, gt1, g_post, g_pre2, sc2, sh2)


def _ffn_kernel(h_ref, w1_ref, w2_ref, x1_ref, gt_ref, g_ref, o_ref, acc_ref):
    j = pl.program_id(2)

    @pl.when(j == 0)
    def _():
        acc_ref[...] = jnp.zeros(acc_ref.shape, F32)

    u = jnp.dot(h_ref[...], w1_ref[...], preferred_element_type=F32)
    u = jnp.square(jnp.maximum(u, 0.0)).astype(BF16)
    acc_ref[...] += jnp.dot(u, w2_ref[...], preferred_element_type=F32)

    @pl.when(j == pl.num_programs(2) - 1)
    def _():
        o_ref[...] = x1_ref[...] + gt_ref[...] * (_rms(acc_ref[...]) * g_ref[...])


def _ffn(h2, w1, w2, x1, gt2, g_post):
    b, t, d = x1.shape
    ff = w1.shape[1]
    tm = _tile(t, 512)
    tf = _tile(ff, 512)
    row = lambda: pl.BlockSpec((None, tm, d), lambda bi, i, j: (bi, i, 0))
    return pl.pallas_call(
        _ffn_kernel,
        grid=(b, t // tm, ff // tf),
        in_specs=[row(),
                  pl.BlockSpec((d, tf), lambda bi, i, j: (0, j)),
                  pl.BlockSpec((tf, d), lambda bi, i, j: (j, 0)),
                  row(),
                  pl.BlockSpec((None, 1, d), lambda bi, i, j: (bi, 0, 0)),
                  pl.BlockSpec((1, d), lambda bi, i, j: (0, 0))],
        out_specs=row(),
        out_shape=jax.ShapeDtypeStruct((b, t, d), F32),
        scratch_shapes=[pltpu.VMEM((tm, d), F32)],
        compiler_params=_params(("parallel", "parallel", "arbitrary")),
        name="ffn",
    )(h2, w1, w2, x1, gt2, g_post)


def _rope_tables(t):
    pos = jnp.arange(t, dtype=F32)[:, None]
    lane = jnp.arange(LANES)[None, :]

    def cos_sin(half):
        inv_freq = jnp.power(ROPE_THETA, -jnp.arange(half, dtype=F32) / half)
        ang = pos * inv_freq[None, :]
        reps = LANES // half
        return jnp.tile(jnp.cos(ang), (1, reps)), jnp.tile(jnp.sin(ang), (1, reps))

    cos128, sin128 = cos_sin(HEAD_DIM // 2)
    sin128 = jnp.where(lane < HEAD_DIM // 2, -sin128, sin128)
    cos64, sin64 = cos_sin(IDX_DIM // 2)
    low = (lane % IDX_DIM) < IDX_DIM // 2
    sa64 = jnp.where(low, -sin64, 0.0)
    sb64 = jnp.where(low, 0.0, sin64)
    is_key = lane < IDX_DIM
    kw = (jnp.where(is_key, cos64, 1.0), jnp.where(is_key, sa64, 0.0), jnp.where(is_key, sb64, 0.0))
    return (cos128, sin128), (cos64, sa64, sb64), kw


def kernel(x, c, w_ada, b_ada, g_pre_mix, g_post_mix, w_in, w_moba_out, w_dsa_out, w_o,
           g_pre_ffn, g_post_ffn, w_ff1, w_ff2):
    b, t, d = x.shape
    m = b * t
    nb = t // MOBA_BLOCK
    assert t % MOBA_BLOCK == 0 and nb <= MAX_BLOCKS and d % LANES == 0 and b <= 16
    n_sel = max(1, min(MOBA_TOPK, nb - 1))
    topk = min(DSA_TOPK_MAX, t // 4)
    rope128, rope64, rope_kw = _rope_tables(t)
    c_pad = jnp.zeros((16, d), F32).at[:b].set(c)

    o_dsa, o_qi, o_kw, o_g = 3 * MOBA_W, 3 * MOBA_W + 3 * DSA_W, 3 * MOBA_W + 3 * DSA_W + IDX_W, \
        3 * MOBA_W + 3 * DSA_W + IDX_W + IDX_DIM + IDX_HEADS

    for l in range(w_ada.shape[0]):
        mod = _ada(c_pad, w_ada[l], b_ada[l][None, :])[:b]
        sh1, sc1, gt1, sh2, sc2, gt2 = [v[:, None, :] for v in jnp.split(mod, 6, axis=-1)]

        wl = w_in[l]
        w_qk = jnp.concatenate([wl[:, :2 * MOBA_W], wl[:, o_dsa:o_dsa + 2 * DSA_W]], axis=1).astype(BF16)
        w_v = jnp.concatenate([wl[:, 2 * MOBA_W:o_dsa], wl[:, o_dsa + 2 * DSA_W:o_qi]], axis=1).astype(BF16)
        w_qi = wl[:, o_qi:o_kw].astype(BF16)
        w_kw = jnp.pad(wl[:, o_kw:o_g], ((0, 0), (0, LANES - IDX_DIM - IDX_HEADS))).astype(BF16)
        w_g = wl[:, o_g:].astype(BF16)

        h = _norm_mod(x, g_pre_mix[l][None, :], sc1, sh1).reshape(m, d)
        (qk,) = _proj(_mm_rope128_kernel, h, w_qk, rope128, [BF16], name="proj_qk")
        vt3 = _proj_t(h, w_v.T, "proj_v")
        (qi,) = _proj(_mm_rope64_kernel, h, w_qi, rope64, [BF16], name="proj_qi")
        klo, khi, wit = _proj_kw(h, w_kw, w_kw.T, rope_kw, t)
        (sg,) = _proj(_mm_sigmoid_kernel, h, w_g, (), [BF16], name="proj_gate")

        qk3 = qk.reshape(b, t, -1)
        kmt = _moba_kmean(qk3, nb)
        selb = _moba_select(qk3, kmt, nb, n_sel)
        oa = _moba_attn(qk3, vt3, selb, nb)
        ob = _dsa_attn(qi.reshape(b, t, -1), klo.reshape(b, t, -1), khi.reshape(b, t, -1),
                       wit, qk3, vt3, topk)

        x, h2 = _mix_out(oa, ob, sg.reshape(b, t, -1), w_moba_out[l].astype(BF16),
                         w_dsa_out[l].astype(BF16), w_o[l].astype(BF16), x, gt1,
                         g_post_mix[l][None, :], g_pre_ffn[l][None, :], sc2, sh2)
        x = _ffn(h2, w_ff1[l].astype(BF16), w_ff2[l].astype(BF16), x, gt2, g_post_ffn[l][None, :])
    return x
```

```python
import functools

import jax
import jax.numpy as jnp
from jax import lax
from jax.experimental import pallas as pl
from jax.experimental.pallas import tpu as pltpu

HEAD_DIM = 128
MOBA_HEADS = 8
MOBA_BLOCK = 256
MOBA_TOPK = 3
DSA_HEADS = 8
IDX_HEADS = 16
IDX_DIM = 64
DSA_TOPK_MAX = 256
ROPE_THETA = 10000.0
RMS_EPS = 1e-6

MOBA_W = MOBA_HEADS * HEAD_DIM
DSA_W = DSA_HEADS * HEAD_DIM
IDX_W = IDX_HEADS * IDX_DIM
LANES = 128
MAX_BLOCKS = LANES // MOBA_HEADS
KEY_TILE = 256
SUB_TILE = 128
LOG2_E = 1.4426950408889634
MASK_BIAS = -30000.0
NEG_BIG = -1e30
INT_MIN = -(2 ** 31)
KEY_OF_NEG_INF = -2139095041
VMEM_LIMIT = 56 * 1024 * 1024

F32 = jnp.float32
BF16 = jnp.bfloat16
_NT = (((1,), (1,)), ((), ()))


def _params(sem):
    return pltpu.CompilerParams(dimension_semantics=sem, vmem_limit_bytes=VMEM_LIMIT)


def _tile(n, pref):
    if n <= pref:
        return n
    t = pref - pref % LANES
    while t >= LANES:
        if n % t == 0:
            return t
        t -= LANES
    return n


def _rms(x):
    return x * lax.rsqrt(jnp.mean(x * x, axis=-1, keepdims=True) + RMS_EPS)


def _ada_kernel(c_ref, w_ref, b_ref, o_ref):
    c = c_ref[...]
    cs = (c * jax.nn.sigmoid(c)).astype(BF16)
    o_ref[...] = jnp.dot(cs, w_ref[...].astype(BF16), preferred_element_type=F32) + b_ref[...]


def _ada(c_pad, w, b):
    rows, d = c_pad.shape
    n = w.shape[1]
    tn = _tile(n, 1024)
    return pl.pallas_call(
        _ada_kernel,
        grid=(n // tn,),
        in_specs=[pl.BlockSpec((rows, d), lambda j: (0, 0)),
                  pl.BlockSpec((d, tn), lambda j: (0, j)),
                  pl.BlockSpec((1, tn), lambda j: (0, j))],
        out_specs=pl.BlockSpec((rows, tn), lambda j: (0, j)),
        out_shape=jax.ShapeDtypeStruct((rows, n), F32),
        compiler_params=_params(("parallel",)),
        name="ada_mod",
    )(c_pad, w, b)


def _norm_mod_kernel(x_ref, g_ref, sc_ref, sh_ref, o_ref):
    y = _rms(x_ref[...])
    o_ref[...] = ((y * g_ref[...]) * (1.0 + sc_ref[...]) + sh_ref[...]).astype(o_ref.dtype)


def _norm_mod(x, g, sc, sh):
    b, t, d = x.shape
    tt = _tile(t, 512)
    vec = pl.BlockSpec((None, 1, d), lambda bi, ti: (bi, 0, 0))
    return pl.pallas_call(
        _norm_mod_kernel,
        grid=(b, t // tt),
        in_specs=[pl.BlockSpec((None, tt, d), lambda bi, ti: (bi, ti, 0)),
                  pl.BlockSpec((1, d), lambda bi, ti: (0, 0)), vec, vec],
        out_specs=pl.BlockSpec((None, tt, d), lambda bi, ti: (bi, ti, 0)),
        out_shape=jax.ShapeDtypeStruct((b, t, d), BF16),
        compiler_params=_params(("parallel", "parallel")),
        name="norm_mod",
    )(x, g, sc, sh)


def _first_token_tile():
    return pl.program_id(1) == 0


def _rope64(y, cos, sin_a, sin_b):
    return y * cos + pltpu.roll(y, LANES - 32, 1) * sin_a + pltpu.roll(y, 32, 1) * sin_b


def _wp_rope128_kernel(h_ref, w_ref, cos_ref, sin_ref, o_ref, wb_ref):
    @pl.when(_first_token_tile())
    def _():
        wb_ref[...] = w_ref[...].astype(BF16)

    acc = jnp.dot(h_ref[...], wb_ref[...], preferred_element_type=F32)
    cos = cos_ref[...]
    sin = sin_ref[...]
    for g in range(acc.shape[1] // LANES):
        y = acc[:, g * LANES:(g + 1) * LANES]
        o_ref[:, g * LANES:(g + 1) * LANES] = (
            y * cos + pltpu.roll(y, LANES // 2, 1) * sin).astype(o_ref.dtype)


def _wp_rope64_kernel(h_ref, w_ref, cos_ref, sa_ref, sb_ref, o_ref, wb_ref):
    @pl.when(_first_token_tile())
    def _():
        wb_ref[...] = w_ref[...].astype(BF16)

    acc = jnp.dot(h_ref[...], wb_ref[...], preferred_element_type=F32)
    cos, sa, sb = cos_ref[...], sa_ref[...], sb_ref[...]
    for g in range(acc.shape[1] // LANES):
        y = acc[:, g * LANES:(g + 1) * LANES]
        o_ref[:, g * LANES:(g + 1) * LANES] = _rope64(y, cos, sa, sb).astype(o_ref.dtype)


def _wp_vt_kernel(h_ref, w_ref, o_ref, wt_ref):
    @pl.when(_first_token_tile())
    def _():
        wt_ref[...] = w_ref[...].T.astype(BF16)

    r = lax.dot_general(wt_ref[...], h_ref[...], _NT, preferred_element_type=F32)
    for c in range(o_ref.shape[0]):
        o_ref[c] = r[:, c * KEY_TILE:(c + 1) * KEY_TILE].astype(o_ref.dtype)


def _wp_gate_kernel(h_ref, wa_ref, wb_ref, o_ref, wg_ref, *, shift):
    @pl.when(_first_token_tile())
    def _():
        both = jnp.concatenate([wa_ref[...], wb_ref[...]], axis=1)
        wg_ref[...] = both[:, shift:shift + wg_ref.shape[1]].astype(BF16)

    acc = jnp.dot(h_ref[...], wg_ref[...], preferred_element_type=F32)
    o_ref[...] = jax.nn.sigmoid(acc).astype(o_ref.dtype)


def _wp_kw_kernel(h_ref, w_ref, cos_ref, sa_ref, sb_ref, klo_ref, khi_ref, wit_ref, wb_ref, wt_ref):
    @pl.when(pl.program_id(0) == 0)
    def _():
        lane = lax.broadcasted_iota(jnp.int32, w_ref.shape, 1)
        w = jnp.where(lane < IDX_DIM + IDX_HEADS, w_ref[...], 0.0)
        wb_ref[...] = w.astype(BF16)
        wt_ref[...] = w.T.astype(BF16)

    h = h_ref[...]
    y = jnp.dot(h, wb_ref[...], preferred_element_type=F32)
    r = _rope64(y, cos_ref[...], sa_ref[...], sb_ref[...])
    lane = lax.broadcasted_iota(jnp.int32, y.shape, 1)
    klo = jnp.where(lane < IDX_DIM, r, 0.0)
    klo_ref[...] = klo.astype(BF16)
    khi_ref[...] = pltpu.roll(klo, IDX_DIM, 1).astype(BF16)
    yt = lax.dot_general(wt_ref[...], h, _NT, preferred_element_type=F32)
    for c in range(wit_ref.shape[0]):
        wit_ref[c] = yt[:, c * KEY_TILE:(c + 1) * KEY_TILE]


def _in_proj(h, w, l, t, tables128, tables64, tables_kw):
    m, k = h.shape
    d = k
    tm = _tile(t, 1024)
    nt = t // tm
    ni = m // tm
    assert tm % KEY_TILE == 0
    tn = 512
    o_dsa = 3 * MOBA_W
    o_qi = o_dsa + 3 * DSA_W
    o_kw = o_qi + IDX_W
    o_g = o_kw + IDX_DIM + IDX_HEADS
    sem = ("arbitrary", "arbitrary")
    h_spec = pl.BlockSpec((tm, k), lambda j, i: (i, 0))
    tab = pl.BlockSpec((tm, LANES), lambda j, i: (i % nt, 0))
    row = pl.BlockSpec((tm, tn), lambda j, i: (i, j))

    def wcol(first_blocks, n_first, second_start):
        return pl.BlockSpec((None, k, tn), lambda j, i: (l, 0, jnp.where(j < n_first, first_blocks + j,
                                                                          second_start + j - n_first)))

    n_m, n_d = 2 * MOBA_W // tn, 2 * DSA_W // tn
    qk = pl.pallas_call(
        _wp_rope128_kernel,
        grid=(n_m + n_d, ni),
        in_specs=[h_spec, wcol(0, n_m, o_dsa // tn), tab, tab],
        out_specs=row,
        out_shape=jax.ShapeDtypeStruct((m, 2 * MOBA_W + 2 * DSA_W), BF16),
        scratch_shapes=[pltpu.VMEM((k, tn), BF16)],
        compiler_params=_params(sem),
        name="proj_qk",
    )(h, w, *tables128)

    n_m, n_d = MOBA_W // tn, DSA_W // tn
    vt = pl.pallas_call(
        _wp_vt_kernel,
        grid=(n_m + n_d, ni),
        in_specs=[h_spec, wcol(2 * MOBA_W // tn, n_m, (o_dsa + 2 * DSA_W) // tn)],
        out_specs=pl.BlockSpec((tm // KEY_TILE, tn, KEY_TILE), lambda j, i: (i, j, 0)),
        out_shape=jax.ShapeDtypeStruct((m // KEY_TILE, MOBA_W + DSA_W, KEY_TILE), BF16),
        scratch_shapes=[pltpu.VMEM((tn, k), BF16)],
        compiler_params=_params(sem),
        name="proj_v",
    )(h, w)

    n_q = IDX_W // tn
    qi = pl.pallas_call(
        _wp_rope64_kernel,
        grid=(n_q, ni),
        in_specs=[h_spec, wcol(o_qi // tn, n_q, 0), tab, tab, tab],
        out_specs=row,
        out_shape=jax.ShapeDtypeStruct((m, IDX_W), BF16),
        scratch_shapes=[pltpu.VMEM((k, tn), BF16)],
        compiler_params=_params(sem),
        name="proj_qi",
    )(h, w, *tables64)

    tab1 = pl.BlockSpec((tm, LANES), lambda i: (i % nt, 0))
    row1 = pl.BlockSpec((tm, LANES), lambda i: (i, 0))
    klo, khi, wit = pl.pallas_call(
        _wp_kw_kernel,
        grid=(ni,),
        in_specs=[pl.BlockSpec((tm, k), lambda i: (i, 0)),
                  pl.BlockSpec((None, k, LANES), lambda i: (l, 0, o_kw // LANES)), tab1, tab1, tab1],
        out_specs=[row1, row1, pl.BlockSpec((tm // KEY_TILE, LANES, KEY_TILE), lambda i: (i, 0, 0))],
        out_shape=[jax.ShapeDtypeStruct((m, LANES), BF16), jax.ShapeDtypeStruct((m, LANES), BF16),
                   jax.ShapeDtypeStruct((m // KEY_TILE, LANES, KEY_TILE), F32)],
        scratch_shapes=[pltpu.VMEM((k, LANES), BF16), pltpu.VMEM((LANES, k), BF16)],
        compiler_params=_params(("arbitrary",)),
        name="proj_kw",
    )(h, w, *tables_kw)

    tg = _tile(2 * d, tn)
    g0 = o_kw // tg
    assert o_kw % tg == 0 and o_g - o_kw < tg
    sg = pl.pallas_call(
        functools.partial(_wp_gate_kernel, shift=o_g - o_kw),
        grid=(2 * d // tg, ni),
        in_specs=[h_spec,
                  pl.BlockSpec((None, k, tg), lambda j, i: (l, 0, g0 + j)),
                  pl.BlockSpec((None, k, tg), lambda j, i: (l, 0, g0 + j + 1))],
        out_specs=pl.BlockSpec((tm, tg), lambda j, i: (i, j)),
        out_shape=jax.ShapeDtypeStruct((m, 2 * d), BF16),
        scratch_shapes=[pltpu.VMEM((k, tg), BF16)],
        compiler_params=_params(sem),
        name="proj_gate",
    )(h, w, w)
    return qk, vt, qi, klo, khi, wit, sg


def _moba_kmean_kernel(k_ref, o_ref, km_ref, *, nb):
    km_ref[...] = jnp.zeros(km_ref.shape, F32)
    for j in range(nb):
        blk = k_ref[j * MOBA_BLOCK:(j + 1) * MOBA_BLOCK, :].astype(F32)
        km_ref[j:j + 1, :] = jnp.sum(blk, axis=0, keepdims=True) * (1.0 / MOBA_BLOCK)
    km = km_ref[...]
    tiled = jnp.concatenate([km] * MOBA_HEADS, axis=0)
    r = lax.shift_right_logical(lax.broadcasted_iota(jnp.int32, tiled.shape, 0), 4)
    c = lax.shift_right_logical(lax.broadcasted_iota(jnp.int32, tiled.shape, 1), 7)
    o_ref[...] = jnp.where(r == c, tiled, 0.0).astype(o_ref.dtype)


def _moba_kmean(qk3, nb):
    b, t, _ = qk3.shape
    return pl.pallas_call(
        functools.partial(_moba_kmean_kernel, nb=nb),
        grid=(b,),
        in_specs=[pl.BlockSpec((None, t, MOBA_W), lambda bi: (bi, 0, 1))],
        out_specs=pl.BlockSpec((None, LANES, MOBA_W), lambda bi: (bi, 0, 0)),
        out_shape=jax.ShapeDtypeStruct((b, LANES, MOBA_W), BF16),
        scratch_shapes=[pltpu.VMEM((MAX_BLOCKS, MOBA_W), F32)],
        compiler_params=_params(("parallel",)),
        name="moba_kmean",
    )(qk3)


def _moba_select_kernel(q_ref, kmt_ref, o_ref, *, n_sel):
    i = pl.program_id(1)
    g = lax.dot_general(q_ref[...], kmt_ref[...], _NT, preferred_element_type=F32)
    lane = lax.broadcasted_iota(jnp.int32, g.shape, 1)
    j = lane & (MAX_BLOCKS - 1)
    past = j < i
    gm = jnp.where(past, g, -jnp.inf)
    rank = jnp.zeros(g.shape, F32)
    for s in range(1, MAX_BLOCKS):
        wrap = (j + s) >= MAX_BLOCKS
        other = jnp.where(wrap, pltpu.roll(gm, MAX_BLOCKS - s, 1), pltpu.roll(gm, LANES - s, 1))
        beats = (other > gm) | ((other == gm) & wrap)
        rank = rank + jnp.where(beats, 1.0, 0.0)
    sel = past & (rank < n_sel)
    bias = jnp.where(sel, 0.0, MASK_BIAS)
    for h in range(MOBA_HEADS):
        shift = (LANES - MAX_BLOCKS * h) % LANES
        bh = bias if shift == 0 else pltpu.roll(bias, shift, 1)
        o_ref[h] = jnp.where(lane < MAX_BLOCKS, bh, 0.0).astype(o_ref.dtype)


def _moba_select(qk3, kmt, nb, n_sel):
    b, t, _ = qk3.shape
    return pl.pallas_call(
        functools.partial(_moba_select_kernel, n_sel=n_sel),
        grid=(b, nb),
        in_specs=[pl.BlockSpec((None, MOBA_BLOCK, MOBA_W), lambda bi, i: (bi, i, 0)),
                  pl.BlockSpec((None, LANES, MOBA_W), lambda bi, i: (bi, 0, 0))],
        out_specs=pl.BlockSpec((None, MOBA_HEADS, MOBA_BLOCK, LANES), lambda bi, i: (bi, 0, i, 0)),
        out_shape=jax.ShapeDtypeStruct((b, MOBA_HEADS, t, LANES), BF16),
        compiler_params=_params(("parallel", "parallel")),
        name="moba_select",
    )(qk3, kmt)


def _flash_step(st, m, l, acc_t, v_t):
    m_new = jnp.maximum(m, jnp.max(st, axis=0, keepdims=True))
    alpha = jnp.exp2(m - m_new)
    p = jnp.exp2(st - m_new)
    l = alpha * l + jnp.sum(p, axis=0, keepdims=True)
    acc_t = alpha * acc_t + jnp.dot(v_t, p.astype(BF16), preferred_element_type=F32)
    return m_new, l, acc_t


def _transpose_bf16(x):
    return x.astype(F32).T.astype(BF16)


def _flash_finish(o_ref, l_ref, acc_ref, heads):
    for h in range(heads):
        hs = slice(h * HEAD_DIM, (h + 1) * HEAD_DIM)
        o_ref[:, hs] = (acc_ref[h] / l_ref[h:h + 1, :]).T.astype(o_ref.dtype)


def _moba_attn_kernel(q_ref, k_ref, vt_ref, sb_ref, o_ref, m_ref, l_ref, acc_ref, qa_ref, *, scale):
    i = pl.program_id(1)
    own = pl.multiple_of(i * MOBA_BLOCK, MOBA_BLOCK)
    c2 = scale * LOG2_E
    sub_shape = (SUB_TILE, MOBA_BLOCK)
    kidx = lax.broadcasted_iota(jnp.int32, sub_shape, 0)
    qidx = lax.broadcasted_iota(jnp.int32, sub_shape, 1)
    for h in range(MOBA_HEADS):
        hs = slice(h * HEAD_DIM, (h + 1) * HEAD_DIM)
        qa_ref[h, :HEAD_DIM, :] = _transpose_bf16(q_ref[:, hs])
        qa_ref[h, HEAD_DIM:, :] = _transpose_bf16(sb_ref[h])
        m = jnp.full((1, MOBA_BLOCK), NEG_BIG, F32)
        l = jnp.zeros((1, MOBA_BLOCK), F32)
        acc = jnp.zeros((HEAD_DIM, MOBA_BLOCK), F32)
        for u in range(MOBA_BLOCK // SUB_TILE):
            st = jnp.dot(k_ref[pl.ds(own + u * SUB_TILE, SUB_TILE), hs], qa_ref[h, :HEAD_DIM, :],
                         preferred_element_type=F32) * c2
            st = jnp.where(kidx + u * SUB_TILE <= qidx, st, -jnp.inf)
            m, l, acc = _flash_step(st, m, l, acc, vt_ref[i, hs, u * SUB_TILE:(u + 1) * SUB_TILE])
        m_ref[h:h + 1, :] = m
        l_ref[h:h + 1, :] = l
        acc_ref[h] = acc

    lane = lax.broadcasted_iota(jnp.int32, (SUB_TILE, LANES), 1)

    def body(j, carry):
        off = pl.multiple_of(j * MOBA_BLOCK, MOBA_BLOCK)
        onehot = jnp.where(lane == j, 1.0, 0.0).astype(BF16)
        for h in range(MOBA_HEADS):
            hs = slice(h * HEAD_DIM, (h + 1) * HEAD_DIM)
            m, l, acc = m_ref[h:h + 1, :], l_ref[h:h + 1, :], acc_ref[h]
            for u in range(MOBA_BLOCK // SUB_TILE):
                k_aug = jnp.concatenate([k_ref[pl.ds(off + u * SUB_TILE, SUB_TILE), hs], onehot], axis=1)
                st = jnp.dot(k_aug, qa_ref[h], preferred_element_type=F32) * c2
                m, l, acc = _flash_step(st, m, l, acc, vt_ref[j, hs, u * SUB_TILE:(u + 1) * SUB_TILE])
            m_ref[h:h + 1, :] = m
            l_ref[h:h + 1, :] = l
            acc_ref[h] = acc
        return carry

    lax.fori_loop(0, i, body, 0)
    _flash_finish(o_ref, l_ref, acc_ref, MOBA_HEADS)


def _moba_attn(qk3, vt3, selb, nb):
    b, t, _ = qk3.shape
    one = pl.Buffered(1)
    return pl.pallas_call(
        functools.partial(_moba_attn_kernel, scale=HEAD_DIM ** -0.5),
        grid=(b, nb),
        in_specs=[pl.BlockSpec((None, MOBA_BLOCK, MOBA_W), lambda bi, i: (bi, i, 0)),
                  pl.BlockSpec((None, t, MOBA_W), lambda bi, i: (bi, 0, 1), pipeline_mode=one),
                  pl.BlockSpec((nb, MOBA_W, KEY_TILE), lambda bi, i: (bi, 0, 0), pipeline_mode=one),
                  pl.BlockSpec((None, MOBA_HEADS, MOBA_BLOCK, LANES), lambda bi, i: (bi, 0, i, 0))],
        out_specs=pl.BlockSpec((None, MOBA_BLOCK, MOBA_W), lambda bi, i: (bi, i, 0)),
        out_shape=jax.ShapeDtypeStruct((b, t, MOBA_W), BF16),
        scratch_shapes=[pltpu.VMEM((MOBA_HEADS, MOBA_BLOCK), F32),
                        pltpu.VMEM((MOBA_HEADS, MOBA_BLOCK), F32),
                        pltpu.VMEM((MOBA_HEADS, HEAD_DIM, MOBA_BLOCK), F32),
                        pltpu.VMEM((MOBA_HEADS, HEAD_DIM + LANES, MOBA_BLOCK), BF16)],
        compiler_params=_params(("parallel", "arbitrary")),
        name="moba_attn",
    )(qk3, qk3, vt3, selb)


def _key_to_float(t):
    return lax.bitcast_convert_type(jnp.where(t >= 0, t, t ^ 0x7FFFFFFF), F32)


def _dsa_kernel(qi_ref, klo_ref, khi_ref, wt_ref, q_ref, k_ref, vt_ref, o_ref,
                sc_ref, jl_ref, m_ref, l_ref, acc_ref, qt_ref, qit_ref, *, topk, scale, idx_scale, tq, seq_bits):
    i = pl.program_id(1)
    nk = i + 1
    shape = (KEY_TILE, tq)
    krow = lax.broadcasted_iota(jnp.int32, shape, 0)
    qpos = i * tq + lax.broadcasted_iota(jnp.int32, shape, 1)

    wt = wt_ref[...]
    for p in range(IDX_HEADS // 2):
        qit_ref[p] = _transpose_bf16(qi_ref[:, p * LANES:(p + 1) * LANES])
    for h in range(DSA_HEADS):
        qt_ref[h] = _transpose_bf16(q_ref[:, h * HEAD_DIM:(h + 1) * HEAD_DIM])

    def score_body(kt, carry):
        off = pl.multiple_of(kt * KEY_TILE, KEY_TILE)
        keys = jnp.concatenate([klo_ref[pl.ds(off, KEY_TILE), :], khi_ref[pl.ds(off, KEY_TILE), :]], axis=0)
        acc = jnp.zeros(shape, F32)
        for p in range(IDX_HEADS // 2):
            s = jnp.dot(keys, qit_ref[p], preferred_element_type=F32)
            acc = acc + (jnp.maximum(s[:KEY_TILE], 0.0) * wt[2 * p:2 * p + 1, :]
                         + jnp.maximum(s[KEY_TILE:], 0.0) * wt[2 * p + 1:2 * p + 2, :])
        kpos = kt * KEY_TILE + krow
        sc_ref[kt] = jnp.where(kpos <= qpos, acc * idx_scale, -jnp.inf)
        return carry

    lax.fori_loop(0, nk, score_body, 0)

    def count(pred):
        def body(kt, acc):
            hit = jnp.where(pred(sc_ref[kt], kt * KEY_TILE + krow), 1.0, 0.0)
            return acc + jnp.sum(hit.reshape(KEY_TILE // 32, 32, tq), axis=0)
        part = lax.fori_loop(0, nk, body, jnp.zeros((32, tq), F32))
        return jnp.sum(part, axis=0, keepdims=True)

    need_select = nk * tq > topk

    @pl.when(jnp.logical_not(need_select))
    def _():
        def body(kt, carry):
            sc_ref[kt] = jnp.where(kt * KEY_TILE + krow <= qpos, 0.0, NEG_BIG)
            return carry
        lax.fori_loop(0, nk, body, 0)

    @pl.when(need_select)
    def _():
        kf = jnp.float32(topk)
        zero = jnp.zeros((1, tq), F32)
        c0 = count(lambda s, kp: s >= zero)
        t0 = jnp.where(c0 >= kf, 0, INT_MIN).astype(jnp.int32)

        def bit_body(b, t):
            cand = t | lax.shift_left(jnp.int32(1), 30 - b)
            cf = _key_to_float(cand)
            c = count(lambda s, kp: s >= cf)
            return jnp.where(c >= kf, cand, t)

        t = lax.fori_loop(0, 31, bit_body, t0)
        thr = _key_to_float(jnp.maximum(t, KEY_OF_NEG_INF))

        need = kf - count(lambda s, kp: s > thr)
        n_eq = count(lambda s, kp: s == thr)
        jl_ref[...] = jnp.full(jl_ref.shape, 2 ** seq_bits, jnp.int32)

        @pl.when(jnp.max(n_eq - need) > 0.0)
        def _():
            def idx_body(b, c):
                cand = c | lax.shift_left(jnp.int32(1), seq_bits - 1 - b)
                g = count(lambda s, kp: (s == thr) & (kp < cand))
                return jnp.where(g < need, cand, c)
            c = lax.fori_loop(0, seq_bits, idx_body, jnp.zeros((1, tq), jnp.int32))
            jl_ref[...] = jnp.broadcast_to(c, jl_ref.shape)

        def bias_body(kt, carry):
            s = sc_ref[kt]
            kpos = kt * KEY_TILE + krow
            sel = ((s > thr) | ((s == thr) & (kpos <= jl_ref[0:1, :]))) & (kpos <= qpos)
            sc_ref[kt] = jnp.where(sel, 0.0, NEG_BIG)
            return carry
        lax.fori_loop(0, nk, bias_body, 0)

    m_ref[...] = jnp.full(m_ref.shape, NEG_BIG, F32)
    l_ref[...] = jnp.zeros(l_ref.shape, F32)
    acc_ref[...] = jnp.zeros(acc_ref.shape, F32)

    c2 = scale * LOG2_E

    def attn_body(kt, carry):
        off = pl.multiple_of(kt * KEY_TILE, KEY_TILE)
        for h in range(DSA_HEADS):
            hs = slice(h * HEAD_DIM, (h + 1) * HEAD_DIM)
            m, l, acc = m_ref[h:h + 1, :], l_ref[h:h + 1, :], acc_ref[h]
            for u in range(KEY_TILE // SUB_TILE):
                us = slice(u * SUB_TILE, (u + 1) * SUB_TILE)
                st = jnp.dot(k_ref[pl.ds(off + u * SUB_TILE, SUB_TILE), hs], qt_ref[h],
                             preferred_element_type=F32) * c2 + sc_ref[kt, us, :]
                m, l, acc = _flash_step(st, m, l, acc, vt_ref[kt, hs, us])
            m_ref[h:h + 1, :] = m
            l_ref[h:h + 1, :] = l
            acc_ref[h] = acc
        return carry

    lax.fori_loop(0, nk, attn_body, 0)
    _flash_finish(o_ref, l_ref, acc_ref, DSA_HEADS)


def _dsa_attn(qi3, klo3, khi3, wit, qk3, vt3, topk):
    b, t, _ = qk3.shape
    tq = KEY_TILE
    nq = t // tq
    seq_bits = max(1, (t - 1).bit_length())
    one = pl.Buffered(1)
    kern = functools.partial(
        _dsa_kernel, topk=topk, scale=HEAD_DIM ** -0.5,
        idx_scale=(IDX_DIM ** -0.5) * (IDX_HEADS ** -0.5), tq=tq, seq_bits=seq_bits)
    return pl.pallas_call(
        kern,
        grid=(b, nq),
        in_specs=[pl.BlockSpec((None, tq, IDX_W), lambda bi, i: (bi, i, 0)),
                  pl.BlockSpec((None, t, LANES), lambda bi, i: (bi, 0, 0), pipeline_mode=one),
                  pl.BlockSpec((None, t, LANES), lambda bi, i: (bi, 0, 0), pipeline_mode=one),
                  pl.BlockSpec((None, IDX_HEADS, tq), lambda bi, i: (bi * nq + i, IDX_DIM // IDX_HEADS, 0)),
                  pl.BlockSpec((None, tq, DSA_W), lambda bi, i: (bi, i, 2)),
                  pl.BlockSpec((None, t, DSA_W), lambda bi, i: (bi, 0, 3), pipeline_mode=one),
                  pl.BlockSpec((nq, DSA_W, KEY_TILE), lambda bi, i: (bi, 1, 0), pipeline_mode=one)],
        out_specs=pl.BlockSpec((None, tq, DSA_W), lambda bi, i: (bi, i, 0)),
        out_shape=jax.ShapeDtypeStruct((b, t, DSA_W), BF16),
        scratch_shapes=[pltpu.VMEM((nq, KEY_TILE, tq), F32),
                        pltpu.VMEM((8, tq), jnp.int32),
                        pltpu.VMEM((DSA_HEADS, tq), F32),
                        pltpu.VMEM((DSA_HEADS, tq), F32),
                        pltpu.VMEM((DSA_HEADS, HEAD_DIM, tq), F32),
                        pltpu.VMEM((DSA_HEADS, HEAD_DIM, tq), BF16),
                        pltpu.VMEM((IDX_HEADS // 2, LANES, tq), BF16)],
        compiler_params=_params(("parallel", "arbitrary")),
        name="dsa_attn",
    )(qi3, klo3, khi3, wit, qk3, qk3, vt3)


def _mix_out_kernel(oa_ref, ob_ref, sg_ref, wa_ref, wb_ref, wo_ref, x_ref, gt_ref, gpost_ref,
                    gpre_ref, sc_ref, sh_ref, x1_ref, h2_ref, *, d):
    ya = jnp.dot(oa_ref[...], wa_ref[...], preferred_element_type=F32)
    yb = jnp.dot(ob_ref[...], wb_ref[...], preferred_element_type=F32)
    z = sg_ref[:, :d].astype(F32) * ya + sg_ref[:, d:].astype(F32) * yb
    y = jnp.dot(z.astype(BF16), wo_ref[...], preferred_element_type=F32)
    x1 = x_ref[...] + gt_ref[...] * (_rms(y) * gpost_ref[...])
    x1_ref[...] = x1
    h2_ref[...] = ((_rms(x1) * gpre_ref[...]) * (1.0 + sc_ref[...]) + sh_ref[...]).astype(h2_ref.dtype)


def _mix_out(oa3, ob3, sg3, wa, wb, wo, x, gt1, g_post, g_pre2, sc2, sh2):
    b, t, d = x.shape
    tm = _tile(t, 256)
    one = pl.Buffered(1)
    row = lambda w: pl.BlockSpec((None, tm, w), lambda bi, i: (bi, i, 0))
    full = lambda a: pl.BlockSpec(a.shape, lambda bi, i: (0, 0), pipeline_mode=one)
    vec = pl.BlockSpec((None, 1, d), lambda bi, i: (bi, 0, 0))
    gain = pl.BlockSpec((1, d), lambda bi, i: (0, 0))
    return pl.pallas_call(
        functools.partial(_mix_out_kernel, d=d),
        grid=(b, t // tm),
        in_specs=[row(MOBA_W), row(DSA_W), row(2 * d), full(wa), full(wb), full(wo), row(d),
                  vec, gain, gain, vec, vec],
        out_specs=[row(d), row(d)],
        out_shape=[jax.ShapeDtypeStruct((b, t, d), F32), jax.ShapeDtypeStruct((b, t, d), BF16)],
        compiler_params=_params(("parallel", "parallel")),
        name="mix_out",
    )(oa3, ob3, sg3, wa, wb, wo, x, gt1, g_post, g_pre2, sc2, sh2)


def _ffn_kernel(h_ref, w1_ref, w2_ref, x1_ref, gt_ref, g_ref, o_ref, acc_ref):
    j = pl.program_id(2)

    @pl.when(j == 0)
    def _():
        acc_ref[...] = jnp.zeros(acc_ref.shape, F32)

    u = jnp.dot(h_ref[...], w1_ref[...], preferred_element_type=F32)
    u = jnp.square(jnp.maximum(u, 0.0)).astype(BF16)
    acc_ref[...] += jnp.dot(u, w2_ref[...], preferred_element_type=F32)

    @pl.when(j == pl.num_programs(2) - 1)
    def _():
        o_ref[...] = x1_ref[...] + gt_ref[...] * (_rms(acc_ref[...]) * g_ref[...])


def _ffn(h2, w1, w2, x1, gt2, g_post):
    b, t, d = x1.shape
    ff = w1.shape[1]
    tm = _tile(t, 512)
    tf = _tile(ff, 1024)
    row = lambda: pl.BlockSpec((None, tm, d), lambda bi, i, j: (bi, i, 0))
    return pl.pallas_call(
        _ffn_kernel,
        grid=(b, t // tm, ff // tf),
        in_specs=[row(),
                  pl.BlockSpec((d, tf), lambda bi, i, j: (0, j)),
                  pl.BlockSpec((tf, d), lambda bi, i, j: (j, 0)),
                  row(),
                  pl.BlockSpec((None, 1, d), lambda bi, i, j: (bi, 0, 0)),
                  pl.BlockSpec((1, d), lambda bi, i, j: (0, 0))],
        out_specs=row(),
        out_shape=jax.ShapeDtypeStruct((b, t, d), F32),
        scratch_shapes=[pltpu.VMEM((tm, d), F32)],
        compiler_params=_params(("parallel", "parallel", "arbitrary")),
        name="ffn",
    )(h2, w1, w2, x1, gt2, g_post)


def _rope_tables(t):
    pos = jnp.arange(t, dtype=F32)[:, None]
    lane = jnp.arange(LANES)[None, :]

    def cos_sin(half):
        inv_freq = jnp.power(ROPE_THETA, -jnp.arange(half, dtype=F32) / half)
        ang = pos * inv_freq[None, :]
        reps = LANES // half
        return jnp.tile(jnp.cos(ang), (1, reps)), jnp.tile(jnp.sin(ang), (1, reps))

    cos128, sin128 = cos_sin(HEAD_DIM // 2)
    sin128 = jnp.where(lane < HEAD_DIM // 2, -sin128, sin128)
    cos64, sin64 = cos_sin(IDX_DIM // 2)
    low = (lane % IDX_DIM) < IDX_DIM // 2
    sa64 = jnp.where(low, -sin64, 0.0)
    sb64 = jnp.where(low, 0.0, sin64)
    is_key = lane < IDX_DIM
    kw = (jnp.where(is_key, cos64, 1.0), jnp.where(is_key, sa64, 0.0), jnp.where(is_key, sb64, 0.0))
    return (cos128, sin128), (cos64, sa64, sb64), kw


def kernel(x, c, w_ada, b_ada, g_pre_mix, g_post_mix, w_in, w_moba_out, w_dsa_out, w_o,
           g_pre_ffn, g_post_ffn, w_ff1, w_ff2):
    b, t, d = x.shape
    m = b * t
    nb = t // MOBA_BLOCK
    assert t % MOBA_BLOCK == 0 and nb <= MAX_BLOCKS and d % LANES == 0 and b <= 16
    n_sel = max(1, min(MOBA_TOPK, nb - 1))
    topk = min(DSA_TOPK_MAX, t // 4)
    rope128, rope64, rope_kw = _rope_tables(t)
    c_pad = jnp.zeros((16, d), F32).at[:b].set(c)

    for l in range(w_ada.shape[0]):
        mod = _ada(c_pad, w_ada[l], b_ada[l][None, :])[:b]
        sh1, sc1, gt1, sh2, sc2, gt2 = [v[:, None, :] for v in jnp.split(mod, 6, axis=-1)]

        h = _norm_mod(x, g_pre_mix[l][None, :], sc1, sh1).reshape(m, d)
        qk, vt3, qi, klo, khi, wit, sg = _in_proj(h, w_in, l, t, rope128, rope64, rope_kw)

        qk3 = qk.reshape(b, t, -1)
        kmt = _moba_kmean(qk3, nb)
        selb = _moba_select(qk3, kmt, nb, n_sel)
        oa = _moba_attn(qk3, vt3, selb, nb)
        ob = _dsa_attn(qi.reshape(b, t, -1), klo.reshape(b, t, -1), khi.reshape(b, t, -1),
                       wit, qk3, vt3, topk)

        x, h2 = _mix_out(oa, ob, sg.reshape(b, t, -1), w_moba_out[l].astype(BF16),
                         w_dsa_out[l].astype(BF16), w_o[l].astype(BF16), x, gt1,
                         g_post_mix[l][None, :], g_pre_ffn[l][None, :], sc2, sh2)
        x = _ffn(h2, w_ff1[l].astype(BF16), w_ff2[l].astype(BF16), x, gt2, g_post_ffn[l][None, :])
    return x
```

```python
import functools

import jax
import jax.numpy as jnp
from jax import lax
from jax.experimental import pallas as pl
from jax.experimental.pallas import tpu as pltpu

HEAD_DIM = 128
MOBA_HEADS = 8
MOBA_BLOCK = 256
MOBA_TOPK = 3
DSA_HEADS = 8
IDX_HEADS = 16
IDX_DIM = 64
DSA_TOPK_MAX = 256
ROPE_THETA = 10000.0
RMS_EPS = 1e-6

MOBA_W = MOBA_HEADS * HEAD_DIM
DSA_W = DSA_HEADS * HEAD_DIM
IDX_W = IDX_HEADS * IDX_DIM
LANES = 128
MAX_BLOCKS = LANES // MOBA_HEADS
KEY_TILE = 256
SUB_TILE = 128
ROW_CHUNK = 512
LOG2_E = 1.4426950408889634
MASK_BIAS = -30000.0
NEG_BIG = -1e30
INT_MIN = -(2 ** 31)
KEY_OF_NEG_INF = -2139095041
VMEM_LIMIT = 56 * 1024 * 1024

F32 = jnp.float32
BF16 = jnp.bfloat16
_NT = (((1,), (1,)), ((), ()))


def _params(sem):
    return pltpu.CompilerParams(dimension_semantics=sem, vmem_limit_bytes=VMEM_LIMIT)


def _tile(n, pref):
    if n <= pref:
        return n
    t = pref - pref % LANES
    while t >= LANES:
        if n % t == 0:
            return t
        t -= LANES
    return n


def _rms(x):
    return x * lax.rsqrt(jnp.mean(x * x, axis=-1, keepdims=True) + RMS_EPS)


def _ada_kernel(c_ref, w_ref, b_ref, o_ref):
    c = c_ref[...]
    cs = (c * jax.nn.sigmoid(c)).astype(BF16)
    o_ref[...] = jnp.dot(cs, w_ref[...].astype(BF16), preferred_element_type=F32) + b_ref[...]


def _ada(c_pad, w, b):
    rows, d = c_pad.shape
    n = w.shape[1]
    tn = _tile(n, 1024)
    return pl.pallas_call(
        _ada_kernel,
        grid=(n // tn,),
        in_specs=[pl.BlockSpec((rows, d), lambda j: (0, 0)),
                  pl.BlockSpec((d, tn), lambda j: (0, j)),
                  pl.BlockSpec((1, tn), lambda j: (0, j))],
        out_specs=pl.BlockSpec((rows, tn), lambda j: (0, j)),
        out_shape=jax.ShapeDtypeStruct((rows, n), F32),
        compiler_params=_params(("parallel",)),
        name="ada_mod",
    )(c_pad, w, b)


def _norm_mod_kernel(x_ref, g_ref, sc_ref, sh_ref, o_ref):
    y = _rms(x_ref[...])
    o_ref[...] = ((y * g_ref[...]) * (1.0 + sc_ref[...]) + sh_ref[...]).astype(o_ref.dtype)


def _norm_mod(x, g, sc, sh):
    b, t, d = x.shape
    tt = _tile(t, 512)
    vec = pl.BlockSpec((None, 1, d), lambda bi, ti: (bi, 0, 0))
    return pl.pallas_call(
        _norm_mod_kernel,
        grid=(b, t // tt),
        in_specs=[pl.BlockSpec((None, tt, d), lambda bi, ti: (bi, ti, 0)),
                  pl.BlockSpec((1, d), lambda bi, ti: (0, 0)), vec, vec],
        out_specs=pl.BlockSpec((None, tt, d), lambda bi, ti: (bi, ti, 0)),
        out_shape=jax.ShapeDtypeStruct((b, t, d), BF16),
        compiler_params=_params(("parallel", "parallel")),
        name="norm_mod",
    )(x, g, sc, sh)


def _first_token_tile():
    return pl.program_id(1) == 0


def _rope64(y, cos, sin_a, sin_b):
    return y * cos + pltpu.roll(y, LANES - 32, 1) * sin_a + pltpu.roll(y, 32, 1) * sin_b


def _row_chunks(ref, step=ROW_CHUNK):
    n = ref.shape[0]
    step = min(n, step)
    return [slice(r, r + step) for r in range(0, n, step)]


def _wp_rope128_kernel(h_ref, w_ref, cos_ref, sin_ref, o_ref, wb_ref):
    @pl.when(_first_token_tile())
    def _():
        wb_ref[...] = w_ref[...].astype(BF16)

    for rows in _row_chunks(h_ref):
        acc = lax.dot_general(h_ref[rows, :], wb_ref[...], _NT, preferred_element_type=F32)
        cos = cos_ref[rows, :]
        sin = sin_ref[rows, :]
        for g in range(acc.shape[1] // LANES):
            y = acc[:, g * LANES:(g + 1) * LANES]
            o_ref[rows, g * LANES:(g + 1) * LANES] = (
                y * cos + pltpu.roll(y, LANES // 2, 1) * sin).astype(o_ref.dtype)


def _wp_rope64_kernel(h_ref, w_ref, cos_ref, sa_ref, sb_ref, o_ref, wb_ref):
    @pl.when(_first_token_tile())
    def _():
        wb_ref[...] = w_ref[...].astype(BF16)

    for rows in _row_chunks(h_ref):
        acc = lax.dot_general(h_ref[rows, :], wb_ref[...], _NT, preferred_element_type=F32)
        cos, sa, sb = cos_ref[rows, :], sa_ref[rows, :], sb_ref[rows, :]
        for g in range(acc.shape[1] // LANES):
            y = acc[:, g * LANES:(g + 1) * LANES]
            o_ref[rows, g * LANES:(g + 1) * LANES] = _rope64(y, cos, sa, sb).astype(o_ref.dtype)


def _wp_vt_kernel(h_ref, w_ref, o_ref, wb_ref):
    @pl.when(_first_token_tile())
    def _():
        wb_ref[...] = w_ref[...].astype(BF16)

    for c in range(o_ref.shape[0]):
        r = lax.dot_general(wb_ref[...], h_ref[c * KEY_TILE:(c + 1) * KEY_TILE, :], _NT,
                            preferred_element_type=F32)
        o_ref[c] = r.astype(o_ref.dtype)


def _wp_gate_kernel(h_ref, wa_ref, wb_ref, o_ref, wg_ref, *, shift):
    @pl.when(_first_token_tile())
    def _():
        both = jnp.concatenate([wa_ref[...], wb_ref[...]], axis=0)
        wg_ref[...] = both[shift:shift + wg_ref.shape[0], :].astype(BF16)

    for rows in _row_chunks(h_ref):
        acc = lax.dot_general(h_ref[rows, :], wg_ref[...], _NT, preferred_element_type=F32)
        o_ref[rows, :] = jax.nn.sigmoid(acc).astype(o_ref.dtype)


def _wp_kw_kernel(h_ref, w_ref, cos_ref, sa_ref, sb_ref, klo_ref, khi_ref, wit_ref, wb_ref):
    @pl.when(pl.program_id(0) == 0)
    def _():
        r = lax.broadcasted_iota(jnp.int32, w_ref.shape, 0)
        wb_ref[...] = jnp.where(r < IDX_DIM + IDX_HEADS, w_ref[...], 0.0).astype(BF16)

    h = h_ref[...]
    y = lax.dot_general(h, wb_ref[...], _NT, preferred_element_type=F32)
    r = _rope64(y, cos_ref[...], sa_ref[...], sb_ref[...])
    lane = lax.broadcasted_iota(jnp.int32, y.shape, 1)
    klo = jnp.where(lane < IDX_DIM, r, 0.0)
    klo_ref[...] = klo.astype(BF16)
    khi_ref[...] = pltpu.roll(klo, IDX_DIM, 1).astype(BF16)
    yt = lax.dot_general(wb_ref[...], h, _NT, preferred_element_type=F32)
    for c in range(wit_ref.shape[0]):
        wit_ref[c] = yt[:, c * KEY_TILE:(c + 1) * KEY_TILE]


def _in_proj(h, wt, l, t, tables128, tables64, tables_kw):
    m, k = h.shape
    d = k
    tm = _tile(t, 1024)
    nt = t // tm
    ni = m // tm
    assert tm % KEY_TILE == 0
    tn = 512
    o_dsa = 3 * MOBA_W
    o_qi = o_dsa + 3 * DSA_W
    o_kw = o_qi + IDX_W
    o_g = o_kw + IDX_DIM + IDX_HEADS
    sem = ("arbitrary", "arbitrary")
    h_spec = pl.BlockSpec((tm, k), lambda j, i: (i, 0))
    tab = pl.BlockSpec((tm, LANES), lambda j, i: (i % nt, 0))
    row = pl.BlockSpec((tm, tn), lambda j, i: (i, j))

    def wrows(first_blocks, n_first, second_start):
        return pl.BlockSpec((None, tn, k), lambda j, i: (l, jnp.where(j < n_first, first_blocks + j,
                                                                      second_start + j - n_first), 0))

    n_m, n_d = 2 * MOBA_W // tn, 2 * DSA_W // tn
    qk = pl.pallas_call(
        _wp_rope128_kernel,
        grid=(n_m + n_d, ni),
        in_specs=[h_spec, wrows(0, n_m, o_dsa // tn), tab, tab],
        out_specs=row,
        out_shape=jax.ShapeDtypeStruct((m, 2 * MOBA_W + 2 * DSA_W), BF16),
        scratch_shapes=[pltpu.VMEM((tn, k), BF16)],
        compiler_params=_params(sem),
        name="proj_qk",
    )(h, wt, *tables128)

    n_m, n_d = MOBA_W // tn, DSA_W // tn
    vt = pl.pallas_call(
        _wp_vt_kernel,
        grid=(n_m + n_d, ni),
        in_specs=[h_spec, wrows(2 * MOBA_W // tn, n_m, (o_dsa + 2 * DSA_W) // tn)],
        out_specs=pl.BlockSpec((tm // KEY_TILE, tn, KEY_TILE), lambda j, i: (i, j, 0)),
        out_shape=jax.ShapeDtypeStruct((m // KEY_TILE, MOBA_W + DSA_W, KEY_TILE), BF16),
        scratch_shapes=[pltpu.VMEM((tn, k), BF16)],
        compiler_params=_params(sem),
        name="proj_v",
    )(h, wt)

    n_q = IDX_W // tn
    qi = pl.pallas_call(
        _wp_rope64_kernel,
        grid=(n_q, ni),
        in_specs=[h_spec, wrows(o_qi // tn, n_q, 0), tab, tab, tab],
        out_specs=row,
        out_shape=jax.ShapeDtypeStruct((m, IDX_W), BF16),
        scratch_shapes=[pltpu.VMEM((tn, k), BF16)],
        compiler_params=_params(sem),
        name="proj_qi",
    )(h, wt, *tables64)

    tab1 = pl.BlockSpec((tm, LANES), lambda i: (i % nt, 0))
    row1 = pl.BlockSpec((tm, LANES), lambda i: (i, 0))
    klo, khi, wit = pl.pallas_call(
        _wp_kw_kernel,
        grid=(ni,),
        in_specs=[pl.BlockSpec((tm, k), lambda i: (i, 0)),
                  pl.BlockSpec((None, LANES, k), lambda i: (l, o_kw // LANES, 0)), tab1, tab1, tab1],
        out_specs=[row1, row1, pl.BlockSpec((tm // KEY_TILE, LANES, KEY_TILE), lambda i: (i, 0, 0))],
        out_shape=[jax.ShapeDtypeStruct((m, LANES), BF16), jax.ShapeDtypeStruct((m, LANES), BF16),
                   jax.ShapeDtypeStruct((m // KEY_TILE, LANES, KEY_TILE), F32)],
        scratch_shapes=[pltpu.VMEM((LANES, k), BF16)],
        compiler_params=_params(("arbitrary",)),
        name="proj_kw",
    )(h, wt, *tables_kw)

    tg = _tile(2 * d, tn)
    g0 = o_kw // tg
    assert o_kw % tg == 0 and o_g - o_kw < tg
    sg = pl.pallas_call(
        functools.partial(_wp_gate_kernel, shift=o_g - o_kw),
        grid=(2 * d // tg, ni),
        in_specs=[h_spec,
                  pl.BlockSpec((None, tg, k), lambda j, i: (l, g0 + j, 0)),
                  pl.BlockSpec((None, tg, k), lambda j, i: (l, g0 + j + 1, 0))],
        out_specs=pl.BlockSpec((tm, tg), lambda j, i: (i, j)),
        out_shape=jax.ShapeDtypeStruct((m, 2 * d), BF16),
        scratch_shapes=[pltpu.VMEM((tg, k), BF16)],
        compiler_params=_params(sem),
        name="proj_gate",
    )(h, wt, wt)
    return qk, vt, qi, klo, khi, wit, sg


def _moba_kmean_kernel(k_ref, o_ref, km_ref, *, nb):
    km_ref[...] = jnp.zeros(km_ref.shape, F32)
    for j in range(nb):
        blk = k_ref[j * MOBA_BLOCK:(j + 1) * MOBA_BLOCK, :].astype(F32)
        km_ref[j:j + 1, :] = jnp.sum(blk, axis=0, keepdims=True) * (1.0 / MOBA_BLOCK)
    km = km_ref[...]
    tiled = jnp.concatenate([km] * MOBA_HEADS, axis=0)
    r = lax.shift_right_logical(lax.broadcasted_iota(jnp.int32, tiled.shape, 0), 4)
    c = lax.shift_right_logical(lax.broadcasted_iota(jnp.int32, tiled.shape, 1), 7)
    o_ref[...] = jnp.where(r == c, tiled, 0.0).astype(o_ref.dtype)


def _moba_kmean(qk3, nb):
    b, t, _ = qk3.shape
    return pl.pallas_call(
        functools.partial(_moba_kmean_kernel, nb=nb),
        grid=(b,),
        in_specs=[pl.BlockSpec((None, t, MOBA_W), lambda bi: (bi, 0, 1))],
        out_specs=pl.BlockSpec((None, LANES, MOBA_W), lambda bi: (bi, 0, 0)),
        out_shape=jax.ShapeDtypeStruct((b, LANES, MOBA_W), BF16),
        scratch_shapes=[pltpu.VMEM((MAX_BLOCKS, MOBA_W), F32)],
        compiler_params=_params(("parallel",)),
        name="moba_kmean",
    )(qk3)


def _moba_select_kernel(q_ref, kmt_ref, o_ref, *, n_sel):
    i = pl.program_id(1)
    g = lax.dot_general(q_ref[...], kmt_ref[...], _NT, preferred_element_type=F32)
    lane = lax.broadcasted_iota(jnp.int32, g.shape, 1)
    j = lane & (MAX_BLOCKS - 1)
    past = j < i
    gm = jnp.where(past, g, -jnp.inf)
    rank = jnp.zeros(g.shape, F32)
    for s in range(1, MAX_BLOCKS):
        wrap = (j + s) >= MAX_BLOCKS
        other = jnp.where(wrap, pltpu.roll(gm, MAX_BLOCKS - s, 1), pltpu.roll(gm, LANES - s, 1))
        beats = (other > gm) | ((other == gm) & wrap)
        rank = rank + jnp.where(beats, 1.0, 0.0)
    sel = past & (rank < n_sel)
    bias = jnp.where(sel, 0.0, MASK_BIAS)
    for h in range(MOBA_HEADS):
        shift = (LANES - MAX_BLOCKS * h) % LANES
        bh = bias if shift == 0 else pltpu.roll(bias, shift, 1)
        o_ref[h] = jnp.where(lane < MAX_BLOCKS, bh, 0.0).astype(o_ref.dtype)


def _moba_select(qk3, kmt, nb, n_sel):
    b, t, _ = qk3.shape
    return pl.pallas_call(
        functools.partial(_moba_select_kernel, n_sel=n_sel),
        grid=(b, nb),
        in_specs=[pl.BlockSpec((None, MOBA_BLOCK, MOBA_W), lambda bi, i: (bi, i, 0)),
                  pl.BlockSpec((None, LANES, MOBA_W), lambda bi, i: (bi, 0, 0))],
        out_specs=pl.BlockSpec((None, MOBA_HEADS, MOBA_BLOCK, LANES), lambda bi, i: (bi, 0, i, 0)),
        out_shape=jax.ShapeDtypeStruct((b, MOBA_HEADS, t, LANES), BF16),
        compiler_params=_params(("parallel", "parallel")),
        name="moba_select",
    )(qk3, kmt)


def _flash_step(st, m, l, acc_t, v_t):
    m_new = jnp.maximum(m, jnp.max(st, axis=0, keepdims=True))
    alpha = jnp.exp2(m - m_new)
    p = jnp.exp2(st - m_new)
    l = alpha * l + jnp.sum(p, axis=0, keepdims=True)
    acc_t = alpha * acc_t + jnp.dot(v_t, p.astype(BF16), preferred_element_type=F32)
    return m_new, l, acc_t


def _flash_step_joint(sts, m, l, acc_t, v_ts):
    m_new = m
    for st in sts:
        m_new = jnp.maximum(m_new, jnp.max(st, axis=0, keepdims=True))
    alpha = jnp.exp2(m - m_new)
    l = alpha * l
    acc_t = alpha * acc_t
    for st, v_t in zip(sts, v_ts):
        p = jnp.exp2(st - m_new)
        l = l + jnp.sum(p, axis=0, keepdims=True)
        acc_t = acc_t + jnp.dot(v_t, p.astype(BF16), preferred_element_type=F32)
    return m_new, l, acc_t


def _transpose_bf16(x):
    return x.astype(F32).T.astype(BF16)


def _flash_finish(o_ref, l_ref, acc_ref, heads):
    for h in range(heads):
        hs = slice(h * HEAD_DIM, (h + 1) * HEAD_DIM)
        o_ref[:, hs] = (acc_ref[h] / l_ref[h:h + 1, :]).T.astype(o_ref.dtype)


def _moba_attn_kernel(q_ref, k_ref, vt_ref, sb_ref, o_ref, m_ref, l_ref, acc_ref, qa_ref, *, scale):
    i = pl.program_id(1)
    own = pl.multiple_of(i * MOBA_BLOCK, MOBA_BLOCK)
    c2 = scale * LOG2_E
    sub_shape = (SUB_TILE, MOBA_BLOCK)
    kidx = lax.broadcasted_iota(jnp.int32, sub_shape, 0)
    qidx = lax.broadcasted_iota(jnp.int32, sub_shape, 1)
    for h in range(MOBA_HEADS):
        hs = slice(h * HEAD_DIM, (h + 1) * HEAD_DIM)
        qa_ref[h, :HEAD_DIM, :] = _transpose_bf16(q_ref[:, hs])
        qa_ref[h, HEAD_DIM:, :] = _transpose_bf16(sb_ref[h])
        m = jnp.full((1, MOBA_BLOCK), NEG_BIG, F32)
        l = jnp.zeros((1, MOBA_BLOCK), F32)
        acc = jnp.zeros((HEAD_DIM, MOBA_BLOCK), F32)
        for u in range(MOBA_BLOCK // SUB_TILE):
            st = jnp.dot(k_ref[pl.ds(own + u * SUB_TILE, SUB_TILE), hs], qa_ref[h, :HEAD_DIM, :],
                         preferred_element_type=F32) * c2
            st = jnp.where(kidx + u * SUB_TILE <= qidx, st, -jnp.inf)
            m, l, acc = _flash_step(st, m, l, acc, vt_ref[i, hs, u * SUB_TILE:(u + 1) * SUB_TILE])
        m_ref[h:h + 1, :] = m
        l_ref[h:h + 1, :] = l
        acc_ref[h] = acc

    lane = lax.broadcasted_iota(jnp.int32, (SUB_TILE, LANES), 1)

    def body(j, carry):
        off = pl.multiple_of(j * MOBA_BLOCK, MOBA_BLOCK)
        onehot = jnp.where(lane == j, 1.0, 0.0).astype(BF16)
        for h in range(MOBA_HEADS):
            hs = slice(h * HEAD_DIM, (h + 1) * HEAD_DIM)
            m, l, acc = m_ref[h:h + 1, :], l_ref[h:h + 1, :], acc_ref[h]
            sts, vts = [], []
            for u in range(MOBA_BLOCK // SUB_TILE):
                k_aug = jnp.concatenate([k_ref[pl.ds(off + u * SUB_TILE, SUB_TILE), hs], onehot], axis=1)
                sts.append(jnp.dot(k_aug, qa_ref[h], preferred_element_type=F32) * c2)
                vts.append(vt_ref[j, hs, u * SUB_TILE:(u + 1) * SUB_TILE])
            m, l, acc = _flash_step_joint(sts, m, l, acc, vts)
            m_ref[h:h + 1, :] = m
            l_ref[h:h + 1, :] = l
            acc_ref[h] = acc
        return carry

    lax.fori_loop(0, i, body, 0)
    _flash_finish(o_ref, l_ref, acc_ref, MOBA_HEADS)


def _moba_attn(qk3, vt3, selb, nb):
    b, t, _ = qk3.shape
    one = pl.Buffered(1)
    return pl.pallas_call(
        functools.partial(_moba_attn_kernel, scale=HEAD_DIM ** -0.5),
        grid=(b, nb),
        in_specs=[pl.BlockSpec((None, MOBA_BLOCK, MOBA_W), lambda bi, i: (bi, i, 0)),
                  pl.BlockSpec((None, t, MOBA_W), lambda bi, i: (bi, 0, 1), pipeline_mode=one),
                  pl.BlockSpec((nb, MOBA_W, KEY_TILE), lambda bi, i: (bi, 0, 0), pipeline_mode=one),
                  pl.BlockSpec((None, MOBA_HEADS, MOBA_BLOCK, LANES), lambda bi, i: (bi, 0, i, 0))],
        out_specs=pl.BlockSpec((None, MOBA_BLOCK, MOBA_W), lambda bi, i: (bi, i, 0)),
        out_shape=jax.ShapeDtypeStruct((b, t, MOBA_W), BF16),
        scratch_shapes=[pltpu.VMEM((MOBA_HEADS, MOBA_BLOCK), F32),
                        pltpu.VMEM((MOBA_HEADS, MOBA_BLOCK), F32),
                        pltpu.VMEM((MOBA_HEADS, HEAD_DIM, MOBA_BLOCK), F32),
                        pltpu.VMEM((MOBA_HEADS, HEAD_DIM + LANES, MOBA_BLOCK), BF16)],
        compiler_params=_params(("parallel", "arbitrary")),
        name="moba_attn",
    )(qk3, qk3, vt3, selb)


def _key_to_float(t):
    return lax.bitcast_convert_type(jnp.where(t >= 0, t, t ^ 0x7FFFFFFF), F32)


def _dsa_kernel(qi_ref, klo_ref, khi_ref, wt_ref, q_ref, k_ref, vt_ref, o_ref,
                sc_ref, jl_ref, m_ref, l_ref, acc_ref, qt_ref, qit_ref, *, topk, scale, idx_scale, tq, seq_bits):
    i = pl.program_id(1)
    nk = i + 1
    shape = (KEY_TILE, tq)
    krow = lax.broadcasted_iota(jnp.int32, shape, 0)
    qpos = i * tq + lax.broadcasted_iota(jnp.int32, shape, 1)

    wt = wt_ref[...]
    for p in range(IDX_HEADS // 2):
        qit_ref[p] = _transpose_bf16(qi_ref[:, p * LANES:(p + 1) * LANES])
    for h in range(DSA_HEADS):
        qt_ref[h] = _transpose_bf16(q_ref[:, h * HEAD_DIM:(h + 1) * HEAD_DIM])

    def score_body(kt, carry):
        off = pl.multiple_of(kt * KEY_TILE, KEY_TILE)
        keys = jnp.concatenate([klo_ref[pl.ds(off, KEY_TILE), :], khi_ref[pl.ds(off, KEY_TILE), :]], axis=0)
        acc = jnp.zeros(shape, F32)
        for p in range(IDX_HEADS // 2):
            s = jnp.dot(keys, qit_ref[p], preferred_element_type=F32)
            acc = acc + (jnp.maximum(s[:KEY_TILE], 0.0) * wt[2 * p:2 * p + 1, :]
                         + jnp.maximum(s[KEY_TILE:], 0.0) * wt[2 * p + 1:2 * p + 2, :])
        kpos = kt * KEY_TILE + krow
        sc_ref[kt] = jnp.where(kpos <= qpos, acc * idx_scale, -jnp.inf)
        return carry

    lax.fori_loop(0, nk, score_body, 0)

    def count(pred):
        def body(kt, acc):
            hit = jnp.where(pred(sc_ref[kt], kt * KEY_TILE + krow), 1.0, 0.0)
            return acc + jnp.sum(hit.reshape(KEY_TILE // 32, 32, tq), axis=0)
        part = lax.fori_loop(0, nk, body, jnp.zeros((32, tq), F32))
        return jnp.sum(part, axis=0, keepdims=True)

    need_select = nk * tq > topk

    @pl.when(jnp.logical_not(need_select))
    def _():
        def body(kt, carry):
            sc_ref[kt] = jnp.where(kt * KEY_TILE + krow <= qpos, 0.0, NEG_BIG)
            return carry
        lax.fori_loop(0, nk, body, 0)

    @pl.when(need_select)
    def _():
        kf = jnp.float32(topk)
        zero = jnp.zeros((1, tq), F32)
        c0 = count(lambda s, kp: s >= zero)
        t0 = jnp.where(c0 >= kf, 0, INT_MIN).astype(jnp.int32)

        def bit_body(b, t):
            cand = t | lax.shift_left(jnp.int32(1), 30 - b)
            cf = _key_to_float(cand)
            c = count(lambda s, kp: s >= cf)
            return jnp.where(c >= kf, cand, t)

        t = lax.fori_loop(0, 31, bit_body, t0)
        thr = _key_to_float(jnp.maximum(t, KEY_OF_NEG_INF))

        need = kf - count(lambda s, kp: s > thr)
        n_eq = count(lambda s, kp: s == thr)
        jl_ref[...] = jnp.full(jl_ref.shape, 2 ** seq_bits, jnp.int32)

        @pl.when(jnp.max(n_eq - need) > 0.0)
        def _():
            def idx_body(b, c):
                cand = c | lax.shift_left(jnp.int32(1), seq_bits - 1 - b)
                g = count(lambda s, kp: (s == thr) & (kp < cand))
                return jnp.where(g < need, cand, c)
            c = lax.fori_loop(0, seq_bits, idx_body, jnp.zeros((1, tq), jnp.int32))
            jl_ref[...] = jnp.broadcast_to(c, jl_ref.shape)

        def bias_body(kt, carry):
            s = sc_ref[kt]
            kpos = kt * KEY_TILE + krow
            sel = ((s > thr) | ((s == thr) & (kpos <= jl_ref[0:1, :]))) & (kpos <= qpos)
            sc_ref[kt] = jnp.where(sel, 0.0, NEG_BIG)
            return carry
        lax.fori_loop(0, nk, bias_body, 0)

    m_ref[...] = jnp.full(m_ref.shape, NEG_BIG, F32)
    l_ref[...] = jnp.zeros(l_ref.shape, F32)
    acc_ref[...] = jnp.zeros(acc_ref.shape, F32)

    c2 = scale * LOG2_E

    def attn_body(kt, carry):
        off = pl.multiple_of(kt * KEY_TILE, KEY_TILE)
        for h in range(DSA_HEADS):
            hs = slice(h * HEAD_DIM, (h + 1) * HEAD_DIM)
            m, l, acc = m_ref[h:h + 1, :], l_ref[h:h + 1, :], acc_ref[h]
            for u in range(KEY_TILE // SUB_TILE):
                us = slice(u * SUB_TILE, (u + 1) * SUB_TILE)
                st = jnp.dot(k_ref[pl.ds(off + u * SUB_TILE, SUB_TILE), hs], qt_ref[h],
                             preferred_element_type=F32) * c2 + sc_ref[kt, us, :]
                m, l, acc = _flash_step(st, m, l, acc, vt_ref[kt, hs, us])
            m_ref[h:h + 1, :] = m
            l_ref[h:h + 1, :] = l
            acc_ref[h] = acc
        return carry

    lax.fori_loop(0, nk, attn_body, 0)
    _flash_finish(o_ref, l_ref, acc_ref, DSA_HEADS)


def _dsa_attn(qi3, klo3, khi3, wit, qk3, vt3, topk):
    b, t, _ = qk3.shape
    tq = KEY_TILE
    nq = t // tq
    seq_bits = max(1, (t - 1).bit_length())
    one = pl.Buffered(1)
    kern = functools.partial(
        _dsa_kernel, topk=topk, scale=HEAD_DIM ** -0.5,
        idx_scale=(IDX_DIM ** -0.5) * (IDX_HEADS ** -0.5), tq=tq, seq_bits=seq_bits)
    return pl.pallas_call(
        kern,
        grid=(b, nq),
        in_specs=[pl.BlockSpec((None, tq, IDX_W), lambda bi, i: (bi, i, 0)),
                  pl.BlockSpec((None, t, LANES), lambda bi, i: (bi, 0, 0), pipeline_mode=one),
                  pl.BlockSpec((None, t, LANES), lambda bi, i: (bi, 0, 0), pipeline_mode=one),
                  pl.BlockSpec((None, IDX_HEADS, tq), lambda bi, i: (bi * nq + i, IDX_DIM // IDX_HEADS, 0)),
                  pl.BlockSpec((None, tq, DSA_W), lambda bi, i: (bi, i, 2)),
                  pl.BlockSpec((None, t, DSA_W), lambda bi, i: (bi, 0, 3), pipeline_mode=one),
                  pl.BlockSpec((nq, DSA_W, KEY_TILE), lambda bi, i: (bi, 1, 0), pipeline_mode=one)],
        out_specs=pl.BlockSpec((None, tq, DSA_W), lambda bi, i: (bi, i, 0)),
        out_shape=jax.ShapeDtypeStruct((b, t, DSA_W), BF16),
        scratch_shapes=[pltpu.VMEM((nq, KEY_TILE, tq), F32),
                        pltpu.VMEM((8, tq), jnp.int32),
                        pltpu.VMEM((DSA_HEADS, tq), F32),
                        pltpu.VMEM((DSA_HEADS, tq), F32),
                        pltpu.VMEM((DSA_HEADS, HEAD_DIM, tq), F32),
                        pltpu.VMEM((DSA_HEADS, HEAD_DIM, tq), BF16),
                        pltpu.VMEM((IDX_HEADS // 2, LANES, tq), BF16)],
        compiler_params=_params(("parallel", "arbitrary")),
        name="dsa_attn",
    )(qi3, klo3, khi3, wit, qk3, qk3, vt3)


def _mix_out_kernel(oa_ref, ob_ref, sg_ref, wa_ref, wb_ref, wo_ref, x_ref, gt_ref, gpost_ref,
                    gpre_ref, sc_ref, sh_ref, x1_ref, h2_ref, *, d):
    ya = jnp.dot(oa_ref[...], wa_ref[...], preferred_element_type=F32)
    yb = jnp.dot(ob_ref[...], wb_ref[...], preferred_element_type=F32)
    z = sg_ref[:, :d].astype(F32) * ya + sg_ref[:, d:].astype(F32) * yb
    y = jnp.dot(z.astype(BF16), wo_ref[...], preferred_element_type=F32)
    x1 = x_ref[...] + gt_ref[...] * (_rms(y) * gpost_ref[...])
    x1_ref[...] = x1
    h2_ref[...] = ((_rms(x1) * gpre_ref[...]) * (1.0 + sc_ref[...]) + sh_ref[...]).astype(h2_ref.dtype)


def _mix_out(oa3, ob3, sg3, wa, wb, wo, x, gt1, g_post, g_pre2, sc2, sh2):
    b, t, d = x.shape
    tm = _tile(t, 256)
    one = pl.Buffered(1)
    row = lambda w: pl.BlockSpec((None, tm, w), lambda bi, i: (bi, i, 0))
    full = lambda a: pl.BlockSpec(a.shape, lambda bi, i: (0, 0), pipeline_mode=one)
    vec = pl.BlockSpec((None, 1, d), lambda bi, i: (bi, 0, 0))
    gain = pl.BlockSpec((1, d), lambda bi, i: (0, 0))
    return pl.pallas_call(
        functools.partial(_mix_out_kernel, d=d),
        grid=(b, t // tm),
        in_specs=[row(MOBA_W), row(DSA_W), row(2 * d), full(wa), full(wb), full(wo), row(d),
                  vec, gain, gain, vec, vec],
        out_specs=[row(d), row(d)],
        out_shape=[jax.ShapeDtypeStruct((b, t, d), F32), jax.ShapeDtypeStruct((b, t, d), BF16)],
        compiler_params=_params(("parallel", "parallel")),
        name="mix_out",
    )(oa3, ob3, sg3, wa, wb, wo, x, gt1, g_post, g_pre2, sc2, sh2)


def _ffn_kernel(h_ref, w1_ref, w2_ref, x1_ref, gt_ref, g_ref, o_ref, acc_ref):
    j = pl.program_id(2)

    @pl.when(j == 0)
    def _():
        acc_ref[...] = jnp.zeros(acc_ref.shape, F32)

    for rows in _row_chunks(h_ref, KEY_TILE):
        u = jnp.dot(h_ref[rows, :], w1_ref[...], preferred_element_type=F32)
        u = jnp.square(jnp.maximum(u, 0.0)).astype(BF16)
        acc_ref[rows, :] += jnp.dot(u, w2_ref[...], preferred_element_type=F32)

    @pl.when(j == pl.num_programs(2) - 1)
    def _():
        o_ref[...] = x1_ref[...] + gt_ref[...] * (_rms(acc_ref[...]) * g_ref[...])


def _ffn(h2, w1, w2, x1, gt2, g_post):
    b, t, d = x1.shape
    ff = w1.shape[1]
    tm = _tile(t, 512)
    tf = _tile(ff, 1024)
    row = lambda: pl.BlockSpec((None, tm, d), lambda bi, i, j: (bi, i, 0))
    return pl.pallas_call(
        _ffn_kernel,
        grid=(b, t // tm, ff // tf),
        in_specs=[row(),
                  pl.BlockSpec((d, tf), lambda bi, i, j: (0, j)),
                  pl.BlockSpec((tf, d), lambda bi, i, j: (j, 0)),
                  row(),
                  pl.BlockSpec((None, 1, d), lambda bi, i, j: (bi, 0, 0)),
                  pl.BlockSpec((1, d), lambda bi, i, j: (0, 0))],
        out_specs=row(),
        out_shape=jax.ShapeDtypeStruct((b, t, d), F32),
        scratch_shapes=[pltpu.VMEM((tm, d), F32)],
        compiler_params=_params(("parallel", "parallel", "arbitrary")),
        name="ffn",
    )(h2, w1, w2, x1, gt2, g_post)


def _rope_tables(t):
    pos = jnp.arange(t, dtype=F32)[:, None]
    lane = jnp.arange(LANES)[None, :]

    def cos_sin(half):
        inv_freq = jnp.power(ROPE_THETA, -jnp.arange(half, dtype=F32) / half)
        ang = pos * inv_freq[None, :]
        reps = LANES // half
        return jnp.tile(jnp.cos(ang), (1, reps)), jnp.tile(jnp.sin(ang), (1, reps))

    cos128, sin128 = cos_sin(HEAD_DIM // 2)
    sin128 = jnp.where(lane < HEAD_DIM // 2, -sin128, sin128)
    cos64, sin64 = cos_sin(IDX_DIM // 2)
    low = (lane % IDX_DIM) < IDX_DIM // 2
    sa64 = jnp.where(low, -sin64, 0.0)
    sb64 = jnp.where(low, 0.0, sin64)
    is_key = lane < IDX_DIM
    kw = (jnp.where(is_key, cos64, 1.0), jnp.where(is_key, sa64, 0.0), jnp.where(is_key, sb64, 0.0))
    return (cos128, sin128), (cos64, sa64, sb64), kw


def kernel(x, c, w_ada, b_ada, g_pre_mix, g_post_mix, w_in, w_moba_out, w_dsa_out, w_o,
           g_pre_ffn, g_post_ffn, w_ff1, w_ff2):
    b, t, d = x.shape
    m = b * t
    nb = t // MOBA_BLOCK
    assert t % MOBA_BLOCK == 0 and nb <= MAX_BLOCKS and d % LANES == 0 and b <= 16
    n_sel = max(1, min(MOBA_TOPK, nb - 1))
    topk = min(DSA_TOPK_MAX, t // 4)
    rope128, rope64, rope_kw = _rope_tables(t)
    c_pad = jnp.zeros((16, d), F32).at[:b].set(c)

    for l in range(w_ada.shape[0]):
        mod = _ada(c_pad, w_ada[l], b_ada[l][None, :])[:b]
        sh1, sc1, gt1, sh2, sc2, gt2 = [v[:, None, :] for v in jnp.split(mod, 6, axis=-1)]

        h = _norm_mod(x, g_pre_mix[l][None, :], sc1, sh1).reshape(m, d)
        qk, vt3, qi, klo, khi, wit, sg = _in_proj(h, jnp.swapaxes(w_in, 1, 2), l, t, rope128, rope64, rope_kw)

        qk3 = qk.reshape(b, t, -1)
        kmt = _moba_kmean(qk3, nb)
        selb = _moba_select(qk3, kmt, nb, n_sel)
        oa = _moba_attn(qk3, vt3, selb, nb)
        ob = _dsa_attn(qi.reshape(b, t, -1), klo.reshape(b, t, -1), khi.reshape(b, t, -1),
                       wit, qk3, vt3, topk)

        x, h2 = _mix_out(oa, ob, sg.reshape(b, t, -1), w_moba_out[l].astype(BF16),
                         w_dsa_out[l].astype(BF16), w_o[l].astype(BF16), x, gt1,
                         g_post_mix[l][None, :], g_pre_ffn[l][None, :], sc2, sh2)
        x = _ffn(h2, w_ff1[l].astype(BF16), w_ff2[l].astype(BF16), x, gt2, g_post_ffn[l][None, :])
    return x
```

```python
import functools

import jax
import jax.numpy as jnp
from jax import lax
from jax.experimental import pallas as pl
from jax.experimental.pallas import tpu as pltpu

HEAD_DIM = 128
MOBA_HEADS = 8
MOBA_BLOCK = 256
MOBA_TOPK = 3
DSA_HEADS = 8
IDX_HEADS = 16
IDX_DIM = 64
DSA_TOPK_MAX = 256
ROPE_THETA = 10000.0
RMS_EPS = 1e-6

MOBA_W = MOBA_HEADS * HEAD_DIM
DSA_W = DSA_HEADS * HEAD_DIM
IDX_W = IDX_HEADS * IDX_DIM
LANES = 128
MAX_BLOCKS = LANES // MOBA_HEADS
KEY_TILE = 256
SUB_TILE = 128
ROW_CHUNK = 512
LOG2_E = 1.4426950408889634
MASK_BIAS = -30000.0
NEG_BIG = -1e30
INT_MIN = -(2 ** 31)
KEY_OF_NEG_INF = -2139095041
VMEM_LIMIT = 56 * 1024 * 1024

F32 = jnp.float32
BF16 = jnp.bfloat16
_NT = (((1,), (1,)), ((), ()))


def _params(sem):
    return pltpu.CompilerParams(dimension_semantics=sem, vmem_limit_bytes=VMEM_LIMIT)


def _tile(n, pref):
    if n <= pref:
        return n
    t = pref - pref % LANES
    while t >= LANES:
        if n % t == 0:
            return t
        t -= LANES
    return n


def _rms(x):
    return x * lax.rsqrt(jnp.mean(x * x, axis=-1, keepdims=True) + RMS_EPS)


def _ada_kernel(c_ref, w_ref, b_ref, o_ref):
    c = c_ref[...]
    cs = (c * jax.nn.sigmoid(c)).astype(BF16)
    o_ref[...] = jnp.dot(cs, w_ref[...].astype(BF16), preferred_element_type=F32) + b_ref[...]


def _ada(c_pad, w, b):
    rows, d = c_pad.shape
    n = w.shape[1]
    tn = _tile(n, 1024)
    return pl.pallas_call(
        _ada_kernel,
        grid=(n // tn,),
        in_specs=[pl.BlockSpec((rows, d), lambda j: (0, 0)),
                  pl.BlockSpec((d, tn), lambda j: (0, j)),
                  pl.BlockSpec((1, tn), lambda j: (0, j))],
        out_specs=pl.BlockSpec((rows, tn), lambda j: (0, j)),
        out_shape=jax.ShapeDtypeStruct((rows, n), F32),
        compiler_params=_params(("parallel",)),
        name="ada_mod",
    )(c_pad, w, b)


def _norm_mod_kernel(x_ref, g_ref, sc_ref, sh_ref, o_ref):
    y = _rms(x_ref[...])
    o_ref[...] = ((y * g_ref[...]) * (1.0 + sc_ref[...]) + sh_ref[...]).astype(o_ref.dtype)


def _norm_mod(x, g, sc, sh):
    b, t, d = x.shape
    tt = _tile(t, 512)
    vec = pl.BlockSpec((None, 1, d), lambda bi, ti: (bi, 0, 0))
    return pl.pallas_call(
        _norm_mod_kernel,
        grid=(b, t // tt),
        in_specs=[pl.BlockSpec((None, tt, d), lambda bi, ti: (bi, ti, 0)),
                  pl.BlockSpec((1, d), lambda bi, ti: (0, 0)), vec, vec],
        out_specs=pl.BlockSpec((None, tt, d), lambda bi, ti: (bi, ti, 0)),
        out_shape=jax.ShapeDtypeStruct((b, t, d), BF16),
        compiler_params=_params(("parallel", "parallel")),
        name="norm_mod",
    )(x, g, sc, sh)


def _first_token_tile():
    return pl.program_id(1) == 0


def _rope64(y, cos, sin_a, sin_b):
    return y * cos + pltpu.roll(y, LANES - 32, 1) * sin_a + pltpu.roll(y, 32, 1) * sin_b


def _row_chunks(ref, step=ROW_CHUNK):
    n = ref.shape[0]
    step = min(n, step)
    return [slice(r, r + step) for r in range(0, n, step)]


def _wp_rope128_kernel(h_ref, w_ref, cos_ref, sin_ref, o_ref, wb_ref):
    @pl.when(_first_token_tile())
    def _():
        wb_ref[...] = w_ref[...].astype(BF16)

    for rows in _row_chunks(h_ref):
        acc = lax.dot_general(h_ref[rows, :], wb_ref[...], _NT, preferred_element_type=F32)
        cos = cos_ref[rows, :]
        sin = sin_ref[rows, :]
        for g in range(acc.shape[1] // LANES):
            y = acc[:, g * LANES:(g + 1) * LANES]
            o_ref[rows, g * LANES:(g + 1) * LANES] = (
                y * cos + pltpu.roll(y, LANES // 2, 1) * sin).astype(o_ref.dtype)


def _wp_rope64_kernel(h_ref, w_ref, cos_ref, sa_ref, sb_ref, o_ref, wb_ref):
    @pl.when(_first_token_tile())
    def _():
        wb_ref[...] = w_ref[...].astype(BF16)

    for rows in _row_chunks(h_ref):
        acc = lax.dot_general(h_ref[rows, :], wb_ref[...], _NT, preferred_element_type=F32)
        cos, sa, sb = cos_ref[rows, :], sa_ref[rows, :], sb_ref[rows, :]
        for g in range(acc.shape[1] // LANES):
            y = acc[:, g * LANES:(g + 1) * LANES]
            o_ref[rows, g * LANES:(g + 1) * LANES] = _rope64(y, cos, sa, sb).astype(o_ref.dtype)


def _wp_vt_kernel(h_ref, w_ref, o_ref, wb_ref):
    @pl.when(_first_token_tile())
    def _():
        wb_ref[...] = w_ref[...].astype(BF16)

    for c in range(o_ref.shape[0]):
        r = lax.dot_general(wb_ref[...], h_ref[c * KEY_TILE:(c + 1) * KEY_TILE, :], _NT,
                            preferred_element_type=F32)
        o_ref[c] = r.astype(o_ref.dtype)


def _wp_gate_kernel(h_ref, wa_ref, wb_ref, o_ref, wg_ref, *, shift):
    @pl.when(_first_token_tile())
    def _():
        keep = wg_ref.shape[0] - shift
        wg_ref[:keep, :] = wa_ref[shift:, :].astype(BF16)
        wg_ref[keep:, :] = wb_ref[:shift, :].astype(BF16)

    for rows in _row_chunks(h_ref):
        acc = lax.dot_general(h_ref[rows, :], wg_ref[...], _NT, preferred_element_type=F32)
        o_ref[rows, :] = jax.nn.sigmoid(acc).astype(o_ref.dtype)


def _wp_kw_kernel(h_ref, w_ref, cos_ref, sa_ref, sb_ref, klo_ref, khi_ref, wit_ref, wb_ref):
    @pl.when(pl.program_id(0) == 0)
    def _():
        r = lax.broadcasted_iota(jnp.int32, w_ref.shape, 0)
        wb_ref[...] = jnp.where(r < IDX_DIM + IDX_HEADS, w_ref[...], 0.0).astype(BF16)

    h = h_ref[...]
    y = lax.dot_general(h, wb_ref[...], _NT, preferred_element_type=F32)
    r = _rope64(y, cos_ref[...], sa_ref[...], sb_ref[...])
    lane = lax.broadcasted_iota(jnp.int32, y.shape, 1)
    klo = jnp.where(lane < IDX_DIM, r, 0.0)
    klo_ref[...] = klo.astype(BF16)
    khi_ref[...] = pltpu.roll(klo, IDX_DIM, 1).astype(BF16)
    yt = lax.dot_general(wb_ref[...], h, _NT, preferred_element_type=F32)
    for c in range(wit_ref.shape[0]):
        wit_ref[c] = yt[:, c * KEY_TILE:(c + 1) * KEY_TILE]


def _in_proj(h, wt, l, t, tables128, tables64, tables_kw):
    m, k = h.shape
    d = k
    tm = _tile(t, 1024)
    nt = t // tm
    ni = m // tm
    assert tm % KEY_TILE == 0
    tn = 1024
    o_dsa = 3 * MOBA_W
    o_qi = o_dsa + 3 * DSA_W
    o_kw = o_qi + IDX_W
    o_g = o_kw + IDX_DIM + IDX_HEADS
    sem = ("arbitrary", "arbitrary")
    h_spec = pl.BlockSpec((tm, k), lambda j, i: (i, 0))
    tab = pl.BlockSpec((tm, LANES), lambda j, i: (i % nt, 0))
    row = pl.BlockSpec((tm, tn), lambda j, i: (i, j))

    def wrows(first_blocks, n_first, second_start):
        return pl.BlockSpec((None, tn, k), lambda j, i: (l, jnp.where(j < n_first, first_blocks + j,
                                                                      second_start + j - n_first), 0))

    n_m, n_d = 2 * MOBA_W // tn, 2 * DSA_W // tn
    qk = pl.pallas_call(
        _wp_rope128_kernel,
        grid=(n_m + n_d, ni),
        in_specs=[h_spec, wrows(0, n_m, o_dsa // tn), tab, tab],
        out_specs=row,
        out_shape=jax.ShapeDtypeStruct((m, 2 * MOBA_W + 2 * DSA_W), BF16),
        scratch_shapes=[pltpu.VMEM((tn, k), BF16)],
        compiler_params=_params(sem),
        name="proj_qk",
    )(h, wt, *tables128)

    n_m, n_d = MOBA_W // tn, DSA_W // tn
    vt = pl.pallas_call(
        _wp_vt_kernel,
        grid=(n_m + n_d, ni),
        in_specs=[h_spec, wrows(2 * MOBA_W // tn, n_m, (o_dsa + 2 * DSA_W) // tn)],
        out_specs=pl.BlockSpec((tm // KEY_TILE, tn, KEY_TILE), lambda j, i: (i, j, 0)),
        out_shape=jax.ShapeDtypeStruct((m // KEY_TILE, MOBA_W + DSA_W, KEY_TILE), BF16),
        scratch_shapes=[pltpu.VMEM((tn, k), BF16)],
        compiler_params=_params(sem),
        name="proj_v",
    )(h, wt)

    n_q = IDX_W // tn
    qi = pl.pallas_call(
        _wp_rope64_kernel,
        grid=(n_q, ni),
        in_specs=[h_spec, wrows(o_qi // tn, n_q, 0), tab, tab, tab],
        out_specs=row,
        out_shape=jax.ShapeDtypeStruct((m, IDX_W), BF16),
        scratch_shapes=[pltpu.VMEM((tn, k), BF16)],
        compiler_params=_params(sem),
        name="proj_qi",
    )(h, wt, *tables64)

    tab1 = pl.BlockSpec((tm, LANES), lambda i: (i % nt, 0))
    row1 = pl.BlockSpec((tm, LANES), lambda i: (i, 0))
    klo, khi, wit = pl.pallas_call(
        _wp_kw_kernel,
        grid=(ni,),
        in_specs=[pl.BlockSpec((tm, k), lambda i: (i, 0)),
                  pl.BlockSpec((None, LANES, k), lambda i: (l, o_kw // LANES, 0)), tab1, tab1, tab1],
        out_specs=[row1, row1, pl.BlockSpec((tm // KEY_TILE, LANES, KEY_TILE), lambda i: (i, 0, 0))],
        out_shape=[jax.ShapeDtypeStruct((m, LANES), BF16), jax.ShapeDtypeStruct((m, LANES), BF16),
                   jax.ShapeDtypeStruct((m // KEY_TILE, LANES, KEY_TILE), F32)],
        scratch_shapes=[pltpu.VMEM((LANES, k), BF16)],
        compiler_params=_params(("arbitrary",)),
        name="proj_kw",
    )(h, wt, *tables_kw)

    tg = _tile(2 * d, tn)
    g0 = o_kw // tg
    shift = o_g - o_kw
    assert o_kw % tg == 0 and shift <= LANES and tg % LANES == 0
    sg = pl.pallas_call(
        functools.partial(_wp_gate_kernel, shift=shift),
        grid=(2 * d // tg, ni),
        in_specs=[h_spec,
                  pl.BlockSpec((None, tg, k), lambda j, i: (l, g0 + j, 0)),
                  pl.BlockSpec((None, LANES, k), lambda j, i: (l, (g0 + j + 1) * (tg // LANES), 0))],
        out_specs=pl.BlockSpec((tm, tg), lambda j, i: (i, j)),
        out_shape=jax.ShapeDtypeStruct((m, 2 * d), BF16),
        scratch_shapes=[pltpu.VMEM((tg, k), BF16)],
        compiler_params=_params(sem),
        name="proj_gate",
    )(h, wt, wt)
    return qk, vt, qi, klo, khi, wit, sg


def _moba_kmean_kernel(k_ref, o_ref, km_ref, *, nb):
    km_ref[...] = jnp.zeros(km_ref.shape, F32)
    for j in range(nb):
        blk = k_ref[j * MOBA_BLOCK:(j + 1) * MOBA_BLOCK, :].astype(F32)
        km_ref[j:j + 1, :] = jnp.sum(blk, axis=0, keepdims=True) * (1.0 / MOBA_BLOCK)
    km = km_ref[...]
    tiled = jnp.concatenate([km] * MOBA_HEADS, axis=0)
    r = lax.shift_right_logical(lax.broadcasted_iota(jnp.int32, tiled.shape, 0), 4)
    c = lax.shift_right_logical(lax.broadcasted_iota(jnp.int32, tiled.shape, 1), 7)
    o_ref[...] = jnp.where(r == c, tiled, 0.0).astype(o_ref.dtype)


def _moba_kmean(qk3, nb):
    b, t, _ = qk3.shape
    return pl.pallas_call(
        functools.partial(_moba_kmean_kernel, nb=nb),
        grid=(b,),
        in_specs=[pl.BlockSpec((None, t, MOBA_W), lambda bi: (bi, 0, 1))],
        out_specs=pl.BlockSpec((None, LANES, MOBA_W), lambda bi: (bi, 0, 0)),
        out_shape=jax.ShapeDtypeStruct((b, LANES, MOBA_W), BF16),
        scratch_shapes=[pltpu.VMEM((MAX_BLOCKS, MOBA_W), F32)],
        compiler_params=_params(("parallel",)),
        name="moba_kmean",
    )(qk3)


def _moba_select_kernel(q_ref, kmt_ref, o_ref, *, n_sel):
    i = pl.program_id(1)
    g = lax.dot_general(q_ref[...], kmt_ref[...], _NT, preferred_element_type=F32)
    lane = lax.broadcasted_iota(jnp.int32, g.shape, 1)
    j = lane & (MAX_BLOCKS - 1)
    past = j < i
    gm = jnp.where(past, g, -jnp.inf)
    rank = jnp.zeros(g.shape, F32)
    for s in range(1, MAX_BLOCKS):
        wrap = (j + s) >= MAX_BLOCKS
        other = jnp.where(wrap, pltpu.roll(gm, MAX_BLOCKS - s, 1), pltpu.roll(gm, LANES - s, 1))
        beats = (other > gm) | ((other == gm) & wrap)
        rank = rank + jnp.where(beats, 1.0, 0.0)
    sel = past & (rank < n_sel)
    bias = jnp.where(sel, 0.0, MASK_BIAS)
    for h in range(MOBA_HEADS):
        shift = (LANES - MAX_BLOCKS * h) % LANES
        bh = bias if shift == 0 else pltpu.roll(bias, shift, 1)
        o_ref[h] = jnp.where(lane < MAX_BLOCKS, bh, 0.0).astype(o_ref.dtype)


def _moba_select(qk3, kmt, nb, n_sel):
    b, t, _ = qk3.shape
    return pl.pallas_call(
        functools.partial(_moba_select_kernel, n_sel=n_sel),
        grid=(b, nb),
        in_specs=[pl.BlockSpec((None, MOBA_BLOCK, MOBA_W), lambda bi, i: (bi, i, 0)),
                  pl.BlockSpec((None, LANES, MOBA_W), lambda bi, i: (bi, 0, 0))],
        out_specs=pl.BlockSpec((None, MOBA_HEADS, MOBA_BLOCK, LANES), lambda bi, i: (bi, 0, i, 0)),
        out_shape=jax.ShapeDtypeStruct((b, MOBA_HEADS, t, LANES), BF16),
        compiler_params=_params(("parallel", "parallel")),
        name="moba_select",
    )(qk3, kmt)


def _flash_step(st, m, l, acc_t, v_t):
    m_new = jnp.maximum(m, jnp.max(st, axis=0, keepdims=True))
    alpha = jnp.exp2(m - m_new)
    p = jnp.exp2(st - m_new)
    l = alpha * l + jnp.sum(p, axis=0, keepdims=True)
    acc_t = alpha * acc_t + jnp.dot(v_t, p.astype(BF16), preferred_element_type=F32)
    return m_new, l, acc_t


def _flash_step_joint(sts, m, l, acc_t, v_ts):
    m_new = m
    for st in sts:
        m_new = jnp.maximum(m_new, jnp.max(st, axis=0, keepdims=True))
    alpha = jnp.exp2(m - m_new)
    l = alpha * l
    acc_t = alpha * acc_t
    for st, v_t in zip(sts, v_ts):
        p = jnp.exp2(st - m_new)
        l = l + jnp.sum(p, axis=0, keepdims=True)
        acc_t = acc_t + jnp.dot(v_t, p.astype(BF16), preferred_element_type=F32)
    return m_new, l, acc_t


def _transpose_bf16(x):
    return x.astype(F32).T.astype(BF16)


def _flash_finish(o_ref, l_ref, acc_ref, heads):
    for h in range(heads):
        hs = slice(h * HEAD_DIM, (h + 1) * HEAD_DIM)
        o_ref[:, hs] = (acc_ref[h] / l_ref[h:h + 1, :]).T.astype(o_ref.dtype)


def _moba_attn_kernel(q_ref, k_ref, vt_ref, sb_ref, o_ref, m_ref, l_ref, acc_ref, qa_ref, sbt_ref, *, scale):
    i = pl.program_id(1)
    own = pl.multiple_of(i * MOBA_BLOCK, MOBA_BLOCK)
    c2 = scale * LOG2_E
    sub_shape = (SUB_TILE, MOBA_BLOCK)
    kidx = lax.broadcasted_iota(jnp.int32, sub_shape, 0)
    qidx = lax.broadcasted_iota(jnp.int32, sub_shape, 1)
    for h in range(MOBA_HEADS):
        hs = slice(h * HEAD_DIM, (h + 1) * HEAD_DIM)
        qa_ref[h, :HEAD_DIM, :] = _transpose_bf16(q_ref[:, hs])
        sbt = sb_ref[h].astype(F32).T
        for jb in range(MAX_BLOCKS):
            sbt_ref[h, jb] = jnp.broadcast_to(sbt[jb:jb + 1, :], (8, MOBA_BLOCK))
        m = jnp.full((1, MOBA_BLOCK), NEG_BIG, F32)
        l = jnp.zeros((1, MOBA_BLOCK), F32)
        acc = jnp.zeros((HEAD_DIM, MOBA_BLOCK), F32)
        for u in range(MOBA_BLOCK // SUB_TILE):
            st = jnp.dot(k_ref[pl.ds(own + u * SUB_TILE, SUB_TILE), hs], qa_ref[h, :HEAD_DIM, :],
                         preferred_element_type=F32) * c2
            st = jnp.where(kidx + u * SUB_TILE <= qidx, st, -jnp.inf)
            m, l, acc = _flash_step(st, m, l, acc, vt_ref[i, hs, u * SUB_TILE:(u + 1) * SUB_TILE])
        m_ref[h:h + 1, :] = m
        l_ref[h:h + 1, :] = l
        acc_ref[h] = acc

    def blocks(js):
        for h in range(MOBA_HEADS):
            hs = slice(h * HEAD_DIM, (h + 1) * HEAD_DIM)
            m, l, acc = m_ref[h:h + 1, :], l_ref[h:h + 1, :], acc_ref[h]
            for j in js:
                off = pl.multiple_of(j * MOBA_BLOCK, MOBA_BLOCK)
                bias = sbt_ref[h, j][0:1, :]
                sts, vts = [], []
                for u in range(MOBA_BLOCK // SUB_TILE):
                    sts.append(jnp.dot(k_ref[pl.ds(off + u * SUB_TILE, SUB_TILE), hs], qa_ref[h],
                                       preferred_element_type=F32) * c2 + bias)
                    vts.append(vt_ref[j, hs, u * SUB_TILE:(u + 1) * SUB_TILE])
                m, l, acc = _flash_step_joint(sts, m, l, acc, vts)
            m_ref[h:h + 1, :] = m
            l_ref[h:h + 1, :] = l
            acc_ref[h] = acc

    def body(t, carry):
        blocks((2 * t, 2 * t + 1))
        return carry

    lax.fori_loop(0, i // 2, body, 0)

    @pl.when(i % 2 == 1)
    def _():
        blocks((i - 1,))

    _flash_finish(o_ref, l_ref, acc_ref, MOBA_HEADS)


def _moba_attn(qk3, vt3, selb, nb):
    b, t, _ = qk3.shape
    one = pl.Buffered(1)
    return pl.pallas_call(
        functools.partial(_moba_attn_kernel, scale=HEAD_DIM ** -0.5),
        grid=(b, nb),
        in_specs=[pl.BlockSpec((None, MOBA_BLOCK, MOBA_W), lambda bi, i: (bi, i, 0)),
                  pl.BlockSpec((None, t, MOBA_W), lambda bi, i: (bi, 0, 1), pipeline_mode=one),
                  pl.BlockSpec((nb, MOBA_W, KEY_TILE), lambda bi, i: (bi, 0, 0), pipeline_mode=one),
                  pl.BlockSpec((None, MOBA_HEADS, MOBA_BLOCK, LANES), lambda bi, i: (bi, 0, i, 0))],
        out_specs=pl.BlockSpec((None, MOBA_BLOCK, MOBA_W), lambda bi, i: (bi, i, 0)),
        out_shape=jax.ShapeDtypeStruct((b, t, MOBA_W), BF16),
        scratch_shapes=[pltpu.VMEM((MOBA_HEADS, MOBA_BLOCK), F32),
                        pltpu.VMEM((MOBA_HEADS, MOBA_BLOCK), F32),
                        pltpu.VMEM((MOBA_HEADS, HEAD_DIM, MOBA_BLOCK), F32),
                        pltpu.VMEM((MOBA_HEADS, HEAD_DIM, MOBA_BLOCK), BF16),
                        pltpu.VMEM((MOBA_HEADS, MAX_BLOCKS, 8, MOBA_BLOCK), F32)],
        compiler_params=_params(("parallel", "arbitrary")),
        name="moba_attn",
    )(qk3, qk3, vt3, selb)


def _key_to_float(t):
    return lax.bitcast_convert_type(jnp.where(t >= 0, t, t ^ 0x7FFFFFFF), F32)


def _dsa_kernel(qi_ref, klo_ref, khi_ref, wt_ref, q_ref, k_ref, vt_ref, o_ref,
                sc_ref, jl_ref, m_ref, l_ref, acc_ref, qt_ref, qit_ref, *, topk, scale, idx_scale, tq, seq_bits):
    i = pl.program_id(1)
    nk = i + 1
    shape = (KEY_TILE, tq)
    krow = lax.broadcasted_iota(jnp.int32, shape, 0)
    qpos = i * tq + lax.broadcasted_iota(jnp.int32, shape, 1)

    wt = wt_ref[...]
    for p in range(IDX_HEADS // 2):
        qit_ref[p] = _transpose_bf16(qi_ref[:, p * LANES:(p + 1) * LANES])
    for h in range(DSA_HEADS):
        qt_ref[h] = _transpose_bf16(q_ref[:, h * HEAD_DIM:(h + 1) * HEAD_DIM])

    def score_body(kt, carry):
        off = pl.multiple_of(kt * KEY_TILE, KEY_TILE)
        keys = jnp.concatenate([klo_ref[pl.ds(off, KEY_TILE), :], khi_ref[pl.ds(off, KEY_TILE), :]], axis=0)
        acc = jnp.zeros(shape, F32)
        for p in range(IDX_HEADS // 2):
            s = jnp.dot(keys, qit_ref[p], preferred_element_type=F32)
            acc = acc + (jnp.maximum(s[:KEY_TILE], 0.0) * wt[2 * p:2 * p + 1, :]
                         + jnp.maximum(s[KEY_TILE:], 0.0) * wt[2 * p + 1:2 * p + 2, :])
        kpos = kt * KEY_TILE + krow
        sc_ref[kt] = jnp.where(kpos <= qpos, acc * idx_scale, -jnp.inf)
        return carry

    lax.fori_loop(0, nk, score_body, 0)

    def count(pred):
        def body(kt, acc):
            hit = jnp.where(pred(sc_ref[kt], kt * KEY_TILE + krow), 1.0, 0.0)
            return acc + jnp.sum(hit.reshape(KEY_TILE // 32, 32, tq), axis=0)
        part = lax.fori_loop(0, nk, body, jnp.zeros((32, tq), F32))
        return jnp.sum(part, axis=0, keepdims=True)

    need_select = nk * tq > topk

    @pl.when(jnp.logical_not(need_select))
    def _():
        def body(kt, carry):
            sc_ref[kt] = jnp.where(kt * KEY_TILE + krow <= qpos, 0.0, NEG_BIG)
            return carry
        lax.fori_loop(0, nk, body, 0)

    @pl.when(need_select)
    def _():
        kf = jnp.float32(topk)
        zero = jnp.zeros((1, tq), F32)
        c0 = count(lambda s, kp: s >= zero)
        t0 = jnp.where(c0 >= kf, 0, INT_MIN).astype(jnp.int32)

        def bit_body(b, t):
            cand = t | lax.shift_left(jnp.int32(1), 30 - b)
            cf = _key_to_float(cand)
            c = count(lambda s, kp: s >= cf)
            return jnp.where(c >= kf, cand, t)

        t = lax.fori_loop(0, 31, bit_body, t0)
        thr = _key_to_float(jnp.maximum(t, KEY_OF_NEG_INF))

        need = kf - count(lambda s, kp: s > thr)
        n_eq = count(lambda s, kp: s == thr)
        jl_ref[...] = jnp.full(jl_ref.shape, 2 ** seq_bits, jnp.int32)

        @pl.when(jnp.max(n_eq - need) > 0.0)
        def _():
            def idx_body(b, c):
                cand = c | lax.shift_left(jnp.int32(1), seq_bits - 1 - b)
                g = count(lambda s, kp: (s == thr) & (kp < cand))
                return jnp.where(g < need, cand, c)
            c = lax.fori_loop(0, seq_bits, idx_body, jnp.zeros((1, tq), jnp.int32))
            jl_ref[...] = jnp.broadcast_to(c, jl_ref.shape)

        def bias_body(kt, carry):
            s = sc_ref[kt]
            kpos = kt * KEY_TILE + krow
            sel = ((s > thr) | ((s == thr) & (kpos <= jl_ref[0:1, :]))) & (kpos <= qpos)
            sc_ref[kt] = jnp.where(sel, 0.0, NEG_BIG)
            return carry
        lax.fori_loop(0, nk, bias_body, 0)

    m_ref[...] = jnp.full(m_ref.shape, NEG_BIG, F32)
    l_ref[...] = jnp.zeros(l_ref.shape, F32)
    acc_ref[...] = jnp.zeros(acc_ref.shape, F32)

    c2 = scale * LOG2_E

    def tiles(kts):
        for h in range(DSA_HEADS):
            hs = slice(h * HEAD_DIM, (h + 1) * HEAD_DIM)
            m, l, acc = m_ref[h:h + 1, :], l_ref[h:h + 1, :], acc_ref[h]
            for kt in kts:
                off = pl.multiple_of(kt * KEY_TILE, KEY_TILE)
                for u in range(KEY_TILE // SUB_TILE):
                    us = slice(u * SUB_TILE, (u + 1) * SUB_TILE)
                    st = jnp.dot(k_ref[pl.ds(off + u * SUB_TILE, SUB_TILE), hs], qt_ref[h],
                                 preferred_element_type=F32) * c2 + sc_ref[kt, us, :]
                    m, l, acc = _flash_step(st, m, l, acc, vt_ref[kt, hs, us])
            m_ref[h:h + 1, :] = m
            l_ref[h:h + 1, :] = l
            acc_ref[h] = acc

    def attn_body(t, carry):
        tiles((2 * t, 2 * t + 1))
        return carry

    lax.fori_loop(0, nk // 2, attn_body, 0)

    @pl.when(nk % 2 == 1)
    def _():
        tiles((nk - 1,))

    _flash_finish(o_ref, l_ref, acc_ref, DSA_HEADS)


def _dsa_attn(qi3, klo3, khi3, wit, qk3, vt3, topk):
    b, t, _ = qk3.shape
    tq = KEY_TILE
    nq = t // tq
    seq_bits = max(1, (t - 1).bit_length())
    one = pl.Buffered(1)
    kern = functools.partial(
        _dsa_kernel, topk=topk, scale=HEAD_DIM ** -0.5,
        idx_scale=(IDX_DIM ** -0.5) * (IDX_HEADS ** -0.5), tq=tq, seq_bits=seq_bits)
    return pl.pallas_call(
        kern,
        grid=(b, nq),
        in_specs=[pl.BlockSpec((None, tq, IDX_W), lambda bi, i: (bi, i, 0)),
                  pl.BlockSpec((None, t, LANES), lambda bi, i: (bi, 0, 0), pipeline_mode=one),
                  pl.BlockSpec((None, t, LANES), lambda bi, i: (bi, 0, 0), pipeline_mode=one),
                  pl.BlockSpec((None, IDX_HEADS, tq), lambda bi, i: (bi * nq + i, IDX_DIM // IDX_HEADS, 0)),
                  pl.BlockSpec((None, tq, DSA_W), lambda bi, i: (bi, i, 2)),
                  pl.BlockSpec((None, t, DSA_W), lambda bi, i: (bi, 0, 3), pipeline_mode=one),
                  pl.BlockSpec((nq, DSA_W, KEY_TILE), lambda bi, i: (bi, 1, 0), pipeline_mode=one)],
        out_specs=pl.BlockSpec((None, tq, DSA_W), lambda bi, i: (bi, i, 0)),
        out_shape=jax.ShapeDtypeStruct((b, t, DSA_W), BF16),
        scratch_shapes=[pltpu.VMEM((nq, KEY_TILE, tq), F32),
                        pltpu.VMEM((8, tq), jnp.int32),
                        pltpu.VMEM((DSA_HEADS, tq), F32),
                        pltpu.VMEM((DSA_HEADS, tq), F32),
                        pltpu.VMEM((DSA_HEADS, HEAD_DIM, tq), F32),
                        pltpu.VMEM((DSA_HEADS, HEAD_DIM, tq), BF16),
                        pltpu.VMEM((IDX_HEADS // 2, LANES, tq), BF16)],
        compiler_params=_params(("parallel", "arbitrary")),
        name="dsa_attn",
    )(qi3, klo3, khi3, wit, qk3, qk3, vt3)


def _mix_out_kernel(oa_ref, ob_ref, sg_ref, wa_ref, wb_ref, wo_ref, x_ref, gt_ref, gpost_ref,
                    gpre_ref, sc_ref, sh_ref, x1_ref, h2_ref, *, d):
    ya = jnp.dot(oa_ref[...], wa_ref[...], preferred_element_type=F32)
    yb = jnp.dot(ob_ref[...], wb_ref[...], preferred_element_type=F32)
    z = sg_ref[:, :d].astype(F32) * ya + sg_ref[:, d:].astype(F32) * yb
    y = jnp.dot(z.astype(BF16), wo_ref[...], preferred_element_type=F32)
    x1 = x_ref[...] + gt_ref[...] * (_rms(y) * gpost_ref[...])
    x1_ref[...] = x1
    h2_ref[...] = ((_rms(x1) * gpre_ref[...]) * (1.0 + sc_ref[...]) + sh_ref[...]).astype(h2_ref.dtype)


def _mix_out(oa3, ob3, sg3, wa, wb, wo, x, gt1, g_post, g_pre2, sc2, sh2):
    b, t, d = x.shape
    tm = _tile(t, 256)
    one = pl.Buffered(1)
    row = lambda w: pl.BlockSpec((None, tm, w), lambda bi, i: (bi, i, 0))
    full = lambda a: pl.BlockSpec(a.shape, lambda bi, i: (0, 0), pipeline_mode=one)
    vec = pl.BlockSpec((None, 1, d), lambda bi, i: (bi, 0, 0))
    gain = pl.BlockSpec((1, d), lambda bi, i: (0, 0))
    return pl.pallas_call(
        functools.partial(_mix_out_kernel, d=d),
        grid=(b, t // tm),
        in_specs=[row(MOBA_W), row(DSA_W), row(2 * d), full(wa), full(wb), full(wo), row(d),
                  vec, gain, gain, vec, vec],
        out_specs=[row(d), row(d)],
        out_shape=[jax.ShapeDtypeStruct((b, t, d), F32), jax.ShapeDtypeStruct((b, t, d), BF16)],
        compiler_params=_params(("parallel", "parallel")),
        name="mix_out",
    )(oa3, ob3, sg3, wa, wb, wo, x, gt1, g_post, g_pre2, sc2, sh2)


def _ffn_kernel(h_ref, w1_ref, w2_ref, x1_ref, gt_ref, g_ref, o_ref, acc_ref):
    j = pl.program_id(2)

    @pl.when(j == 0)
    def _():
        acc_ref[...] = jnp.zeros(acc_ref.shape, F32)

    for rows in _row_chunks(h_ref, KEY_TILE):
        u = jnp.dot(h_ref[rows, :], w1_ref[...], preferred_element_type=F32)
        u = jnp.square(jnp.maximum(u, 0.0)).astype(BF16)
        acc_ref[rows, :] += jnp.dot(u, w2_ref[...], preferred_element_type=F32)

    @pl.when(j == pl.num_programs(2) - 1)
    def _():
        o_ref[...] = x1_ref[...] + gt_ref[...] * (_rms(acc_ref[...]) * g_ref[...])


def _ffn(h2, w1, w2, x1, gt2, g_post):
    b, t, d = x1.shape
    ff = w1.shape[1]
    tm = _tile(t, 512)
    tf = _tile(ff, 1024)
    row = lambda: pl.BlockSpec((None, tm, d), lambda bi, i, j: (bi, i, 0))
    return pl.pallas_call(
        _ffn_kernel,
        grid=(b, t // tm, ff // tf),
        in_specs=[row(),
                  pl.BlockSpec((d, tf), lambda bi, i, j: (0, j)),
                  pl.BlockSpec((tf, d), lambda bi, i, j: (j, 0)),
                  row(),
                  pl.BlockSpec((None, 1, d), lambda bi, i, j: (bi, 0, 0)),
                  pl.BlockSpec((1, d), lambda bi, i, j: (0, 0))],
        out_specs=row(),
        out_shape=jax.ShapeDtypeStruct((b, t, d), F32),
        scratch_shapes=[pltpu.VMEM((tm, d), F32)],
        compiler_params=_params(("parallel", "parallel", "arbitrary")),
        name="ffn",
    )(h2, w1, w2, x1, gt2, g_post)


def _rope_tables(t):
    pos = jnp.arange(t, dtype=F32)[:, None]
    lane = jnp.arange(LANES)[None, :]

    def cos_sin(half):
        inv_freq = jnp.power(ROPE_THETA, -jnp.arange(half, dtype=F32) / half)
        ang = pos * inv_freq[None, :]
        reps = LANES // half
        return jnp.tile(jnp.cos(ang), (1, reps)), jnp.tile(jnp.sin(ang), (1, reps))

    cos128, sin128 = cos_sin(HEAD_DIM // 2)
    sin128 = jnp.where(lane < HEAD_DIM // 2, -sin128, sin128)
    cos64, sin64 = cos_sin(IDX_DIM // 2)
    low = (lane % IDX_DIM) < IDX_DIM // 2
    sa64 = jnp.where(low, -sin64, 0.0)
    sb64 = jnp.where(low, 0.0, sin64)
    is_key = lane < IDX_DIM
    kw = (jnp.where(is_key, cos64, 1.0), jnp.where(is_key, sa64, 0.0), jnp.where(is_key, sb64, 0.0))
    return (cos128, sin128), (cos64, sa64, sb64), kw


def kernel(x, c, w_ada, b_ada, g_pre_mix, g_post_mix, w_in, w_moba_out, w_dsa_out, w_o,
           g_pre_ffn, g_post_ffn, w_ff1, w_ff2):
    b, t, d = x.shape
    m = b * t
    nb = t // MOBA_BLOCK
    assert t % MOBA_BLOCK == 0 and nb <= MAX_BLOCKS and d % LANES == 0 and b <= 16
    n_sel = max(1, min(MOBA_TOPK, nb - 1))
    topk = min(DSA_TOPK_MAX, t // 4)
    rope128, rope64, rope_kw = _rope_tables(t)
    c_pad = jnp.zeros((16, d), F32).at[:b].set(c)

    for l in range(w_ada.shape[0]):
        mod = _ada(c_pad, w_ada[l], b_ada[l][None, :])[:b]
        sh1, sc1, gt1, sh2, sc2, gt2 = [v[:, None, :] for v in jnp.split(mod, 6, axis=-1)]

        h = _norm_mod(x, g_pre_mix[l][None, :], sc1, sh1).reshape(m, d)
        qk, vt3, qi, klo, khi, wit, sg = _in_proj(h, jnp.swapaxes(w_in, 1, 2), l, t, rope128, rope64, rope_kw)

        qk3 = qk.reshape(b, t, -1)
        kmt = _moba_kmean(qk3, nb)
        selb = _moba_select(qk3, kmt, nb, n_sel)
        oa = _moba_attn(qk3, vt3, selb, nb)
        ob = _dsa_attn(qi.reshape(b, t, -1), klo.reshape(b, t, -1), khi.reshape(b, t, -1),
                       wit, qk3, vt3, topk)

        x, h2 = _mix_out(oa, ob, sg.reshape(b, t, -1), w_moba_out[l].astype(BF16),
                         w_dsa_out[l].astype(BF16), w_o[l].astype(BF16), x, gt1,
                         g_post_mix[l][None, :], g_pre_ffn[l][None, :], sc2, sh2)
        x = _ffn(h2, w_ff1[l].astype(BF16), w_ff2[l].astype(BF16), x, gt2, g_post_ffn[l][None, :])
    return x
```

```python
import functools

import jax
import jax.numpy as jnp
from jax import lax
from jax.experimental import pallas as pl
from jax.experimental.pallas import tpu as pltpu

HEAD_DIM = 128
MOBA_HEADS = 8
MOBA_BLOCK = 256
MOBA_TOPK = 3
DSA_HEADS = 8
IDX_HEADS = 16
IDX_DIM = 64
DSA_TOPK_MAX = 256
ROPE_THETA = 10000.0
RMS_EPS = 1e-6

MOBA_W = MOBA_HEADS * HEAD_DIM
DSA_W = DSA_HEADS * HEAD_DIM
IDX_W = IDX_HEADS * IDX_DIM
LANES = 128
MAX_BLOCKS = LANES // MOBA_HEADS
KEY_TILE = 256
SUB_TILE = 128
ROW_CHUNK = 512
LOG2_E = 1.4426950408889634
MASK_BIAS = -30000.0
NEG_BIG = -1e30
INT_MIN = -(2 ** 31)
KEY_OF_NEG_INF = -2139095041
VMEM_LIMIT = 56 * 1024 * 1024

F32 = jnp.float32
BF16 = jnp.bfloat16
_NT = (((1,), (1,)), ((), ()))


def _params(sem):
    return pltpu.CompilerParams(dimension_semantics=sem, vmem_limit_bytes=VMEM_LIMIT)


def _tile(n, pref):
    if n <= pref:
        return n
    t = pref - pref % LANES
    while t >= LANES:
        if n % t == 0:
            return t
        t -= LANES
    return n


def _rms(x):
    return x * lax.rsqrt(jnp.mean(x * x, axis=-1, keepdims=True) + RMS_EPS)


def _ada_kernel(c_ref, w_ref, b_ref, o_ref):
    c = c_ref[...]
    cs = (c * jax.nn.sigmoid(c)).astype(BF16)
    o_ref[...] = jnp.dot(cs, w_ref[...].astype(BF16), preferred_element_type=F32) + b_ref[...]


def _ada(c_pad, w, b):
    rows, d = c_pad.shape
    n = w.shape[1]
    tn = _tile(n, 1024)
    return pl.pallas_call(
        _ada_kernel,
        grid=(n // tn,),
        in_specs=[pl.BlockSpec((rows, d), lambda j: (0, 0)),
                  pl.BlockSpec((d, tn), lambda j: (0, j)),
                  pl.BlockSpec((1, tn), lambda j: (0, j))],
        out_specs=pl.BlockSpec((rows, tn), lambda j: (0, j)),
        out_shape=jax.ShapeDtypeStruct((rows, n), F32),
        compiler_params=_params(("parallel",)),
        name="ada_mod",
    )(c_pad, w, b)


def _norm_mod_kernel(x_ref, g_ref, sc_ref, sh_ref, o_ref):
    y = _rms(x_ref[...])
    o_ref[...] = ((y * g_ref[...]) * (1.0 + sc_ref[...]) + sh_ref[...]).astype(o_ref.dtype)


def _norm_mod(x, g, sc, sh):
    b, t, d = x.shape
    tt = _tile(t, 512)
    vec = pl.BlockSpec((None, 1, d), lambda bi, ti: (bi, 0, 0))
    return pl.pallas_call(
        _norm_mod_kernel,
        grid=(b, t // tt),
        in_specs=[pl.BlockSpec((None, tt, d), lambda bi, ti: (bi, ti, 0)),
                  pl.BlockSpec((1, d), lambda bi, ti: (0, 0)), vec, vec],
        out_specs=pl.BlockSpec((None, tt, d), lambda bi, ti: (bi, ti, 0)),
        out_shape=jax.ShapeDtypeStruct((b, t, d), BF16),
        compiler_params=_params(("parallel", "parallel")),
        name="norm_mod",
    )(x, g, sc, sh)


def _first_token_tile():
    return pl.program_id(1) == 0


def _rope64(y, cos, sin_a, sin_b):
    return y * cos + pltpu.roll(y, LANES - 32, 1) * sin_a + pltpu.roll(y, 32, 1) * sin_b


def _row_chunks(ref, step=ROW_CHUNK):
    n = ref.shape[0]
    step = min(n, step)
    return [slice(r, r + step) for r in range(0, n, step)]


def _wp_rope128_kernel(h_ref, w_ref, cos_ref, sin_ref, o_ref, wb_ref):
    @pl.when(_first_token_tile())
    def _():
        wb_ref[...] = w_ref[...].astype(BF16)

    for rows in _row_chunks(h_ref):
        acc = lax.dot_general(h_ref[rows, :], wb_ref[...], _NT, preferred_element_type=F32)
        cos = cos_ref[rows, :]
        sin = sin_ref[rows, :]
        for g in range(acc.shape[1] // LANES):
            y = acc[:, g * LANES:(g + 1) * LANES]
            o_ref[rows, g * LANES:(g + 1) * LANES] = (
                y * cos + pltpu.roll(y, LANES // 2, 1) * sin).astype(o_ref.dtype)


def _wp_rope64_kernel(h_ref, w_ref, cos_ref, sa_ref, sb_ref, o_ref, wb_ref):
    @pl.when(_first_token_tile())
    def _():
        wb_ref[...] = w_ref[...].astype(BF16)

    for rows in _row_chunks(h_ref):
        acc = lax.dot_general(h_ref[rows, :], wb_ref[...], _NT, preferred_element_type=F32)
        cos, sa, sb = cos_ref[rows, :], sa_ref[rows, :], sb_ref[rows, :]
        for g in range(acc.shape[1] // LANES):
            y = acc[:, g * LANES:(g + 1) * LANES]
            o_ref[rows, g * LANES:(g + 1) * LANES] = _rope64(y, cos, sa, sb).astype(o_ref.dtype)


def _wp_vt_kernel(h_ref, w_ref, o_ref, wb_ref):
    @pl.when(_first_token_tile())
    def _():
        wb_ref[...] = w_ref[...].astype(BF16)

    for c in range(o_ref.shape[0]):
        r = lax.dot_general(wb_ref[...], h_ref[c * KEY_TILE:(c + 1) * KEY_TILE, :], _NT,
                            preferred_element_type=F32)
        o_ref[c] = r.astype(o_ref.dtype)


def _wp_gate_kernel(h_ref, wa_ref, wb_ref, o_ref, wg_ref, *, shift):
    @pl.when(_first_token_tile())
    def _():
        keep = wg_ref.shape[0] - shift
        wg_ref[:keep, :] = wa_ref[shift:, :].astype(BF16)
        wg_ref[keep:, :] = wb_ref[:shift, :].astype(BF16)

    for rows in _row_chunks(h_ref):
        acc = lax.dot_general(h_ref[rows, :], wg_ref[...], _NT, preferred_element_type=F32)
        o_ref[rows, :] = jax.nn.sigmoid(acc).astype(o_ref.dtype)


def _wp_kw_kernel(h_ref, w_ref, cos_ref, sa_ref, sb_ref, klo_ref, khi_ref, wit_ref, wb_ref):
    @pl.when(pl.program_id(0) == 0)
    def _():
        r = lax.broadcasted_iota(jnp.int32, w_ref.shape, 0)
        wb_ref[...] = jnp.where(r < IDX_DIM + IDX_HEADS, w_ref[...], 0.0).astype(BF16)

    h = h_ref[...]
    y = lax.dot_general(h, wb_ref[...], _NT, preferred_element_type=F32)
    r = _rope64(y, cos_ref[...], sa_ref[...], sb_ref[...])
    lane = lax.broadcasted_iota(jnp.int32, y.shape, 1)
    klo = jnp.where(lane < IDX_DIM, r, 0.0)
    klo_ref[...] = klo.astype(BF16)
    khi_ref[...] = pltpu.roll(klo, IDX_DIM, 1).astype(BF16)
    yt = lax.dot_general(wb_ref[...], h, _NT, preferred_element_type=F32)
    for c in range(wit_ref.shape[0]):
        wit_ref[c] = yt[:, c * KEY_TILE:(c + 1) * KEY_TILE]


def _in_proj(h, wt, l, t, tables128, tables64, tables_kw):
    m, k = h.shape
    d = k
    tm = _tile(t, 1024)
    nt = t // tm
    ni = m // tm
    assert tm % KEY_TILE == 0
    tn = 1024
    o_dsa = 3 * MOBA_W
    o_qi = o_dsa + 3 * DSA_W
    o_kw = o_qi + IDX_W
    o_g = o_kw + IDX_DIM + IDX_HEADS
    sem = ("arbitrary", "arbitrary")
    h_spec = pl.BlockSpec((tm, k), lambda j, i: (i, 0))
    tab = pl.BlockSpec((tm, LANES), lambda j, i: (i % nt, 0))
    row = pl.BlockSpec((tm, tn), lambda j, i: (i, j))

    def wrows(first_blocks, n_first, second_start):
        return pl.BlockSpec((None, tn, k), lambda j, i: (l, jnp.where(j < n_first, first_blocks + j,
                                                                      second_start + j - n_first), 0))

    n_m, n_d = 2 * MOBA_W // tn, 2 * DSA_W // tn
    qk = pl.pallas_call(
        _wp_rope128_kernel,
        grid=(n_m + n_d, ni),
        in_specs=[h_spec, wrows(0, n_m, o_dsa // tn), tab, tab],
        out_specs=row,
        out_shape=jax.ShapeDtypeStruct((m, 2 * MOBA_W + 2 * DSA_W), BF16),
        scratch_shapes=[pltpu.VMEM((tn, k), BF16)],
        compiler_params=_params(sem),
        name="proj_qk",
    )(h, wt, *tables128)

    n_m, n_d = MOBA_W // tn, DSA_W // tn
    vt = pl.pallas_call(
        _wp_vt_kernel,
        grid=(n_m + n_d, ni),
        in_specs=[h_spec, wrows(2 * MOBA_W // tn, n_m, (o_dsa + 2 * DSA_W) // tn)],
        out_specs=pl.BlockSpec((tm // KEY_TILE, tn, KEY_TILE), lambda j, i: (i, j, 0)),
        out_shape=jax.ShapeDtypeStruct((m // KEY_TILE, MOBA_W + DSA_W, KEY_TILE), BF16),
        scratch_shapes=[pltpu.VMEM((tn, k), BF16)],
        compiler_params=_params(sem),
        name="proj_v",
    )(h, wt)

    n_q = IDX_W // tn
    qi = pl.pallas_call(
        _wp_rope64_kernel,
        grid=(n_q, ni),
        in_specs=[h_spec, wrows(o_qi // tn, n_q, 0), tab, tab, tab],
        out_specs=row,
        out_shape=jax.ShapeDtypeStruct((m, IDX_W), BF16),
        scratch_shapes=[pltpu.VMEM((tn, k), BF16)],
        compiler_params=_params(sem),
        name="proj_qi",
    )(h, wt, *tables64)

    tab1 = pl.BlockSpec((tm, LANES), lambda i: (i % nt, 0))
    row1 = pl.BlockSpec((tm, LANES), lambda i: (i, 0))
    klo, khi, wit = pl.pallas_call(
        _wp_kw_kernel,
        grid=(ni,),
        in_specs=[pl.BlockSpec((tm, k), lambda i: (i, 0)),
                  pl.BlockSpec((None, LANES, k), lambda i: (l, o_kw // LANES, 0)), tab1, tab1, tab1],
        out_specs=[row1, row1, pl.BlockSpec((tm // KEY_TILE, LANES, KEY_TILE), lambda i: (i, 0, 0))],
        out_shape=[jax.ShapeDtypeStruct((m, LANES), BF16), jax.ShapeDtypeStruct((m, LANES), BF16),
                   jax.ShapeDtypeStruct((m // KEY_TILE, LANES, KEY_TILE), F32)],
        scratch_shapes=[pltpu.VMEM((LANES, k), BF16)],
        compiler_params=_params(("arbitrary",)),
        name="proj_kw",
    )(h, wt, *tables_kw)

    tg = _tile(2 * d, tn)
    g0 = o_kw // tg
    shift = o_g - o_kw
    assert o_kw % tg == 0 and shift <= LANES and tg % LANES == 0
    sg = pl.pallas_call(
        functools.partial(_wp_gate_kernel, shift=shift),
        grid=(2 * d // tg, ni),
        in_specs=[h_spec,
                  pl.BlockSpec((None, tg, k), lambda j, i: (l, g0 + j, 0)),
                  pl.BlockSpec((None, LANES, k), lambda j, i: (l, (g0 + j + 1) * (tg // LANES), 0))],
        out_specs=pl.BlockSpec((tm, tg), lambda j, i: (i, j)),
        out_shape=jax.ShapeDtypeStruct((m, 2 * d), BF16),
        scratch_shapes=[pltpu.VMEM((tg, k), BF16)],
        compiler_params=_params(sem),
        name="proj_gate",
    )(h, wt, wt)
    return qk, vt, qi, klo, khi, wit, sg


def _moba_kmean_kernel(k_ref, o_ref, km_ref, *, nb):
    km_ref[...] = jnp.zeros(km_ref.shape, F32)
    for j in range(nb):
        blk = k_ref[j * MOBA_BLOCK:(j + 1) * MOBA_BLOCK, :].astype(F32)
        km_ref[j:j + 1, :] = jnp.sum(blk, axis=0, keepdims=True) * (1.0 / MOBA_BLOCK)
    km = km_ref[...]
    tiled = jnp.concatenate([km] * MOBA_HEADS, axis=0)
    r = lax.shift_right_logical(lax.broadcasted_iota(jnp.int32, tiled.shape, 0), 4)
    c = lax.shift_right_logical(lax.broadcasted_iota(jnp.int32, tiled.shape, 1), 7)
    o_ref[...] = jnp.where(r == c, tiled, 0.0).astype(o_ref.dtype)


def _moba_kmean(qk3, nb):
    b, t, _ = qk3.shape
    return pl.pallas_call(
        functools.partial(_moba_kmean_kernel, nb=nb),
        grid=(b,),
        in_specs=[pl.BlockSpec((None, t, MOBA_W), lambda bi: (bi, 0, 1))],
        out_specs=pl.BlockSpec((None, LANES, MOBA_W), lambda bi: (bi, 0, 0)),
        out_shape=jax.ShapeDtypeStruct((b, LANES, MOBA_W), BF16),
        scratch_shapes=[pltpu.VMEM((MAX_BLOCKS, MOBA_W), F32)],
        compiler_params=_params(("parallel",)),
        name="moba_kmean",
    )(qk3)


def _moba_block_bias(q_ref, kmt_ref, sbt_ref, i, n_sel):
    g = lax.dot_general(kmt_ref[...], q_ref[...], _NT, preferred_element_type=F32)
    shape = (MAX_BLOCKS, g.shape[1])
    j = lax.broadcasted_iota(jnp.int32, shape, 0)
    past = j < i
    for h in range(MOBA_HEADS):
        gm = jnp.where(past, g[h * MAX_BLOCKS:(h + 1) * MAX_BLOCKS, :], -jnp.inf)
        rank = jnp.zeros(shape, F32)
        for jo in range(MAX_BLOCKS):
            other = gm[jo:jo + 1, :]
            beats = (other > gm) | ((other == gm) & (jo < j))
            rank = rank + jnp.where(beats, 1.0, 0.0)
        bias = jnp.where(past & (rank < n_sel), 0.0, MASK_BIAS)
        for jb in range(MAX_BLOCKS):
            sbt_ref[h, jb] = jnp.broadcast_to(bias[jb:jb + 1, :], (8, shape[1]))


def _flash_step(st, m, l, acc_t, v_t):
    m_new = jnp.maximum(m, jnp.max(st, axis=0, keepdims=True))
    alpha = jnp.exp2(m - m_new)
    p = jnp.exp2(st - m_new)
    l = alpha * l + jnp.sum(p, axis=0, keepdims=True)
    acc_t = alpha * acc_t + jnp.dot(v_t, p.astype(BF16), preferred_element_type=F32)
    return m_new, l, acc_t


def _flash_step_joint(sts, m, l, acc_t, v_ts):
    m_new = m
    for st in sts:
        m_new = jnp.maximum(m_new, jnp.max(st, axis=0, keepdims=True))
    alpha = jnp.exp2(m - m_new)
    l = alpha * l
    acc_t = alpha * acc_t
    for st, v_t in zip(sts, v_ts):
        p = jnp.exp2(st - m_new)
        l = l + jnp.sum(p, axis=0, keepdims=True)
        acc_t = acc_t + jnp.dot(v_t, p.astype(BF16), preferred_element_type=F32)
    return m_new, l, acc_t


def _transpose_bf16(x):
    return x.astype(F32).T.astype(BF16)


def _flash_finish(o_ref, l_ref, acc_ref, heads):
    for h in range(heads):
        hs = slice(h * HEAD_DIM, (h + 1) * HEAD_DIM)
        o_ref[:, hs] = (acc_ref[h] / l_ref[h:h + 1, :]).T.astype(o_ref.dtype)


def _moba_attn_kernel(q_ref, k_ref, vt_ref, kmt_ref, o_ref, m_ref, l_ref, acc_ref, qa_ref, sbt_ref,
                      *, scale, n_sel):
    i = pl.program_id(1)
    own = pl.multiple_of(i * MOBA_BLOCK, MOBA_BLOCK)
    c2 = scale * LOG2_E
    sub_shape = (SUB_TILE, MOBA_BLOCK)
    kidx = lax.broadcasted_iota(jnp.int32, sub_shape, 0)
    qidx = lax.broadcasted_iota(jnp.int32, sub_shape, 1)
    _moba_block_bias(q_ref, kmt_ref, sbt_ref, i, n_sel)
    for h in range(MOBA_HEADS):
        hs = slice(h * HEAD_DIM, (h + 1) * HEAD_DIM)
        qa_ref[h] = _transpose_bf16(q_ref[:, hs])
        m = jnp.full((1, MOBA_BLOCK), NEG_BIG, F32)
        l = jnp.zeros((1, MOBA_BLOCK), F32)
        acc = jnp.zeros((HEAD_DIM, MOBA_BLOCK), F32)
        for u in range(MOBA_BLOCK // SUB_TILE):
            st = jnp.dot(k_ref[pl.ds(own + u * SUB_TILE, SUB_TILE), hs], qa_ref[h],
                         preferred_element_type=F32) * c2
            st = jnp.where(kidx + u * SUB_TILE <= qidx, st, -jnp.inf)
            m, l, acc = _flash_step(st, m, l, acc, vt_ref[i, hs, u * SUB_TILE:(u + 1) * SUB_TILE])
        m_ref[h:h + 1, :] = m
        l_ref[h:h + 1, :] = l
        acc_ref[h] = acc

    def blocks(js):
        for h in range(MOBA_HEADS):
            hs = slice(h * HEAD_DIM, (h + 1) * HEAD_DIM)
            m, l, acc = m_ref[h:h + 1, :], l_ref[h:h + 1, :], acc_ref[h]
            for j in js:
                off = pl.multiple_of(j * MOBA_BLOCK, MOBA_BLOCK)
                bias = sbt_ref[h, j][0:1, :]
                sts, vts = [], []
                for u in range(MOBA_BLOCK // SUB_TILE):
                    sts.append(jnp.dot(k_ref[pl.ds(off + u * SUB_TILE, SUB_TILE), hs], qa_ref[h],
                                       preferred_element_type=F32) * c2 + bias)
                    vts.append(vt_ref[j, hs, u * SUB_TILE:(u + 1) * SUB_TILE])
                m, l, acc = _flash_step_joint(sts, m, l, acc, vts)
            m_ref[h:h + 1, :] = m
            l_ref[h:h + 1, :] = l
            acc_ref[h] = acc

    def body(t, carry):
        blocks((2 * t, 2 * t + 1))
        return carry

    lax.fori_loop(0, i // 2, body, 0)

    @pl.when(i % 2 == 1)
    def _():
        blocks((i - 1,))

    _flash_finish(o_ref, l_ref, acc_ref, MOBA_HEADS)


def _moba_attn(qk3, vt3, kmt, nb, n_sel):
    b, t, _ = qk3.shape
    one = pl.Buffered(1)
    return pl.pallas_call(
        functools.partial(_moba_attn_kernel, scale=HEAD_DIM ** -0.5, n_sel=n_sel),
        grid=(b, nb),
        in_specs=[pl.BlockSpec((None, MOBA_BLOCK, MOBA_W), lambda bi, i: (bi, i, 0)),
                  pl.BlockSpec((None, t, MOBA_W), lambda bi, i: (bi, 0, 1), pipeline_mode=one),
                  pl.BlockSpec((nb, MOBA_W, KEY_TILE), lambda bi, i: (bi, 0, 0), pipeline_mode=one),
                  pl.BlockSpec((None, LANES, MOBA_W), lambda bi, i: (bi, 0, 0), pipeline_mode=one)],
        out_specs=pl.BlockSpec((None, MOBA_BLOCK, MOBA_W), lambda bi, i: (bi, i, 0)),
        out_shape=jax.ShapeDtypeStruct((b, t, MOBA_W), BF16),
        scratch_shapes=[pltpu.VMEM((MOBA_HEADS, MOBA_BLOCK), F32),
                        pltpu.VMEM((MOBA_HEADS, MOBA_BLOCK), F32),
                        pltpu.VMEM((MOBA_HEADS, HEAD_DIM, MOBA_BLOCK), F32),
                        pltpu.VMEM((MOBA_HEADS, HEAD_DIM, MOBA_BLOCK), BF16),
                        pltpu.VMEM((MOBA_HEADS, MAX_BLOCKS, 8, MOBA_BLOCK), F32)],
        compiler_params=_params(("parallel", "arbitrary")),
        name="moba_attn",
    )(qk3, qk3, vt3, kmt)


def _key_to_float(t):
    return lax.bitcast_convert_type(jnp.where(t >= 0, t, t ^ 0x7FFFFFFF), F32)


def _dsa_kernel(qi_ref, klo_ref, khi_ref, wt_ref, q_ref, k_ref, vt_ref, o_ref,
                sc_ref, jl_ref, m_ref, l_ref, acc_ref, qt_ref, qit_ref, *, topk, scale, idx_scale, tq, seq_bits):
    i = pl.program_id(1)
    nk = i + 1
    shape = (KEY_TILE, tq)
    krow = lax.broadcasted_iota(jnp.int32, shape, 0)
    qpos = i * tq + lax.broadcasted_iota(jnp.int32, shape, 1)

    wt = wt_ref[...]
    for p in range(IDX_HEADS // 2):
        qit_ref[p] = _transpose_bf16(qi_ref[:, p * LANES:(p + 1) * LANES])
    for h in range(DSA_HEADS):
        qt_ref[h] = _transpose_bf16(q_ref[:, h * HEAD_DIM:(h + 1) * HEAD_DIM])

    def score_tiles(kts):
        offs = [pl.multiple_of(kt * KEY_TILE, KEY_TILE) for kt in kts]
        keys = jnp.concatenate(
            [r[pl.ds(off, KEY_TILE), :] for off in offs for r in (klo_ref, khi_ref)], axis=0)
        accs = [jnp.zeros(shape, F32) for _ in kts]
        for p in range(IDX_HEADS // 2):
            s = jnp.dot(keys, qit_ref[p], preferred_element_type=F32)
            for n in range(len(kts)):
                lo = s[2 * n * KEY_TILE:(2 * n + 1) * KEY_TILE]
                hi = s[(2 * n + 1) * KEY_TILE:(2 * n + 2) * KEY_TILE]
                accs[n] = accs[n] + (jnp.maximum(lo, 0.0) * wt[2 * p:2 * p + 1, :]
                                     + jnp.maximum(hi, 0.0) * wt[2 * p + 1:2 * p + 2, :])
        for kt, acc in zip(kts, accs):
            kpos = kt * KEY_TILE + krow
            sc_ref[kt] = jnp.where(kpos <= qpos, acc * idx_scale, -jnp.inf)

    def score_body(t, carry):
        score_tiles((2 * t, 2 * t + 1))
        return carry

    lax.fori_loop(0, nk // 2, score_body, 0)

    @pl.when(nk % 2 == 1)
    def _():
        score_tiles((nk - 1,))

    def count(pred):
        def body(kt, acc):
            hit = jnp.where(pred(sc_ref[kt], kt * KEY_TILE + krow), 1.0, 0.0)
            return acc + jnp.sum(hit.reshape(KEY_TILE // 32, 32, tq), axis=0)
        part = lax.fori_loop(0, nk, body, jnp.zeros((32, tq), F32))
        return jnp.sum(part, axis=0, keepdims=True)

    need_select = nk * tq > topk

    @pl.when(jnp.logical_not(need_select))
    def _():
        def body(kt, carry):
            sc_ref[kt] = jnp.where(kt * KEY_TILE + krow <= qpos, 0.0, NEG_BIG)
            return carry
        lax.fori_loop(0, nk, body, 0)

    @pl.when(need_select)
    def _():
        kf = jnp.float32(topk)
        zero = jnp.zeros((1, tq), F32)
        c0 = count(lambda s, kp: s >= zero)
        t0 = jnp.where(c0 >= kf, 0, INT_MIN).astype(jnp.int32)

        def bit_body(b, t):
            cand = t | lax.shift_left(jnp.int32(1), 30 - b)
            cf = _key_to_float(cand)
            c = count(lambda s, kp: s >= cf)
            return jnp.where(c >= kf, cand, t)

        t = lax.fori_loop(0, 31, bit_body, t0)
        thr = _key_to_float(jnp.maximum(t, KEY_OF_NEG_INF))

        need = kf - count(lambda s, kp: s > thr)
        n_eq = count(lambda s, kp: s == thr)
        jl_ref[...] = jnp.full(jl_ref.shape, 2 ** seq_bits, jnp.int32)

        @pl.when(jnp.max(n_eq - need) > 0.0)
        def _():
            def idx_body(b, c):
                cand = c | lax.shift_left(jnp.int32(1), seq_bits - 1 - b)
                g = count(lambda s, kp: (s == thr) & (kp < cand))
                return jnp.where(g < need, cand, c)
            c = lax.fori_loop(0, seq_bits, idx_body, jnp.zeros((1, tq), jnp.int32))
            jl_ref[...] = jnp.broadcast_to(c, jl_ref.shape)

        def bias_body(kt, carry):
            s = sc_ref[kt]
            kpos = kt * KEY_TILE + krow
            sel = ((s > thr) | ((s == thr) & (kpos <= jl_ref[0:1, :]))) & (kpos <= qpos)
            sc_ref[kt] = jnp.where(sel, 0.0, NEG_BIG)
            return carry
        lax.fori_loop(0, nk, bias_body, 0)

    m_ref[...] = jnp.full(m_ref.shape, NEG_BIG, F32)
    l_ref[...] = jnp.zeros(l_ref.shape, F32)
    acc_ref[...] = jnp.zeros(acc_ref.shape, F32)

    c2 = scale * LOG2_E

    def tiles(kts):
        for h in range(DSA_HEADS):
            hs = slice(h * HEAD_DIM, (h + 1) * HEAD_DIM)
            m, l, acc = m_ref[h:h + 1, :], l_ref[h:h + 1, :], acc_ref[h]
            for kt in kts:
                off = pl.multiple_of(kt * KEY_TILE, KEY_TILE)
                for u in range(KEY_TILE // SUB_TILE):
                    us = slice(u * SUB_TILE, (u + 1) * SUB_TILE)
                    st = jnp.dot(k_ref[pl.ds(off + u * SUB_TILE, SUB_TILE), hs], qt_ref[h],
                                 preferred_element_type=F32) * c2 + sc_ref[kt, us, :]
                    m, l, acc = _flash_step(st, m, l, acc, vt_ref[kt, hs, us])
            m_ref[h:h + 1, :] = m
            l_ref[h:h + 1, :] = l
            acc_ref[h] = acc

    def attn_body(t, carry):
        tiles((2 * t, 2 * t + 1))
        return carry

    lax.fori_loop(0, nk // 2, attn_body, 0)

    @pl.when(nk % 2 == 1)
    def _():
        tiles((nk - 1,))

    _flash_finish(o_ref, l_ref, acc_ref, DSA_HEADS)


def _dsa_attn(qi3, klo3, khi3, wit, qk3, vt3, topk):
    b, t, _ = qk3.shape
    tq = KEY_TILE
    nq = t // tq
    seq_bits = max(1, (t - 1).bit_length())
    one = pl.Buffered(1)
    kern = functools.partial(
        _dsa_kernel, topk=topk, scale=HEAD_DIM ** -0.5,
        idx_scale=(IDX_DIM ** -0.5) * (IDX_HEADS ** -0.5), tq=tq, seq_bits=seq_bits)
    return pl.pallas_call(
        kern,
        grid=(b, nq),
        in_specs=[pl.BlockSpec((None, tq, IDX_W), lambda bi, i: (bi, i, 0)),
                  pl.BlockSpec((None, t, LANES), lambda bi, i: (bi, 0, 0), pipeline_mode=one),
                  pl.BlockSpec((None, t, LANES), lambda bi, i: (bi, 0, 0), pipeline_mode=one),
                  pl.BlockSpec((None, IDX_HEADS, tq), lambda bi, i: (bi * nq + i, IDX_DIM // IDX_HEADS, 0)),
                  pl.BlockSpec((None, tq, DSA_W), lambda bi, i: (bi, i, 2)),
                  pl.BlockSpec((None, t, DSA_W), lambda bi, i: (bi, 0, 3), pipeline_mode=one),
                  pl.BlockSpec((nq, DSA_W, KEY_TILE), lambda bi, i: (bi, 1, 0), pipeline_mode=one)],
        out_specs=pl.BlockSpec((None, tq, DSA_W), lambda bi, i: (bi, i, 0)),
        out_shape=jax.ShapeDtypeStruct((b, t, DSA_W), BF16),
        scratch_shapes=[pltpu.VMEM((nq, KEY_TILE, tq), F32),
                        pltpu.VMEM((8, tq), jnp.int32),
                        pltpu.VMEM((DSA_HEADS, tq), F32),
                        pltpu.VMEM((DSA_HEADS, tq), F32),
                        pltpu.VMEM((DSA_HEADS, HEAD_DIM, tq), F32),
                        pltpu.VMEM((DSA_HEADS, HEAD_DIM, tq), BF16),
                        pltpu.VMEM((IDX_HEADS // 2, LANES, tq), BF16)],
        compiler_params=_params(("parallel", "arbitrary")),
        name="dsa_attn",
    )(qi3, klo3, khi3, wit, qk3, qk3, vt3)


def _mix_out_kernel(oa_ref, ob_ref, sg_ref, wa_ref, wb_ref, wo_ref, x_ref, gt_ref, gpost_ref,
                    gpre_ref, sc_ref, sh_ref, x1_ref, h2_ref, *, d):
    ya = jnp.dot(oa_ref[...], wa_ref[...], preferred_element_type=F32)
    yb = jnp.dot(ob_ref[...], wb_ref[...], preferred_element_type=F32)
    z = sg_ref[:, :d].astype(F32) * ya + sg_ref[:, d:].astype(F32) * yb
    y = jnp.dot(z.astype(BF16), wo_ref[...], preferred_element_type=F32)
    x1 = x_ref[...] + gt_ref[...] * (_rms(y) * gpost_ref[...])
    x1_ref[...] = x1
    h2_ref[...] = ((_rms(x1) * gpre_ref[...]) * (1.0 + sc_ref[...]) + sh_ref[...]).astype(h2_ref.dtype)


def _mix_out(oa3, ob3, sg3, wa, wb, wo, x, gt1, g_post, g_pre2, sc2, sh2):
    b, t, d = x.shape
    tm = _tile(t, 256)
    one = pl.Buffered(1)
    row = lambda w: pl.BlockSpec((None, tm, w), lambda bi, i: (bi, i, 0))
    full = lambda a: pl.BlockSpec(a.shape, lambda bi, i: (0, 0), pipeline_mode=one)
    vec = pl.BlockSpec((None, 1, d), lambda bi, i: (bi, 0, 0))
    gain = pl.BlockSpec((1, d), lambda bi, i: (0, 0))
    return pl.pallas_call(
        functools.partial(_mix_out_kernel, d=d),
        grid=(b, t // tm),
        in_specs=[row(MOBA_W), row(DSA_W), row(2 * d), full(wa), full(wb), full(wo), row(d),
                  vec, gain, gain, vec, vec],
        out_specs=[row(d), row(d)],
        out_shape=[jax.ShapeDtypeStruct((b, t, d), F32), jax.ShapeDtypeStruct((b, t, d), BF16)],
        compiler_params=_params(("parallel", "parallel")),
        name="mix_out",
    )(oa3, ob3, sg3, wa, wb, wo, x, gt1, g_post, g_pre2, sc2, sh2)


def _ffn_kernel(h_ref, w1_ref, w2_ref, x1_ref, gt_ref, g_ref, o_ref, acc_ref):
    j = pl.program_id(2)

    @pl.when(j == 0)
    def _():
        acc_ref[...] = jnp.zeros(acc_ref.shape, F32)

    for rows in _row_chunks(h_ref, KEY_TILE):
        u = jnp.dot(h_ref[rows, :], w1_ref[...], preferred_element_type=F32)
        u = jnp.square(jnp.maximum(u, 0.0)).astype(BF16)
        acc_ref[rows, :] += jnp.dot(u, w2_ref[...], preferred_element_type=F32)

    @pl.when(j == pl.num_programs(2) - 1)
    def _():
        o_ref[...] = x1_ref[...] + gt_ref[...] * (_rms(acc_ref[...]) * g_ref[...])


def _ffn(h2, w1, w2, x1, gt2, g_post):
    b, t, d = x1.shape
    ff = w1.shape[1]
    tm = _tile(t, 512)
    tf = _tile(ff, 1024)
    row = lambda: pl.BlockSpec((None, tm, d), lambda bi, i, j: (bi, i, 0))
    return pl.pallas_call(
        _ffn_kernel,
        grid=(b, t // tm, ff // tf),
        in_specs=[row(),
                  pl.BlockSpec((d, tf), lambda bi, i, j: (0, j)),
                  pl.BlockSpec((tf, d), lambda bi, i, j: (j, 0)),
                  row(),
                  pl.BlockSpec((None, 1, d), lambda bi, i, j: (bi, 0, 0)),
                  pl.BlockSpec((1, d), lambda bi, i, j: (0, 0))],
        out_specs=row(),
        out_shape=jax.ShapeDtypeStruct((b, t, d), F32),
        scratch_shapes=[pltpu.VMEM((tm, d), F32)],
        compiler_params=_params(("parallel", "parallel", "arbitrary")),
        name="ffn",
    )(h2, w1, w2, x1, gt2, g_post)


def _rope_tables(t):
    pos = jnp.arange(t, dtype=F32)[:, None]
    lane = jnp.arange(LANES)[None, :]

    def cos_sin(half):
        inv_freq = jnp.power(ROPE_THETA, -jnp.arange(half, dtype=F32) / half)
        ang = pos * inv_freq[None, :]
        reps = LANES // half
        return jnp.tile(jnp.cos(ang), (1, reps)), jnp.tile(jnp.sin(ang), (1, reps))

    cos128, sin128 = cos_sin(HEAD_DIM // 2)
    sin128 = jnp.where(lane < HEAD_DIM // 2, -sin128, sin128)
    cos64, sin64 = cos_sin(IDX_DIM // 2)
    low = (lane % IDX_DIM) < IDX_DIM // 2
    sa64 = jnp.where(low, -sin64, 0.0)
    sb64 = jnp.where(low, 0.0, sin64)
    is_key = lane < IDX_DIM
    kw = (jnp.where(is_key, cos64, 1.0), jnp.where(is_key, sa64, 0.0), jnp.where(is_key, sb64, 0.0))
    return (cos128, sin128), (cos64, sa64, sb64), kw


def kernel(x, c, w_ada, b_ada, g_pre_mix, g_post_mix, w_in, w_moba_out, w_dsa_out, w_o,
           g_pre_ffn, g_post_ffn, w_ff1, w_ff2):
    b, t, d = x.shape
    m = b * t
    nb = t // MOBA_BLOCK
    assert t % MOBA_BLOCK == 0 and nb <= MAX_BLOCKS and d % LANES == 0 and b <= 16
    n_sel = max(1, min(MOBA_TOPK, nb - 1))
    topk = min(DSA_TOPK_MAX, t // 4)
    rope128, rope64, rope_kw = _rope_tables(t)
    c_pad = jnp.zeros((16, d), F32).at[:b].set(c)

    for l in range(w_ada.shape[0]):
        mod = _ada(c_pad, w_ada[l], b_ada[l][None, :])[:b]
        sh1, sc1, gt1, sh2, sc2, gt2 = [v[:, None, :] for v in jnp.split(mod, 6, axis=-1)]

        h = _norm_mod(x, g_pre_mix[l][None, :], sc1, sh1).reshape(m, d)
        qk, vt3, qi, klo, khi, wit, sg = _in_proj(h, jnp.swapaxes(w_in, 1, 2), l, t, rope128, rope64, rope_kw)

        qk3 = qk.reshape(b, t, -1)
        kmt = _moba_kmean(qk3, nb)
        oa = _moba_attn(qk3, vt3, kmt, nb, n_sel)
        ob = _dsa_attn(qi.reshape(b, t, -1), klo.reshape(b, t, -1), khi.reshape(b, t, -1),
                       wit, qk3, vt3, topk)

        x, h2 = _mix_out(oa, ob, sg.reshape(b, t, -1), w_moba_out[l].astype(BF16),
                         w_dsa_out[l].astype(BF16), w_o[l].astype(BF16), x, gt1,
                         g_post_mix[l][None, :], g_pre_ffn[l][None, :], sc2, sh2)
        x = _ffn(h2, w_ff1[l].astype(BF16), w_ff2[l].astype(BF16), x, gt2, g_post_ffn[l][None, :])
    return x
```

```python
import functools

import jax
import jax.numpy as jnp
from jax import lax
from jax.experimental import pallas as pl
from jax.experimental.pallas import tpu as pltpu

HEAD_DIM = 128
MOBA_HEADS = 8
MOBA_BLOCK = 256
MOBA_TOPK = 3
DSA_HEADS = 8
IDX_HEADS = 16
IDX_DIM = 64
DSA_TOPK_MAX = 256
ROPE_THETA = 10000.0
RMS_EPS = 1e-6

MOBA_W = MOBA_HEADS * HEAD_DIM
DSA_W = DSA_HEADS * HEAD_DIM
IDX_W = IDX_HEADS * IDX_DIM
LANES = 128
MAX_BLOCKS = LANES // MOBA_HEADS
KEY_TILE = 256
SUB_TILE = 128
ROW_CHUNK = 512
LOG2_E = 1.4426950408889634
MASK_BIAS = -30000.0
NEG_BIG = -1e30
I16_MIN = -(2 ** 15)
PACKED_ROWS = 16
VMEM_LIMIT = 56 * 1024 * 1024

F32 = jnp.float32
BF16 = jnp.bfloat16
_NT = (((1,), (1,)), ((), ()))


def _params(sem):
    return pltpu.CompilerParams(dimension_semantics=sem, vmem_limit_bytes=VMEM_LIMIT)


def _tile(n, pref):
    if n <= pref:
        return n
    t = pref - pref % LANES
    while t >= LANES:
        if n % t == 0:
            return t
        t -= LANES
    return n


def _rms(x):
    return x * lax.rsqrt(jnp.mean(x * x, axis=-1, keepdims=True) + RMS_EPS)


def _ada_kernel(c_ref, w_ref, b_ref, o_ref):
    c = c_ref[...]
    cs = (c * jax.nn.sigmoid(c)).astype(BF16)
    o_ref[...] = jnp.dot(cs, w_ref[...].astype(BF16), preferred_element_type=F32) + b_ref[...]


def _ada(c_pad, w, b):
    rows, d = c_pad.shape
    n = w.shape[1]
    tn = _tile(n, 1024)
    return pl.pallas_call(
        _ada_kernel,
        grid=(n // tn,),
        in_specs=[pl.BlockSpec((rows, d), lambda j: (0, 0)),
                  pl.BlockSpec((d, tn), lambda j: (0, j)),
                  pl.BlockSpec((1, tn), lambda j: (0, j))],
        out_specs=pl.BlockSpec((rows, tn), lambda j: (0, j)),
        out_shape=jax.ShapeDtypeStruct((rows, n), F32),
        compiler_params=_params(("parallel",)),
        name="ada_mod",
    )(c_pad, w, b)


def _norm_mod_kernel(x_ref, g_ref, sc_ref, sh_ref, o_ref):
    y = _rms(x_ref[...])
    o_ref[...] = ((y * g_ref[...]) * (1.0 + sc_ref[...]) + sh_ref[...]).astype(o_ref.dtype)


def _norm_mod(x, g, sc, sh):
    b, t, d = x.shape
    tt = _tile(t, 512)
    vec = pl.BlockSpec((None, 1, d), lambda bi, ti: (bi, 0, 0))
    return pl.pallas_call(
        _norm_mod_kernel,
        grid=(b, t // tt),
        in_specs=[pl.BlockSpec((None, tt, d), lambda bi, ti: (bi, ti, 0)),
                  pl.BlockSpec((1, d), lambda bi, ti: (0, 0)), vec, vec],
        out_specs=pl.BlockSpec((None, tt, d), lambda bi, ti: (bi, ti, 0)),
        out_shape=jax.ShapeDtypeStruct((b, t, d), BF16),
        compiler_params=_params(("parallel", "parallel")),
        name="norm_mod",
    )(x, g, sc, sh)


def _first_token_tile():
    return pl.program_id(1) == 0


def _rope64(y, cos, sin_a, sin_b):
    return y * cos + pltpu.roll(y, LANES - 32, 1) * sin_a + pltpu.roll(y, 32, 1) * sin_b


def _row_chunks(ref, step=ROW_CHUNK):
    n = ref.shape[0]
    step = min(n, step)
    return [slice(r, r + step) for r in range(0, n, step)]


def _wp_rope128_kernel(h_ref, w_ref, cos_ref, sin_ref, o_ref, wb_ref):
    @pl.when(_first_token_tile())
    def _():
        wb_ref[...] = w_ref[...].astype(BF16)

    for rows in _row_chunks(h_ref):
        acc = lax.dot_general(h_ref[rows, :], wb_ref[...], _NT, preferred_element_type=F32)
        cos = cos_ref[rows, :]
        sin = sin_ref[rows, :]
        for g in range(acc.shape[1] // LANES):
            y = acc[:, g * LANES:(g + 1) * LANES]
            o_ref[rows, g * LANES:(g + 1) * LANES] = (
                y * cos + pltpu.roll(y, LANES // 2, 1) * sin).astype(o_ref.dtype)


def _wp_rope64_kernel(h_ref, w_ref, cos_ref, sa_ref, sb_ref, o_ref, wb_ref):
    @pl.when(_first_token_tile())
    def _():
        wb_ref[...] = w_ref[...].astype(BF16)

    for rows in _row_chunks(h_ref):
        acc = lax.dot_general(h_ref[rows, :], wb_ref[...], _NT, preferred_element_type=F32)
        cos, sa, sb = cos_ref[rows, :], sa_ref[rows, :], sb_ref[rows, :]
        for g in range(acc.shape[1] // LANES):
            y = acc[:, g * LANES:(g + 1) * LANES]
            o_ref[rows, g * LANES:(g + 1) * LANES] = _rope64(y, cos, sa, sb).astype(o_ref.dtype)


def _wp_vt_kernel(h_ref, w_ref, o_ref, wb_ref):
    @pl.when(_first_token_tile())
    def _():
        wb_ref[...] = w_ref[...].astype(BF16)

    for c in range(o_ref.shape[0]):
        r = lax.dot_general(wb_ref[...], h_ref[c * KEY_TILE:(c + 1) * KEY_TILE, :], _NT,
                            preferred_element_type=F32)
        o_ref[c] = r.astype(o_ref.dtype)


def _wp_gate_kernel(h_ref, wa_ref, wb_ref, o_ref, wg_ref, *, shift):
    @pl.when(_first_token_tile())
    def _():
        keep = wg_ref.shape[0] - shift
        wg_ref[:keep, :] = wa_ref[shift:, :].astype(BF16)
        wg_ref[keep:, :] = wb_ref[:shift, :].astype(BF16)

    for rows in _row_chunks(h_ref):
        acc = lax.dot_general(h_ref[rows, :], wg_ref[...], _NT, preferred_element_type=F32)
        o_ref[rows, :] = jax.nn.sigmoid(acc).astype(o_ref.dtype)


def _wp_kw_kernel(h_ref, w_ref, cos_ref, sa_ref, sb_ref, klo_ref, khi_ref, wit_ref, wb_ref):
    @pl.when(pl.program_id(0) == 0)
    def _():
        r = lax.broadcasted_iota(jnp.int32, w_ref.shape, 0)
        wb_ref[...] = jnp.where(r < IDX_DIM + IDX_HEADS, w_ref[...], 0.0).astype(BF16)

    h = h_ref[...]
    y = lax.dot_general(h, wb_ref[...], _NT, preferred_element_type=F32)
    r = _rope64(y, cos_ref[...], sa_ref[...], sb_ref[...])
    lane = lax.broadcasted_iota(jnp.int32, y.shape, 1)
    klo = jnp.where(lane < IDX_DIM, r, 0.0)
    klo_ref[...] = klo.astype(BF16)
    khi_ref[...] = pltpu.roll(klo, IDX_DIM, 1).astype(BF16)
    yt = lax.dot_general(wb_ref[...], h, _NT, preferred_element_type=F32)
    for c in range(wit_ref.shape[0]):
        wit_ref[c] = yt[:, c * KEY_TILE:(c + 1) * KEY_TILE]


def _in_proj(h, wt, l, t, tables128, tables64, tables_kw):
    m, k = h.shape
    d = k
    tm = _tile(t, 1024)
    nt = t // tm
    ni = m // tm
    assert tm % KEY_TILE == 0
    tn = 1024
    o_dsa = 3 * MOBA_W
    o_qi = o_dsa + 3 * DSA_W
    o_kw = o_qi + IDX_W
    o_g = o_kw + IDX_DIM + IDX_HEADS
    sem = ("arbitrary", "arbitrary")
    h_spec = pl.BlockSpec((tm, k), lambda j, i: (i, 0))
    tab = pl.BlockSpec((tm, LANES), lambda j, i: (i % nt, 0))
    row = pl.BlockSpec((tm, tn), lambda j, i: (i, j))

    def wrows(first_blocks, n_first, second_start):
        return pl.BlockSpec((None, tn, k), lambda j, i: (l, jnp.where(j < n_first, first_blocks + j,
                                                                      second_start + j - n_first), 0))

    n_m, n_d = 2 * MOBA_W // tn, 2 * DSA_W // tn
    qk = pl.pallas_call(
        _wp_rope128_kernel,
        grid=(n_m + n_d, ni),
        in_specs=[h_spec, wrows(0, n_m, o_dsa // tn), tab, tab],
        out_specs=row,
        out_shape=jax.ShapeDtypeStruct((m, 2 * MOBA_W + 2 * DSA_W), BF16),
        scratch_shapes=[pltpu.VMEM((tn, k), BF16)],
        compiler_params=_params(sem),
        name="proj_qk",
    )(h, wt, *tables128)

    n_m, n_d = MOBA_W // tn, DSA_W // tn
    vt = pl.pallas_call(
        _wp_vt_kernel,
        grid=(n_m + n_d, ni),
        in_specs=[h_spec, wrows(2 * MOBA_W // tn, n_m, (o_dsa + 2 * DSA_W) // tn)],
        out_specs=pl.BlockSpec((tm // KEY_TILE, tn, KEY_TILE), lambda j, i: (i, j, 0)),
        out_shape=jax.ShapeDtypeStruct((m // KEY_TILE, MOBA_W + DSA_W, KEY_TILE), BF16),
        scratch_shapes=[pltpu.VMEM((tn, k), BF16)],
        compiler_params=_params(sem),
        name="proj_v",
    )(h, wt)

    n_q = IDX_W // tn
    qi = pl.pallas_call(
        _wp_rope64_kernel,
        grid=(n_q, ni),
        in_specs=[h_spec, wrows(o_qi // tn, n_q, 0), tab, tab, tab],
        out_specs=row,
        out_shape=jax.ShapeDtypeStruct((m, IDX_W), BF16),
        scratch_shapes=[pltpu.VMEM((tn, k), BF16)],
        compiler_params=_params(sem),
        name="proj_qi",
    )(h, wt, *tables64)

    tab1 = pl.BlockSpec((tm, LANES), lambda i: (i % nt, 0))
    row1 = pl.BlockSpec((tm, LANES), lambda i: (i, 0))
    klo, khi, wit = pl.pallas_call(
        _wp_kw_kernel,
        grid=(ni,),
        in_specs=[pl.BlockSpec((tm, k), lambda i: (i, 0)),
                  pl.BlockSpec((None, LANES, k), lambda i: (l, o_kw // LANES, 0)), tab1, tab1, tab1],
        out_specs=[row1, row1, pl.BlockSpec((tm // KEY_TILE, LANES, KEY_TILE), lambda i: (i, 0, 0))],
        out_shape=[jax.ShapeDtypeStruct((m, LANES), BF16), jax.ShapeDtypeStruct((m, LANES), BF16),
                   jax.ShapeDtypeStruct((m // KEY_TILE, LANES, KEY_TILE), F32)],
        scratch_shapes=[pltpu.VMEM((LANES, k), BF16)],
        compiler_params=_params(("arbitrary",)),
        name="proj_kw",
    )(h, wt, *tables_kw)

    tg = _tile(2 * d, tn)
    g0 = o_kw // tg
    shift = o_g - o_kw
    assert o_kw % tg == 0 and shift <= LANES and tg % LANES == 0
    sg = pl.pallas_call(
        functools.partial(_wp_gate_kernel, shift=shift),
        grid=(2 * d // tg, ni),
        in_specs=[h_spec,
                  pl.BlockSpec((None, tg, k), lambda j, i: (l, g0 + j, 0)),
                  pl.BlockSpec((None, LANES, k), lambda j, i: (l, (g0 + j + 1) * (tg // LANES), 0))],
        out_specs=pl.BlockSpec((tm, tg), lambda j, i: (i, j)),
        out_shape=jax.ShapeDtypeStruct((m, 2 * d), BF16),
        scratch_shapes=[pltpu.VMEM((tg, k), BF16)],
        compiler_params=_params(sem),
        name="proj_gate",
    )(h, wt, wt)
    return qk, vt, qi, klo, khi, wit, sg


def _moba_kmean_kernel(k_ref, o_ref, km_ref, *, nb):
    km_ref[...] = jnp.zeros(km_ref.shape, F32)
    for j in range(nb):
        blk = k_ref[j * MOBA_BLOCK:(j + 1) * MOBA_BLOCK, :].astype(F32)
        km_ref[j:j + 1, :] = jnp.sum(blk, axis=0, keepdims=True) * (1.0 / MOBA_BLOCK)
    km = km_ref[...]
    tiled = jnp.concatenate([km] * MOBA_HEADS, axis=0)
    r = lax.shift_right_logical(lax.broadcasted_iota(jnp.int32, tiled.shape, 0), 4)
    c = lax.shift_right_logical(lax.broadcasted_iota(jnp.int32, tiled.shape, 1), 7)
    o_ref[...] = jnp.where(r == c, tiled, 0.0).astype(o_ref.dtype)


def _moba_kmean(qk3, nb):
    b, t, _ = qk3.shape
    return pl.pallas_call(
        functools.partial(_moba_kmean_kernel, nb=nb),
        grid=(b,),
        in_specs=[pl.BlockSpec((None, t, MOBA_W), lambda bi: (bi, 0, 1))],
        out_specs=pl.BlockSpec((None, LANES, MOBA_W), lambda bi: (bi, 0, 0)),
        out_shape=jax.ShapeDtypeStruct((b, LANES, MOBA_W), BF16),
        scratch_shapes=[pltpu.VMEM((MAX_BLOCKS, MOBA_W), F32)],
        compiler_params=_params(("parallel",)),
        name="moba_kmean",
    )(qk3)


def _moba_block_bias(q_ref, kmt_ref, sbt_ref, i, n_sel):
    g = lax.dot_general(kmt_ref[...], q_ref[...], _NT, preferred_element_type=F32)
    shape = (MAX_BLOCKS, g.shape[1])
    j = lax.broadcasted_iota(jnp.int32, shape, 0)
    past = j < i
    for h in range(MOBA_HEADS):
        gm = jnp.where(past, g[h * MAX_BLOCKS:(h + 1) * MAX_BLOCKS, :], -jnp.inf)
        rank = jnp.zeros(shape, F32)
        for jo in range(MAX_BLOCKS):
            other = gm[jo:jo + 1, :]
            beats = (other > gm) | ((other == gm) & (jo < j))
            rank = rank + jnp.where(beats, 1.0, 0.0)
        bias = jnp.where(past & (rank < n_sel), 0.0, MASK_BIAS)
        for jb in range(MAX_BLOCKS):
            sbt_ref[h, jb] = jnp.broadcast_to(bias[jb:jb + 1, :], (8, shape[1]))


def _flash_step(st, m, l, acc_t, v_t):
    m_new = jnp.maximum(m, jnp.max(st, axis=0, keepdims=True))
    alpha = jnp.exp2(m - m_new)
    p = jnp.exp2(st - m_new)
    l = alpha * l + jnp.sum(p, axis=0, keepdims=True)
    acc_t = alpha * acc_t + jnp.dot(v_t, p.astype(BF16), preferred_element_type=F32)
    return m_new, l, acc_t


def _flash_step_joint(sts, m, l, acc_t, v_ts):
    m_new = m
    for st in sts:
        m_new = jnp.maximum(m_new, jnp.max(st, axis=0, keepdims=True))
    alpha = jnp.exp2(m - m_new)
    l = alpha * l
    acc_t = alpha * acc_t
    for st, v_t in zip(sts, v_ts):
        p = jnp.exp2(st - m_new)
        l = l + jnp.sum(p, axis=0, keepdims=True)
        acc_t = acc_t + jnp.dot(v_t, p.astype(BF16), preferred_element_type=F32)
    return m_new, l, acc_t


def _transpose_bf16(x):
    return x.astype(F32).T.astype(BF16)


def _flash_finish(o_ref, l_ref, acc_ref, heads):
    for h in range(heads):
        hs = slice(h * HEAD_DIM, (h + 1) * HEAD_DIM)
        o_ref[:, hs] = (acc_ref[h] / l_ref[h:h + 1, :]).T.astype(o_ref.dtype)


def _moba_attn_kernel(q_ref, k_ref, vt_ref, kmt_ref, o_ref, m_ref, l_ref, acc_ref, qa_ref, sbt_ref,
                      *, scale, n_sel):
    i = pl.program_id(1)
    own = pl.multiple_of(i * MOBA_BLOCK, MOBA_BLOCK)
    c2 = scale * LOG2_E
    sub_shape = (SUB_TILE, MOBA_BLOCK)
    kidx = lax.broadcasted_iota(jnp.int32, sub_shape, 0)
    qidx = lax.broadcasted_iota(jnp.int32, sub_shape, 1)
    _moba_block_bias(q_ref, kmt_ref, sbt_ref, i, n_sel)
    for h in range(MOBA_HEADS):
        hs = slice(h * HEAD_DIM, (h + 1) * HEAD_DIM)
        qa_ref[h] = _transpose_bf16(q_ref[:, hs])
        m = jnp.full((1, MOBA_BLOCK), NEG_BIG, F32)
        l = jnp.zeros((1, MOBA_BLOCK), F32)
        acc = jnp.zeros((HEAD_DIM, MOBA_BLOCK), F32)
        for u in range(MOBA_BLOCK // SUB_TILE):
            st = jnp.dot(k_ref[pl.ds(own + u * SUB_TILE, SUB_TILE), hs], qa_ref[h],
                         preferred_element_type=F32) * c2
            st = jnp.where(kidx + u * SUB_TILE <= qidx, st, -jnp.inf)
            m, l, acc = _flash_step(st, m, l, acc, vt_ref[i, hs, u * SUB_TILE:(u + 1) * SUB_TILE])
        m_ref[h:h + 1, :] = m
        l_ref[h:h + 1, :] = l
        acc_ref[h] = acc

    def blocks(js):
        for h in range(MOBA_HEADS):
            hs = slice(h * HEAD_DIM, (h + 1) * HEAD_DIM)
            m, l, acc = m_ref[h:h + 1, :], l_ref[h:h + 1, :], acc_ref[h]
            for j in js:
                off = pl.multiple_of(j * MOBA_BLOCK, MOBA_BLOCK)
                bias = sbt_ref[h, j][0:1, :]
                sts, vts = [], []
                for u in range(MOBA_BLOCK // SUB_TILE):
                    sts.append(jnp.dot(k_ref[pl.ds(off + u * SUB_TILE, SUB_TILE), hs], qa_ref[h],
                                       preferred_element_type=F32) * c2 + bias)
                    vts.append(vt_ref[j, hs, u * SUB_TILE:(u + 1) * SUB_TILE])
                m, l, acc = _flash_step_joint(sts, m, l, acc, vts)
            m_ref[h:h + 1, :] = m
            l_ref[h:h + 1, :] = l
            acc_ref[h] = acc

    def body(t, carry):
        blocks((2 * t, 2 * t + 1))
        return carry

    lax.fori_loop(0, i // 2, body, 0)

    @pl.when(i % 2 == 1)
    def _():
        blocks((i - 1,))

    _flash_finish(o_ref, l_ref, acc_ref, MOBA_HEADS)


def _moba_attn(qk3, vt3, kmt, nb, n_sel):
    b, t, _ = qk3.shape
    one = pl.Buffered(1)
    return pl.pallas_call(
        functools.partial(_moba_attn_kernel, scale=HEAD_DIM ** -0.5, n_sel=n_sel),
        grid=(b, nb),
        in_specs=[pl.BlockSpec((None, MOBA_BLOCK, MOBA_W), lambda bi, i: (bi, i, 0)),
                  pl.BlockSpec((None, t, MOBA_W), lambda bi, i: (bi, 0, 1), pipeline_mode=one),
                  pl.BlockSpec((nb, MOBA_W, KEY_TILE), lambda bi, i: (bi, 0, 0), pipeline_mode=one),
                  pl.BlockSpec((None, LANES, MOBA_W), lambda bi, i: (bi, 0, 0), pipeline_mode=one)],
        out_specs=pl.BlockSpec((None, MOBA_BLOCK, MOBA_W), lambda bi, i: (bi, i, 0)),
        out_shape=jax.ShapeDtypeStruct((b, t, MOBA_W), BF16),
        scratch_shapes=[pltpu.VMEM((MOBA_HEADS, MOBA_BLOCK), F32),
                        pltpu.VMEM((MOBA_HEADS, MOBA_BLOCK), F32),
                        pltpu.VMEM((MOBA_HEADS, HEAD_DIM, MOBA_BLOCK), F32),
                        pltpu.VMEM((MOBA_HEADS, HEAD_DIM, MOBA_BLOCK), BF16),
                        pltpu.VMEM((MOBA_HEADS, MAX_BLOCKS, 8, MOBA_BLOCK), F32)],
        compiler_params=_params(("parallel", "arbitrary")),
        name="moba_attn",
    )(qk3, qk3, vt3, kmt)


def _key_to_float(t):
    return lax.bitcast_convert_type(jnp.where(t >= 0, t, t ^ 0x7FFFFFFF), F32)


def _dsa_kernel(qi_ref, klo_ref, khi_ref, wt_ref, q_ref, k_ref, vt_ref, o_ref,
                sc_ref, jl_ref, m_ref, l_ref, acc_ref, qt_ref, qit_ref, hi_ref, lo_ref,
                *, topk, scale, idx_scale, tq, seq_bits):
    i = pl.program_id(1)
    nk = i + 1
    shape = (KEY_TILE, tq)
    krow = lax.broadcasted_iota(jnp.int32, shape, 0)
    qpos = i * tq + lax.broadcasted_iota(jnp.int32, shape, 1)

    wt = wt_ref[...]
    for p in range(IDX_HEADS // 2):
        qit_ref[p] = _transpose_bf16(qi_ref[:, p * LANES:(p + 1) * LANES])
    for h in range(DSA_HEADS):
        qt_ref[h] = _transpose_bf16(q_ref[:, h * HEAD_DIM:(h + 1) * HEAD_DIM])

    def score_tiles(kts):
        offs = [pl.multiple_of(kt * KEY_TILE, KEY_TILE) for kt in kts]
        keys = jnp.concatenate(
            [r[pl.ds(off, KEY_TILE), :] for off in offs for r in (klo_ref, khi_ref)], axis=0)
        accs = [jnp.zeros(shape, F32) for _ in kts]
        for p in range(IDX_HEADS // 2):
            s = jnp.dot(keys, qit_ref[p], preferred_element_type=F32)
            for n in range(len(kts)):
                lo = s[2 * n * KEY_TILE:(2 * n + 1) * KEY_TILE]
                hi = s[(2 * n + 1) * KEY_TILE:(2 * n + 2) * KEY_TILE]
                accs[n] = accs[n] + (jnp.maximum(lo, 0.0) * wt[2 * p:2 * p + 1, :]
                                     + jnp.maximum(hi, 0.0) * wt[2 * p + 1:2 * p + 2, :])
        for kt, acc in zip(kts, accs):
            kpos = kt * KEY_TILE + krow
            s = jnp.where(kpos <= qpos, acc * idx_scale, -jnp.inf)
            sc_ref[kt] = s
            bits = lax.bitcast_convert_type(s, jnp.int32)
            key = jnp.where(bits >= 0, bits, bits ^ 0x7FFFFFFF)
            hi_ref[kt] = lax.shift_right_arithmetic(key, 16).astype(jnp.int16)
            lo_ref[kt] = ((key & 0xFFFF) + I16_MIN).astype(jnp.int16)

    def score_body(t, carry):
        score_tiles((2 * t, 2 * t + 1))
        return carry

    lax.fori_loop(0, nk // 2, score_body, 0)

    @pl.when(nk % 2 == 1)
    def _():
        score_tiles((nk - 1,))

    def count(pred):
        def body(kt, acc):
            hit = jnp.where(pred(sc_ref[kt], kt * KEY_TILE + krow), 1.0, 0.0)
            return acc + jnp.sum(hit.reshape(KEY_TILE // 32, 32, tq), axis=0)
        part = lax.fori_loop(0, nk, body, jnp.zeros((32, tq), F32))
        return jnp.sum(part, axis=0, keepdims=True)

    def count16(ref, pred):
        def body(kt, acc):
            hit = jnp.where(pred(ref[kt]), jnp.int16(1), jnp.int16(0))
            parts = [hit[r:r + PACKED_ROWS, :] for r in range(0, KEY_TILE, PACKED_ROWS)]
            while len(parts) > 1:
                parts = [a + b for a, b in zip(parts[::2], parts[1::2])]
            return acc + parts[0]
        part = lax.fori_loop(0, nk, body, jnp.zeros((PACKED_ROWS, tq), jnp.int16))
        return jnp.sum(part.astype(jnp.int32), axis=0, keepdims=True)

    def radix16(ref, need):
        c0 = count16(ref, lambda x: x >= jnp.int16(0))
        t0 = jnp.where(c0 >= need, 0, I16_MIN).astype(jnp.int32)

        def bit_body(b, t):
            cand = t | lax.shift_left(jnp.int32(1), 14 - b)
            c16 = cand.astype(jnp.int16)
            c = count16(ref, lambda x: x >= c16)
            return jnp.where(c >= need, cand, t)

        return lax.fori_loop(0, 15, bit_body, t0)

    need_select = nk * tq > topk

    @pl.when(jnp.logical_not(need_select))
    def _():
        def body(kt, carry):
            sc_ref[kt] = jnp.where(kt * KEY_TILE + krow <= qpos, 0.0, NEG_BIG)
            return carry
        lax.fori_loop(0, nk, body, 0)

    @pl.when(need_select)
    def _():
        kf = jnp.float32(topk)
        t_hi = radix16(hi_ref, jnp.full((1, tq), topk, jnp.int32))
        hi16 = t_hi.astype(jnp.int16)
        need_lo = topk - count16(hi_ref, lambda x: x > hi16)

        def mask_body(kt, carry):
            lo_ref[kt] = jnp.where(hi_ref[kt] == hi16, lo_ref[kt], jnp.int16(I16_MIN))
            return carry
        lax.fori_loop(0, nk, mask_body, 0)
        t_lo = radix16(lo_ref, need_lo)
        t = lax.shift_left(t_hi, 16) | ((t_lo - I16_MIN) & 0xFFFF)
        thr = _key_to_float(t)

        need = kf - count(lambda s, kp: s > thr)
        n_eq = count(lambda s, kp: s == thr)
        jl_ref[...] = jnp.full(jl_ref.shape, 2 ** seq_bits, jnp.int32)

        @pl.when(jnp.max(n_eq - need) > 0.0)
        def _():
            def idx_body(b, c):
                cand = c | lax.shift_left(jnp.int32(1), seq_bits - 1 - b)
                g = count(lambda s, kp: (s == thr) & (kp < cand))
                return jnp.where(g < need, cand, c)
            c = lax.fori_loop(0, seq_bits, idx_body, jnp.zeros((1, tq), jnp.int32))
            jl_ref[...] = jnp.broadcast_to(c, jl_ref.shape)

        def bias_body(kt, carry):
            s = sc_ref[kt]
            kpos = kt * KEY_TILE + krow
            sel = ((s > thr) | ((s == thr) & (kpos <= jl_ref[0:1, :]))) & (kpos <= qpos)
            sc_ref[kt] = jnp.where(sel, 0.0, NEG_BIG)
            return carry
        lax.fori_loop(0, nk, bias_body, 0)

    m_ref[...] = jnp.full(m_ref.shape, NEG_BIG, F32)
    l_ref[...] = jnp.zeros(l_ref.shape, F32)
    acc_ref[...] = jnp.zeros(acc_ref.shape, F32)

    c2 = scale * LOG2_E

    def tiles(kts):
        for h in range(DSA_HEADS):
            hs = slice(h * HEAD_DIM, (h + 1) * HEAD_DIM)
            m, l, acc = m_ref[h:h + 1, :], l_ref[h:h + 1, :], acc_ref[h]
            for kt in kts:
                off = pl.multiple_of(kt * KEY_TILE, KEY_TILE)
                for u in range(KEY_TILE // SUB_TILE):
                    us = slice(u * SUB_TILE, (u + 1) * SUB_TILE)
                    st = jnp.dot(k_ref[pl.ds(off + u * SUB_TILE, SUB_TILE), hs], qt_ref[h],
                                 preferred_element_type=F32) * c2 + sc_ref[kt, us, :]
                    m, l, acc = _flash_step(st, m, l, acc, vt_ref[kt, hs, us])
            m_ref[h:h + 1, :] = m
            l_ref[h:h + 1, :] = l
            acc_ref[h] = acc

    def attn_body(t, carry):
        tiles((2 * t, 2 * t + 1))
        return carry

    lax.fori_loop(0, nk // 2, attn_body, 0)

    @pl.when(nk % 2 == 1)
    def _():
        tiles((nk - 1,))

    _flash_finish(o_ref, l_ref, acc_ref, DSA_HEADS)


def _dsa_attn(qi3, klo3, khi3, wit, qk3, vt3, topk):
    b, t, _ = qk3.shape
    tq = KEY_TILE
    nq = t // tq
    seq_bits = max(1, (t - 1).bit_length())
    one = pl.Buffered(1)
    kern = functools.partial(
        _dsa_kernel, topk=topk, scale=HEAD_DIM ** -0.5,
        idx_scale=(IDX_DIM ** -0.5) * (IDX_HEADS ** -0.5), tq=tq, seq_bits=seq_bits)
    return pl.pallas_call(
        kern,
        grid=(b, nq),
        in_specs=[pl.BlockSpec((None, tq, IDX_W), lambda bi, i: (bi, i, 0)),
                  pl.BlockSpec((None, t, LANES), lambda bi, i: (bi, 0, 0), pipeline_mode=one),
                  pl.BlockSpec((None, t, LANES), lambda bi, i: (bi, 0, 0), pipeline_mode=one),
                  pl.BlockSpec((None, IDX_HEADS, tq), lambda bi, i: (bi * nq + i, IDX_DIM // IDX_HEADS, 0)),
                  pl.BlockSpec((None, tq, DSA_W), lambda bi, i: (bi, i, 2)),
                  pl.BlockSpec((None, t, DSA_W), lambda bi, i: (bi, 0, 3), pipeline_mode=one),
                  pl.BlockSpec((nq, DSA_W, KEY_TILE), lambda bi, i: (bi, 1, 0), pipeline_mode=one)],
        out_specs=pl.BlockSpec((None, tq, DSA_W), lambda bi, i: (bi, i, 0)),
        out_shape=jax.ShapeDtypeStruct((b, t, DSA_W), BF16),
        scratch_shapes=[pltpu.VMEM((nq, KEY_TILE, tq), F32),
                        pltpu.VMEM((8, tq), jnp.int32),
                        pltpu.VMEM((DSA_HEADS, tq), F32),
                        pltpu.VMEM((DSA_HEADS, tq), F32),
                        pltpu.VMEM((DSA_HEADS, HEAD_DIM, tq), F32),
                        pltpu.VMEM((DSA_HEADS, HEAD_DIM, tq), BF16),
                        pltpu.VMEM((IDX_HEADS // 2, LANES, tq), BF16),
                        pltpu.VMEM((nq, KEY_TILE, tq), jnp.int16),
                        pltpu.VMEM((nq, KEY_TILE, tq), jnp.int16)],
        compiler_params=_params(("parallel", "arbitrary")),
        name="dsa_attn",
    )(qi3, klo3, khi3, wit, qk3, qk3, vt3)


def _mix_out_kernel(oa_ref, ob_ref, sg_ref, wa_ref, wb_ref, wo_ref, x_ref, gt_ref, gpost_ref,
                    gpre_ref, sc_ref, sh_ref, x1_ref, h2_ref, *, d):
    ya = jnp.dot(oa_ref[...], wa_ref[...], preferred_element_type=F32)
    yb = jnp.dot(ob_ref[...], wb_ref[...], preferred_element_type=F32)
    z = sg_ref[:, :d].astype(F32) * ya + sg_ref[:, d:].astype(F32) * yb
    y = jnp.dot(z.astype(BF16), wo_ref[...], preferred_element_type=F32)
    x1 = x_ref[...] + gt_ref[...] * (_rms(y) * gpost_ref[...])
    x1_ref[...] = x1
    h2_ref[...] = ((_rms(x1) * gpre_ref[...]) * (1.0 + sc_ref[...]) + sh_ref[...]).astype(h2_ref.dtype)


def _mix_out(oa3, ob3, sg3, wa, wb, wo, x, gt1, g_post, g_pre2, sc2, sh2):
    b, t, d = x.shape
    tm = _tile(t, 256)
    one = pl.Buffered(1)
    row = lambda w: pl.BlockSpec((None, tm, w), lambda bi, i: (bi, i, 0))
    full = lambda a: pl.BlockSpec(a.shape, lambda bi, i: (0, 0), pipeline_mode=one)
    vec = pl.BlockSpec((None, 1, d), lambda bi, i: (bi, 0, 0))
    gain = pl.BlockSpec((1, d), lambda bi, i: (0, 0))
    return pl.pallas_call(
        functools.partial(_mix_out_kernel, d=d),
        grid=(b, t // tm),
        in_specs=[row(MOBA_W), row(DSA_W), row(2 * d), full(wa), full(wb), full(wo), row(d),
                  vec, gain, gain, vec, vec],
        out_specs=[row(d), row(d)],
        out_shape=[jax.ShapeDtypeStruct((b, t, d), F32), jax.ShapeDtypeStruct((b, t, d), BF16)],
        compiler_params=_params(("parallel", "parallel")),
        name="mix_out",
    )(oa3, ob3, sg3, wa, wb, wo, x, gt1, g_post, g_pre2, sc2, sh2)


def _ffn_kernel(h_ref, w1_ref, w2_ref, x1_ref, gt_ref, g_ref, o_ref, acc_ref):
    j = pl.program_id(2)

    @pl.when(j == 0)
    def _():
        acc_ref[...] = jnp.zeros(acc_ref.shape, F32)

    for rows in _row_chunks(h_ref, KEY_TILE):
        u = jnp.dot(h_ref[rows, :], w1_ref[...], preferred_element_type=F32)
        u = jnp.square(jnp.maximum(u, 0.0)).astype(BF16)
        acc_ref[rows, :] += jnp.dot(u, w2_ref[...], preferred_element_type=F32)

    @pl.when(j == pl.num_programs(2) - 1)
    def _():
        o_ref[...] = x1_ref[...] + gt_ref[...] * (_rms(acc_ref[...]) * g_ref[...])


def _ffn(h2, w1, w2, x1, gt2, g_post):
    b, t, d = x1.shape
    ff = w1.shape[1]
    tm = _tile(t, 512)
    tf = _tile(ff, 1024)
    row = lambda: pl.BlockSpec((None, tm, d), lambda bi, i, j: (bi, i, 0))
    return pl.pallas_call(
        _ffn_kernel,
        grid=(b, t // tm, ff // tf),
        in_specs=[row(),
                  pl.BlockSpec((d, tf), lambda bi, i, j: (0, j)),
                  pl.BlockSpec((tf, d), lambda bi, i, j: (j, 0)),
                  row(),
                  pl.BlockSpec((None, 1, d), lambda bi, i, j: (bi, 0, 0)),
                  pl.BlockSpec((1, d), lambda bi, i, j: (0, 0))],
        out_specs=row(),
        out_shape=jax.ShapeDtypeStruct((b, t, d), F32),
        scratch_shapes=[pltpu.VMEM((tm, d), F32)],
        compiler_params=_params(("parallel", "parallel", "arbitrary")),
        name="ffn",
    )(h2, w1, w2, x1, gt2, g_post)


def _rope_tables(t):
    pos = jnp.arange(t, dtype=F32)[:, None]
    lane = jnp.arange(LANES)[None, :]

    def cos_sin(half):
        inv_freq = jnp.power(ROPE_THETA, -jnp.arange(half, dtype=F32) / half)
        ang = pos * inv_freq[None, :]
        reps = LANES // half
        return jnp.tile(jnp.cos(ang), (1, reps)), jnp.tile(jnp.sin(ang), (1, reps))

    cos128, sin128 = cos_sin(HEAD_DIM // 2)
    sin128 = jnp.where(lane < HEAD_DIM // 2, -sin128, sin128)
    cos64, sin64 = cos_sin(IDX_DIM // 2)
    low = (lane % IDX_DIM) < IDX_DIM // 2
    sa64 = jnp.where(low, -sin64, 0.0)
    sb64 = jnp.where(low, 0.0, sin64)
    is_key = lane < IDX_DIM
    kw = (jnp.where(is_key, cos64, 1.0), jnp.where(is_key, sa64, 0.0), jnp.where(is_key, sb64, 0.0))
    return (cos128, sin128), (cos64, sa64, sb64), kw


def kernel(x, c, w_ada, b_ada, g_pre_mix, g_post_mix, w_in, w_moba_out, w_dsa_out, w_o,
           g_pre_ffn, g_post_ffn, w_ff1, w_ff2):
    b, t, d = x.shape
    m = b * t
    nb = t // MOBA_BLOCK
    assert t % MOBA_BLOCK == 0 and nb <= MAX_BLOCKS and d % LANES == 0 and b <= 16
    n_sel = max(1, min(MOBA_TOPK, nb - 1))
    topk = min(DSA_TOPK_MAX, t // 4)
    rope128, rope64, rope_kw = _rope_tables(t)
    c_pad = jnp.zeros((16, d), F32).at[:b].set(c)

    for l in range(w_ada.shape[0]):
        mod = _ada(c_pad, w_ada[l], b_ada[l][None, :])[:b]
        sh1, sc1, gt1, sh2, sc2, gt2 = [v[:, None, :] for v in jnp.split(mod, 6, axis=-1)]

        h = _norm_mod(x, g_pre_mix[l][None, :], sc1, sh1).reshape(m, d)
        qk, vt3, qi, klo, khi, wit, sg = _in_proj(h, jnp.swapaxes(w_in, 1, 2), l, t, rope128, rope64, rope_kw)

        qk3 = qk.reshape(b, t, -1)
        kmt = _moba_kmean(qk3, nb)
        oa = _moba_attn(qk3, vt3, kmt, nb, n_sel)
        ob = _dsa_attn(qi.reshape(b, t, -1), klo.reshape(b, t, -1), khi.reshape(b, t, -1),
                       wit, qk3, vt3, topk)

        x, h2 = _mix_out(oa, ob, sg.reshape(b, t, -1), w_moba_out[l].astype(BF16),
                         w_dsa_out[l].astype(BF16), w_o[l].astype(BF16), x, gt1,
                         g_post_mix[l][None, :], g_pre_ffn[l][None, :], sc2, sh2)
        x = _ffn(h2, w_ff1[l].astype(BF16), w_ff2[l].astype(BF16), x, gt2, g_post_ffn[l][None, :])
    return x
```

```python
import functools

import jax
import jax.numpy as jnp
from jax import lax
from jax.experimental import pallas as pl
from jax.experimental.pallas import tpu as pltpu

HEAD_DIM = 128
MOBA_HEADS = 8
MOBA_BLOCK = 256
MOBA_TOPK = 3
DSA_HEADS = 8
IDX_HEADS = 16
IDX_DIM = 64
DSA_TOPK_MAX = 256
ROPE_THETA = 10000.0
RMS_EPS = 1e-6

MOBA_W = MOBA_HEADS * HEAD_DIM
DSA_W = DSA_HEADS * HEAD_DIM
IDX_W = IDX_HEADS * IDX_DIM
LANES = 128
MAX_BLOCKS = LANES // MOBA_HEADS
KEY_TILE = 256
SUB_TILE = 128
ROW_CHUNK = 512
LOG2_E = 1.4426950408889634
MASK_BIAS = -30000.0
NEG_BIG = -1e30
BISECT_FIXED = 18
VMEM_LIMIT = 56 * 1024 * 1024

F32 = jnp.float32
BF16 = jnp.bfloat16
_NT = (((1,), (1,)), ((), ()))


def _params(sem):
    return pltpu.CompilerParams(dimension_semantics=sem, vmem_limit_bytes=VMEM_LIMIT)


def _tile(n, pref):
    if n <= pref:
        return n
    t = pref - pref % LANES
    while t >= LANES:
        if n % t == 0:
            return t
        t -= LANES
    return n


def _rms(x):
    return x * lax.rsqrt(jnp.mean(x * x, axis=-1, keepdims=True) + RMS_EPS)


def _ada_kernel(c_ref, w_ref, b_ref, o_ref):
    c = c_ref[...]
    cs = (c * jax.nn.sigmoid(c)).astype(BF16)
    o_ref[...] = jnp.dot(cs, w_ref[...].astype(BF16), preferred_element_type=F32) + b_ref[...]


def _ada(c_pad, w, b):
    rows, d = c_pad.shape
    n = w.shape[1]
    tn = _tile(n, 1024)
    return pl.pallas_call(
        _ada_kernel,
        grid=(n // tn,),
        in_specs=[pl.BlockSpec((rows, d), lambda j: (0, 0)),
                  pl.BlockSpec((d, tn), lambda j: (0, j)),
                  pl.BlockSpec((1, tn), lambda j: (0, j))],
        out_specs=pl.BlockSpec((rows, tn), lambda j: (0, j)),
        out_shape=jax.ShapeDtypeStruct((rows, n), F32),
        compiler_params=_params(("parallel",)),
        name="ada_mod",
    )(c_pad, w, b)


def _norm_mod_kernel(x_ref, g_ref, sc_ref, sh_ref, o_ref):
    y = _rms(x_ref[...])
    o_ref[...] = ((y * g_ref[...]) * (1.0 + sc_ref[...]) + sh_ref[...]).astype(o_ref.dtype)


def _norm_mod(x, g, sc, sh):
    b, t, d = x.shape
    tt = _tile(t, 512)
    vec = pl.BlockSpec((None, 1, d), lambda bi, ti: (bi, 0, 0))
    return pl.pallas_call(
        _norm_mod_kernel,
        grid=(b, t // tt),
        in_specs=[pl.BlockSpec((None, tt, d), lambda bi, ti: (bi, ti, 0)),
                  pl.BlockSpec((1, d), lambda bi, ti: (0, 0)), vec, vec],
        out_specs=pl.BlockSpec((None, tt, d), lambda bi, ti: (bi, ti, 0)),
        out_shape=jax.ShapeDtypeStruct((b, t, d), BF16),
        compiler_params=_params(("parallel", "parallel")),
        name="norm_mod",
    )(x, g, sc, sh)


def _first_token_tile():
    return pl.program_id(1) == 0


def _rope64(y, cos, sin_a, sin_b):
    return y * cos + pltpu.roll(y, LANES - 32, 1) * sin_a + pltpu.roll(y, 32, 1) * sin_b


def _row_chunks(ref, step=ROW_CHUNK):
    n = ref.shape[0]
    step = min(n, step)
    return [slice(r, r + step) for r in range(0, n, step)]


def _wp_rope128_kernel(h_ref, w_ref, cos_ref, sin_ref, o_ref, wb_ref):
    @pl.when(_first_token_tile())
    def _():
        wb_ref[...] = w_ref[...].astype(BF16)

    for rows in _row_chunks(h_ref):
        acc = lax.dot_general(h_ref[rows, :], wb_ref[...], _NT, preferred_element_type=F32)
        cos = cos_ref[rows, :]
        sin = sin_ref[rows, :]
        for g in range(acc.shape[1] // LANES):
            y = acc[:, g * LANES:(g + 1) * LANES]
            o_ref[rows, g * LANES:(g + 1) * LANES] = (
                y * cos + pltpu.roll(y, LANES // 2, 1) * sin).astype(o_ref.dtype)


def _wp_rope64_kernel(h_ref, w_ref, cos_ref, sa_ref, sb_ref, o_ref, wb_ref):
    @pl.when(_first_token_tile())
    def _():
        wb_ref[...] = w_ref[...].astype(BF16)

    for rows in _row_chunks(h_ref):
        acc = lax.dot_general(h_ref[rows, :], wb_ref[...], _NT, preferred_element_type=F32)
        cos, sa, sb = cos_ref[rows, :], sa_ref[rows, :], sb_ref[rows, :]
        for g in range(acc.shape[1] // LANES):
            y = acc[:, g * LANES:(g + 1) * LANES]
            o_ref[rows, g * LANES:(g + 1) * LANES] = _rope64(y, cos, sa, sb).astype(o_ref.dtype)


def _wp_vt_kernel(h_ref, w_ref, o_ref, wb_ref):
    @pl.when(_first_token_tile())
    def _():
        wb_ref[...] = w_ref[...].astype(BF16)

    for c in range(o_ref.shape[0]):
        r = lax.dot_general(wb_ref[...], h_ref[c * KEY_TILE:(c + 1) * KEY_TILE, :], _NT,
                            preferred_element_type=F32)
        o_ref[c] = r.astype(o_ref.dtype)


def _wp_gate_kernel(h_ref, wa_ref, wb_ref, o_ref, wg_ref, *, shift):
    @pl.when(_first_token_tile())
    def _():
        keep = wg_ref.shape[0] - shift
        wg_ref[:keep, :] = wa_ref[shift:, :].astype(BF16)
        wg_ref[keep:, :] = wb_ref[:shift, :].astype(BF16)

    for rows in _row_chunks(h_ref):
        acc = lax.dot_general(h_ref[rows, :], wg_ref[...], _NT, preferred_element_type=F32)
        o_ref[rows, :] = jax.nn.sigmoid(acc).astype(o_ref.dtype)


def _wp_kw_kernel(h_ref, w_ref, cos_ref, sa_ref, sb_ref, klo_ref, khi_ref, wit_ref, wb_ref):
    @pl.when(pl.program_id(0) == 0)
    def _():
        r = lax.broadcasted_iota(jnp.int32, w_ref.shape, 0)
        wb_ref[...] = jnp.where(r < IDX_DIM + IDX_HEADS, w_ref[...], 0.0).astype(BF16)

    h = h_ref[...]
    y = lax.dot_general(h, wb_ref[...], _NT, preferred_element_type=F32)
    r = _rope64(y, cos_ref[...], sa_ref[...], sb_ref[...])
    lane = lax.broadcasted_iota(jnp.int32, y.shape, 1)
    klo = jnp.where(lane < IDX_DIM, r, 0.0)
    klo_ref[...] = klo.astype(BF16)
    khi_ref[...] = pltpu.roll(klo, IDX_DIM, 1).astype(BF16)
    yt = lax.dot_general(wb_ref[...], h, _NT, preferred_element_type=F32)
    for c in range(wit_ref.shape[0]):
        wit_ref[c] = yt[:, c * KEY_TILE:(c + 1) * KEY_TILE]


def _in_proj(h, wt, l, t, tables128, tables64, tables_kw):
    m, k = h.shape
    d = k
    tm = _tile(t, 1024)
    nt = t // tm
    ni = m // tm
    assert tm % KEY_TILE == 0
    tn = 1024
    o_dsa = 3 * MOBA_W
    o_qi = o_dsa + 3 * DSA_W
    o_kw = o_qi + IDX_W
    o_g = o_kw + IDX_DIM + IDX_HEADS
    sem = ("arbitrary", "arbitrary")
    h_spec = pl.BlockSpec((tm, k), lambda j, i: (i, 0))
    tab = pl.BlockSpec((tm, LANES), lambda j, i: (i % nt, 0))
    row = pl.BlockSpec((tm, tn), lambda j, i: (i, j))

    def wrows(first_blocks, n_first, second_start):
        return pl.BlockSpec((None, tn, k), lambda j, i: (l, jnp.where(j < n_first, first_blocks + j,
                                                                      second_start + j - n_first), 0))

    n_m, n_d = 2 * MOBA_W // tn, 2 * DSA_W // tn
    qk = pl.pallas_call(
        _wp_rope128_kernel,
        grid=(n_m + n_d, ni),
        in_specs=[h_spec, wrows(0, n_m, o_dsa // tn), tab, tab],
        out_specs=row,
        out_shape=jax.ShapeDtypeStruct((m, 2 * MOBA_W + 2 * DSA_W), BF16),
        scratch_shapes=[pltpu.VMEM((tn, k), BF16)],
        compiler_params=_params(sem),
        name="proj_qk",
    )(h, wt, *tables128)

    n_m, n_d = MOBA_W // tn, DSA_W // tn
    vt = pl.pallas_call(
        _wp_vt_kernel,
        grid=(n_m + n_d, ni),
        in_specs=[h_spec, wrows(2 * MOBA_W // tn, n_m, (o_dsa + 2 * DSA_W) // tn)],
        out_specs=pl.BlockSpec((tm // KEY_TILE, tn, KEY_TILE), lambda j, i: (i, j, 0)),
        out_shape=jax.ShapeDtypeStruct((m // KEY_TILE, MOBA_W + DSA_W, KEY_TILE), BF16),
        scratch_shapes=[pltpu.VMEM((tn, k), BF16)],
        compiler_params=_params(sem),
        name="proj_v",
    )(h, wt)

    n_q = IDX_W // tn
    qi = pl.pallas_call(
        _wp_rope64_kernel,
        grid=(n_q, ni),
        in_specs=[h_spec, wrows(o_qi // tn, n_q, 0), tab, tab, tab],
        out_specs=row,
        out_shape=jax.ShapeDtypeStruct((m, IDX_W), BF16),
        scratch_shapes=[pltpu.VMEM((tn, k), BF16)],
        compiler_params=_params(sem),
        name="proj_qi",
    )(h, wt, *tables64)

    tab1 = pl.BlockSpec((tm, LANES), lambda i: (i % nt, 0))
    row1 = pl.BlockSpec((tm, LANES), lambda i: (i, 0))
    klo, khi, wit = pl.pallas_call(
        _wp_kw_kernel,
        grid=(ni,),
        in_specs=[pl.BlockSpec((tm, k), lambda i: (i, 0)),
                  pl.BlockSpec((None, LANES, k), lambda i: (l, o_kw // LANES, 0)), tab1, tab1, tab1],
        out_specs=[row1, row1, pl.BlockSpec((tm // KEY_TILE, LANES, KEY_TILE), lambda i: (i, 0, 0))],
        out_shape=[jax.ShapeDtypeStruct((m, LANES), BF16), jax.ShapeDtypeStruct((m, LANES), BF16),
                   jax.ShapeDtypeStruct((m // KEY_TILE, LANES, KEY_TILE), F32)],
        scratch_shapes=[pltpu.VMEM((LANES, k), BF16)],
        compiler_params=_params(("arbitrary",)),
        name="proj_kw",
    )(h, wt, *tables_kw)

    tg = _tile(2 * d, tn)
    g0 = o_kw // tg
    shift = o_g - o_kw
    assert o_kw % tg == 0 and shift <= LANES and tg % LANES == 0
    sg = pl.pallas_call(
        functools.partial(_wp_gate_kernel, shift=shift),
        grid=(2 * d // tg, ni),
        in_specs=[h_spec,
                  pl.BlockSpec((None, tg, k), lambda j, i: (l, g0 + j, 0)),
                  pl.BlockSpec((None, LANES, k), lambda j, i: (l, (g0 + j + 1) * (tg // LANES), 0))],
        out_specs=pl.BlockSpec((tm, tg), lambda j, i: (i, j)),
        out_shape=jax.ShapeDtypeStruct((m, 2 * d), BF16),
        scratch_shapes=[pltpu.VMEM((tg, k), BF16)],
        compiler_params=_params(sem),
        name="proj_gate",
    )(h, wt, wt)
    return qk, vt, qi, klo, khi, wit, sg


def _moba_kmean_kernel(k_ref, o_ref, km_ref, *, nb):
    km_ref[...] = jnp.zeros(km_ref.shape, F32)
    for j in range(nb):
        blk = k_ref[j * MOBA_BLOCK:(j + 1) * MOBA_BLOCK, :].astype(F32)
        km_ref[j:j + 1, :] = jnp.sum(blk, axis=0, keepdims=True) * (1.0 / MOBA_BLOCK)
    km = km_ref[...]
    tiled = jnp.concatenate([km] * MOBA_HEADS, axis=0)
    r = lax.shift_right_logical(lax.broadcasted_iota(jnp.int32, tiled.shape, 0), 4)
    c = lax.shift_right_logical(lax.broadcasted_iota(jnp.int32, tiled.shape, 1), 7)
    o_ref[...] = jnp.where(r == c, tiled, 0.0).astype(o_ref.dtype)


def _moba_kmean(qk3, nb):
    b, t, _ = qk3.shape
    return pl.pallas_call(
        functools.partial(_moba_kmean_kernel, nb=nb),
        grid=(b,),
        in_specs=[pl.BlockSpec((None, t, MOBA_W), lambda bi: (bi, 0, 1))],
        out_specs=pl.BlockSpec((None, LANES, MOBA_W), lambda bi: (bi, 0, 0)),
        out_shape=jax.ShapeDtypeStruct((b, LANES, MOBA_W), BF16),
        scratch_shapes=[pltpu.VMEM((MAX_BLOCKS, MOBA_W), F32)],
        compiler_params=_params(("parallel",)),
        name="moba_kmean",
    )(qk3)


def _moba_block_bias(q_ref, kmt_ref, sbt_ref, i, n_sel):
    g = lax.dot_general(kmt_ref[...], q_ref[...], _NT, preferred_element_type=F32)
    shape = (MAX_BLOCKS, g.shape[1])
    j = lax.broadcasted_iota(jnp.int32, shape, 0)
    past = j < i
    for h in range(MOBA_HEADS):
        gm = jnp.where(past, g[h * MAX_BLOCKS:(h + 1) * MAX_BLOCKS, :], -jnp.inf)
        rank = jnp.zeros(shape, F32)
        for jo in range(MAX_BLOCKS):
            other = gm[jo:jo + 1, :]
            beats = (other > gm) | ((other == gm) & (jo < j))
            rank = rank + jnp.where(beats, 1.0, 0.0)
        bias = jnp.where(past & (rank < n_sel), 0.0, MASK_BIAS)
        for jb in range(MAX_BLOCKS):
            sbt_ref[h, jb] = jnp.broadcast_to(bias[jb:jb + 1, :], (8, shape[1]))


def _flash_step(st, m, l, acc_t, v_t):
    m_new = jnp.maximum(m, jnp.max(st, axis=0, keepdims=True))
    alpha = jnp.exp2(m - m_new)
    p = jnp.exp2(st - m_new)
    l = alpha * l + jnp.sum(p, axis=0, keepdims=True)
    acc_t = alpha * acc_t + jnp.dot(v_t, p.astype(BF16), preferred_element_type=F32)
    return m_new, l, acc_t


def _flash_step_joint(sts, m, l, acc_t, v_ts):
    m_new = m
    for st in sts:
        m_new = jnp.maximum(m_new, jnp.max(st, axis=0, keepdims=True))
    alpha = jnp.exp2(m - m_new)
    l = alpha * l
    acc_t = alpha * acc_t
    for st, v_t in zip(sts, v_ts):
        p = jnp.exp2(st - m_new)
        l = l + jnp.sum(p, axis=0, keepdims=True)
        acc_t = acc_t + jnp.dot(v_t, p.astype(BF16), preferred_element_type=F32)
    return m_new, l, acc_t


def _transpose_bf16(x):
    return x.astype(F32).T.astype(BF16)


def _flash_finish(o_ref, l_ref, acc_ref, heads):
    for h in range(heads):
        hs = slice(h * HEAD_DIM, (h + 1) * HEAD_DIM)
        o_ref[:, hs] = (acc_ref[h] / l_ref[h:h + 1, :]).T.astype(o_ref.dtype)


def _moba_attn_kernel(q_ref, k_ref, vt_ref, kmt_ref, o_ref, m_ref, l_ref, acc_ref, qa_ref, sbt_ref,
                      *, scale, n_sel):
    i = pl.program_id(1)
    own = pl.multiple_of(i * MOBA_BLOCK, MOBA_BLOCK)
    c2 = scale * LOG2_E
    sub_shape = (SUB_TILE, MOBA_BLOCK)
    kidx = lax.broadcasted_iota(jnp.int32, sub_shape, 0)
    qidx = lax.broadcasted_iota(jnp.int32, sub_shape, 1)
    _moba_block_bias(q_ref, kmt_ref, sbt_ref, i, n_sel)
    for h in range(MOBA_HEADS):
        hs = slice(h * HEAD_DIM, (h + 1) * HEAD_DIM)
        qa_ref[h] = _transpose_bf16(q_ref[:, hs])
        m = jnp.full((1, MOBA_BLOCK), NEG_BIG, F32)
        l = jnp.zeros((1, MOBA_BLOCK), F32)
        acc = jnp.zeros((HEAD_DIM, MOBA_BLOCK), F32)
        for u in range(MOBA_BLOCK // SUB_TILE):
            st = jnp.dot(k_ref[pl.ds(own + u * SUB_TILE, SUB_TILE), hs], qa_ref[h],
                         preferred_element_type=F32) * c2
            st = jnp.where(kidx + u * SUB_TILE <= qidx, st, -jnp.inf)
            m, l, acc = _flash_step(st, m, l, acc, vt_ref[i, hs, u * SUB_TILE:(u + 1) * SUB_TILE])
        m_ref[h:h + 1, :] = m
        l_ref[h:h + 1, :] = l
        acc_ref[h] = acc

    def blocks(js):
        for h in range(MOBA_HEADS):
            hs = slice(h * HEAD_DIM, (h + 1) * HEAD_DIM)
            m, l, acc = m_ref[h:h + 1, :], l_ref[h:h + 1, :], acc_ref[h]
            for j in js:
                off = pl.multiple_of(j * MOBA_BLOCK, MOBA_BLOCK)
                bias = sbt_ref[h, j][0:1, :]
                sts, vts = [], []
                for u in range(MOBA_BLOCK // SUB_TILE):
                    sts.append(jnp.dot(k_ref[pl.ds(off + u * SUB_TILE, SUB_TILE), hs], qa_ref[h],
                                       preferred_element_type=F32) * c2 + bias)
                    vts.append(vt_ref[j, hs, u * SUB_TILE:(u + 1) * SUB_TILE])
                m, l, acc = _flash_step_joint(sts, m, l, acc, vts)
            m_ref[h:h + 1, :] = m
            l_ref[h:h + 1, :] = l
            acc_ref[h] = acc

    def body(t, carry):
        blocks((2 * t, 2 * t + 1))
        return carry

    lax.fori_loop(0, i // 2, body, 0)

    @pl.when(i % 2 == 1)
    def _():
        blocks((i - 1,))

    _flash_finish(o_ref, l_ref, acc_ref, MOBA_HEADS)


def _moba_attn(qk3, vt3, kmt, nb, n_sel):
    b, t, _ = qk3.shape
    one = pl.Buffered(1)
    return pl.pallas_call(
        functools.partial(_moba_attn_kernel, scale=HEAD_DIM ** -0.5, n_sel=n_sel),
        grid=(b, nb),
        in_specs=[pl.BlockSpec((None, MOBA_BLOCK, MOBA_W), lambda bi, i: (bi, i, 0)),
                  pl.BlockSpec((None, t, MOBA_W), lambda bi, i: (bi, 0, 1), pipeline_mode=one),
                  pl.BlockSpec((nb, MOBA_W, KEY_TILE), lambda bi, i: (bi, 0, 0), pipeline_mode=one),
                  pl.BlockSpec((None, LANES, MOBA_W), lambda bi, i: (bi, 0, 0), pipeline_mode=one)],
        out_specs=pl.BlockSpec((None, MOBA_BLOCK, MOBA_W), lambda bi, i: (bi, i, 0)),
        out_shape=jax.ShapeDtypeStruct((b, t, MOBA_W), BF16),
        scratch_shapes=[pltpu.VMEM((MOBA_HEADS, MOBA_BLOCK), F32),
                        pltpu.VMEM((MOBA_HEADS, MOBA_BLOCK), F32),
                        pltpu.VMEM((MOBA_HEADS, HEAD_DIM, MOBA_BLOCK), F32),
                        pltpu.VMEM((MOBA_HEADS, HEAD_DIM, MOBA_BLOCK), BF16),
                        pltpu.VMEM((MOBA_HEADS, MAX_BLOCKS, 8, MOBA_BLOCK), F32)],
        compiler_params=_params(("parallel", "arbitrary")),
        name="moba_attn",
    )(qk3, qk3, vt3, kmt)


def _float_to_key(x):
    b = lax.bitcast_convert_type(x, jnp.int32)
    return jnp.where(b >= 0, b, b ^ 0x7FFFFFFF)


def _key_to_float(t):
    return lax.bitcast_convert_type(jnp.where(t >= 0, t, t ^ 0x7FFFFFFF), F32)


def _dsa_kernel(qi_ref, klo_ref, khi_ref, wt_ref, q_ref, k_ref, vt_ref, o_ref,
                sc_ref, jl_ref, m_ref, l_ref, acc_ref, qt_ref, qit_ref,
                *, topk, scale, idx_scale, tq, seq_bits):
    i = pl.program_id(1)
    nk = i + 1
    shape = (KEY_TILE, tq)
    krow = lax.broadcasted_iota(jnp.int32, shape, 0)
    qpos = i * tq + lax.broadcasted_iota(jnp.int32, shape, 1)

    wt = wt_ref[...]
    for p in range(IDX_HEADS // 2):
        qit_ref[p] = _transpose_bf16(qi_ref[:, p * LANES:(p + 1) * LANES])
    for h in range(DSA_HEADS):
        qt_ref[h] = _transpose_bf16(q_ref[:, h * HEAD_DIM:(h + 1) * HEAD_DIM])

    def score_tiles(kts):
        offs = [pl.multiple_of(kt * KEY_TILE, KEY_TILE) for kt in kts]
        keys = jnp.concatenate(
            [r[pl.ds(off, KEY_TILE), :] for off in offs for r in (klo_ref, khi_ref)], axis=0)
        accs = [jnp.zeros(shape, F32) for _ in kts]
        for p in range(IDX_HEADS // 2):
            s = jnp.dot(keys, qit_ref[p], preferred_element_type=F32)
            for n in range(len(kts)):
                lo = s[2 * n * KEY_TILE:(2 * n + 1) * KEY_TILE]
                hi = s[(2 * n + 1) * KEY_TILE:(2 * n + 2) * KEY_TILE]
                accs[n] = accs[n] + (jnp.maximum(lo, 0.0) * wt[2 * p:2 * p + 1, :]
                                     + jnp.maximum(hi, 0.0) * wt[2 * p + 1:2 * p + 2, :])
        for kt, acc in zip(kts, accs):
            kpos = kt * KEY_TILE + krow
            sc_ref[kt] = jnp.where(kpos <= qpos, acc * idx_scale, -jnp.inf)

    def score_body(t, carry):
        score_tiles((2 * t, 2 * t + 1))
        return carry

    lax.fori_loop(0, nk // 2, score_body, 0)

    @pl.when(nk % 2 == 1)
    def _():
        score_tiles((nk - 1,))

    def count(pred):
        def body(kt, acc):
            hit = jnp.where(pred(sc_ref[kt], kt * KEY_TILE + krow), 1.0, 0.0)
            return acc + jnp.sum(hit.reshape(KEY_TILE // 32, 32, tq), axis=0)
        part = lax.fori_loop(0, nk, body, jnp.zeros((32, tq), F32))
        return jnp.sum(part, axis=0, keepdims=True)

    need_select = nk * tq > topk

    @pl.when(jnp.logical_not(need_select))
    def _():
        def body(kt, carry):
            sc_ref[kt] = jnp.where(kt * KEY_TILE + krow <= qpos, 0.0, NEG_BIG)
            return carry
        lax.fori_loop(0, nk, body, 0)

    @pl.when(need_select)
    def _():
        kf = jnp.float32(topk)

        def ext_body(kt, carry):
            s = sc_ref[kt]
            lo = jnp.min(jnp.where(s > -jnp.inf, s, jnp.inf).reshape(KEY_TILE // 32, 32, tq), axis=0)
            hi = jnp.max(s.reshape(KEY_TILE // 32, 32, tq), axis=0)
            return jnp.minimum(carry[0], lo), jnp.maximum(carry[1], hi)

        lo, hi = lax.fori_loop(0, nk, ext_body, (jnp.full((32, tq), jnp.inf, F32),
                                                 jnp.full((32, tq), -jnp.inf, F32)))
        lo = jnp.min(lo, axis=0, keepdims=True)
        hi = _key_to_float(_float_to_key(jnp.max(hi, axis=0, keepdims=True)) + 1)
        n_fin = count(lambda s, kp: s >= lo)
        done = jnp.where(n_fin <= kf, 1.0, 0.0)

        def step(state):
            lo, hi, clo, done = state
            mid = lo * 0.5 + hi * 0.5
            stuck = (mid <= lo) | (mid >= hi)
            c = count(lambda s, kp: s >= mid)
            up = (c >= kf) & (done < 0.5)
            down = (c < kf) & (done < 0.5)
            lo = jnp.where(up, mid, lo)
            clo = jnp.where(up, c, clo)
            hi = jnp.where(down, mid, hi)
            done = jnp.where(stuck | (clo == kf), 1.0, done)
            return lo, hi, clo, done

        state = lax.fori_loop(0, BISECT_FIXED, lambda _, st: step(st), (lo, hi, n_fin, done))
        state = lax.while_loop(lambda st: jnp.min(st[3]) < 0.5, lambda st: step(step(st)), state)
        thr = jnp.where(n_fin < kf, -jnp.inf, state[0])

        need = kf - count(lambda s, kp: s > thr)
        n_eq = count(lambda s, kp: s == thr)
        jl_ref[...] = jnp.full(jl_ref.shape, 2 ** seq_bits, jnp.int32)

        @pl.when(jnp.max(n_eq - need) > 0.0)
        def _():
            def idx_body(b, c):
                cand = c | lax.shift_left(jnp.int32(1), seq_bits - 1 - b)
                g = count(lambda s, kp: (s == thr) & (kp < cand))
                return jnp.where(g < need, cand, c)
            c = lax.fori_loop(0, seq_bits, idx_body, jnp.zeros((1, tq), jnp.int32))
            jl_ref[...] = jnp.broadcast_to(c, jl_ref.shape)

        def bias_body(kt, carry):
            s = sc_ref[kt]
            kpos = kt * KEY_TILE + krow
            sel = ((s > thr) | ((s == thr) & (kpos <= jl_ref[0:1, :]))) & (kpos <= qpos)
            sc_ref[kt] = jnp.where(sel, 0.0, NEG_BIG)
            return carry
        lax.fori_loop(0, nk, bias_body, 0)

    m_ref[...] = jnp.full(m_ref.shape, NEG_BIG, F32)
    l_ref[...] = jnp.zeros(l_ref.shape, F32)
    acc_ref[...] = jnp.zeros(acc_ref.shape, F32)

    c2 = scale * LOG2_E

    def tiles(kts):
        for h in range(DSA_HEADS):
            hs = slice(h * HEAD_DIM, (h + 1) * HEAD_DIM)
            m, l, acc = m_ref[h:h + 1, :], l_ref[h:h + 1, :], acc_ref[h]
            for kt in kts:
                off = pl.multiple_of(kt * KEY_TILE, KEY_TILE)
                for u in range(KEY_TILE // SUB_TILE):
                    us = slice(u * SUB_TILE, (u + 1) * SUB_TILE)
                    st = jnp.dot(k_ref[pl.ds(off + u * SUB_TILE, SUB_TILE), hs], qt_ref[h],
                                 preferred_element_type=F32) * c2 + sc_ref[kt, us, :]
                    m, l, acc = _flash_step(st, m, l, acc, vt_ref[kt, hs, us])
            m_ref[h:h + 1, :] = m
            l_ref[h:h + 1, :] = l
            acc_ref[h] = acc

    def attn_body(t, carry):
        tiles((2 * t, 2 * t + 1))
        return carry

    lax.fori_loop(0, nk // 2, attn_body, 0)

    @pl.when(nk % 2 == 1)
    def _():
        tiles((nk - 1,))

    _flash_finish(o_ref, l_ref, acc_ref, DSA_HEADS)


def _dsa_attn(qi3, klo3, khi3, wit, qk3, vt3, topk):
    b, t, _ = qk3.shape
    tq = KEY_TILE
    nq = t // tq
    seq_bits = max(1, (t - 1).bit_length())
    one = pl.Buffered(1)
    kern = functools.partial(
        _dsa_kernel, topk=topk, scale=HEAD_DIM ** -0.5,
        idx_scale=(IDX_DIM ** -0.5) * (IDX_HEADS ** -0.5), tq=tq, seq_bits=seq_bits)
    return pl.pallas_call(
        kern,
        grid=(b, nq),
        in_specs=[pl.BlockSpec((None, tq, IDX_W), lambda bi, i: (bi, i, 0)),
                  pl.BlockSpec((None, t, LANES), lambda bi, i: (bi, 0, 0), pipeline_mode=one),
                  pl.BlockSpec((None, t, LANES), lambda bi, i: (bi, 0, 0), pipeline_mode=one),
                  pl.BlockSpec((None, IDX_HEADS, tq), lambda bi, i: (bi * nq + i, IDX_DIM // IDX_HEADS, 0)),
                  pl.BlockSpec((None, tq, DSA_W), lambda bi, i: (bi, i, 2)),
                  pl.BlockSpec((None, t, DSA_W), lambda bi, i: (bi, 0, 3), pipeline_mode=one),
                  pl.BlockSpec((nq, DSA_W, KEY_TILE), lambda bi, i: (bi, 1, 0), pipeline_mode=one)],
        out_specs=pl.BlockSpec((None, tq, DSA_W), lambda bi, i: (bi, i, 0)),
        out_shape=jax.ShapeDtypeStruct((b, t, DSA_W), BF16),
        scratch_shapes=[pltpu.VMEM((nq, KEY_TILE, tq), F32),
                        pltpu.VMEM((8, tq), jnp.int32),
                        pltpu.VMEM((DSA_HEADS, tq), F32),
                        pltpu.VMEM((DSA_HEADS, tq), F32),
                        pltpu.VMEM((DSA_HEADS, HEAD_DIM, tq), F32),
                        pltpu.VMEM((DSA_HEADS, HEAD_DIM, tq), BF16),
                        pltpu.VMEM((IDX_HEADS // 2, LANES, tq), BF16)],
        compiler_params=_params(("parallel", "arbitrary")),
        name="dsa_attn",
    )(qi3, klo3, khi3, wit, qk3, qk3, vt3)


def _mix_out_kernel(oa_ref, ob_ref, sg_ref, wa_ref, wb_ref, wo_ref, x_ref, gt_ref, gpost_ref,
                    gpre_ref, sc_ref, sh_ref, x1_ref, h2_ref, *, d):
    ya = jnp.dot(oa_ref[...], wa_ref[...], preferred_element_type=F32)
    yb = jnp.dot(ob_ref[...], wb_ref[...], preferred_element_type=F32)
    z = sg_ref[:, :d].astype(F32) * ya + sg_ref[:, d:].astype(F32) * yb
    y = jnp.dot(z.astype(BF16), wo_ref[...], preferred_element_type=F32)
    x1 = x_ref[...] + gt_ref[...] * (_rms(y) * gpost_ref[...])
    x1_ref[...] = x1
    h2_ref[...] = ((_rms(x1) * gpre_ref[...]) * (1.0 + sc_ref[...]) + sh_ref[...]).astype(h2_ref.dtype)


def _mix_out(oa3, ob3, sg3, wa, wb, wo, x, gt1, g_post, g_pre2, sc2, sh2):
    b, t, d = x.shape
    tm = _tile(t, 256)
    one = pl.Buffered(1)
    row = lambda w: pl.BlockSpec((None, tm, w), lambda bi, i: (bi, i, 0))
    full = lambda a: pl.BlockSpec(a.shape, lambda bi, i: (0, 0), pipeline_mode=one)
    vec = pl.BlockSpec((None, 1, d), lambda bi, i: (bi, 0, 0))
    gain = pl.BlockSpec((1, d), lambda bi, i: (0, 0))
    return pl.pallas_call(
        functools.partial(_mix_out_kernel, d=d),
        grid=(b, t // tm),
        in_specs=[row(MOBA_W), row(DSA_W), row(2 * d), full(wa), full(wb), full(wo), row(d),
                  vec, gain, gain, vec, vec],
        out_specs=[row(d), row(d)],
        out_shape=[jax.ShapeDtypeStruct((b, t, d), F32), jax.ShapeDtypeStruct((b, t, d), BF16)],
        compiler_params=_params(("parallel", "parallel")),
        name="mix_out",
    )(oa3, ob3, sg3, wa, wb, wo, x, gt1, g_post, g_pre2, sc2, sh2)


def _ffn_kernel(h_ref, w1_ref, w2_ref, x1_ref, gt_ref, g_ref, o_ref, acc_ref):
    j = pl.program_id(2)

    @pl.when(j == 0)
    def _():
        acc_ref[...] = jnp.zeros(acc_ref.shape, F32)

    for rows in _row_chunks(h_ref, KEY_TILE):
        u = jnp.dot(h_ref[rows, :], w1_ref[...], preferred_element_type=F32)
        u = jnp.square(jnp.maximum(u, 0.0)).astype(BF16)
        acc_ref[rows, :] += jnp.dot(u, w2_ref[...], preferred_element_type=F32)

    @pl.when(j == pl.num_programs(2) - 1)
    def _():
        o_ref[...] = x1_ref[...] + gt_ref[...] * (_rms(acc_ref[...]) * g_ref[...])


def _ffn(h2, w1, w2, x1, gt2, g_post):
    b, t, d = x1.shape
    ff = w1.shape[1]
    tm = _tile(t, 512)
    tf = _tile(ff, 1024)
    row = lambda: pl.BlockSpec((None, tm, d), lambda bi, i, j: (bi, i, 0))
    return pl.pallas_call(
        _ffn_kernel,
        grid=(b, t // tm, ff // tf),
        in_specs=[row(),
                  pl.BlockSpec((d, tf), lambda bi, i, j: (0, j)),
                  pl.BlockSpec((tf, d), lambda bi, i, j: (j, 0)),
                  row(),
                  pl.BlockSpec((None, 1, d), lambda bi, i, j: (bi, 0, 0)),
                  pl.BlockSpec((1, d), lambda bi, i, j: (0, 0))],
        out_specs=row(),
        out_shape=jax.ShapeDtypeStruct((b, t, d), F32),
        scratch_shapes=[pltpu.VMEM((tm, d), F32)],
        compiler_params=_params(("parallel", "parallel", "arbitrary")),
        name="ffn",
    )(h2, w1, w2, x1, gt2, g_post)


def _rope_tables(t):
    pos = jnp.arange(t, dtype=F32)[:, None]
    lane = jnp.arange(LANES)[None, :]

    def cos_sin(half):
        inv_freq = jnp.power(ROPE_THETA, -jnp.arange(half, dtype=F32) / half)
        ang = pos * inv_freq[None, :]
        reps = LANES // half
        return jnp.tile(jnp.cos(ang), (1, reps)), jnp.tile(jnp.sin(ang), (1, reps))

    cos128, sin128 = cos_sin(HEAD_DIM // 2)
    sin128 = jnp.where(lane < HEAD_DIM // 2, -sin128, sin128)
    cos64, sin64 = cos_sin(IDX_DIM // 2)
    low = (lane % IDX_DIM) < IDX_DIM // 2
    sa64 = jnp.where(low, -sin64, 0.0)
    sb64 = jnp.where(low, 0.0, sin64)
    is_key = lane < IDX_DIM
    kw = (jnp.where(is_key, cos64, 1.0), jnp.where(is_key, sa64, 0.0), jnp.where(is_key, sb64, 0.0))
    return (cos128, sin128), (cos64, sa64, sb64), kw


def kernel(x, c, w_ada, b_ada, g_pre_mix, g_post_mix, w_in, w_moba_out, w_dsa_out, w_o,
           g_pre_ffn, g_post_ffn, w_ff1, w_ff2):
    b, t, d = x.shape
    m = b * t
    nb = t // MOBA_BLOCK
    assert t % MOBA_BLOCK == 0 and nb <= MAX_BLOCKS and d % LANES == 0 and b <= 16
    n_sel = max(1, min(MOBA_TOPK, nb - 1))
    topk = min(DSA_TOPK_MAX, t // 4)
    rope128, rope64, rope_kw = _rope_tables(t)
    c_pad = jnp.zeros((16, d), F32).at[:b].set(c)

    for l in range(w_ada.shape[0]):
        mod = _ada(c_pad, w_ada[l], b_ada[l][None, :])[:b]
        sh1, sc1, gt1, sh2, sc2, gt2 = [v[:, None, :] for v in jnp.split(mod, 6, axis=-1)]

        h = _norm_mod(x, g_pre_mix[l][None, :], sc1, sh1).reshape(m, d)
        qk, vt3, qi, klo, khi, wit, sg = _in_proj(h, jnp.swapaxes(w_in, 1, 2), l, t, rope128, rope64, rope_kw)

        qk3 = qk.reshape(b, t, -1)
        kmt = _moba_kmean(qk3, nb)
        oa = _moba_attn(qk3, vt3, kmt, nb, n_sel)
        ob = _dsa_attn(qi.reshape(b, t, -1), klo.reshape(b, t, -1), khi.reshape(b, t, -1),
                       wit, qk3, vt3, topk)

        x, h2 = _mix_out(oa, ob, sg.reshape(b, t, -1), w_moba_out[l].astype(BF16),
                         w_dsa_out[l].astype(BF16), w_o[l].astype(BF16), x, gt1,
                         g_post_mix[l][None, :], g_pre_ffn[l][None, :], sc2, sh2)
        x = _ffn(h2, w_ff1[l].astype(BF16), w_ff2[l].astype(BF16), x, gt2, g_post_ffn[l][None, :])
    return x
```

```python
import functools

import jax
import jax.numpy as jnp
from jax import lax
from jax.experimental import pallas as pl
from jax.experimental.pallas import tpu as pltpu

HEAD_DIM = 128
MOBA_HEADS = 8
MOBA_BLOCK = 256
MOBA_TOPK = 3
DSA_HEADS = 8
IDX_HEADS = 16
IDX_DIM = 64
DSA_TOPK_MAX = 256
ROPE_THETA = 10000.0
RMS_EPS = 1e-6

MOBA_W = MOBA_HEADS * HEAD_DIM
DSA_W = DSA_HEADS * HEAD_DIM
IDX_W = IDX_HEADS * IDX_DIM
LANES = 128
MAX_BLOCKS = LANES // MOBA_HEADS
KEY_TILE = 256
SUB_TILE = 128
ROW_CHUNK = 512
LOG2_E = 1.4426950408889634
QK_SCALE_LOG2 = HEAD_DIM ** -0.5 * LOG2_E
MASK_BIAS = -30000.0
NEG_BIG = -1e30
BISECT_FIXED = 18
VMEM_LIMIT = 56 * 1024 * 1024

F32 = jnp.float32
BF16 = jnp.bfloat16
_NT = (((1,), (1,)), ((), ()))


def _params(sem):
    return pltpu.CompilerParams(dimension_semantics=sem, vmem_limit_bytes=VMEM_LIMIT)


def _tile(n, pref):
    if n <= pref:
        return n
    t = pref - pref % LANES
    while t >= LANES:
        if n % t == 0:
            return t
        t -= LANES
    return n


def _rms(x):
    return x * lax.rsqrt(jnp.mean(x * x, axis=-1, keepdims=True) + RMS_EPS)


def _ada_kernel(c_ref, w_ref, b_ref, o_ref):
    c = c_ref[...]
    cs = (c * jax.nn.sigmoid(c)).astype(BF16)
    o_ref[...] = jnp.dot(cs, w_ref[...].astype(BF16), preferred_element_type=F32) + b_ref[...]


def _ada(c_pad, w, b):
    rows, d = c_pad.shape
    n = w.shape[1]
    tn = _tile(n, 1024)
    return pl.pallas_call(
        _ada_kernel,
        grid=(n // tn,),
        in_specs=[pl.BlockSpec((rows, d), lambda j: (0, 0)),
                  pl.BlockSpec((d, tn), lambda j: (0, j)),
                  pl.BlockSpec((1, tn), lambda j: (0, j))],
        out_specs=pl.BlockSpec((rows, tn), lambda j: (0, j)),
        out_shape=jax.ShapeDtypeStruct((rows, n), F32),
        compiler_params=_params(("parallel",)),
        name="ada_mod",
    )(c_pad, w, b)


def _norm_mod_kernel(x_ref, g_ref, sc_ref, sh_ref, o_ref):
    y = _rms(x_ref[...])
    o_ref[...] = ((y * g_ref[...]) * (1.0 + sc_ref[...]) + sh_ref[...]).astype(o_ref.dtype)


def _norm_mod(x, g, sc, sh):
    b, t, d = x.shape
    tt = _tile(t, 512)
    vec = pl.BlockSpec((None, 1, d), lambda bi, ti: (bi, 0, 0))
    return pl.pallas_call(
        _norm_mod_kernel,
        grid=(b, t // tt),
        in_specs=[pl.BlockSpec((None, tt, d), lambda bi, ti: (bi, ti, 0)),
                  pl.BlockSpec((1, d), lambda bi, ti: (0, 0)), vec, vec],
        out_specs=pl.BlockSpec((None, tt, d), lambda bi, ti: (bi, ti, 0)),
        out_shape=jax.ShapeDtypeStruct((b, t, d), BF16),
        compiler_params=_params(("parallel", "parallel")),
        name="norm_mod",
    )(x, g, sc, sh)


def _first_token_tile():
    return pl.program_id(1) == 0


def _rope64(y, cos, sin_a, sin_b):
    return y * cos + pltpu.roll(y, LANES - 32, 1) * sin_a + pltpu.roll(y, 32, 1) * sin_b


def _row_chunks(ref, step=ROW_CHUNK):
    n = ref.shape[0]
    step = min(n, step)
    return [slice(r, r + step) for r in range(0, n, step)]


def _wp_rope128_kernel(h_ref, w_ref, cos_ref, sin_ref, o_ref, wb_ref):
    @pl.when(_first_token_tile())
    def _():
        wb_ref[...] = w_ref[...].astype(BF16)

    for rows in _row_chunks(h_ref):
        acc = lax.dot_general(h_ref[rows, :], wb_ref[...], _NT, preferred_element_type=F32)
        cos = cos_ref[rows, :]
        sin = sin_ref[rows, :]
        for g in range(acc.shape[1] // LANES):
            y = acc[:, g * LANES:(g + 1) * LANES]
            o_ref[rows, g * LANES:(g + 1) * LANES] = (
                y * cos + pltpu.roll(y, LANES // 2, 1) * sin).astype(o_ref.dtype)


def _wp_rope64_kernel(h_ref, w_ref, cos_ref, sa_ref, sb_ref, o_ref, wb_ref):
    @pl.when(_first_token_tile())
    def _():
        wb_ref[...] = w_ref[...].astype(BF16)

    for rows in _row_chunks(h_ref):
        acc = lax.dot_general(h_ref[rows, :], wb_ref[...], _NT, preferred_element_type=F32)
        cos, sa, sb = cos_ref[rows, :], sa_ref[rows, :], sb_ref[rows, :]
        for g in range(acc.shape[1] // LANES):
            y = acc[:, g * LANES:(g + 1) * LANES]
            o_ref[rows, g * LANES:(g + 1) * LANES] = _rope64(y, cos, sa, sb).astype(o_ref.dtype)


def _wp_vt_kernel(h_ref, w_ref, o_ref, wb_ref):
    @pl.when(_first_token_tile())
    def _():
        wb_ref[...] = w_ref[...].astype(BF16)

    for c in range(o_ref.shape[0]):
        r = lax.dot_general(wb_ref[...], h_ref[c * KEY_TILE:(c + 1) * KEY_TILE, :], _NT,
                            preferred_element_type=F32)
        o_ref[c] = r.astype(o_ref.dtype)


def _wp_gate_kernel(h_ref, wa_ref, wb_ref, o_ref, wg_ref, *, shift):
    @pl.when(_first_token_tile())
    def _():
        keep = wg_ref.shape[0] - shift
        wg_ref[:keep, :] = wa_ref[shift:, :].astype(BF16)
        wg_ref[keep:, :] = wb_ref[:shift, :].astype(BF16)

    for rows in _row_chunks(h_ref):
        acc = lax.dot_general(h_ref[rows, :], wg_ref[...], _NT, preferred_element_type=F32)
        o_ref[rows, :] = jax.nn.sigmoid(acc).astype(o_ref.dtype)


def _wp_kw_kernel(h_ref, w_ref, cos_ref, sa_ref, sb_ref, klo_ref, khi_ref, wit_ref, wb_ref):
    @pl.when(pl.program_id(0) == 0)
    def _():
        r = lax.broadcasted_iota(jnp.int32, w_ref.shape, 0)
        wb_ref[...] = jnp.where(r < IDX_DIM + IDX_HEADS, w_ref[...], 0.0).astype(BF16)

    h = h_ref[...]
    y = lax.dot_general(h, wb_ref[...], _NT, preferred_element_type=F32)
    r = _rope64(y, cos_ref[...], sa_ref[...], sb_ref[...])
    lane = lax.broadcasted_iota(jnp.int32, y.shape, 1)
    klo = jnp.where(lane < IDX_DIM, r, 0.0)
    klo_ref[...] = klo.astype(BF16)
    khi_ref[...] = pltpu.roll(klo, IDX_DIM, 1).astype(BF16)
    yt = lax.dot_general(wb_ref[...], h, _NT, preferred_element_type=F32)
    for c in range(wit_ref.shape[0]):
        wit_ref[c] = yt[:, c * KEY_TILE:(c + 1) * KEY_TILE]


def _in_proj(h, wt, l, t, tables128, tables64, tables_kw):
    m, k = h.shape
    d = k
    tm = _tile(t, 1024)
    nt = t // tm
    ni = m // tm
    assert tm % KEY_TILE == 0
    tn = 1024
    o_dsa = 3 * MOBA_W
    o_qi = o_dsa + 3 * DSA_W
    o_kw = o_qi + IDX_W
    o_g = o_kw + IDX_DIM + IDX_HEADS
    sem = ("arbitrary", "arbitrary")
    h_spec = pl.BlockSpec((tm, k), lambda j, i: (i, 0))
    tab = pl.BlockSpec((tm, LANES), lambda j, i: (i % nt, 0))
    row = pl.BlockSpec((tm, tn), lambda j, i: (i, j))

    def wrows(first_blocks, n_first, second_start):
        return pl.BlockSpec((None, tn, k), lambda j, i: (l, jnp.where(j < n_first, first_blocks + j,
                                                                      second_start + j - n_first), 0))

    assert MOBA_W == DSA_W and MOBA_W % tn == 0
    n_m, n_d = 2 * MOBA_W // tn, 2 * DSA_W // tn
    qk_tab = pl.BlockSpec((None, tm, LANES), lambda j, i: ((j * tn // MOBA_W) % 2, i % nt, 0))
    qk = pl.pallas_call(
        _wp_rope128_kernel,
        grid=(n_m + n_d, ni),
        in_specs=[h_spec, wrows(0, n_m, o_dsa // tn), qk_tab, qk_tab],
        out_specs=row,
        out_shape=jax.ShapeDtypeStruct((m, 2 * MOBA_W + 2 * DSA_W), BF16),
        scratch_shapes=[pltpu.VMEM((tn, k), BF16)],
        compiler_params=_params(sem),
        name="proj_qk",
    )(h, wt, *tables128)

    n_m, n_d = MOBA_W // tn, DSA_W // tn
    vt = pl.pallas_call(
        _wp_vt_kernel,
        grid=(n_m + n_d, ni),
        in_specs=[h_spec, wrows(2 * MOBA_W // tn, n_m, (o_dsa + 2 * DSA_W) // tn)],
        out_specs=pl.BlockSpec((tm // KEY_TILE, tn, KEY_TILE), lambda j, i: (i, j, 0)),
        out_shape=jax.ShapeDtypeStruct((m // KEY_TILE, MOBA_W + DSA_W, KEY_TILE), BF16),
        scratch_shapes=[pltpu.VMEM((tn, k), BF16)],
        compiler_params=_params(sem),
        name="proj_v",
    )(h, wt)

    n_q = IDX_W // tn
    qi = pl.pallas_call(
        _wp_rope64_kernel,
        grid=(n_q, ni),
        in_specs=[h_spec, wrows(o_qi // tn, n_q, 0), tab, tab, tab],
        out_specs=row,
        out_shape=jax.ShapeDtypeStruct((m, IDX_W), BF16),
        scratch_shapes=[pltpu.VMEM((tn, k), BF16)],
        compiler_params=_params(sem),
        name="proj_qi",
    )(h, wt, *tables64)

    tab1 = pl.BlockSpec((tm, LANES), lambda i: (i % nt, 0))
    row1 = pl.BlockSpec((tm, LANES), lambda i: (i, 0))
    klo, khi, wit = pl.pallas_call(
        _wp_kw_kernel,
        grid=(ni,),
        in_specs=[pl.BlockSpec((tm, k), lambda i: (i, 0)),
                  pl.BlockSpec((None, LANES, k), lambda i: (l, o_kw // LANES, 0)), tab1, tab1, tab1],
        out_specs=[row1, row1, pl.BlockSpec((tm // KEY_TILE, LANES, KEY_TILE), lambda i: (i, 0, 0))],
        out_shape=[jax.ShapeDtypeStruct((m, LANES), BF16), jax.ShapeDtypeStruct((m, LANES), BF16),
                   jax.ShapeDtypeStruct((m // KEY_TILE, LANES, KEY_TILE), F32)],
        scratch_shapes=[pltpu.VMEM((LANES, k), BF16)],
        compiler_params=_params(("arbitrary",)),
        name="proj_kw",
    )(h, wt, *tables_kw)

    tg = _tile(2 * d, tn)
    g0 = o_kw // tg
    shift = o_g - o_kw
    assert o_kw % tg == 0 and shift <= LANES and tg % LANES == 0
    sg = pl.pallas_call(
        functools.partial(_wp_gate_kernel, shift=shift),
        grid=(2 * d // tg, ni),
        in_specs=[h_spec,
                  pl.BlockSpec((None, tg, k), lambda j, i: (l, g0 + j, 0)),
                  pl.BlockSpec((None, LANES, k), lambda j, i: (l, (g0 + j + 1) * (tg // LANES), 0))],
        out_specs=pl.BlockSpec((tm, tg), lambda j, i: (i, j)),
        out_shape=jax.ShapeDtypeStruct((m, 2 * d), BF16),
        scratch_shapes=[pltpu.VMEM((tg, k), BF16)],
        compiler_params=_params(sem),
        name="proj_gate",
    )(h, wt, wt)
    return qk, vt, qi, klo, khi, wit, sg


def _moba_kmean_kernel(k_ref, o_ref, km_ref, *, nb):
    km_ref[...] = jnp.zeros(km_ref.shape, F32)
    for j in range(nb):
        blk = k_ref[j * MOBA_BLOCK:(j + 1) * MOBA_BLOCK, :].astype(F32)
        km_ref[j:j + 1, :] = jnp.sum(blk, axis=0, keepdims=True) * (1.0 / MOBA_BLOCK)
    km = km_ref[...]
    tiled = jnp.concatenate([km] * MOBA_HEADS, axis=0)
    r = lax.shift_right_logical(lax.broadcasted_iota(jnp.int32, tiled.shape, 0), 4)
    c = lax.shift_right_logical(lax.broadcasted_iota(jnp.int32, tiled.shape, 1), 7)
    o_ref[...] = jnp.where(r == c, tiled, 0.0).astype(o_ref.dtype)


def _moba_kmean(qk3, nb):
    b, t, _ = qk3.shape
    return pl.pallas_call(
        functools.partial(_moba_kmean_kernel, nb=nb),
        grid=(b,),
        in_specs=[pl.BlockSpec((None, t, MOBA_W), lambda bi: (bi, 0, 1))],
        out_specs=pl.BlockSpec((None, LANES, MOBA_W), lambda bi: (bi, 0, 0)),
        out_shape=jax.ShapeDtypeStruct((b, LANES, MOBA_W), BF16),
        scratch_shapes=[pltpu.VMEM((MAX_BLOCKS, MOBA_W), F32)],
        compiler_params=_params(("parallel",)),
        name="moba_kmean",
    )(qk3)


def _moba_block_bias(q_ref, kmt_ref, sbt_ref, i, n_sel):
    g = lax.dot_general(kmt_ref[...], q_ref[...], _NT, preferred_element_type=F32)
    shape = (MAX_BLOCKS, g.shape[1])
    j = lax.broadcasted_iota(jnp.int32, shape, 0)
    past = j < i
    for h in range(MOBA_HEADS):
        gm = jnp.where(past, g[h * MAX_BLOCKS:(h + 1) * MAX_BLOCKS, :], -jnp.inf)
        rank = jnp.zeros(shape, F32)
        for jo in range(MAX_BLOCKS):
            other = gm[jo:jo + 1, :]
            beats = (other > gm) | ((other == gm) & (jo < j))
            rank = rank + jnp.where(beats, 1.0, 0.0)
        bias = jnp.where(past & (rank < n_sel), 0.0, MASK_BIAS)
        for jb in range(MAX_BLOCKS):
            sbt_ref[h, jb] = jnp.broadcast_to(bias[jb:jb + 1, :], (8, shape[1]))


def _flash_step(st, m, l, acc_t, v_t):
    m_new = jnp.maximum(m, jnp.max(st, axis=0, keepdims=True))
    alpha = jnp.exp2(m - m_new)
    p = jnp.exp2(st - m_new)
    l = alpha * l + jnp.sum(p, axis=0, keepdims=True)
    acc_t = alpha * acc_t + jnp.dot(v_t, p.astype(BF16), preferred_element_type=F32)
    return m_new, l, acc_t


def _flash_step_joint(sts, m, l, acc_t, v_ts):
    m_new = m
    for st in sts:
        m_new = jnp.maximum(m_new, jnp.max(st, axis=0, keepdims=True))
    alpha = jnp.exp2(m - m_new)
    l = alpha * l
    acc_t = alpha * acc_t
    for st, v_t in zip(sts, v_ts):
        p = jnp.exp2(st - m_new)
        l = l + jnp.sum(p, axis=0, keepdims=True)
        acc_t = acc_t + jnp.dot(v_t, p.astype(BF16), preferred_element_type=F32)
    return m_new, l, acc_t


def _transpose_bf16(x):
    return x.astype(F32).T.astype(BF16)


def _flash_finish(o_ref, l_ref, acc_ref, heads):
    for h in range(heads):
        hs = slice(h * HEAD_DIM, (h + 1) * HEAD_DIM)
        o_ref[:, hs] = (acc_ref[h] / l_ref[h:h + 1, :]).T.astype(o_ref.dtype)


def _moba_attn_kernel(q_ref, k_ref, vt_ref, kmt_ref, o_ref, m_ref, l_ref, acc_ref, qa_ref, sbt_ref,
                      *, n_sel):
    i = pl.program_id(1)
    own = pl.multiple_of(i * MOBA_BLOCK, MOBA_BLOCK)
    sub_shape = (SUB_TILE, MOBA_BLOCK)
    kidx = lax.broadcasted_iota(jnp.int32, sub_shape, 0)
    qidx = lax.broadcasted_iota(jnp.int32, sub_shape, 1)
    _moba_block_bias(q_ref, kmt_ref, sbt_ref, i, n_sel)
    for h in range(MOBA_HEADS):
        hs = slice(h * HEAD_DIM, (h + 1) * HEAD_DIM)
        qa_ref[h] = _transpose_bf16(q_ref[:, hs])
        m = jnp.full((1, MOBA_BLOCK), NEG_BIG, F32)
        l = jnp.zeros((1, MOBA_BLOCK), F32)
        acc = jnp.zeros((HEAD_DIM, MOBA_BLOCK), F32)
        for u in range(MOBA_BLOCK // SUB_TILE):
            st = jnp.dot(k_ref[pl.ds(own + u * SUB_TILE, SUB_TILE), hs], qa_ref[h],
                         preferred_element_type=F32)
            st = jnp.where(kidx + u * SUB_TILE <= qidx, st, -jnp.inf)
            m, l, acc = _flash_step(st, m, l, acc, vt_ref[i, hs, u * SUB_TILE:(u + 1) * SUB_TILE])
        m_ref[h:h + 1, :] = m
        l_ref[h:h + 1, :] = l
        acc_ref[h] = acc

    def blocks(js):
        for h in range(MOBA_HEADS):
            hs = slice(h * HEAD_DIM, (h + 1) * HEAD_DIM)
            m, l, acc = m_ref[h:h + 1, :], l_ref[h:h + 1, :], acc_ref[h]
            for j in js:
                off = pl.multiple_of(j * MOBA_BLOCK, MOBA_BLOCK)
                bias = sbt_ref[h, j][0:1, :]
                sts, vts = [], []
                for u in range(MOBA_BLOCK // SUB_TILE):
                    sts.append(jnp.dot(k_ref[pl.ds(off + u * SUB_TILE, SUB_TILE), hs], qa_ref[h],
                                       preferred_element_type=F32) + bias)
                    vts.append(vt_ref[j, hs, u * SUB_TILE:(u + 1) * SUB_TILE])
                m, l, acc = _flash_step_joint(sts, m, l, acc, vts)
            m_ref[h:h + 1, :] = m
            l_ref[h:h + 1, :] = l
            acc_ref[h] = acc

    def body(t, carry):
        blocks((2 * t, 2 * t + 1))
        return carry

    lax.fori_loop(0, i // 2, body, 0)

    @pl.when(i % 2 == 1)
    def _():
        blocks((i - 1,))

    _flash_finish(o_ref, l_ref, acc_ref, MOBA_HEADS)


def _moba_attn(qk3, vt3, kmt, nb, n_sel):
    b, t, _ = qk3.shape
    one = pl.Buffered(1)
    return pl.pallas_call(
        functools.partial(_moba_attn_kernel, n_sel=n_sel),
        grid=(b, nb),
        in_specs=[pl.BlockSpec((None, MOBA_BLOCK, MOBA_W), lambda bi, i: (bi, i, 0)),
                  pl.BlockSpec((None, t, MOBA_W), lambda bi, i: (bi, 0, 1), pipeline_mode=one),
                  pl.BlockSpec((nb, MOBA_W, KEY_TILE), lambda bi, i: (bi, 0, 0), pipeline_mode=one),
                  pl.BlockSpec((None, LANES, MOBA_W), lambda bi, i: (bi, 0, 0), pipeline_mode=one)],
        out_specs=pl.BlockSpec((None, MOBA_BLOCK, MOBA_W), lambda bi, i: (bi, i, 0)),
        out_shape=jax.ShapeDtypeStruct((b, t, MOBA_W), BF16),
        scratch_shapes=[pltpu.VMEM((MOBA_HEADS, MOBA_BLOCK), F32),
                        pltpu.VMEM((MOBA_HEADS, MOBA_BLOCK), F32),
                        pltpu.VMEM((MOBA_HEADS, HEAD_DIM, MOBA_BLOCK), F32),
                        pltpu.VMEM((MOBA_HEADS, HEAD_DIM, MOBA_BLOCK), BF16),
                        pltpu.VMEM((MOBA_HEADS, MAX_BLOCKS, 8, MOBA_BLOCK), F32)],
        compiler_params=_params(("parallel", "arbitrary")),
        name="moba_attn",
    )(qk3, qk3, vt3, kmt)


def _float_to_key(x):
    b = lax.bitcast_convert_type(x, jnp.int32)
    return jnp.where(b >= 0, b, b ^ 0x7FFFFFFF)


def _key_to_float(t):
    return lax.bitcast_convert_type(jnp.where(t >= 0, t, t ^ 0x7FFFFFFF), F32)


def _dsa_kernel(qi_ref, klo_ref, khi_ref, wt_ref, q_ref, k_ref, vt_ref, o_ref,
                sc_ref, jl_ref, m_ref, l_ref, acc_ref, qt_ref, qit_ref,
                *, topk, idx_scale, tq, seq_bits):
    i = pl.program_id(1)
    nk = i + 1
    shape = (KEY_TILE, tq)
    krow = lax.broadcasted_iota(jnp.int32, shape, 0)
    qpos = i * tq + lax.broadcasted_iota(jnp.int32, shape, 1)

    wt = wt_ref[...]
    for p in range(IDX_HEADS // 2):
        qit_ref[p] = _transpose_bf16(qi_ref[:, p * LANES:(p + 1) * LANES])
    for h in range(DSA_HEADS):
        qt_ref[h] = _transpose_bf16(q_ref[:, h * HEAD_DIM:(h + 1) * HEAD_DIM])

    def score_tiles(kts):
        offs = [pl.multiple_of(kt * KEY_TILE, KEY_TILE) for kt in kts]
        keys = jnp.concatenate(
            [r[pl.ds(off, KEY_TILE), :] for off in offs for r in (klo_ref, khi_ref)], axis=0)
        accs = [jnp.zeros(shape, F32) for _ in kts]
        for p in range(IDX_HEADS // 2):
            s = jnp.dot(keys, qit_ref[p], preferred_element_type=F32)
            for n in range(len(kts)):
                lo = s[2 * n * KEY_TILE:(2 * n + 1) * KEY_TILE]
                hi = s[(2 * n + 1) * KEY_TILE:(2 * n + 2) * KEY_TILE]
                accs[n] = accs[n] + (jnp.maximum(lo, 0.0) * wt[2 * p:2 * p + 1, :]
                                     + jnp.maximum(hi, 0.0) * wt[2 * p + 1:2 * p + 2, :])
        for kt, acc in zip(kts, accs):
            kpos = kt * KEY_TILE + krow
            sc_ref[kt] = jnp.where(kpos <= qpos, acc * idx_scale, -jnp.inf)

    def score_body(t, carry):
        score_tiles((2 * t, 2 * t + 1))
        return carry

    lax.fori_loop(0, nk // 2, score_body, 0)

    @pl.when(nk % 2 == 1)
    def _():
        score_tiles((nk - 1,))

    def count(pred):
        def body(kt, acc):
            hit = jnp.where(pred(sc_ref[kt], kt * KEY_TILE + krow), 1.0, 0.0)
            return acc + jnp.sum(hit.reshape(KEY_TILE // 32, 32, tq), axis=0)
        part = lax.fori_loop(0, nk, body, jnp.zeros((32, tq), F32))
        return jnp.sum(part, axis=0, keepdims=True)

    need_select = nk * tq > topk

    @pl.when(jnp.logical_not(need_select))
    def _():
        def body(kt, carry):
            sc_ref[kt] = jnp.where(kt * KEY_TILE + krow <= qpos, 0.0, NEG_BIG)
            return carry
        lax.fori_loop(0, nk, body, 0)

    @pl.when(need_select)
    def _():
        kf = jnp.float32(topk)

        def ext_body(kt, carry):
            s = sc_ref[kt]
            lo = jnp.min(jnp.where(s > -jnp.inf, s, jnp.inf).reshape(KEY_TILE // 32, 32, tq), axis=0)
            hi = jnp.max(s.reshape(KEY_TILE // 32, 32, tq), axis=0)
            return jnp.minimum(carry[0], lo), jnp.maximum(carry[1], hi)

        lo, hi = lax.fori_loop(0, nk, ext_body, (jnp.full((32, tq), jnp.inf, F32),
                                                 jnp.full((32, tq), -jnp.inf, F32)))
        lo = jnp.min(lo, axis=0, keepdims=True)
        hi = _key_to_float(_float_to_key(jnp.max(hi, axis=0, keepdims=True)) + 1)
        n_fin = count(lambda s, kp: s >= lo)
        done = jnp.where(n_fin <= kf, 1.0, 0.0)

        def step(state):
            lo, hi, clo, done = state
            mid = lo * 0.5 + hi * 0.5
            stuck = (mid <= lo) | (mid >= hi)
            c = count(lambda s, kp: s >= mid)
            up = (c >= kf) & (done < 0.5)
            down = (c < kf) & (done < 0.5)
            lo = jnp.where(up, mid, lo)
            clo = jnp.where(up, c, clo)
            hi = jnp.where(down, mid, hi)
            done = jnp.where(stuck | (clo == kf), 1.0, done)
            return lo, hi, clo, done

        state = lax.fori_loop(0, BISECT_FIXED, lambda _, st: step(st), (lo, hi, n_fin, done))
        state = lax.while_loop(lambda st: jnp.min(st[3]) < 0.5, lambda st: step(step(st)), state)
        thr = jnp.where(n_fin < kf, -jnp.inf, state[0])

        need = kf - count(lambda s, kp: s > thr)
        n_eq = count(lambda s, kp: s == thr)
        jl_ref[...] = jnp.full(jl_ref.shape, 2 ** seq_bits, jnp.int32)

        @pl.when(jnp.max(n_eq - need) > 0.0)
        def _():
            def idx_body(b, c):
                cand = c | lax.shift_left(jnp.int32(1), seq_bits - 1 - b)
                g = count(lambda s, kp: (s == thr) & (kp < cand))
                return jnp.where(g < need, cand, c)
            c = lax.fori_loop(0, seq_bits, idx_body, jnp.zeros((1, tq), jnp.int32))
            jl_ref[...] = jnp.broadcast_to(c, jl_ref.shape)

        bound = jnp.minimum(jl_ref[0:1, :], qpos[0:1, :])

        def bias_body(kt, carry):
            s = sc_ref[kt]
            sel = (s > thr) | ((s == thr) & (kt * KEY_TILE + krow <= bound))
            sc_ref[kt] = jnp.where(sel, 0.0, NEG_BIG)
            return carry
        lax.fori_loop(0, nk, bias_body, 0)

    m_ref[...] = jnp.full(m_ref.shape, NEG_BIG, F32)
    l_ref[...] = jnp.zeros(l_ref.shape, F32)
    acc_ref[...] = jnp.zeros(acc_ref.shape, F32)

    def tiles(kts):
        for h in range(DSA_HEADS):
            hs = slice(h * HEAD_DIM, (h + 1) * HEAD_DIM)
            m, l, acc = m_ref[h:h + 1, :], l_ref[h:h + 1, :], acc_ref[h]
            for kt in kts:
                off = pl.multiple_of(kt * KEY_TILE, KEY_TILE)
                for u in range(KEY_TILE // SUB_TILE):
                    us = slice(u * SUB_TILE, (u + 1) * SUB_TILE)
                    st = jnp.dot(k_ref[pl.ds(off + u * SUB_TILE, SUB_TILE), hs], qt_ref[h],
                                 preferred_element_type=F32) + sc_ref[kt, us, :]
                    m, l, acc = _flash_step(st, m, l, acc, vt_ref[kt, hs, us])
            m_ref[h:h + 1, :] = m
            l_ref[h:h + 1, :] = l
            acc_ref[h] = acc

    def attn_body(t, carry):
        tiles((2 * t, 2 * t + 1))
        return carry

    lax.fori_loop(0, nk // 2, attn_body, 0)

    @pl.when(nk % 2 == 1)
    def _():
        tiles((nk - 1,))

    _flash_finish(o_ref, l_ref, acc_ref, DSA_HEADS)


def _dsa_attn(qi3, klo3, khi3, wit, qk3, vt3, topk):
    b, t, _ = qk3.shape
    tq = KEY_TILE
    nq = t // tq
    seq_bits = max(1, (t - 1).bit_length())
    one = pl.Buffered(1)
    kern = functools.partial(
        _dsa_kernel, topk=topk,
        idx_scale=(IDX_DIM ** -0.5) * (IDX_HEADS ** -0.5), tq=tq, seq_bits=seq_bits)
    return pl.pallas_call(
        kern,
        grid=(b, nq),
        in_specs=[pl.BlockSpec((None, tq, IDX_W), lambda bi, i: (bi, i, 0)),
                  pl.BlockSpec((None, t, LANES), lambda bi, i: (bi, 0, 0), pipeline_mode=one),
                  pl.BlockSpec((None, t, LANES), lambda bi, i: (bi, 0, 0), pipeline_mode=one),
                  pl.BlockSpec((None, IDX_HEADS, tq), lambda bi, i: (bi * nq + i, IDX_DIM // IDX_HEADS, 0)),
                  pl.BlockSpec((None, tq, DSA_W), lambda bi, i: (bi, i, 2)),
                  pl.BlockSpec((None, t, DSA_W), lambda bi, i: (bi, 0, 3), pipeline_mode=one),
                  pl.BlockSpec((nq, DSA_W, KEY_TILE), lambda bi, i: (bi, 1, 0), pipeline_mode=one)],
        out_specs=pl.BlockSpec((None, tq, DSA_W), lambda bi, i: (bi, i, 0)),
        out_shape=jax.ShapeDtypeStruct((b, t, DSA_W), BF16),
        scratch_shapes=[pltpu.VMEM((nq, KEY_TILE, tq), F32),
                        pltpu.VMEM((8, tq), jnp.int32),
                        pltpu.VMEM((DSA_HEADS, tq), F32),
                        pltpu.VMEM((DSA_HEADS, tq), F32),
                        pltpu.VMEM((DSA_HEADS, HEAD_DIM, tq), F32),
                        pltpu.VMEM((DSA_HEADS, HEAD_DIM, tq), BF16),
                        pltpu.VMEM((IDX_HEADS // 2, LANES, tq), BF16)],
        compiler_params=_params(("parallel", "arbitrary")),
        name="dsa_attn",
    )(qi3, klo3, khi3, wit, qk3, qk3, vt3)


def _mix_out_kernel(oa_ref, ob_ref, sg_ref, wa_ref, wb_ref, wo_ref, x_ref, gt_ref, gpost_ref,
                    gpre_ref, sc_ref, sh_ref, x1_ref, h2_ref, *, d):
    ya = jnp.dot(oa_ref[...], wa_ref[...], preferred_element_type=F32)
    yb = jnp.dot(ob_ref[...], wb_ref[...], preferred_element_type=F32)
    z = sg_ref[:, :d].astype(F32) * ya + sg_ref[:, d:].astype(F32) * yb
    y = jnp.dot(z.astype(BF16), wo_ref[...], preferred_element_type=F32)
    x1 = x_ref[...] + gt_ref[...] * (_rms(y) * gpost_ref[...])
    x1_ref[...] = x1
    h2_ref[...] = ((_rms(x1) * gpre_ref[...]) * (1.0 + sc_ref[...]) + sh_ref[...]).astype(h2_ref.dtype)


def _mix_out(oa3, ob3, sg3, wa, wb, wo, x, gt1, g_post, g_pre2, sc2, sh2):
    b, t, d = x.shape
    tm = _tile(t, 256)
    one = pl.Buffered(1)
    row = lambda w: pl.BlockSpec((None, tm, w), lambda bi, i: (bi, i, 0))
    full = lambda a: pl.BlockSpec(a.shape, lambda bi, i: (0, 0), pipeline_mode=one)
    vec = pl.BlockSpec((None, 1, d), lambda bi, i: (bi, 0, 0))
    gain = pl.BlockSpec((1, d), lambda bi, i: (0, 0))
    return pl.pallas_call(
        functools.partial(_mix_out_kernel, d=d),
        grid=(b, t // tm),
        in_specs=[row(MOBA_W), row(DSA_W), row(2 * d), full(wa), full(wb), full(wo), row(d),
                  vec, gain, gain, vec, vec],
        out_specs=[row(d), row(d)],
        out_shape=[jax.ShapeDtypeStruct((b, t, d), F32), jax.ShapeDtypeStruct((b, t, d), BF16)],
        compiler_params=_params(("parallel", "parallel")),
        name="mix_out",
    )(oa3, ob3, sg3, wa, wb, wo, x, gt1, g_post, g_pre2, sc2, sh2)


def _ffn_kernel(h_ref, w1_ref, w2_ref, x1_ref, gt_ref, g_ref, o_ref, acc_ref):
    j = pl.program_id(2)

    @pl.when(j == 0)
    def _():
        acc_ref[...] = jnp.zeros(acc_ref.shape, F32)

    for rows in _row_chunks(h_ref, KEY_TILE):
        u = jnp.dot(h_ref[rows, :], w1_ref[...], preferred_element_type=F32)
        u = jnp.square(jnp.maximum(u, 0.0)).astype(BF16)
        acc_ref[rows, :] += jnp.dot(u, w2_ref[...], preferred_element_type=F32)

    @pl.when(j == pl.num_programs(2) - 1)
    def _():
        o_ref[...] = x1_ref[...] + gt_ref[...] * (_rms(acc_ref[...]) * g_ref[...])


def _ffn(h2, w1, w2, x1, gt2, g_post):
    b, t, d = x1.shape
    ff = w1.shape[1]
    tm = _tile(t, 512)
    tf = _tile(ff, 1024)
    row = lambda: pl.BlockSpec((None, tm, d), lambda bi, i, j: (bi, i, 0))
    return pl.pallas_call(
        _ffn_kernel,
        grid=(b, t // tm, ff // tf),
        in_specs=[row(),
                  pl.BlockSpec((d, tf), lambda bi, i, j: (0, j)),
                  pl.BlockSpec((tf, d), lambda bi, i, j: (j, 0)),
                  row(),
                  pl.BlockSpec((None, 1, d), lambda bi, i, j: (bi, 0, 0)),
                  pl.BlockSpec((1, d), lambda bi, i, j: (0, 0))],
        out_specs=row(),
        out_shape=jax.ShapeDtypeStruct((b, t, d), F32),
        scratch_shapes=[pltpu.VMEM((tm, d), F32)],
        compiler_params=_params(("parallel", "parallel", "arbitrary")),
        name="ffn",
    )(h2, w1, w2, x1, gt2, g_post)


def _rope_tables(t):
    pos = jnp.arange(t, dtype=F32)[:, None]
    lane = jnp.arange(LANES)[None, :]

    def cos_sin(half):
        inv_freq = jnp.power(ROPE_THETA, -jnp.arange(half, dtype=F32) / half)
        ang = pos * inv_freq[None, :]
        reps = LANES // half
        return jnp.tile(jnp.cos(ang), (1, reps)), jnp.tile(jnp.sin(ang), (1, reps))

    cos128, sin128 = cos_sin(HEAD_DIM // 2)
    sin128 = jnp.where(lane < HEAD_DIM // 2, -sin128, sin128)
    cos128 = jnp.stack([cos128 * QK_SCALE_LOG2, cos128])
    sin128 = jnp.stack([sin128 * QK_SCALE_LOG2, sin128])
    cos64, sin64 = cos_sin(IDX_DIM // 2)
    low = (lane % IDX_DIM) < IDX_DIM // 2
    sa64 = jnp.where(low, -sin64, 0.0)
    sb64 = jnp.where(low, 0.0, sin64)
    is_key = lane < IDX_DIM
    kw = (jnp.where(is_key, cos64, 1.0), jnp.where(is_key, sa64, 0.0), jnp.where(is_key, sb64, 0.0))
    return (cos128, sin128), (cos64, sa64, sb64), kw


def kernel(x, c, w_ada, b_ada, g_pre_mix, g_post_mix, w_in, w_moba_out, w_dsa_out, w_o,
           g_pre_ffn, g_post_ffn, w_ff1, w_ff2):
    b, t, d = x.shape
    m = b * t
    nb = t // MOBA_BLOCK
    assert t % MOBA_BLOCK == 0 and nb <= MAX_BLOCKS and d % LANES == 0 and b <= 16
    n_sel = max(1, min(MOBA_TOPK, nb - 1))
    topk = min(DSA_TOPK_MAX, t // 4)
    rope128, rope64, rope_kw = _rope_tables(t)
    c_pad = jnp.zeros((16, d), F32).at[:b].set(c)

    for l in range(w_ada.shape[0]):
        mod = _ada(c_pad, w_ada[l], b_ada[l][None, :])[:b]
        sh1, sc1, gt1, sh2, sc2, gt2 = [v[:, None, :] for v in jnp.split(mod, 6, axis=-1)]

        h = _norm_mod(x, g_pre_mix[l][None, :], sc1, sh1).reshape(m, d)
        qk, vt3, qi, klo, khi, wit, sg = _in_proj(h, jnp.swapaxes(w_in, 1, 2), l, t, rope128, rope64, rope_kw)

        qk3 = qk.reshape(b, t, -1)
        kmt = _moba_kmean(qk3, nb)
        oa = _moba_attn(qk3, vt3, kmt, nb, n_sel)
        ob = _dsa_attn(qi.reshape(b, t, -1), klo.reshape(b, t, -1), khi.reshape(b, t, -1),
                       wit, qk3, vt3, topk)

        x, h2 = _mix_out(oa, ob, sg.reshape(b, t, -1), w_moba_out[l].astype(BF16),
                         w_dsa_out[l].astype(BF16), w_o[l].astype(BF16), x, gt1,
                         g_post_mix[l][None, :], g_pre_ffn[l][None, :], sc2, sh2)
        x = _ffn(h2, w_ff1[l].astype(BF16), w_ff2[l].astype(BF16), x, gt2, g_post_ffn[l][None, :])
    return x
```

```python
import functools

import jax
import jax.numpy as jnp
from jax import lax
from jax.experimental import pallas as pl
from jax.experimental.pallas import tpu as pltpu

HEAD_DIM = 128
MOBA_HEADS = 8
MOBA_BLOCK = 256
MOBA_TOPK = 3
DSA_HEADS = 8
IDX_HEADS = 16
IDX_DIM = 64
DSA_TOPK_MAX = 256
ROPE_THETA = 10000.0
RMS_EPS = 1e-6

MOBA_W = MOBA_HEADS * HEAD_DIM
DSA_W = DSA_HEADS * HEAD_DIM
IDX_W = IDX_HEADS * IDX_DIM
LANES = 128
MAX_BLOCKS = LANES // MOBA_HEADS
KEY_TILE = 256
SUB_TILE = 128
ROW_CHUNK = 512
LOG2_E = 1.4426950408889634
QK_SCALE_LOG2 = HEAD_DIM ** -0.5 * LOG2_E
MASK_BIAS = -30000.0
NEG_BIG = -1e30
BISECT_FIXED = 18
VMEM_LIMIT = 56 * 1024 * 1024

F32 = jnp.float32
BF16 = jnp.bfloat16
_NT = (((1,), (1,)), ((), ()))


def _params(sem):
    return pltpu.CompilerParams(dimension_semantics=sem, vmem_limit_bytes=VMEM_LIMIT)


def _tile(n, pref):
    if n <= pref:
        return n
    t = pref - pref % LANES
    while t >= LANES:
        if n % t == 0:
            return t
        t -= LANES
    return n


def _rms(x):
    return x * lax.rsqrt(jnp.mean(x * x, axis=-1, keepdims=True) + RMS_EPS)


def _ada_kernel(c_ref, w_ref, b_ref, o_ref):
    c = c_ref[...]
    cs = (c * jax.nn.sigmoid(c)).astype(BF16)
    o_ref[...] = jnp.dot(cs, w_ref[...].astype(BF16), preferred_element_type=F32) + b_ref[...]


def _ada(c_pad, w, b):
    rows, d = c_pad.shape
    n = w.shape[1]
    tn = _tile(n, 1024)
    return pl.pallas_call(
        _ada_kernel,
        grid=(n // tn,),
        in_specs=[pl.BlockSpec((rows, d), lambda j: (0, 0)),
                  pl.BlockSpec((d, tn), lambda j: (0, j)),
                  pl.BlockSpec((1, tn), lambda j: (0, j))],
        out_specs=pl.BlockSpec((rows, tn), lambda j: (0, j)),
        out_shape=jax.ShapeDtypeStruct((rows, n), F32),
        compiler_params=_params(("parallel",)),
        name="ada_mod",
    )(c_pad, w, b)


def _norm_mod_kernel(x_ref, g_ref, sc_ref, sh_ref, o_ref):
    y = _rms(x_ref[...])
    o_ref[...] = ((y * g_ref[...]) * (1.0 + sc_ref[...]) + sh_ref[...]).astype(o_ref.dtype)


def _norm_mod(x, g, sc, sh):
    b, t, d = x.shape
    tt = _tile(t, 512)
    vec = pl.BlockSpec((None, 1, d), lambda bi, ti: (bi, 0, 0))
    return pl.pallas_call(
        _norm_mod_kernel,
        grid=(b, t // tt),
        in_specs=[pl.BlockSpec((None, tt, d), lambda bi, ti: (bi, ti, 0)),
                  pl.BlockSpec((1, d), lambda bi, ti: (0, 0)), vec, vec],
        out_specs=pl.BlockSpec((None, tt, d), lambda bi, ti: (bi, ti, 0)),
        out_shape=jax.ShapeDtypeStruct((b, t, d), BF16),
        compiler_params=_params(("parallel", "parallel")),
        name="norm_mod",
    )(x, g, sc, sh)


def _first_token_tile():
    return pl.program_id(1) == 0


def _rope64(y, cos, sin_a, sin_b):
    return y * cos + pltpu.roll(y, LANES - 32, 1) * sin_a + pltpu.roll(y, 32, 1) * sin_b


def _row_chunks(ref, step=ROW_CHUNK):
    n = ref.shape[0]
    step = min(n, step)
    return [slice(r, r + step) for r in range(0, n, step)]


def _wp_rope128_kernel(h_ref, w_ref, cos_ref, sin_ref, o_ref, wb_ref):
    @pl.when(_first_token_tile())
    def _():
        wb_ref[...] = w_ref[...].astype(BF16)

    for rows in _row_chunks(h_ref):
        acc = lax.dot_general(h_ref[rows, :], wb_ref[...], _NT, preferred_element_type=F32)
        cos = cos_ref[rows, :]
        sin = sin_ref[rows, :]
        for g in range(acc.shape[1] // LANES):
            y = acc[:, g * LANES:(g + 1) * LANES]
            o_ref[rows, g * LANES:(g + 1) * LANES] = (
                y * cos + pltpu.roll(y, LANES // 2, 1) * sin).astype(o_ref.dtype)


def _swap_row_halves(y, width):
    half = width // 2
    parts = []
    for r in range(0, y.shape[0], width):
        parts += [y[r + half:r + width], y[r:r + half]]
    return jnp.concatenate(parts, axis=0)


def _wp_rope_t_kernel(h_ref, w_ref, cos_ref, sin_ref, o_ref, wb_ref, *, head_dim):
    @pl.when(_first_token_tile())
    def _():
        wb_ref[...] = w_ref[...].astype(BF16)

    for c in range(o_ref.shape[0]):
        tok = slice(c * KEY_TILE, (c + 1) * KEY_TILE)
        r = lax.dot_general(wb_ref[...], h_ref[tok, :], _NT, preferred_element_type=F32)
        for g in range(0, r.shape[0], LANES):
            y = r[g:g + LANES]
            o_ref[c, g:g + LANES, :] = (y * cos_ref[:, tok]
                                        + _swap_row_halves(y, head_dim) * sin_ref[:, tok]).astype(o_ref.dtype)


def _wp_vt_kernel(h_ref, w_ref, o_ref, wb_ref):
    @pl.when(_first_token_tile())
    def _():
        wb_ref[...] = w_ref[...].astype(BF16)

    for c in range(o_ref.shape[0]):
        r = lax.dot_general(wb_ref[...], h_ref[c * KEY_TILE:(c + 1) * KEY_TILE, :], _NT,
                            preferred_element_type=F32)
        o_ref[c] = r.astype(o_ref.dtype)


def _wp_gate_kernel(h_ref, wa_ref, wb_ref, o_ref, wg_ref, *, shift):
    @pl.when(_first_token_tile())
    def _():
        keep = wg_ref.shape[0] - shift
        wg_ref[:keep, :] = wa_ref[shift:, :].astype(BF16)
        wg_ref[keep:, :] = wb_ref[:shift, :].astype(BF16)

    for rows in _row_chunks(h_ref):
        acc = lax.dot_general(h_ref[rows, :], wg_ref[...], _NT, preferred_element_type=F32)
        o_ref[rows, :] = jax.nn.sigmoid(acc).astype(o_ref.dtype)


def _wp_kw_kernel(h_ref, w_ref, cos_ref, sa_ref, sb_ref, klo_ref, khi_ref, wit_ref, wb_ref):
    @pl.when(pl.program_id(0) == 0)
    def _():
        r = lax.broadcasted_iota(jnp.int32, w_ref.shape, 0)
        wb_ref[...] = jnp.where(r < IDX_DIM + IDX_HEADS, w_ref[...], 0.0).astype(BF16)

    h = h_ref[...]
    y = lax.dot_general(h, wb_ref[...], _NT, preferred_element_type=F32)
    r = _rope64(y, cos_ref[...], sa_ref[...], sb_ref[...])
    lane = lax.broadcasted_iota(jnp.int32, y.shape, 1)
    klo = jnp.where(lane < IDX_DIM, r, 0.0)
    klo_ref[...] = klo.astype(BF16)
    khi_ref[...] = pltpu.roll(klo, IDX_DIM, 1).astype(BF16)
    yt = lax.dot_general(wb_ref[...], h, _NT, preferred_element_type=F32)
    for c in range(wit_ref.shape[0]):
        wit_ref[c] = yt[:, c * KEY_TILE:(c + 1) * KEY_TILE]


def _in_proj(h, wt, l, t, tables128, tables64, tables_kw):
    m, k = h.shape
    d = k
    tm = _tile(t, 1024)
    nt = t // tm
    ni = m // tm
    assert tm % KEY_TILE == 0
    tn = 1024
    o_dsa = 3 * MOBA_W
    o_qi = o_dsa + 3 * DSA_W
    o_kw = o_qi + IDX_W
    o_g = o_kw + IDX_DIM + IDX_HEADS
    sem = ("arbitrary", "arbitrary")
    h_spec = pl.BlockSpec((tm, k), lambda j, i: (i, 0))
    tab = pl.BlockSpec((tm, LANES), lambda j, i: (i % nt, 0))
    tabt = pl.BlockSpec((LANES, tm), lambda j, i: (0, i % nt))
    row = pl.BlockSpec((tm, tn), lambda j, i: (i, j))
    out_t = pl.BlockSpec((tm // KEY_TILE, tn, KEY_TILE), lambda j, i: (i, j, 0))

    def wrows(first_blocks, n_first, second_start):
        return pl.BlockSpec((None, tn, k), lambda j, i: (l, jnp.where(j < n_first, first_blocks + j,
                                                                      second_start + j - n_first), 0))

    assert MOBA_W == DSA_W and MOBA_W % tn == 0
    n_m, n_d = MOBA_W // tn, DSA_W // tn

    qt = pl.pallas_call(
        functools.partial(_wp_rope_t_kernel, head_dim=HEAD_DIM),
        grid=(n_m + n_d, ni),
        in_specs=[h_spec, wrows(0, n_m, o_dsa // tn), tabt, tabt],
        out_specs=out_t,
        out_shape=jax.ShapeDtypeStruct((m // KEY_TILE, MOBA_W + DSA_W, KEY_TILE), BF16),
        scratch_shapes=[pltpu.VMEM((tn, k), BF16)],
        compiler_params=_params(sem),
        name="proj_q",
    )(h, wt, *tables128[0])

    kk = pl.pallas_call(
        _wp_rope128_kernel,
        grid=(n_m + n_d, ni),
        in_specs=[h_spec, wrows(MOBA_W // tn, n_m, (o_dsa + DSA_W) // tn), tab, tab],
        out_specs=row,
        out_shape=jax.ShapeDtypeStruct((m, MOBA_W + DSA_W), BF16),
        scratch_shapes=[pltpu.VMEM((tn, k), BF16)],
        compiler_params=_params(sem),
        name="proj_k",
    )(h, wt, *tables128[1])

    vt = pl.pallas_call(
        _wp_vt_kernel,
        grid=(n_m + n_d, ni),
        in_specs=[h_spec, wrows(2 * MOBA_W // tn, n_m, (o_dsa + 2 * DSA_W) // tn)],
        out_specs=out_t,
        out_shape=jax.ShapeDtypeStruct((m // KEY_TILE, MOBA_W + DSA_W, KEY_TILE), BF16),
        scratch_shapes=[pltpu.VMEM((tn, k), BF16)],
        compiler_params=_params(sem),
        name="proj_v",
    )(h, wt)

    n_q = IDX_W // tn
    qit = pl.pallas_call(
        functools.partial(_wp_rope_t_kernel, head_dim=IDX_DIM),
        grid=(n_q, ni),
        in_specs=[h_spec, wrows(o_qi // tn, n_q, 0), tabt, tabt],
        out_specs=out_t,
        out_shape=jax.ShapeDtypeStruct((m // KEY_TILE, IDX_W, KEY_TILE), BF16),
        scratch_shapes=[pltpu.VMEM((tn, k), BF16)],
        compiler_params=_params(sem),
        name="proj_qi",
    )(h, wt, *tables64)

    tab1 = pl.BlockSpec((tm, LANES), lambda i: (i % nt, 0))
    row1 = pl.BlockSpec((tm, LANES), lambda i: (i, 0))
    klo, khi, wit = pl.pallas_call(
        _wp_kw_kernel,
        grid=(ni,),
        in_specs=[pl.BlockSpec((tm, k), lambda i: (i, 0)),
                  pl.BlockSpec((None, LANES, k), lambda i: (l, o_kw // LANES, 0)), tab1, tab1, tab1],
        out_specs=[row1, row1, pl.BlockSpec((tm // KEY_TILE, LANES, KEY_TILE), lambda i: (i, 0, 0))],
        out_shape=[jax.ShapeDtypeStruct((m, LANES), BF16), jax.ShapeDtypeStruct((m, LANES), BF16),
                   jax.ShapeDtypeStruct((m // KEY_TILE, LANES, KEY_TILE), F32)],
        scratch_shapes=[pltpu.VMEM((LANES, k), BF16)],
        compiler_params=_params(("arbitrary",)),
        name="proj_kw",
    )(h, wt, *tables_kw)

    tg = _tile(2 * d, tn)
    g0 = o_kw // tg
    shift = o_g - o_kw
    assert o_kw % tg == 0 and shift <= LANES and tg % LANES == 0
    sg = pl.pallas_call(
        functools.partial(_wp_gate_kernel, shift=shift),
        grid=(2 * d // tg, ni),
        in_specs=[h_spec,
                  pl.BlockSpec((None, tg, k), lambda j, i: (l, g0 + j, 0)),
                  pl.BlockSpec((None, LANES, k), lambda j, i: (l, (g0 + j + 1) * (tg // LANES), 0))],
        out_specs=pl.BlockSpec((tm, tg), lambda j, i: (i, j)),
        out_shape=jax.ShapeDtypeStruct((m, 2 * d), BF16),
        scratch_shapes=[pltpu.VMEM((tg, k), BF16)],
        compiler_params=_params(sem),
        name="proj_gate",
    )(h, wt, wt)
    return qt, kk, vt, qit, klo, khi, wit, sg


def _moba_kmean_kernel(k_ref, o_ref, km_ref, *, nb):
    km_ref[...] = jnp.zeros(km_ref.shape, F32)
    for j in range(nb):
        blk = k_ref[j * MOBA_BLOCK:(j + 1) * MOBA_BLOCK, :].astype(F32)
        km_ref[j:j + 1, :] = jnp.sum(blk, axis=0, keepdims=True) * (1.0 / MOBA_BLOCK)
    km = km_ref[...]
    tiled = jnp.concatenate([km] * MOBA_HEADS, axis=0)
    r = lax.shift_right_logical(lax.broadcasted_iota(jnp.int32, tiled.shape, 0), 4)
    c = lax.shift_right_logical(lax.broadcasted_iota(jnp.int32, tiled.shape, 1), 7)
    o_ref[...] = jnp.where(r == c, tiled, 0.0).astype(o_ref.dtype)


def _moba_kmean(kk3, nb):
    b, t, _ = kk3.shape
    return pl.pallas_call(
        functools.partial(_moba_kmean_kernel, nb=nb),
        grid=(b,),
        in_specs=[pl.BlockSpec((None, t, MOBA_W), lambda bi: (bi, 0, 0))],
        out_specs=pl.BlockSpec((None, LANES, MOBA_W), lambda bi: (bi, 0, 0)),
        out_shape=jax.ShapeDtypeStruct((b, LANES, MOBA_W), BF16),
        scratch_shapes=[pltpu.VMEM((MAX_BLOCKS, MOBA_W), F32)],
        compiler_params=_params(("parallel",)),
        name="moba_kmean",
    )(kk3)


def _moba_block_bias(qt_ref, kmt_ref, sbt_ref, i, n_sel):
    g = jnp.dot(kmt_ref[...], qt_ref[...], preferred_element_type=F32)
    shape = (MAX_BLOCKS, g.shape[1])
    j = lax.broadcasted_iota(jnp.int32, shape, 0)
    past = j < i
    for h in range(MOBA_HEADS):
        gm = jnp.where(past, g[h * MAX_BLOCKS:(h + 1) * MAX_BLOCKS, :], -jnp.inf)
        rank = jnp.zeros(shape, F32)
        for jo in range(MAX_BLOCKS):
            other = gm[jo:jo + 1, :]
            beats = (other > gm) | ((other == gm) & (jo < j))
            rank = rank + jnp.where(beats, 1.0, 0.0)
        bias = jnp.where(past & (rank < n_sel), 0.0, MASK_BIAS)
        for jb in range(MAX_BLOCKS):
            sbt_ref[h, jb] = jnp.broadcast_to(bias[jb:jb + 1, :], (8, shape[1]))


def _flash_step(st, m, l, acc_t, v_t):
    m_new = jnp.maximum(m, jnp.max(st, axis=0, keepdims=True))
    alpha = jnp.exp2(m - m_new)
    p = jnp.exp2(st - m_new)
    l = alpha * l + jnp.sum(p, axis=0, keepdims=True)
    acc_t = alpha * acc_t + jnp.dot(v_t, p.astype(BF16), preferred_element_type=F32)
    return m_new, l, acc_t


def _flash_step_joint(sts, m, l, acc_t, v_ts):
    m_new = m
    for st in sts:
        m_new = jnp.maximum(m_new, jnp.max(st, axis=0, keepdims=True))
    alpha = jnp.exp2(m - m_new)
    l = alpha * l
    acc_t = alpha * acc_t
    for st, v_t in zip(sts, v_ts):
        p = jnp.exp2(st - m_new)
        l = l + jnp.sum(p, axis=0, keepdims=True)
        acc_t = acc_t + jnp.dot(v_t, p.astype(BF16), preferred_element_type=F32)
    return m_new, l, acc_t


def _flash_finish(o_ref, l_ref, acc_ref, heads):
    for h in range(heads):
        hs = slice(h * HEAD_DIM, (h + 1) * HEAD_DIM)
        o_ref[:, hs] = (acc_ref[h] / l_ref[h:h + 1, :]).T.astype(o_ref.dtype)


def _moba_attn_kernel(qt_ref, k_ref, vt_ref, kmt_ref, o_ref, m_ref, l_ref, acc_ref, sbt_ref, *, n_sel):
    i = pl.program_id(1)
    own = pl.multiple_of(i * MOBA_BLOCK, MOBA_BLOCK)
    sub_shape = (SUB_TILE, MOBA_BLOCK)
    kidx = lax.broadcasted_iota(jnp.int32, sub_shape, 0)
    qidx = lax.broadcasted_iota(jnp.int32, sub_shape, 1)
    _moba_block_bias(qt_ref, kmt_ref, sbt_ref, i, n_sel)
    for h in range(MOBA_HEADS):
        hs = slice(h * HEAD_DIM, (h + 1) * HEAD_DIM)
        m = jnp.full((1, MOBA_BLOCK), NEG_BIG, F32)
        l = jnp.zeros((1, MOBA_BLOCK), F32)
        acc = jnp.zeros((HEAD_DIM, MOBA_BLOCK), F32)
        for u in range(MOBA_BLOCK // SUB_TILE):
            st = jnp.dot(k_ref[pl.ds(own + u * SUB_TILE, SUB_TILE), hs], qt_ref[hs, :],
                         preferred_element_type=F32)
            st = jnp.where(kidx + u * SUB_TILE <= qidx, st, -jnp.inf)
            m, l, acc = _flash_step(st, m, l, acc, vt_ref[i, hs, u * SUB_TILE:(u + 1) * SUB_TILE])
        m_ref[h:h + 1, :] = m
        l_ref[h:h + 1, :] = l
        acc_ref[h] = acc

    def blocks(js):
        for h in range(MOBA_HEADS):
            hs = slice(h * HEAD_DIM, (h + 1) * HEAD_DIM)
            m, l, acc = m_ref[h:h + 1, :], l_ref[h:h + 1, :], acc_ref[h]
            for j in js:
                off = pl.multiple_of(j * MOBA_BLOCK, MOBA_BLOCK)
                bias = sbt_ref[h, j][0:1, :]
                sts, vts = [], []
                for u in range(MOBA_BLOCK // SUB_TILE):
                    sts.append(jnp.dot(k_ref[pl.ds(off + u * SUB_TILE, SUB_TILE), hs], qt_ref[hs, :],
                                       preferred_element_type=F32) + bias)
                    vts.append(vt_ref[j, hs, u * SUB_TILE:(u + 1) * SUB_TILE])
                m, l, acc = _flash_step_joint(sts, m, l, acc, vts)
            m_ref[h:h + 1, :] = m
            l_ref[h:h + 1, :] = l
            acc_ref[h] = acc

    def body(t, carry):
        blocks((2 * t, 2 * t + 1))
        return carry

    lax.fori_loop(0, i // 2, body, 0)

    @pl.when(i % 2 == 1)
    def _():
        blocks((i - 1,))

    _flash_finish(o_ref, l_ref, acc_ref, MOBA_HEADS)


def _moba_attn(qt3, kk3, vt3, kmt, nb, n_sel):
    b, t, _ = kk3.shape
    one = pl.Buffered(1)
    return pl.pallas_call(
        functools.partial(_moba_attn_kernel, n_sel=n_sel),
        grid=(b, nb),
        in_specs=[pl.BlockSpec((None, MOBA_W, MOBA_BLOCK), lambda bi, i: (bi * nb + i, 0, 0)),
                  pl.BlockSpec((None, t, MOBA_W), lambda bi, i: (bi, 0, 0), pipeline_mode=one),
                  pl.BlockSpec((nb, MOBA_W, KEY_TILE), lambda bi, i: (bi, 0, 0), pipeline_mode=one),
                  pl.BlockSpec((None, LANES, MOBA_W), lambda bi, i: (bi, 0, 0), pipeline_mode=one)],
        out_specs=pl.BlockSpec((None, MOBA_BLOCK, MOBA_W), lambda bi, i: (bi, i, 0)),
        out_shape=jax.ShapeDtypeStruct((b, t, MOBA_W), BF16),
        scratch_shapes=[pltpu.VMEM((MOBA_HEADS, MOBA_BLOCK), F32),
                        pltpu.VMEM((MOBA_HEADS, MOBA_BLOCK), F32),
                        pltpu.VMEM((MOBA_HEADS, HEAD_DIM, MOBA_BLOCK), F32),
                        pltpu.VMEM((MOBA_HEADS, MAX_BLOCKS, 8, MOBA_BLOCK), F32)],
        compiler_params=_params(("parallel", "arbitrary")),
        name="moba_attn",
    )(qt3, kk3, vt3, kmt)


def _float_to_key(x):
    b = lax.bitcast_convert_type(x, jnp.int32)
    return jnp.where(b >= 0, b, b ^ 0x7FFFFFFF)


def _key_to_float(t):
    return lax.bitcast_convert_type(jnp.where(t >= 0, t, t ^ 0x7FFFFFFF), F32)


def _dsa_kernel(qit_ref, klo_ref, khi_ref, wt_ref, qt_ref, k_ref, vt_ref, o_ref,
                sc_ref, jl_ref, m_ref, l_ref, acc_ref, *, topk, idx_scale, tq, seq_bits):
    i = pl.program_id(1)
    nk = i + 1
    shape = (KEY_TILE, tq)
    krow = lax.broadcasted_iota(jnp.int32, shape, 0)
    qpos = i * tq + lax.broadcasted_iota(jnp.int32, shape, 1)

    wt = wt_ref[...]

    def score_tiles(kts):
        offs = [pl.multiple_of(kt * KEY_TILE, KEY_TILE) for kt in kts]
        keys = jnp.concatenate(
            [r[pl.ds(off, KEY_TILE), :] for off in offs for r in (klo_ref, khi_ref)], axis=0)
        accs = [jnp.zeros(shape, F32) for _ in kts]
        for p in range(IDX_HEADS // 2):
            s = jnp.dot(keys, qit_ref[p * LANES:(p + 1) * LANES, :], preferred_element_type=F32)
            for n in range(len(kts)):
                lo = s[2 * n * KEY_TILE:(2 * n + 1) * KEY_TILE]
                hi = s[(2 * n + 1) * KEY_TILE:(2 * n + 2) * KEY_TILE]
                accs[n] = accs[n] + (jnp.maximum(lo, 0.0) * wt[2 * p:2 * p + 1, :]
                                     + jnp.maximum(hi, 0.0) * wt[2 * p + 1:2 * p + 2, :])
        for kt, acc in zip(kts, accs):
            kpos = kt * KEY_TILE + krow
            sc_ref[kt] = jnp.where(kpos <= qpos, acc * idx_scale, -jnp.inf)

    def score_body(t, carry):
        score_tiles((2 * t, 2 * t + 1))
        return carry

    lax.fori_loop(0, nk // 2, score_body, 0)

    @pl.when(nk % 2 == 1)
    def _():
        score_tiles((nk - 1,))

    def count(pred):
        def body(kt, acc):
            hit = jnp.where(pred(sc_ref[kt], kt * KEY_TILE + krow), 1.0, 0.0)
            return acc + jnp.sum(hit.reshape(KEY_TILE // 32, 32, tq), axis=0)
        part = lax.fori_loop(0, nk, body, jnp.zeros((32, tq), F32))
        return jnp.sum(part, axis=0, keepdims=True)

    need_select = nk * tq > topk

    @pl.when(jnp.logical_not(need_select))
    def _():
        def body(kt, carry):
            sc_ref[kt] = jnp.where(kt * KEY_TILE + krow <= qpos, 0.0, NEG_BIG)
            return carry
        lax.fori_loop(0, nk, body, 0)

    @pl.when(need_select)
    def _():
        kf = jnp.float32(topk)

        def ext_body(kt, carry):
            s = sc_ref[kt]
            lo = jnp.min(jnp.where(s > -jnp.inf, s, jnp.inf).reshape(KEY_TILE // 32, 32, tq), axis=0)
            hi = jnp.max(s.reshape(KEY_TILE // 32, 32, tq), axis=0)
            return jnp.minimum(carry[0], lo), jnp.maximum(carry[1], hi)

        lo, hi = lax.fori_loop(0, nk, ext_body, (jnp.full((32, tq), jnp.inf, F32),
                                                 jnp.full((32, tq), -jnp.inf, F32)))
        lo = jnp.min(lo, axis=0, keepdims=True)
        hi = _key_to_float(_float_to_key(jnp.max(hi, axis=0, keepdims=True)) + 1)
        n_fin = count(lambda s, kp: s >= lo)
        done = jnp.where(n_fin <= kf, 1.0, 0.0)

        def step(state):
            lo, hi, clo, done = state
            mid = lo * 0.5 + hi * 0.5
            stuck = (mid <= lo) | (mid >= hi)
            c = count(lambda s, kp: s >= mid)
            up = (c >= kf) & (done < 0.5)
            down = (c < kf) & (done < 0.5)
            lo = jnp.where(up, mid, lo)
            clo = jnp.where(up, c, clo)
            hi = jnp.where(down, mid, hi)
            done = jnp.where(stuck | (clo == kf), 1.0, done)
            return lo, hi, clo, done

        state = lax.fori_loop(0, BISECT_FIXED, lambda _, st: step(st), (lo, hi, n_fin, done))
        state = lax.while_loop(lambda st: jnp.min(st[3]) < 0.5, lambda st: step(step(st)), state)
        thr = jnp.where(n_fin < kf, -jnp.inf, state[0])

        need = kf - count(lambda s, kp: s > thr)
        n_eq = count(lambda s, kp: s == thr)
        jl_ref[...] = jnp.full(jl_ref.shape, 2 ** seq_bits, jnp.int32)

        @pl.when(jnp.max(n_eq - need) > 0.0)
        def _():
            def idx_body(b, c):
                cand = c | lax.shift_left(jnp.int32(1), seq_bits - 1 - b)
                g = count(lambda s, kp: (s == thr) & (kp < cand))
                return jnp.where(g < need, cand, c)
            c = lax.fori_loop(0, seq_bits, idx_body, jnp.zeros((1, tq), jnp.int32))
            jl_ref[...] = jnp.broadcast_to(c, jl_ref.shape)

        bound = jnp.minimum(jl_ref[0:1, :], qpos[0:1, :])

        def bias_body(kt, carry):
            s = sc_ref[kt]
            sel = (s > thr) | ((s == thr) & (kt * KEY_TILE + krow <= bound))
            sc_ref[kt] = jnp.where(sel, 0.0, NEG_BIG)
            return carry
        lax.fori_loop(0, nk, bias_body, 0)

    m_ref[...] = jnp.full(m_ref.shape, NEG_BIG, F32)
    l_ref[...] = jnp.zeros(l_ref.shape, F32)
    acc_ref[...] = jnp.zeros(acc_ref.shape, F32)

    def tiles(kts):
        for h in range(DSA_HEADS):
            hs = slice(h * HEAD_DIM, (h + 1) * HEAD_DIM)
            m, l, acc = m_ref[h:h + 1, :], l_ref[h:h + 1, :], acc_ref[h]
            for kt in kts:
                off = pl.multiple_of(kt * KEY_TILE, KEY_TILE)
                for u in range(KEY_TILE // SUB_TILE):
                    us = slice(u * SUB_TILE, (u + 1) * SUB_TILE)
                    st = jnp.dot(k_ref[pl.ds(off + u * SUB_TILE, SUB_TILE), hs], qt_ref[hs, :],
                                 preferred_element_type=F32) + sc_ref[kt, us, :]
                    m, l, acc = _flash_step(st, m, l, acc, vt_ref[kt, hs, us])
            m_ref[h:h + 1, :] = m
            l_ref[h:h + 1, :] = l
            acc_ref[h] = acc

    def attn_body(t, carry):
        tiles((2 * t, 2 * t + 1))
        return carry

    lax.fori_loop(0, nk // 2, attn_body, 0)

    @pl.when(nk % 2 == 1)
    def _():
        tiles((nk - 1,))

    _flash_finish(o_ref, l_ref, acc_ref, DSA_HEADS)


def _dsa_attn(qit3, klo3, khi3, wit, qt3, kk3, vt3, topk):
    b, t, _ = kk3.shape
    tq = KEY_TILE
    nq = t // tq
    seq_bits = max(1, (t - 1).bit_length())
    one = pl.Buffered(1)
    kern = functools.partial(
        _dsa_kernel, topk=topk,
        idx_scale=(IDX_DIM ** -0.5) * (IDX_HEADS ** -0.5), tq=tq, seq_bits=seq_bits)
    return pl.pallas_call(
        kern,
        grid=(b, nq),
        in_specs=[pl.BlockSpec((None, IDX_W, tq), lambda bi, i: (bi * nq + i, 0, 0)),
                  pl.BlockSpec((None, t, LANES), lambda bi, i: (bi, 0, 0), pipeline_mode=one),
                  pl.BlockSpec((None, t, LANES), lambda bi, i: (bi, 0, 0), pipeline_mode=one),
                  pl.BlockSpec((None, IDX_HEADS, tq), lambda bi, i: (bi * nq + i, IDX_DIM // IDX_HEADS, 0)),
                  pl.BlockSpec((None, DSA_W, tq), lambda bi, i: (bi * nq + i, 1, 0)),
                  pl.BlockSpec((None, t, DSA_W), lambda bi, i: (bi, 0, 1), pipeline_mode=one),
                  pl.BlockSpec((nq, DSA_W, KEY_TILE), lambda bi, i: (bi, 1, 0), pipeline_mode=one)],
        out_specs=pl.BlockSpec((None, tq, DSA_W), lambda bi, i: (bi, i, 0)),
        out_shape=jax.ShapeDtypeStruct((b, t, DSA_W), BF16),
        scratch_shapes=[pltpu.VMEM((nq, KEY_TILE, tq), F32),
                        pltpu.VMEM((8, tq), jnp.int32),
                        pltpu.VMEM((DSA_HEADS, tq), F32),
                        pltpu.VMEM((DSA_HEADS, tq), F32),
                        pltpu.VMEM((DSA_HEADS, HEAD_DIM, tq), F32)],
        compiler_params=_params(("parallel", "arbitrary")),
        name="dsa_attn",
    )(qit3, klo3, khi3, wit, qt3, kk3, vt3)


def _mix_out_kernel(oa_ref, ob_ref, sg_ref, wa_ref, wb_ref, wo_ref, x_ref, gt_ref, gpost_ref,
                    gpre_ref, sc_ref, sh_ref, x1_ref, h2_ref, *, d):
    ya = jnp.dot(oa_ref[...], wa_ref[...], preferred_element_type=F32)
    yb = jnp.dot(ob_ref[...], wb_ref[...], preferred_element_type=F32)
    z = sg_ref[:, :d].astype(F32) * ya + sg_ref[:, d:].astype(F32) * yb
    y = jnp.dot(z.astype(BF16), wo_ref[...], preferred_element_type=F32)
    x1 = x_ref[...] + gt_ref[...] * (_rms(y) * gpost_ref[...])
    x1_ref[...] = x1
    h2_ref[...] = ((_rms(x1) * gpre_ref[...]) * (1.0 + sc_ref[...]) + sh_ref[...]).astype(h2_ref.dtype)


def _mix_out(oa3, ob3, sg3, wa, wb, wo, x, gt1, g_post, g_pre2, sc2, sh2):
    b, t, d = x.shape
    tm = _tile(t, 256)
    one = pl.Buffered(1)
    row = lambda w: pl.BlockSpec((None, tm, w), lambda bi, i: (bi, i, 0))
    full = lambda a: pl.BlockSpec(a.shape, lambda bi, i: (0, 0), pipeline_mode=one)
    vec = pl.BlockSpec((None, 1, d), lambda bi, i: (bi, 0, 0))
    gain = pl.BlockSpec((1, d), lambda bi, i: (0, 0))
    return pl.pallas_call(
        functools.partial(_mix_out_kernel, d=d),
        grid=(b, t // tm),
        in_specs=[row(MOBA_W), row(DSA_W), row(2 * d), full(wa), full(wb), full(wo), row(d),
                  vec, gain, gain, vec, vec],
        out_specs=[row(d), row(d)],
        out_shape=[jax.ShapeDtypeStruct((b, t, d), F32), jax.ShapeDtypeStruct((b, t, d), BF16)],
        compiler_params=_params(("parallel", "parallel")),
        name="mix_out",
    )(oa3, ob3, sg3, wa, wb, wo, x, gt1, g_post, g_pre2, sc2, sh2)


def _ffn_kernel(h_ref, w1_ref, w2_ref, x1_ref, gt_ref, g_ref, o_ref, acc_ref):
    j = pl.program_id(2)

    @pl.when(j == 0)
    def _():
        acc_ref[...] = jnp.zeros(acc_ref.shape, F32)

    for rows in _row_chunks(h_ref, KEY_TILE):
        u = jnp.dot(h_ref[rows, :], w1_ref[...], preferred_element_type=F32)
        u = jnp.square(jnp.maximum(u, 0.0)).astype(BF16)
        acc_ref[rows, :] += jnp.dot(u, w2_ref[...], preferred_element_type=F32)

    @pl.when(j == pl.num_programs(2) - 1)
    def _():
        o_ref[...] = x1_ref[...] + gt_ref[...] * (_rms(acc_ref[...]) * g_ref[...])


def _ffn(h2, w1, w2, x1, gt2, g_post):
    b, t, d = x1.shape
    ff = w1.shape[1]
    tm = _tile(t, 512)
    tf = _tile(ff, 1024)
    row = lambda: pl.BlockSpec((None, tm, d), lambda bi, i, j: (bi, i, 0))
    return pl.pallas_call(
        _ffn_kernel,
        grid=(b, t // tm, ff // tf),
        in_specs=[row(),
                  pl.BlockSpec((d, tf), lambda bi, i, j: (0, j)),
                  pl.BlockSpec((tf, d), lambda bi, i, j: (j, 0)),
                  row(),
                  pl.BlockSpec((None, 1, d), lambda bi, i, j: (bi, 0, 0)),
                  pl.BlockSpec((1, d), lambda bi, i, j: (0, 0))],
        out_specs=row(),
        out_shape=jax.ShapeDtypeStruct((b, t, d), F32),
        scratch_shapes=[pltpu.VMEM((tm, d), F32)],
        compiler_params=_params(("parallel", "parallel", "arbitrary")),
        name="ffn",
    )(h2, w1, w2, x1, gt2, g_post)


def _rope_tables(t):
    pos = jnp.arange(t, dtype=F32)[:, None]
    lane = jnp.arange(LANES)[None, :]

    def cos_sin(half):
        inv_freq = jnp.power(ROPE_THETA, -jnp.arange(half, dtype=F32) / half)
        ang = pos * inv_freq[None, :]
        reps = LANES // half
        return jnp.tile(jnp.cos(ang), (1, reps)), jnp.tile(jnp.sin(ang), (1, reps))

    cos128, sin128 = cos_sin(HEAD_DIM // 2)
    sin128 = jnp.where(lane < HEAD_DIM // 2, -sin128, sin128)
    rope128 = (((cos128 * QK_SCALE_LOG2).T, (sin128 * QK_SCALE_LOG2).T), (cos128, sin128))
    cos64, sin64 = cos_sin(IDX_DIM // 2)
    low = (lane % IDX_DIM) < IDX_DIM // 2
    sa64 = jnp.where(low, -sin64, 0.0)
    sb64 = jnp.where(low, 0.0, sin64)
    is_key = lane < IDX_DIM
    kw = (jnp.where(is_key, cos64, 1.0), jnp.where(is_key, sa64, 0.0), jnp.where(is_key, sb64, 0.0))
    return rope128, (cos64.T, (sa64 + sb64).T), kw


def kernel(x, c, w_ada, b_ada, g_pre_mix, g_post_mix, w_in, w_moba_out, w_dsa_out, w_o,
           g_pre_ffn, g_post_ffn, w_ff1, w_ff2):
    b, t, d = x.shape
    m = b * t
    nb = t // MOBA_BLOCK
    assert t % MOBA_BLOCK == 0 and nb <= MAX_BLOCKS and d % LANES == 0 and b <= 16
    n_sel = max(1, min(MOBA_TOPK, nb - 1))
    topk = min(DSA_TOPK_MAX, t // 4)
    rope128, rope64, rope_kw = _rope_tables(t)
    c_pad = jnp.zeros((16, d), F32).at[:b].set(c)

    for l in range(w_ada.shape[0]):
        mod = _ada(c_pad, w_ada[l], b_ada[l][None, :])[:b]
        sh1, sc1, gt1, sh2, sc2, gt2 = [v[:, None, :] for v in jnp.split(mod, 6, axis=-1)]

        h = _norm_mod(x, g_pre_mix[l][None, :], sc1, sh1).reshape(m, d)
        qt3, kk, vt3, qit3, klo, khi, wit, sg = _in_proj(
            h, jnp.swapaxes(w_in, 1, 2), l, t, rope128, rope64, rope_kw)

        kk3 = kk.reshape(b, t, -1)
        kmt = _moba_kmean(kk3, nb)
        oa = _moba_attn(qt3, kk3, vt3, kmt, nb, n_sel)
        ob = _dsa_attn(qit3, klo.reshape(b, t, -1), khi.reshape(b, t, -1), wit, qt3, kk3, vt3, topk)

        x, h2 = _mix_out(oa, ob, sg.reshape(b, t, -1), w_moba_out[l].astype(BF16),
                         w_dsa_out[l].astype(BF16), w_o[l].astype(BF16), x, gt1,
                         g_post_mix[l][None, :], g_pre_ffn[l][None, :], sc2, sh2)
        x = _ffn(h2, w_ff1[l].astype(BF16), w_ff2[l].astype(BF16), x, gt2, g_post_ffn[l][None, :])
    return x
```

```python
import functools

import jax
import jax.numpy as jnp
from jax import lax
from jax.experimental import pallas as pl
from jax.experimental.pallas import tpu as pltpu

HEAD_DIM = 128
MOBA_HEADS = 8
MOBA_BLOCK = 256
MOBA_TOPK = 3
DSA_HEADS = 8
IDX_HEADS = 16
IDX_DIM = 64
DSA_TOPK_MAX = 256
ROPE_THETA = 10000.0
RMS_EPS = 1e-6

MOBA_W = MOBA_HEADS * HEAD_DIM
DSA_W = DSA_HEADS * HEAD_DIM
IDX_W = IDX_HEADS * IDX_DIM
LANES = 128
MAX_BLOCKS = LANES // MOBA_HEADS
KEY_TILE = 256
SUB_TILE = 128
ROW_CHUNK = 512
LOG2_E = 1.4426950408889634
QK_SCALE_LOG2 = HEAD_DIM ** -0.5 * LOG2_E
MASK_BIAS = -30000.0
NEG_BIG = -1e30
BISECT_FIXED = 18
VMEM_LIMIT = 56 * 1024 * 1024

F32 = jnp.float32
BF16 = jnp.bfloat16
_NT = (((1,), (1,)), ((), ()))


def _params(sem):
    return pltpu.CompilerParams(dimension_semantics=sem, vmem_limit_bytes=VMEM_LIMIT)


def _tile(n, pref):
    if n <= pref:
        return n
    t = pref - pref % LANES
    while t >= LANES:
        if n % t == 0:
            return t
        t -= LANES
    return n


def _side_cast_specs(arrays, n_steps, step_of, row_blocks):
    ins, outs, shapes = [], [], []
    for w, axis in arrays:
        n = w.shape[axis]
        unit = row_blocks if axis == 0 else LANES
        target = max(unit, n // n_steps)
        slab = next(c for c in range(target - target % unit, 0, -unit) if n % c == 0)
        last = n // slab - 1
        block = (slab, w.shape[1]) if axis == 0 else (w.shape[0], slab)

        def imap(*g, axis=axis, last=last):
            idx = jnp.minimum(step_of(*g), last)
            return (idx, 0) if axis == 0 else (0, idx)

        ins.append(pl.BlockSpec(block, imap))
        outs.append(pl.BlockSpec(block, imap))
        shapes.append(jax.ShapeDtypeStruct(w.shape, BF16))
    return ins, outs, shapes


def _rms(x):
    return x * lax.rsqrt(jnp.mean(x * x, axis=-1, keepdims=True) + RMS_EPS)


def _ada_kernel(c_ref, w_ref, b_ref, o_ref):
    c = c_ref[...]
    cs = (c * jax.nn.sigmoid(c)).astype(BF16)
    o_ref[...] = jnp.dot(cs, w_ref[...].astype(BF16), preferred_element_type=F32) + b_ref[...]


def _ada(c_pad, w, b):
    rows, d = c_pad.shape
    n = w.shape[1]
    tn = _tile(n, 1024)
    return pl.pallas_call(
        _ada_kernel,
        grid=(n // tn,),
        in_specs=[pl.BlockSpec((rows, d), lambda j: (0, 0)),
                  pl.BlockSpec((d, tn), lambda j: (0, j)),
                  pl.BlockSpec((1, tn), lambda j: (0, j))],
        out_specs=pl.BlockSpec((rows, tn), lambda j: (0, j)),
        out_shape=jax.ShapeDtypeStruct((rows, n), F32),
        compiler_params=_params(("parallel",)),
        name="ada_mod",
    )(c_pad, w, b)


def _norm_mod_kernel(x_ref, g_ref, sc_ref, sh_ref, o_ref):
    y = _rms(x_ref[...])
    o_ref[...] = ((y * g_ref[...]) * (1.0 + sc_ref[...]) + sh_ref[...]).astype(o_ref.dtype)


def _norm_mod(x, g, sc, sh):
    b, t, d = x.shape
    tt = _tile(t, 512)
    vec = pl.BlockSpec((None, 1, d), lambda bi, ti: (bi, 0, 0))
    return pl.pallas_call(
        _norm_mod_kernel,
        grid=(b, t // tt),
        in_specs=[pl.BlockSpec((None, tt, d), lambda bi, ti: (bi, ti, 0)),
                  pl.BlockSpec((1, d), lambda bi, ti: (0, 0)), vec, vec],
        out_specs=pl.BlockSpec((None, tt, d), lambda bi, ti: (bi, ti, 0)),
        out_shape=jax.ShapeDtypeStruct((b, t, d), BF16),
        compiler_params=_params(("parallel", "parallel")),
        name="norm_mod",
    )(x, g, sc, sh)


def _first_token_tile():
    return pl.program_id(1) == 0


def _rope64(y, cos, sin_a, sin_b):
    return y * cos + pltpu.roll(y, LANES - 32, 1) * sin_a + pltpu.roll(y, 32, 1) * sin_b


def _row_chunks(ref, step=ROW_CHUNK):
    n = ref.shape[0]
    step = min(n, step)
    return [slice(r, r + step) for r in range(0, n, step)]


def _wp_rope128_kernel(h_ref, w_ref, cos_ref, sin_ref, o_ref, wb_ref):
    @pl.when(_first_token_tile())
    def _():
        wb_ref[...] = w_ref[...].astype(BF16)

    for rows in _row_chunks(h_ref):
        acc = lax.dot_general(h_ref[rows, :], wb_ref[...], _NT, preferred_element_type=F32)
        cos = cos_ref[rows, :]
        sin = sin_ref[rows, :]
        for g in range(acc.shape[1] // LANES):
            y = acc[:, g * LANES:(g + 1) * LANES]
            o_ref[rows, g * LANES:(g + 1) * LANES] = (
                y * cos + pltpu.roll(y, LANES // 2, 1) * sin).astype(o_ref.dtype)


def _swap_row_halves(y, width):
    half = width // 2
    parts = []
    for r in range(0, y.shape[0], width):
        parts += [y[r + half:r + width], y[r:r + half]]
    return jnp.concatenate(parts, axis=0)


def _wp_rope_t_kernel(h_ref, w_ref, cos_ref, sin_ref, o_ref, wb_ref, *, head_dim):
    @pl.when(_first_token_tile())
    def _():
        wb_ref[...] = w_ref[...].astype(BF16)

    for c in range(o_ref.shape[0]):
        tok = slice(c * KEY_TILE, (c + 1) * KEY_TILE)
        r = lax.dot_general(wb_ref[...], h_ref[tok, :], _NT, preferred_element_type=F32)
        for g in range(0, r.shape[0], LANES):
            y = r[g:g + LANES]
            o_ref[c, g:g + LANES, :] = (y * cos_ref[:, tok]
                                        + _swap_row_halves(y, head_dim) * sin_ref[:, tok]).astype(o_ref.dtype)


def _wp_vt_kernel(h_ref, w_ref, o_ref, wb_ref):
    @pl.when(_first_token_tile())
    def _():
        wb_ref[...] = w_ref[...].astype(BF16)

    for c in range(o_ref.shape[0]):
        r = lax.dot_general(wb_ref[...], h_ref[c * KEY_TILE:(c + 1) * KEY_TILE, :], _NT,
                            preferred_element_type=F32)
        o_ref[c] = r.astype(o_ref.dtype)


def _wp_gate_kernel(h_ref, wa_ref, wb_ref, c0_ref, c1_ref, c2_ref, o_ref, d0_ref, d1_ref, d2_ref, wg_ref,
                    *, shift):
    d0_ref[...] = c0_ref[...].astype(BF16)
    d1_ref[...] = c1_ref[...].astype(BF16)
    d2_ref[...] = c2_ref[...].astype(BF16)

    @pl.when(_first_token_tile())
    def _():
        keep = wg_ref.shape[0] - shift
        wg_ref[:keep, :] = wa_ref[shift:, :].astype(BF16)
        wg_ref[keep:, :] = wb_ref[:shift, :].astype(BF16)

    for rows in _row_chunks(h_ref):
        acc = lax.dot_general(h_ref[rows, :], wg_ref[...], _NT, preferred_element_type=F32)
        o_ref[rows, :] = jax.nn.sigmoid(acc).astype(o_ref.dtype)


def _wp_kw_kernel(h_ref, w_ref, cos_ref, sa_ref, sb_ref, klo_ref, khi_ref, wit_ref, wb_ref):
    @pl.when(pl.program_id(0) == 0)
    def _():
        r = lax.broadcasted_iota(jnp.int32, w_ref.shape, 0)
        wb_ref[...] = jnp.where(r < IDX_DIM + IDX_HEADS, w_ref[...], 0.0).astype(BF16)

    h = h_ref[...]
    y = lax.dot_general(h, wb_ref[...], _NT, preferred_element_type=F32)
    r = _rope64(y, cos_ref[...], sa_ref[...], sb_ref[...])
    lane = lax.broadcasted_iota(jnp.int32, y.shape, 1)
    klo = jnp.where(lane < IDX_DIM, r, 0.0)
    klo_ref[...] = klo.astype(BF16)
    khi_ref[...] = pltpu.roll(klo, IDX_DIM, 1).astype(BF16)
    yt = lax.dot_general(wb_ref[...], h, _NT, preferred_element_type=F32)
    for c in range(wit_ref.shape[0]):
        wit_ref[c] = yt[:, c * KEY_TILE:(c + 1) * KEY_TILE]


def _in_proj(h, wt, l, t, tables128, tables64, tables_kw, out_weights):
    m, k = h.shape
    d = k
    tm = _tile(t, 1024)
    nt = t // tm
    ni = m // tm
    assert tm % KEY_TILE == 0
    tn = 1024
    o_dsa = 3 * MOBA_W
    o_qi = o_dsa + 3 * DSA_W
    o_kw = o_qi + IDX_W
    o_g = o_kw + IDX_DIM + IDX_HEADS
    sem = ("arbitrary", "arbitrary")
    h_spec = pl.BlockSpec((tm, k), lambda j, i: (i, 0))
    tab = pl.BlockSpec((tm, LANES), lambda j, i: (i % nt, 0))
    tabt = pl.BlockSpec((LANES, tm), lambda j, i: (0, i % nt))
    row = pl.BlockSpec((tm, tn), lambda j, i: (i, j))
    out_t = pl.BlockSpec((tm // KEY_TILE, tn, KEY_TILE), lambda j, i: (i, j, 0))

    def wrows(first_blocks, n_first, second_start):
        return pl.BlockSpec((None, tn, k), lambda j, i: (l, jnp.where(j < n_first, first_blocks + j,
                                                                      second_start + j - n_first), 0))

    assert MOBA_W == DSA_W and MOBA_W % tn == 0
    n_m, n_d = MOBA_W // tn, DSA_W // tn

    qt = pl.pallas_call(
        functools.partial(_wp_rope_t_kernel, head_dim=HEAD_DIM),
        grid=(n_m + n_d, ni),
        in_specs=[h_spec, wrows(0, n_m, o_dsa // tn), tabt, tabt],
        out_specs=out_t,
        out_shape=jax.ShapeDtypeStruct((m // KEY_TILE, MOBA_W + DSA_W, KEY_TILE), BF16),
        scratch_shapes=[pltpu.VMEM((tn, k), BF16)],
        compiler_params=_params(sem),
        name="proj_q",
    )(h, wt, *tables128[0])

    kk = pl.pallas_call(
        _wp_rope128_kernel,
        grid=(n_m + n_d, ni),
        in_specs=[h_spec, wrows(MOBA_W // tn, n_m, (o_dsa + DSA_W) // tn), tab, tab],
        out_specs=row,
        out_shape=jax.ShapeDtypeStruct((m, MOBA_W + DSA_W), BF16),
        scratch_shapes=[pltpu.VMEM((tn, k), BF16)],
        compiler_params=_params(sem),
        name="proj_k",
    )(h, wt, *tables128[1])

    vt = pl.pallas_call(
        _wp_vt_kernel,
        grid=(n_m + n_d, ni),
        in_specs=[h_spec, wrows(2 * MOBA_W // tn, n_m, (o_dsa + 2 * DSA_W) // tn)],
        out_specs=out_t,
        out_shape=jax.ShapeDtypeStruct((m // KEY_TILE, MOBA_W + DSA_W, KEY_TILE), BF16),
        scratch_shapes=[pltpu.VMEM((tn, k), BF16)],
        compiler_params=_params(sem),
        name="proj_v",
    )(h, wt)

    n_q = IDX_W // tn
    qit = pl.pallas_call(
        functools.partial(_wp_rope_t_kernel, head_dim=IDX_DIM),
        grid=(n_q, ni),
        in_specs=[h_spec, wrows(o_qi // tn, n_q, 0), tabt, tabt],
        out_specs=out_t,
        out_shape=jax.ShapeDtypeStruct((m // KEY_TILE, IDX_W, KEY_TILE), BF16),
        scratch_shapes=[pltpu.VMEM((tn, k), BF16)],
        compiler_params=_params(sem),
        name="proj_qi",
    )(h, wt, *tables64)

    tab1 = pl.BlockSpec((tm, LANES), lambda i: (i % nt, 0))
    row1 = pl.BlockSpec((tm, LANES), lambda i: (i, 0))
    klo, khi, wit = pl.pallas_call(
        _wp_kw_kernel,
        grid=(ni,),
        in_specs=[pl.BlockSpec((tm, k), lambda i: (i, 0)),
                  pl.BlockSpec((None, LANES, k), lambda i: (l, o_kw // LANES, 0)), tab1, tab1, tab1],
        out_specs=[row1, row1, pl.BlockSpec((tm // KEY_TILE, LANES, KEY_TILE), lambda i: (i, 0, 0))],
        out_shape=[jax.ShapeDtypeStruct((m, LANES), BF16), jax.ShapeDtypeStruct((m, LANES), BF16),
                   jax.ShapeDtypeStruct((m // KEY_TILE, LANES, KEY_TILE), F32)],
        scratch_shapes=[pltpu.VMEM((LANES, k), BF16)],
        compiler_params=_params(("arbitrary",)),
        name="proj_kw",
    )(h, wt, *tables_kw)

    tg = _tile(2 * d, tn)
    g0 = o_kw // tg
    shift = o_g - o_kw
    assert o_kw % tg == 0 and shift <= LANES and tg % LANES == 0
    ng = 2 * d // tg
    c_in, c_out, c_shape = _side_cast_specs([(w, 0) for w in out_weights], ng * ni,
                                            lambda j, i: j * ni + i, 16)
    sg, *out_weights_bf16 = pl.pallas_call(
        functools.partial(_wp_gate_kernel, shift=shift),
        grid=(ng, ni),
        in_specs=[h_spec,
                  pl.BlockSpec((None, tg, k), lambda j, i: (l, g0 + j, 0)),
                  pl.BlockSpec((None, LANES, k), lambda j, i: (l, (g0 + j + 1) * (tg // LANES), 0))] + c_in,
        out_specs=[pl.BlockSpec((tm, tg), lambda j, i: (i, j))] + c_out,
        out_shape=[jax.ShapeDtypeStruct((m, 2 * d), BF16)] + c_shape,
        scratch_shapes=[pltpu.VMEM((tg, k), BF16)],
        compiler_params=_params(sem),
        name="proj_gate",
    )(h, wt, wt, *out_weights)
    return qt, kk, vt, qit, klo, khi, wit, sg, out_weights_bf16


def _moba_kmean_kernel(k_ref, o_ref, km_ref, *, nb):
    km_ref[...] = jnp.zeros(km_ref.shape, F32)
    for j in range(nb):
        blk = k_ref[j * MOBA_BLOCK:(j + 1) * MOBA_BLOCK, :].astype(F32)
        km_ref[j:j + 1, :] = jnp.sum(blk, axis=0, keepdims=True) * (1.0 / MOBA_BLOCK)
    km = km_ref[...]
    tiled = jnp.concatenate([km] * MOBA_HEADS, axis=0)
    r = lax.shift_right_logical(lax.broadcasted_iota(jnp.int32, tiled.shape, 0), 4)
    c = lax.shift_right_logical(lax.broadcasted_iota(jnp.int32, tiled.shape, 1), 7)
    o_ref[...] = jnp.where(r == c, tiled, 0.0).astype(o_ref.dtype)


def _moba_kmean(kk3, nb):
    b, t, _ = kk3.shape
    return pl.pallas_call(
        functools.partial(_moba_kmean_kernel, nb=nb),
        grid=(b,),
        in_specs=[pl.BlockSpec((None, t, MOBA_W), lambda bi: (bi, 0, 0))],
        out_specs=pl.BlockSpec((None, LANES, MOBA_W), lambda bi: (bi, 0, 0)),
        out_shape=jax.ShapeDtypeStruct((b, LANES, MOBA_W), BF16),
        scratch_shapes=[pltpu.VMEM((MAX_BLOCKS, MOBA_W), F32)],
        compiler_params=_params(("parallel",)),
        name="moba_kmean",
    )(kk3)


def _moba_block_bias(qt_ref, kmt_ref, sbt_ref, i, n_sel):
    g = jnp.dot(kmt_ref[...], qt_ref[...], preferred_element_type=F32)
    shape = (MAX_BLOCKS, g.shape[1])
    j = lax.broadcasted_iota(jnp.int32, shape, 0)
    past = j < i
    for h in range(MOBA_HEADS):
        gm = jnp.where(past, g[h * MAX_BLOCKS:(h + 1) * MAX_BLOCKS, :], -jnp.inf)
        rank = jnp.zeros(shape, F32)
        for jo in range(MAX_BLOCKS):
            other = gm[jo:jo + 1, :]
            beats = (other > gm) | ((other == gm) & (jo < j))
            rank = rank + jnp.where(beats, 1.0, 0.0)
        bias = jnp.where(past & (rank < n_sel), 0.0, MASK_BIAS)
        for jb in range(MAX_BLOCKS):
            sbt_ref[h, jb] = jnp.broadcast_to(bias[jb:jb + 1, :], (8, shape[1]))


def _flash_step(st, m, l, acc_t, v_t):
    m_new = jnp.maximum(m, jnp.max(st, axis=0, keepdims=True))
    alpha = jnp.exp2(m - m_new)
    p = jnp.exp2(st - m_new)
    l = alpha * l + jnp.sum(p, axis=0, keepdims=True)
    acc_t = alpha * acc_t + jnp.dot(v_t, p.astype(BF16), preferred_element_type=F32)
    return m_new, l, acc_t


def _flash_step_joint(sts, m, l, acc_t, v_ts):
    m_new = m
    for st in sts:
        m_new = jnp.maximum(m_new, jnp.max(st, axis=0, keepdims=True))
    alpha = jnp.exp2(m - m_new)
    l = alpha * l
    acc_t = alpha * acc_t
    for st, v_t in zip(sts, v_ts):
        p = jnp.exp2(st - m_new)
        l = l + jnp.sum(p, axis=0, keepdims=True)
        acc_t = acc_t + jnp.dot(v_t, p.astype(BF16), preferred_element_type=F32)
    return m_new, l, acc_t


def _flash_finish(o_ref, l_ref, acc_ref, heads):
    for h in range(heads):
        hs = slice(h * HEAD_DIM, (h + 1) * HEAD_DIM)
        o_ref[:, hs] = (acc_ref[h] / l_ref[h:h + 1, :]).T.astype(o_ref.dtype)


def _moba_attn_kernel(qt_ref, k_ref, vt_ref, kmt_ref, o_ref, m_ref, l_ref, acc_ref, sbt_ref, *, n_sel):
    i = pl.program_id(1)
    own = pl.multiple_of(i * MOBA_BLOCK, MOBA_BLOCK)
    sub_shape = (SUB_TILE, MOBA_BLOCK)
    kidx = lax.broadcasted_iota(jnp.int32, sub_shape, 0)
    qidx = lax.broadcasted_iota(jnp.int32, sub_shape, 1)
    _moba_block_bias(qt_ref, kmt_ref, sbt_ref, i, n_sel)
    for h in range(MOBA_HEADS):
        hs = slice(h * HEAD_DIM, (h + 1) * HEAD_DIM)
        m = jnp.full((1, MOBA_BLOCK), NEG_BIG, F32)
        l = jnp.zeros((1, MOBA_BLOCK), F32)
        acc = jnp.zeros((HEAD_DIM, MOBA_BLOCK), F32)
        for u in range(MOBA_BLOCK // SUB_TILE):
            st = jnp.dot(k_ref[pl.ds(own + u * SUB_TILE, SUB_TILE), hs], qt_ref[hs, :],
                         preferred_element_type=F32)
            st = jnp.where(kidx + u * SUB_TILE <= qidx, st, -jnp.inf)
            m, l, acc = _flash_step(st, m, l, acc, vt_ref[i, hs, u * SUB_TILE:(u + 1) * SUB_TILE])
        m_ref[h:h + 1, :] = m
        l_ref[h:h + 1, :] = l
        acc_ref[h] = acc

    def blocks(js):
        for h in range(MOBA_HEADS):
            hs = slice(h * HEAD_DIM, (h + 1) * HEAD_DIM)
            m, l, acc = m_ref[h:h + 1, :], l_ref[h:h + 1, :], acc_ref[h]
            for j in js:
                off = pl.multiple_of(j * MOBA_BLOCK, MOBA_BLOCK)
                bias = sbt_ref[h, j][0:1, :]
                sts, vts = [], []
                for u in range(MOBA_BLOCK // SUB_TILE):
                    sts.append(jnp.dot(k_ref[pl.ds(off + u * SUB_TILE, SUB_TILE), hs], qt_ref[hs, :],
                                       preferred_element_type=F32) + bias)
                    vts.append(vt_ref[j, hs, u * SUB_TILE:(u + 1) * SUB_TILE])
                m, l, acc = _flash_step_joint(sts, m, l, acc, vts)
            m_ref[h:h + 1, :] = m
            l_ref[h:h + 1, :] = l
            acc_ref[h] = acc

    def body(t, carry):
        blocks((2 * t, 2 * t + 1))
        return carry

    lax.fori_loop(0, i // 2, body, 0)

    @pl.when(i % 2 == 1)
    def _():
        blocks((i - 1,))

    _flash_finish(o_ref, l_ref, acc_ref, MOBA_HEADS)


def _moba_attn(qt3, kk3, vt3, kmt, nb, n_sel):
    b, t, _ = kk3.shape
    one = pl.Buffered(1)
    return pl.pallas_call(
        functools.partial(_moba_attn_kernel, n_sel=n_sel),
        grid=(b, nb),
        in_specs=[pl.BlockSpec((None, MOBA_W, MOBA_BLOCK), lambda bi, i: (bi * nb + i, 0, 0)),
                  pl.BlockSpec((None, t, MOBA_W), lambda bi, i: (bi, 0, 0), pipeline_mode=one),
                  pl.BlockSpec((nb, MOBA_W, KEY_TILE), lambda bi, i: (bi, 0, 0), pipeline_mode=one),
                  pl.BlockSpec((None, LANES, MOBA_W), lambda bi, i: (bi, 0, 0), pipeline_mode=one)],
        out_specs=pl.BlockSpec((None, MOBA_BLOCK, MOBA_W), lambda bi, i: (bi, i, 0)),
        out_shape=jax.ShapeDtypeStruct((b, t, MOBA_W), BF16),
        scratch_shapes=[pltpu.VMEM((MOBA_HEADS, MOBA_BLOCK), F32),
                        pltpu.VMEM((MOBA_HEADS, MOBA_BLOCK), F32),
                        pltpu.VMEM((MOBA_HEADS, HEAD_DIM, MOBA_BLOCK), F32),
                        pltpu.VMEM((MOBA_HEADS, MAX_BLOCKS, 8, MOBA_BLOCK), F32)],
        compiler_params=_params(("parallel", "arbitrary")),
        name="moba_attn",
    )(qt3, kk3, vt3, kmt)


def _float_to_key(x):
    b = lax.bitcast_convert_type(x, jnp.int32)
    return jnp.where(b >= 0, b, b ^ 0x7FFFFFFF)


def _key_to_float(t):
    return lax.bitcast_convert_type(jnp.where(t >= 0, t, t ^ 0x7FFFFFFF), F32)


def _dsa_kernel(qit_ref, klo_ref, khi_ref, wt_ref, qt_ref, k_ref, vt_ref, o_ref,
                sc_ref, jl_ref, m_ref, l_ref, acc_ref, *, topk, idx_scale, tq, seq_bits):
    i = pl.program_id(1)
    nk = i + 1
    shape = (KEY_TILE, tq)
    krow = lax.broadcasted_iota(jnp.int32, shape, 0)
    qpos = i * tq + lax.broadcasted_iota(jnp.int32, shape, 1)

    wt = wt_ref[...]

    def score_tiles(kts):
        offs = [pl.multiple_of(kt * KEY_TILE, KEY_TILE) for kt in kts]
        keys = jnp.concatenate(
            [r[pl.ds(off, KEY_TILE), :] for off in offs for r in (klo_ref, khi_ref)], axis=0)
        accs = [jnp.zeros(shape, F32) for _ in kts]
        for p in range(IDX_HEADS // 2):
            s = jnp.dot(keys, qit_ref[p * LANES:(p + 1) * LANES, :], preferred_element_type=F32)
            for n in range(len(kts)):
                lo = s[2 * n * KEY_TILE:(2 * n + 1) * KEY_TILE]
                hi = s[(2 * n + 1) * KEY_TILE:(2 * n + 2) * KEY_TILE]
                accs[n] = accs[n] + (jnp.maximum(lo, 0.0) * wt[2 * p:2 * p + 1, :]
                                     + jnp.maximum(hi, 0.0) * wt[2 * p + 1:2 * p + 2, :])
        for kt, acc in zip(kts, accs):
            kpos = kt * KEY_TILE + krow
            sc_ref[kt] = jnp.where(kpos <= qpos, acc * idx_scale, -jnp.inf)

    def score_body(t, carry):
        score_tiles((2 * t, 2 * t + 1))
        return carry

    lax.fori_loop(0, nk // 2, score_body, 0)

    @pl.when(nk % 2 == 1)
    def _():
        score_tiles((nk - 1,))

    def count(pred):
        def body(kt, acc):
            hit = jnp.where(pred(sc_ref[kt], kt * KEY_TILE + krow), 1.0, 0.0)
            return acc + jnp.sum(hit.reshape(KEY_TILE // 32, 32, tq), axis=0)
        part = lax.fori_loop(0, nk, body, jnp.zeros((32, tq), F32))
        return jnp.sum(part, axis=0, keepdims=True)

    need_select = nk * tq > topk

    @pl.when(jnp.logical_not(need_select))
    def _():
        def body(kt, carry):
            sc_ref[kt] = jnp.where(kt * KEY_TILE + krow <= qpos, 0.0, NEG_BIG)
            return carry
        lax.fori_loop(0, nk, body, 0)

    @pl.when(need_select)
    def _():
        kf = jnp.float32(topk)

        def ext_body(kt, carry):
            s = sc_ref[kt]
            lo = jnp.min(jnp.where(s > -jnp.inf, s, jnp.inf).reshape(KEY_TILE // 32, 32, tq), axis=0)
            hi = jnp.max(s.reshape(KEY_TILE // 32, 32, tq), axis=0)
            return jnp.minimum(carry[0], lo), jnp.maximum(carry[1], hi)

        lo, hi = lax.fori_loop(0, nk, ext_body, (jnp.full((32, tq), jnp.inf, F32),
                                                 jnp.full((32, tq), -jnp.inf, F32)))
        lo = jnp.min(lo, axis=0, keepdims=True)
        hi = _key_to_float(_float_to_key(jnp.max(hi, axis=0, keepdims=True)) + 1)
        n_fin = count(lambda s, kp: s >= lo)
        done = jnp.where(n_fin <= kf, 1.0, 0.0)

        def step(state):
            lo, hi, clo, done = state
            mid = lo * 0.5 + hi * 0.5
            stuck = (mid <= lo) | (mid >= hi)
            c = count(lambda s, kp: s >= mid)
            up = (c >= kf) & (done < 0.5)
            down = (c < kf) & (done < 0.5)
            lo = jnp.where(up, mid, lo)
            clo = jnp.where(up, c, clo)
            hi = jnp.where(down, mid, hi)
            done = jnp.where(stuck | (clo == kf), 1.0, done)
            return lo, hi, clo, done

        state = lax.fori_loop(0, BISECT_FIXED, lambda _, st: step(st), (lo, hi, n_fin, done))
        state = lax.while_loop(lambda st: jnp.min(st[3]) < 0.5, lambda st: step(step(st)), state)
        thr = jnp.where(n_fin < kf, -jnp.inf, state[0])

        need = kf - count(lambda s, kp: s > thr)
        n_eq = count(lambda s, kp: s == thr)
        jl_ref[...] = jnp.full(jl_ref.shape, 2 ** seq_bits, jnp.int32)

        @pl.when(jnp.max(n_eq - need) > 0.0)
        def _():
            def idx_body(b, c):
                cand = c | lax.shift_left(jnp.int32(1), seq_bits - 1 - b)
                g = count(lambda s, kp: (s == thr) & (kp < cand))
                return jnp.where(g < need, cand, c)
            c = lax.fori_loop(0, seq_bits, idx_body, jnp.zeros((1, tq), jnp.int32))
            jl_ref[...] = jnp.broadcast_to(c, jl_ref.shape)

        bound = jnp.minimum(jl_ref[0:1, :], qpos[0:1, :])

        def bias_body(kt, carry):
            s = sc_ref[kt]
            sel = (s > thr) | ((s == thr) & (kt * KEY_TILE + krow <= bound))
            sc_ref[kt] = jnp.where(sel, 0.0, NEG_BIG)
            return carry
        lax.fori_loop(0, nk, bias_body, 0)

    m_ref[...] = jnp.full(m_ref.shape, NEG_BIG, F32)
    l_ref[...] = jnp.zeros(l_ref.shape, F32)
    acc_ref[...] = jnp.zeros(acc_ref.shape, F32)

    def tiles(kts):
        for h in range(DSA_HEADS):
            hs = slice(h * HEAD_DIM, (h + 1) * HEAD_DIM)
            m, l, acc = m_ref[h:h + 1, :], l_ref[h:h + 1, :], acc_ref[h]
            for kt in kts:
                off = pl.multiple_of(kt * KEY_TILE, KEY_TILE)
                for u in range(KEY_TILE // SUB_TILE):
                    us = slice(u * SUB_TILE, (u + 1) * SUB_TILE)
                    st = jnp.dot(k_ref[pl.ds(off + u * SUB_TILE, SUB_TILE), hs], qt_ref[hs, :],
                                 preferred_element_type=F32) + sc_ref[kt, us, :]
                    m, l, acc = _flash_step(st, m, l, acc, vt_ref[kt, hs, us])
            m_ref[h:h + 1, :] = m
            l_ref[h:h + 1, :] = l
            acc_ref[h] = acc

    def attn_body(t, carry):
        tiles((2 * t, 2 * t + 1))
        return carry

    lax.fori_loop(0, nk // 2, attn_body, 0)

    @pl.when(nk % 2 == 1)
    def _():
        tiles((nk - 1,))

    _flash_finish(o_ref, l_ref, acc_ref, DSA_HEADS)


def _dsa_attn(qit3, klo3, khi3, wit, qt3, kk3, vt3, topk):
    b, t, _ = kk3.shape
    tq = KEY_TILE
    nq = t // tq
    seq_bits = max(1, (t - 1).bit_length())
    one = pl.Buffered(1)
    kern = functools.partial(
        _dsa_kernel, topk=topk,
        idx_scale=(IDX_DIM ** -0.5) * (IDX_HEADS ** -0.5), tq=tq, seq_bits=seq_bits)
    return pl.pallas_call(
        kern,
        grid=(b, nq),
        in_specs=[pl.BlockSpec((None, IDX_W, tq), lambda bi, i: (bi * nq + i, 0, 0)),
                  pl.BlockSpec((None, t, LANES), lambda bi, i: (bi, 0, 0), pipeline_mode=one),
                  pl.BlockSpec((None, t, LANES), lambda bi, i: (bi, 0, 0), pipeline_mode=one),
                  pl.BlockSpec((None, IDX_HEADS, tq), lambda bi, i: (bi * nq + i, IDX_DIM // IDX_HEADS, 0)),
                  pl.BlockSpec((None, DSA_W, tq), lambda bi, i: (bi * nq + i, 1, 0)),
                  pl.BlockSpec((None, t, DSA_W), lambda bi, i: (bi, 0, 1), pipeline_mode=one),
                  pl.BlockSpec((nq, DSA_W, KEY_TILE), lambda bi, i: (bi, 1, 0), pipeline_mode=one)],
        out_specs=pl.BlockSpec((None, tq, DSA_W), lambda bi, i: (bi, i, 0)),
        out_shape=jax.ShapeDtypeStruct((b, t, DSA_W), BF16),
        scratch_shapes=[pltpu.VMEM((nq, KEY_TILE, tq), F32),
                        pltpu.VMEM((8, tq), jnp.int32),
                        pltpu.VMEM((DSA_HEADS, tq), F32),
                        pltpu.VMEM((DSA_HEADS, tq), F32),
                        pltpu.VMEM((DSA_HEADS, HEAD_DIM, tq), F32)],
        compiler_params=_params(("parallel", "arbitrary")),
        name="dsa_attn",
    )(qit3, klo3, khi3, wit, qt3, kk3, vt3)


def _mix_out_kernel(oa_ref, ob_ref, sg_ref, wa_ref, wb_ref, wo_ref, x_ref, gt_ref, gpost_ref,
                    gpre_ref, sc_ref, sh_ref, w1f_ref, w2f_ref, x1_ref, h2_ref, w1b_ref, w2b_ref, *, d):
    w1b_ref[...] = w1f_ref[...].astype(BF16)
    w2b_ref[...] = w2f_ref[...].astype(BF16)
    ya = jnp.dot(oa_ref[...], wa_ref[...], preferred_element_type=F32)
    yb = jnp.dot(ob_ref[...], wb_ref[...], preferred_element_type=F32)
    z = sg_ref[:, :d].astype(F32) * ya + sg_ref[:, d:].astype(F32) * yb
    y = jnp.dot(z.astype(BF16), wo_ref[...], preferred_element_type=F32)
    x1 = x_ref[...] + gt_ref[...] * (_rms(y) * gpost_ref[...])
    x1_ref[...] = x1
    h2_ref[...] = ((_rms(x1) * gpre_ref[...]) * (1.0 + sc_ref[...]) + sh_ref[...]).astype(h2_ref.dtype)


def _mix_out(oa3, ob3, sg3, wa, wb, wo, x, gt1, g_post, g_pre2, sc2, sh2, w1f, w2f):
    b, t, d = x.shape
    tm = _tile(t, 256)
    nt = t // tm
    c_in, c_out, c_shape = _side_cast_specs([(w1f, 1), (w2f, 0)], b * nt, lambda bi, i: bi * nt + i, 16)
    one = pl.Buffered(1)
    row = lambda w: pl.BlockSpec((None, tm, w), lambda bi, i: (bi, i, 0))
    full = lambda a: pl.BlockSpec(a.shape, lambda bi, i: (0, 0), pipeline_mode=one)
    vec = pl.BlockSpec((None, 1, d), lambda bi, i: (bi, 0, 0))
    gain = pl.BlockSpec((1, d), lambda bi, i: (0, 0))
    return pl.pallas_call(
        functools.partial(_mix_out_kernel, d=d),
        grid=(b, t // tm),
        in_specs=[row(MOBA_W), row(DSA_W), row(2 * d), full(wa), full(wb), full(wo), row(d),
                  vec, gain, gain, vec, vec] + c_in,
        out_specs=[row(d), row(d)] + c_out,
        out_shape=[jax.ShapeDtypeStruct((b, t, d), F32), jax.ShapeDtypeStruct((b, t, d), BF16)] + c_shape,
        compiler_params=_params(("arbitrary", "arbitrary")),
        name="mix_out",
    )(oa3, ob3, sg3, wa, wb, wo, x, gt1, g_post, g_pre2, sc2, sh2, w1f, w2f)


def _ffn_kernel(h_ref, w1_ref, w2_ref, x1_ref, gt_ref, g_ref, o_ref, acc_ref):
    j = pl.program_id(2)

    @pl.when(j == 0)
    def _():
        acc_ref[...] = jnp.zeros(acc_ref.shape, F32)

    for rows in _row_chunks(h_ref, KEY_TILE):
        u = jnp.dot(h_ref[rows, :], w1_ref[...], preferred_element_type=F32)
        u = jnp.square(jnp.maximum(u, 0.0)).astype(BF16)
        acc_ref[rows, :] += jnp.dot(u, w2_ref[...], preferred_element_type=F32)

    @pl.when(j == pl.num_programs(2) - 1)
    def _():
        o_ref[...] = x1_ref[...] + gt_ref[...] * (_rms(acc_ref[...]) * g_ref[...])


def _ffn(h2, w1, w2, x1, gt2, g_post):
    b, t, d = x1.shape
    ff = w1.shape[1]
    tm = _tile(t, 512)
    tf = _tile(ff, 1024)
    row = lambda: pl.BlockSpec((None, tm, d), lambda bi, i, j: (bi, i, 0))
    return pl.pallas_call(
        _ffn_kernel,
        grid=(b, t // tm, ff // tf),
        in_specs=[row(),
                  pl.BlockSpec((d, tf), lambda bi, i, j: (0, j)),
                  pl.BlockSpec((tf, d), lambda bi, i, j: (j, 0)),
                  row(),
                  pl.BlockSpec((None, 1, d), lambda bi, i, j: (bi, 0, 0)),
                  pl.BlockSpec((1, d), lambda bi, i, j: (0, 0))],
        out_specs=row(),
        out_shape=jax.ShapeDtypeStruct((b, t, d), F32),
        scratch_shapes=[pltpu.VMEM((tm, d), F32)],
        compiler_params=_params(("parallel", "parallel", "arbitrary")),
        name="ffn",
    )(h2, w1, w2, x1, gt2, g_post)


def _rope_tables(t):
    pos = jnp.arange(t, dtype=F32)[:, None]
    lane = jnp.arange(LANES)[None, :]

    def cos_sin(half):
        inv_freq = jnp.power(ROPE_THETA, -jnp.arange(half, dtype=F32) / half)
        ang = pos * inv_freq[None, :]
        reps = LANES // half
        return jnp.tile(jnp.cos(ang), (1, reps)), jnp.tile(jnp.sin(ang), (1, reps))

    cos128, sin128 = cos_sin(HEAD_DIM // 2)
    sin128 = jnp.where(lane < HEAD_DIM // 2, -sin128, sin128)
    rope128 = (((cos128 * QK_SCALE_LOG2).T, (sin128 * QK_SCALE_LOG2).T), (cos128, sin128))
    cos64, sin64 = cos_sin(IDX_DIM // 2)
    low = (lane % IDX_DIM) < IDX_DIM // 2
    sa64 = jnp.where(low, -sin64, 0.0)
    sb64 = jnp.where(low, 0.0, sin64)
    is_key = lane < IDX_DIM
    kw = (jnp.where(is_key, cos64, 1.0), jnp.where(is_key, sa64, 0.0), jnp.where(is_key, sb64, 0.0))
    return rope128, (cos64.T, (sa64 + sb64).T), kw


def kernel(x, c, w_ada, b_ada, g_pre_mix, g_post_mix, w_in, w_moba_out, w_dsa_out, w_o,
           g_pre_ffn, g_post_ffn, w_ff1, w_ff2):
    b, t, d = x.shape
    m = b * t
    nb = t // MOBA_BLOCK
    assert t % MOBA_BLOCK == 0 and nb <= MAX_BLOCKS and d % LANES == 0 and b <= 16
    n_sel = max(1, min(MOBA_TOPK, nb - 1))
    topk = min(DSA_TOPK_MAX, t // 4)
    rope128, rope64, rope_kw = _rope_tables(t)
    c_pad = jnp.zeros((16, d), F32).at[:b].set(c)

    for l in range(w_ada.shape[0]):
        mod = _ada(c_pad, w_ada[l], b_ada[l][None, :])[:b]
        sh1, sc1, gt1, sh2, sc2, gt2 = [v[:, None, :] for v in jnp.split(mod, 6, axis=-1)]

        h = _norm_mod(x, g_pre_mix[l][None, :], sc1, sh1).reshape(m, d)
        qt3, kk, vt3, qit3, klo, khi, wit, sg, (wa, wb, wo) = _in_proj(
            h, jnp.swapaxes(w_in, 1, 2), l, t, rope128, rope64, rope_kw,
            (w_moba_out[l], w_dsa_out[l], w_o[l]))

        kk3 = kk.reshape(b, t, -1)
        kmt = _moba_kmean(kk3, nb)
        oa = _moba_attn(qt3, kk3, vt3, kmt, nb, n_sel)
        ob = _dsa_attn(qit3, klo.reshape(b, t, -1), khi.reshape(b, t, -1), wit, qt3, kk3, vt3, topk)

        x, h2, w1, w2 = _mix_out(oa, ob, sg.reshape(b, t, -1), wa, wb, wo, x, gt1,
                                 g_post_mix[l][None, :], g_pre_ffn[l][None, :], sc2, sh2,
                                 w_ff1[l], w_ff2[l])
        x = _ffn(h2, w1, w2, x, gt2, g_post_ffn[l][None, :])
    return x
```

```python
import functools

import jax
import jax.numpy as jnp
from jax import lax
from jax.experimental import pallas as pl
from jax.experimental.pallas import tpu as pltpu

HEAD_DIM = 128
MOBA_HEADS = 8
MOBA_BLOCK = 256
MOBA_TOPK = 3
DSA_HEADS = 8
IDX_HEADS = 16
IDX_DIM = 64
DSA_TOPK_MAX = 256
ROPE_THETA = 10000.0
RMS_EPS = 1e-6

MOBA_W = MOBA_HEADS * HEAD_DIM
DSA_W = DSA_HEADS * HEAD_DIM
IDX_W = IDX_HEADS * IDX_DIM
LANES = 128
MAX_BLOCKS = LANES // MOBA_HEADS
KEY_TILE = 256
SUB_TILE = 128
ROW_CHUNK = 512
LOG2_E = 1.4426950408889634
QK_SCALE_LOG2 = HEAD_DIM ** -0.5 * LOG2_E
MASK_BIAS = -30000.0
NEG_BIG = -1e30
BISECT_FIXED = 18
VMEM_LIMIT = 56 * 1024 * 1024

F32 = jnp.float32
BF16 = jnp.bfloat16
_NT = (((1,), (1,)), ((), ()))


def _params(sem):
    return pltpu.CompilerParams(dimension_semantics=sem, vmem_limit_bytes=VMEM_LIMIT)


def _tile(n, pref):
    if n <= pref:
        return n
    t = pref - pref % LANES
    while t >= LANES:
        if n % t == 0:
            return t
        t -= LANES
    return n


def _side_cast_specs(arrays, n_steps, step_of, row_blocks):
    ins, outs, shapes = [], [], []
    for w, axis in arrays:
        n = w.shape[axis]
        unit = row_blocks if axis == 0 else LANES
        target = max(unit, n // n_steps)
        slab = next(c for c in range(target - target % unit, 0, -unit) if n % c == 0)
        last = n // slab - 1
        block = (slab, w.shape[1]) if axis == 0 else (w.shape[0], slab)

        def imap(*g, axis=axis, last=last):
            idx = jnp.minimum(step_of(*g), last)
            return (idx, 0) if axis == 0 else (0, idx)

        ins.append(pl.BlockSpec(block, imap))
        outs.append(pl.BlockSpec(block, imap))
        shapes.append(jax.ShapeDtypeStruct(w.shape, BF16))
    return ins, outs, shapes


def _rms(x):
    return x * lax.rsqrt(jnp.mean(x * x, axis=-1, keepdims=True) + RMS_EPS)


def _ada_kernel(c_ref, w_ref, b_ref, o_ref):
    c = c_ref[...]
    cs = (c * jax.nn.sigmoid(c)).astype(BF16)
    o_ref[...] = jnp.dot(cs, w_ref[...].astype(BF16), preferred_element_type=F32) + b_ref[...]


def _ada(c_pad, w, b):
    rows, d = c_pad.shape
    n = w.shape[1]
    tn = _tile(n, 1024)
    return pl.pallas_call(
        _ada_kernel,
        grid=(n // tn,),
        in_specs=[pl.BlockSpec((rows, d), lambda j: (0, 0)),
                  pl.BlockSpec((d, tn), lambda j: (0, j)),
                  pl.BlockSpec((1, tn), lambda j: (0, j))],
        out_specs=pl.BlockSpec((rows, tn), lambda j: (0, j)),
        out_shape=jax.ShapeDtypeStruct((rows, n), F32),
        compiler_params=_params(("parallel",)),
        name="ada_mod",
    )(c_pad, w, b)


def _norm_mod_kernel(x_ref, g_ref, sc_ref, sh_ref, o_ref):
    y = _rms(x_ref[...])
    o_ref[...] = ((y * g_ref[...]) * (1.0 + sc_ref[...]) + sh_ref[...]).astype(o_ref.dtype)


def _norm_mod(x, g, sc, sh):
    b, t, d = x.shape
    tt = _tile(t, 1024)
    vec = pl.BlockSpec((None, 1, d), lambda bi, ti: (bi, 0, 0))
    return pl.pallas_call(
        _norm_mod_kernel,
        grid=(b, t // tt),
        in_specs=[pl.BlockSpec((None, tt, d), lambda bi, ti: (bi, ti, 0)),
                  pl.BlockSpec((1, d), lambda bi, ti: (0, 0)), vec, vec],
        out_specs=pl.BlockSpec((None, tt, d), lambda bi, ti: (bi, ti, 0)),
        out_shape=jax.ShapeDtypeStruct((b, t, d), BF16),
        compiler_params=_params(("parallel", "parallel")),
        name="norm_mod",
    )(x, g, sc, sh)


def _first_token_tile():
    return pl.program_id(1) == 0


def _rope64(y, cos, sin_a, sin_b):
    return y * cos + pltpu.roll(y, LANES - 32, 1) * sin_a + pltpu.roll(y, 32, 1) * sin_b


def _row_chunks(ref, step=ROW_CHUNK):
    n = ref.shape[0]
    step = min(n, step)
    return [slice(r, r + step) for r in range(0, n, step)]


def _wp_rope128_kernel(h_ref, w_ref, cos_ref, sin_ref, o_ref, wb_ref):
    @pl.when(_first_token_tile())
    def _():
        wb_ref[...] = w_ref[...].astype(BF16)

    for rows in _row_chunks(h_ref):
        acc = lax.dot_general(h_ref[rows, :], wb_ref[...], _NT, preferred_element_type=F32)
        cos = cos_ref[rows, :]
        sin = sin_ref[rows, :]
        for g in range(acc.shape[1] // LANES):
            y = acc[:, g * LANES:(g + 1) * LANES]
            o_ref[rows, g * LANES:(g + 1) * LANES] = (
                y * cos + pltpu.roll(y, LANES // 2, 1) * sin).astype(o_ref.dtype)


def _swap_row_halves(y, width):
    half = width // 2
    parts = []
    for r in range(0, y.shape[0], width):
        parts += [y[r + half:r + width], y[r:r + half]]
    return jnp.concatenate(parts, axis=0)


def _wp_rope_t_kernel(h_ref, w_ref, cos_ref, sin_ref, o_ref, wb_ref, *, head_dim):
    @pl.when(_first_token_tile())
    def _():
        wb_ref[...] = w_ref[...].astype(BF16)

    for c in range(o_ref.shape[0]):
        tok = slice(c * KEY_TILE, (c + 1) * KEY_TILE)
        r = lax.dot_general(wb_ref[...], h_ref[tok, :], _NT, preferred_element_type=F32)
        for g in range(0, r.shape[0], LANES):
            y = r[g:g + LANES]
            o_ref[c, g:g + LANES, :] = (y * cos_ref[:, tok]
                                        + _swap_row_halves(y, head_dim) * sin_ref[:, tok]).astype(o_ref.dtype)


def _wp_vt_kernel(h_ref, w_ref, o_ref, wb_ref):
    @pl.when(_first_token_tile())
    def _():
        wb_ref[...] = w_ref[...].astype(BF16)

    for c in range(o_ref.shape[0]):
        r = lax.dot_general(wb_ref[...], h_ref[c * KEY_TILE:(c + 1) * KEY_TILE, :], _NT,
                            preferred_element_type=F32)
        o_ref[c] = r.astype(o_ref.dtype)


def _wp_gate_kernel(h_ref, wa_ref, wb_ref, c0_ref, c1_ref, c2_ref, o_ref, d0_ref, d1_ref, d2_ref, wg_ref,
                    *, shift):
    d0_ref[...] = c0_ref[...].astype(BF16)
    d1_ref[...] = c1_ref[...].astype(BF16)
    d2_ref[...] = c2_ref[...].astype(BF16)

    @pl.when(_first_token_tile())
    def _():
        keep = wg_ref.shape[0] - shift
        wg_ref[:keep, :] = wa_ref[shift:, :].astype(BF16)
        wg_ref[keep:, :] = wb_ref[:shift, :].astype(BF16)

    for rows in _row_chunks(h_ref):
        acc = lax.dot_general(h_ref[rows, :], wg_ref[...], _NT, preferred_element_type=F32)
        o_ref[rows, :] = jax.nn.sigmoid(acc).astype(o_ref.dtype)


def _wp_kw_kernel(h_ref, w_ref, cos_ref, sa_ref, sb_ref, klo_ref, khi_ref, wit_ref, wb_ref):
    @pl.when(pl.program_id(0) == 0)
    def _():
        r = lax.broadcasted_iota(jnp.int32, w_ref.shape, 0)
        wb_ref[...] = jnp.where(r < IDX_DIM + IDX_HEADS, w_ref[...], 0.0).astype(BF16)

    h = h_ref[...]
    y = lax.dot_general(h, wb_ref[...], _NT, preferred_element_type=F32)
    r = _rope64(y, cos_ref[...], sa_ref[...], sb_ref[...])
    lane = lax.broadcasted_iota(jnp.int32, y.shape, 1)
    klo = jnp.where(lane < IDX_DIM, r, 0.0)
    klo_ref[...] = klo.astype(BF16)
    khi_ref[...] = pltpu.roll(klo, IDX_DIM, 1).astype(BF16)
    for c in range(wit_ref.shape[0]):
        wit_ref[c] = y[c * KEY_TILE:(c + 1) * KEY_TILE, :].T


def _in_proj(h, wt, l, t, tables128, tables64, tables_kw, out_weights):
    m, k = h.shape
    d = k
    tm = _tile(t, 1024)
    nt = t // tm
    ni = m // tm
    assert tm % KEY_TILE == 0
    tn = 1024
    o_dsa = 3 * MOBA_W
    o_qi = o_dsa + 3 * DSA_W
    o_kw = o_qi + IDX_W
    o_g = o_kw + IDX_DIM + IDX_HEADS
    sem = ("arbitrary", "arbitrary")
    h_spec = pl.BlockSpec((tm, k), lambda j, i: (i, 0))
    tab = pl.BlockSpec((tm, LANES), lambda j, i: (i % nt, 0))
    tabt = pl.BlockSpec((LANES, tm), lambda j, i: (0, i % nt))
    row = pl.BlockSpec((tm, tn), lambda j, i: (i, j))
    out_t = pl.BlockSpec((tm // KEY_TILE, tn, KEY_TILE), lambda j, i: (i, j, 0))

    def wrows(first_blocks, n_first, second_start):
        return pl.BlockSpec((None, tn, k), lambda j, i: (l, jnp.where(j < n_first, first_blocks + j,
                                                                      second_start + j - n_first), 0))

    assert MOBA_W == DSA_W and MOBA_W % tn == 0
    n_m, n_d = MOBA_W // tn, DSA_W // tn

    qt = pl.pallas_call(
        functools.partial(_wp_rope_t_kernel, head_dim=HEAD_DIM),
        grid=(n_m + n_d, ni),
        in_specs=[h_spec, wrows(0, n_m, o_dsa // tn), tabt, tabt],
        out_specs=out_t,
        out_shape=jax.ShapeDtypeStruct((m // KEY_TILE, MOBA_W + DSA_W, KEY_TILE), BF16),
        scratch_shapes=[pltpu.VMEM((tn, k), BF16)],
        compiler_params=_params(sem),
        name="proj_q",
    )(h, wt, *tables128[0])

    kk = pl.pallas_call(
        _wp_rope128_kernel,
        grid=(n_m + n_d, ni),
        in_specs=[h_spec, wrows(MOBA_W // tn, n_m, (o_dsa + DSA_W) // tn), tab, tab],
        out_specs=row,
        out_shape=jax.ShapeDtypeStruct((m, MOBA_W + DSA_W), BF16),
        scratch_shapes=[pltpu.VMEM((tn, k), BF16)],
        compiler_params=_params(sem),
        name="proj_k",
    )(h, wt, *tables128[1])

    vt = pl.pallas_call(
        _wp_vt_kernel,
        grid=(n_m + n_d, ni),
        in_specs=[h_spec, wrows(2 * MOBA_W // tn, n_m, (o_dsa + 2 * DSA_W) // tn)],
        out_specs=out_t,
        out_shape=jax.ShapeDtypeStruct((m // KEY_TILE, MOBA_W + DSA_W, KEY_TILE), BF16),
        scratch_shapes=[pltpu.VMEM((tn, k), BF16)],
        compiler_params=_params(sem),
        name="proj_v",
    )(h, wt)

    n_q = IDX_W // tn
    qit = pl.pallas_call(
        functools.partial(_wp_rope_t_kernel, head_dim=IDX_DIM),
        grid=(n_q, ni),
        in_specs=[h_spec, wrows(o_qi // tn, n_q, 0), tabt, tabt],
        out_specs=out_t,
        out_shape=jax.ShapeDtypeStruct((m // KEY_TILE, IDX_W, KEY_TILE), BF16),
        scratch_shapes=[pltpu.VMEM((tn, k), BF16)],
        compiler_params=_params(sem),
        name="proj_qi",
    )(h, wt, *tables64)

    tab1 = pl.BlockSpec((tm, LANES), lambda i: (i % nt, 0))
    row1 = pl.BlockSpec((tm, LANES), lambda i: (i, 0))
    klo, khi, wit = pl.pallas_call(
        _wp_kw_kernel,
        grid=(ni,),
        in_specs=[pl.BlockSpec((tm, k), lambda i: (i, 0)),
                  pl.BlockSpec((None, LANES, k), lambda i: (l, o_kw // LANES, 0)), tab1, tab1, tab1],
        out_specs=[row1, row1, pl.BlockSpec((tm // KEY_TILE, LANES, KEY_TILE), lambda i: (i, 0, 0))],
        out_shape=[jax.ShapeDtypeStruct((m, LANES), BF16), jax.ShapeDtypeStruct((m, LANES), BF16),
                   jax.ShapeDtypeStruct((m // KEY_TILE, LANES, KEY_TILE), F32)],
        scratch_shapes=[pltpu.VMEM((LANES, k), BF16)],
        compiler_params=_params(("arbitrary",)),
        name="proj_kw",
    )(h, wt, *tables_kw)

    tg = _tile(2 * d, tn)
    g0 = o_kw // tg
    shift = o_g - o_kw
    assert o_kw % tg == 0 and shift <= LANES and tg % LANES == 0
    ng = 2 * d // tg
    c_in, c_out, c_shape = _side_cast_specs([(w, 0) for w in out_weights], ng * ni,
                                            lambda j, i: j * ni + i, 16)
    sg, *out_weights_bf16 = pl.pallas_call(
        functools.partial(_wp_gate_kernel, shift=shift),
        grid=(ng, ni),
        in_specs=[h_spec,
                  pl.BlockSpec((None, tg, k), lambda j, i: (l, g0 + j, 0)),
                  pl.BlockSpec((None, LANES, k), lambda j, i: (l, (g0 + j + 1) * (tg // LANES), 0))] + c_in,
        out_specs=[pl.BlockSpec((tm, tg), lambda j, i: (i, j))] + c_out,
        out_shape=[jax.ShapeDtypeStruct((m, 2 * d), BF16)] + c_shape,
        scratch_shapes=[pltpu.VMEM((tg, k), BF16)],
        compiler_params=_params(sem),
        name="proj_gate",
    )(h, wt, wt, *out_weights)
    return qt, kk, vt, qit, klo, khi, wit, sg, out_weights_bf16


def _moba_kmean_kernel(k_ref, o_ref, km_ref, *, nb):
    km_ref[...] = jnp.zeros(km_ref.shape, F32)
    for j in range(nb):
        blk = k_ref[j * MOBA_BLOCK:(j + 1) * MOBA_BLOCK, :].astype(F32)
        km_ref[j:j + 1, :] = jnp.sum(blk, axis=0, keepdims=True) * (1.0 / MOBA_BLOCK)
    km = km_ref[...]
    tiled = jnp.concatenate([km] * MOBA_HEADS, axis=0)
    r = lax.shift_right_logical(lax.broadcasted_iota(jnp.int32, tiled.shape, 0), 4)
    c = lax.shift_right_logical(lax.broadcasted_iota(jnp.int32, tiled.shape, 1), 7)
    o_ref[...] = jnp.where(r == c, tiled, 0.0).astype(o_ref.dtype)


def _moba_kmean(kk3, nb):
    b, t, _ = kk3.shape
    return pl.pallas_call(
        functools.partial(_moba_kmean_kernel, nb=nb),
        grid=(b,),
        in_specs=[pl.BlockSpec((None, t, MOBA_W), lambda bi: (bi, 0, 0))],
        out_specs=pl.BlockSpec((None, LANES, MOBA_W), lambda bi: (bi, 0, 0)),
        out_shape=jax.ShapeDtypeStruct((b, LANES, MOBA_W), BF16),
        scratch_shapes=[pltpu.VMEM((MAX_BLOCKS, MOBA_W), F32)],
        compiler_params=_params(("parallel",)),
        name="moba_kmean",
    )(kk3)


def _moba_block_bias(qt_ref, kmt_ref, sbt_ref, i, n_sel):
    g = jnp.dot(kmt_ref[...], qt_ref[...], preferred_element_type=F32)
    shape = (MAX_BLOCKS, g.shape[1])
    j = lax.broadcasted_iota(jnp.int32, shape, 0)
    past = j < i
    for h in range(MOBA_HEADS):
        gm = jnp.where(past, g[h * MAX_BLOCKS:(h + 1) * MAX_BLOCKS, :], -jnp.inf)
        rank = jnp.zeros(shape, F32)
        for jo in range(MAX_BLOCKS):
            other = gm[jo:jo + 1, :]
            beats = (other > gm) | ((other == gm) & (jo < j))
            rank = rank + jnp.where(beats, 1.0, 0.0)
        bias = jnp.where(past & (rank < n_sel), 0.0, MASK_BIAS)
        for jb in range(MAX_BLOCKS):
            sbt_ref[h, jb] = jnp.broadcast_to(bias[jb:jb + 1, :], (8, shape[1]))


def _flash_step(st, m, l, acc_t, v_t):
    m_new = jnp.maximum(m, jnp.max(st, axis=0, keepdims=True))
    alpha = jnp.exp2(m - m_new)
    p = jnp.exp2(st - m_new)
    l = alpha * l + jnp.sum(p, axis=0, keepdims=True)
    acc_t = alpha * acc_t + jnp.dot(v_t, p.astype(BF16), preferred_element_type=F32)
    return m_new, l, acc_t


def _flash_step_joint(sts, m, l, acc_t, v_ts):
    m_new = m
    for st in sts:
        m_new = jnp.maximum(m_new, jnp.max(st, axis=0, keepdims=True))
    alpha = jnp.exp2(m - m_new)
    l = alpha * l
    acc_t = alpha * acc_t
    for st, v_t in zip(sts, v_ts):
        p = jnp.exp2(st - m_new)
        l = l + jnp.sum(p, axis=0, keepdims=True)
        acc_t = acc_t + jnp.dot(v_t, p.astype(BF16), preferred_element_type=F32)
    return m_new, l, acc_t


def _flash_finish(o_ref, l_ref, acc_ref, heads):
    for h in range(heads):
        hs = slice(h * HEAD_DIM, (h + 1) * HEAD_DIM)
        o_ref[:, hs] = (acc_ref[h] / l_ref[h:h + 1, :]).T.astype(o_ref.dtype)


def _moba_attn_kernel(qt_ref, k_ref, vt_ref, kmt_ref, o_ref, m_ref, l_ref, acc_ref, sbt_ref, *, n_sel):
    i = pl.program_id(1)
    own = pl.multiple_of(i * MOBA_BLOCK, MOBA_BLOCK)
    sub_shape = (SUB_TILE, MOBA_BLOCK)
    kidx = lax.broadcasted_iota(jnp.int32, sub_shape, 0)
    qidx = lax.broadcasted_iota(jnp.int32, sub_shape, 1)
    _moba_block_bias(qt_ref, kmt_ref, sbt_ref, i, n_sel)
    for h in range(MOBA_HEADS):
        hs = slice(h * HEAD_DIM, (h + 1) * HEAD_DIM)
        m = jnp.full((1, MOBA_BLOCK), NEG_BIG, F32)
        l = jnp.zeros((1, MOBA_BLOCK), F32)
        acc = jnp.zeros((HEAD_DIM, MOBA_BLOCK), F32)
        for u in range(MOBA_BLOCK // SUB_TILE):
            st = jnp.dot(k_ref[pl.ds(own + u * SUB_TILE, SUB_TILE), hs], qt_ref[hs, :],
                         preferred_element_type=F32)
            st = jnp.where(kidx + u * SUB_TILE <= qidx, st, -jnp.inf)
            m, l, acc = _flash_step(st, m, l, acc, vt_ref[i, hs, u * SUB_TILE:(u + 1) * SUB_TILE])
        m_ref[h:h + 1, :] = m
        l_ref[h:h + 1, :] = l
        acc_ref[h] = acc

    def blocks(js):
        for h in range(MOBA_HEADS):
            hs = slice(h * HEAD_DIM, (h + 1) * HEAD_DIM)
            m, l, acc = m_ref[h:h + 1, :], l_ref[h:h + 1, :], acc_ref[h]
            for j in js:
                off = pl.multiple_of(j * MOBA_BLOCK, MOBA_BLOCK)
                bias = sbt_ref[h, j][0:1, :]
                sts, vts = [], []
                for u in range(MOBA_BLOCK // SUB_TILE):
                    sts.append(jnp.dot(k_ref[pl.ds(off + u * SUB_TILE, SUB_TILE), hs], qt_ref[hs, :],
                                       preferred_element_type=F32) + bias)
                    vts.append(vt_ref[j, hs, u * SUB_TILE:(u + 1) * SUB_TILE])
                m, l, acc = _flash_step_joint(sts, m, l, acc, vts)
            m_ref[h:h + 1, :] = m
            l_ref[h:h + 1, :] = l
            acc_ref[h] = acc

    def body(t, carry):
        blocks((2 * t, 2 * t + 1))
        return carry

    lax.fori_loop(0, i // 2, body, 0)

    @pl.when(i % 2 == 1)
    def _():
        blocks((i - 1,))

    _flash_finish(o_ref, l_ref, acc_ref, MOBA_HEADS)


def _moba_attn(qt3, kk3, vt3, kmt, nb, n_sel):
    b, t, _ = kk3.shape
    one = pl.Buffered(1)
    return pl.pallas_call(
        functools.partial(_moba_attn_kernel, n_sel=n_sel),
        grid=(b, nb),
        in_specs=[pl.BlockSpec((None, MOBA_W, MOBA_BLOCK), lambda bi, i: (bi * nb + i, 0, 0)),
                  pl.BlockSpec((None, t, MOBA_W), lambda bi, i: (bi, 0, 0), pipeline_mode=one),
                  pl.BlockSpec((nb, MOBA_W, KEY_TILE), lambda bi, i: (bi, 0, 0), pipeline_mode=one),
                  pl.BlockSpec((None, LANES, MOBA_W), lambda bi, i: (bi, 0, 0), pipeline_mode=one)],
        out_specs=pl.BlockSpec((None, MOBA_BLOCK, MOBA_W), lambda bi, i: (bi, i, 0)),
        out_shape=jax.ShapeDtypeStruct((b, t, MOBA_W), BF16),
        scratch_shapes=[pltpu.VMEM((MOBA_HEADS, MOBA_BLOCK), F32),
                        pltpu.VMEM((MOBA_HEADS, MOBA_BLOCK), F32),
                        pltpu.VMEM((MOBA_HEADS, HEAD_DIM, MOBA_BLOCK), F32),
                        pltpu.VMEM((MOBA_HEADS, MAX_BLOCKS, 8, MOBA_BLOCK), F32)],
        compiler_params=_params(("parallel", "arbitrary")),
        name="moba_attn",
    )(qt3, kk3, vt3, kmt)


def _float_to_key(x):
    b = lax.bitcast_convert_type(x, jnp.int32)
    return jnp.where(b >= 0, b, b ^ 0x7FFFFFFF)


def _key_to_float(t):
    return lax.bitcast_convert_type(jnp.where(t >= 0, t, t ^ 0x7FFFFFFF), F32)


def _dsa_kernel(qit_ref, klo_ref, khi_ref, wt_ref, qt_ref, k_ref, vt_ref, o_ref,
                sc_ref, jl_ref, m_ref, l_ref, acc_ref, *, topk, idx_scale, tq, seq_bits):
    i = pl.program_id(1)
    nk = i + 1
    shape = (KEY_TILE, tq)
    krow = lax.broadcasted_iota(jnp.int32, shape, 0)
    qpos = i * tq + lax.broadcasted_iota(jnp.int32, shape, 1)

    wt = wt_ref[...]

    def score_tiles(kts):
        offs = [pl.multiple_of(kt * KEY_TILE, KEY_TILE) for kt in kts]
        keys = jnp.concatenate(
            [r[pl.ds(off, KEY_TILE), :] for off in offs for r in (klo_ref, khi_ref)], axis=0)
        accs = [jnp.zeros(shape, F32) for _ in kts]
        for p in range(IDX_HEADS // 2):
            s = jnp.dot(keys, qit_ref[p * LANES:(p + 1) * LANES, :], preferred_element_type=F32)
            for n in range(len(kts)):
                lo = s[2 * n * KEY_TILE:(2 * n + 1) * KEY_TILE]
                hi = s[(2 * n + 1) * KEY_TILE:(2 * n + 2) * KEY_TILE]
                accs[n] = accs[n] + (jnp.maximum(lo, 0.0) * wt[2 * p:2 * p + 1, :]
                                     + jnp.maximum(hi, 0.0) * wt[2 * p + 1:2 * p + 2, :])
        for kt, acc in zip(kts, accs):
            kpos = kt * KEY_TILE + krow
            sc_ref[kt] = jnp.where(kpos <= qpos, acc * idx_scale, -jnp.inf)

    def score_body(t, carry):
        score_tiles((2 * t, 2 * t + 1))
        return carry

    lax.fori_loop(0, nk // 2, score_body, 0)

    @pl.when(nk % 2 == 1)
    def _():
        score_tiles((nk - 1,))

    def count(pred):
        def body(kt, acc):
            hit = jnp.where(pred(sc_ref[kt], kt * KEY_TILE + krow), 1.0, 0.0)
            return acc + jnp.sum(hit.reshape(KEY_TILE // 32, 32, tq), axis=0)
        part = lax.fori_loop(0, nk, body, jnp.zeros((32, tq), F32))
        return jnp.sum(part, axis=0, keepdims=True)

    need_select = nk * tq > topk

    @pl.when(jnp.logical_not(need_select))
    def _():
        def body(kt, carry):
            sc_ref[kt] = jnp.where(kt * KEY_TILE + krow <= qpos, 0.0, NEG_BIG)
            return carry
        lax.fori_loop(0, nk, body, 0)

    @pl.when(need_select)
    def _():
        kf = jnp.float32(topk)

        def ext_body(kt, carry):
            s = sc_ref[kt]
            lo = jnp.min(jnp.where(s > -jnp.inf, s, jnp.inf).reshape(KEY_TILE // 32, 32, tq), axis=0)
            hi = jnp.max(s.reshape(KEY_TILE // 32, 32, tq), axis=0)
            return jnp.minimum(carry[0], lo), jnp.maximum(carry[1], hi)

        lo, hi = lax.fori_loop(0, nk, ext_body, (jnp.full((32, tq), jnp.inf, F32),
                                                 jnp.full((32, tq), -jnp.inf, F32)))
        lo = jnp.min(lo, axis=0, keepdims=True)
        hi = _key_to_float(_float_to_key(jnp.max(hi, axis=0, keepdims=True)) + 1)
        n_fin = count(lambda s, kp: s >= lo)
        done = jnp.where(n_fin <= kf, 1.0, 0.0)

        def step(state):
            lo, hi, clo, done = state
            mid = lo * 0.5 + hi * 0.5
            stuck = (mid <= lo) | (mid >= hi)
            c = count(lambda s, kp: s >= mid)
            up = (c >= kf) & (done < 0.5)
            down = (c < kf) & (done < 0.5)
            lo = jnp.where(up, mid, lo)
            clo = jnp.where(up, c, clo)
            hi = jnp.where(down, mid, hi)
            done = jnp.where(stuck | (clo == kf), 1.0, done)
            return lo, hi, clo, done

        state = lax.fori_loop(0, BISECT_FIXED, lambda _, st: step(st), (lo, hi, n_fin, done))
        state = lax.while_loop(lambda st: jnp.min(st[3]) < 0.5, lambda st: step(step(st)), state)
        thr = jnp.where(n_fin < kf, -jnp.inf, state[0])

        need = kf - count(lambda s, kp: s > thr)
        n_eq = count(lambda s, kp: s == thr)
        jl_ref[...] = jnp.full(jl_ref.shape, 2 ** seq_bits, jnp.int32)

        @pl.when(jnp.max(n_eq - need) > 0.0)
        def _():
            def idx_body(b, c):
                cand = c | lax.shift_left(jnp.int32(1), seq_bits - 1 - b)
                g = count(lambda s, kp: (s == thr) & (kp < cand))
                return jnp.where(g < need, cand, c)
            c = lax.fori_loop(0, seq_bits, idx_body, jnp.zeros((1, tq), jnp.int32))
            jl_ref[...] = jnp.broadcast_to(c, jl_ref.shape)

        bound = jnp.minimum(jl_ref[0:1, :], qpos[0:1, :])

        def bias_body(kt, carry):
            s = sc_ref[kt]
            sel = (s > thr) | ((s == thr) & (kt * KEY_TILE + krow <= bound))
            sc_ref[kt] = jnp.where(sel, 0.0, NEG_BIG)
            return carry
        lax.fori_loop(0, nk, bias_body, 0)

    m_ref[...] = jnp.full(m_ref.shape, NEG_BIG, F32)
    l_ref[...] = jnp.zeros(l_ref.shape, F32)
    acc_ref[...] = jnp.zeros(acc_ref.shape, F32)

    def tiles(kts):
        for h in range(DSA_HEADS):
            hs = slice(h * HEAD_DIM, (h + 1) * HEAD_DIM)
            m, l, acc = m_ref[h:h + 1, :], l_ref[h:h + 1, :], acc_ref[h]
            for kt in kts:
                off = pl.multiple_of(kt * KEY_TILE, KEY_TILE)
                for u in range(KEY_TILE // SUB_TILE):
                    us = slice(u * SUB_TILE, (u + 1) * SUB_TILE)
                    st = jnp.dot(k_ref[pl.ds(off + u * SUB_TILE, SUB_TILE), hs], qt_ref[hs, :],
                                 preferred_element_type=F32) + sc_ref[kt, us, :]
                    m, l, acc = _flash_step(st, m, l, acc, vt_ref[kt, hs, us])
            m_ref[h:h + 1, :] = m
            l_ref[h:h + 1, :] = l
            acc_ref[h] = acc

    def attn_body(t, carry):
        tiles((2 * t, 2 * t + 1))
        return carry

    lax.fori_loop(0, nk // 2, attn_body, 0)

    @pl.when(nk % 2 == 1)
    def _():
        tiles((nk - 1,))

    _flash_finish(o_ref, l_ref, acc_ref, DSA_HEADS)


def _dsa_attn(qit3, klo3, khi3, wit, qt3, kk3, vt3, topk):
    b, t, _ = kk3.shape
    tq = KEY_TILE
    nq = t // tq
    seq_bits = max(1, (t - 1).bit_length())
    one = pl.Buffered(1)
    kern = functools.partial(
        _dsa_kernel, topk=topk,
        idx_scale=(IDX_DIM ** -0.5) * (IDX_HEADS ** -0.5), tq=tq, seq_bits=seq_bits)
    return pl.pallas_call(
        kern,
        grid=(b, nq),
        in_specs=[pl.BlockSpec((None, IDX_W, tq), lambda bi, i: (bi * nq + i, 0, 0)),
                  pl.BlockSpec((None, t, LANES), lambda bi, i: (bi, 0, 0), pipeline_mode=one),
                  pl.BlockSpec((None, t, LANES), lambda bi, i: (bi, 0, 0), pipeline_mode=one),
                  pl.BlockSpec((None, IDX_HEADS, tq), lambda bi, i: (bi * nq + i, IDX_DIM // IDX_HEADS, 0)),
                  pl.BlockSpec((None, DSA_W, tq), lambda bi, i: (bi * nq + i, 1, 0)),
                  pl.BlockSpec((None, t, DSA_W), lambda bi, i: (bi, 0, 1), pipeline_mode=one),
                  pl.BlockSpec((nq, DSA_W, KEY_TILE), lambda bi, i: (bi, 1, 0), pipeline_mode=one)],
        out_specs=pl.BlockSpec((None, tq, DSA_W), lambda bi, i: (bi, i, 0)),
        out_shape=jax.ShapeDtypeStruct((b, t, DSA_W), BF16),
        scratch_shapes=[pltpu.VMEM((nq, KEY_TILE, tq), F32),
                        pltpu.VMEM((8, tq), jnp.int32),
                        pltpu.VMEM((DSA_HEADS, tq), F32),
                        pltpu.VMEM((DSA_HEADS, tq), F32),
                        pltpu.VMEM((DSA_HEADS, HEAD_DIM, tq), F32)],
        compiler_params=_params(("parallel", "arbitrary")),
        name="dsa_attn",
    )(qit3, klo3, khi3, wit, qt3, kk3, vt3)


def _mix_out_kernel(oa_ref, ob_ref, sg_ref, wa_ref, wb_ref, wo_ref, x_ref, gt_ref, gpost_ref,
                    gpre_ref, sc_ref, sh_ref, w1f_ref, w2f_ref, x1_ref, h2_ref, w1b_ref, w2b_ref, *, d):
    w1b_ref[...] = w1f_ref[...].astype(BF16)
    w2b_ref[...] = w2f_ref[...].astype(BF16)
    ya = jnp.dot(oa_ref[...], wa_ref[...], preferred_element_type=F32)
    yb = jnp.dot(ob_ref[...], wb_ref[...], preferred_element_type=F32)
    z = sg_ref[:, :d].astype(F32) * ya + sg_ref[:, d:].astype(F32) * yb
    y = jnp.dot(z.astype(BF16), wo_ref[...], preferred_element_type=F32)
    x1 = x_ref[...] + gt_ref[...] * (_rms(y) * gpost_ref[...])
    x1_ref[...] = x1
    h2_ref[...] = ((_rms(x1) * gpre_ref[...]) * (1.0 + sc_ref[...]) + sh_ref[...]).astype(h2_ref.dtype)


def _mix_out(oa3, ob3, sg3, wa, wb, wo, x, gt1, g_post, g_pre2, sc2, sh2, w1f, w2f):
    b, t, d = x.shape
    tm = _tile(t, 256)
    nt = t // tm
    c_in, c_out, c_shape = _side_cast_specs([(w1f, 1), (w2f, 0)], b * nt, lambda bi, i: bi * nt + i, 16)
    one = pl.Buffered(1)
    row = lambda w: pl.BlockSpec((None, tm, w), lambda bi, i: (bi, i, 0))
    full = lambda a: pl.BlockSpec(a.shape, lambda bi, i: (0, 0), pipeline_mode=one)
    vec = pl.BlockSpec((None, 1, d), lambda bi, i: (bi, 0, 0))
    gain = pl.BlockSpec((1, d), lambda bi, i: (0, 0))
    return pl.pallas_call(
        functools.partial(_mix_out_kernel, d=d),
        grid=(b, t // tm),
        in_specs=[row(MOBA_W), row(DSA_W), row(2 * d), full(wa), full(wb), full(wo), row(d),
                  vec, gain, gain, vec, vec] + c_in,
        out_specs=[row(d), row(d)] + c_out,
        out_shape=[jax.ShapeDtypeStruct((b, t, d), F32), jax.ShapeDtypeStruct((b, t, d), BF16)] + c_shape,
        compiler_params=_params(("arbitrary", "arbitrary")),
        name="mix_out",
    )(oa3, ob3, sg3, wa, wb, wo, x, gt1, g_post, g_pre2, sc2, sh2, w1f, w2f)


def _ffn_kernel(h_ref, w1_ref, w2_ref, x1_ref, gt_ref, g_ref, o_ref, acc_ref):
    j = pl.program_id(2)

    @pl.when(j == 0)
    def _():
        acc_ref[...] = jnp.zeros(acc_ref.shape, F32)

    for rows in _row_chunks(h_ref, KEY_TILE):
        u = jnp.dot(h_ref[rows, :], w1_ref[...], preferred_element_type=F32)
        u = jnp.square(jnp.maximum(u, 0.0)).astype(BF16)
        acc_ref[rows, :] += jnp.dot(u, w2_ref[...], preferred_element_type=F32)

    @pl.when(j == pl.num_programs(2) - 1)
    def _():
        o_ref[...] = x1_ref[...] + gt_ref[...] * (_rms(acc_ref[...]) * g_ref[...])


def _ffn(h2, w1, w2, x1, gt2, g_post):
    b, t, d = x1.shape
    ff = w1.shape[1]
    tm = _tile(t, 512)
    tf = _tile(ff, 1024)
    row = lambda: pl.BlockSpec((None, tm, d), lambda bi, i, j: (bi, i, 0))
    return pl.pallas_call(
        _ffn_kernel,
        grid=(b, t // tm, ff // tf),
        in_specs=[row(),
                  pl.BlockSpec((d, tf), lambda bi, i, j: (0, j)),
                  pl.BlockSpec((tf, d), lambda bi, i, j: (j, 0)),
                  row(),
                  pl.BlockSpec((None, 1, d), lambda bi, i, j: (bi, 0, 0)),
                  pl.BlockSpec((1, d), lambda bi, i, j: (0, 0))],
        out_specs=row(),
        out_shape=jax.ShapeDtypeStruct((b, t, d), F32),
        scratch_shapes=[pltpu.VMEM((tm, d), F32)],
        compiler_params=_params(("parallel", "parallel", "arbitrary")),
        name="ffn",
    )(h2, w1, w2, x1, gt2, g_post)


def _rope_tables(t):
    pos = jnp.arange(t, dtype=F32)
    lane = jnp.arange(LANES)

    def tables(half, token_minor):
        inv_freq = jnp.tile(jnp.power(ROPE_THETA, -jnp.arange(half, dtype=F32) / half), LANES // half)
        sign = jnp.where(lane % (2 * half) < half, -1.0, 1.0)
        if token_minor:
            ang, sign = inv_freq[:, None] * pos[None, :], sign[:, None]
        else:
            ang, sign = pos[:, None] * inv_freq[None, :], sign[None, :]
        return jnp.cos(ang), jnp.sin(ang) * sign

    cos_q, sin_q = tables(HEAD_DIM // 2, True)
    rope128 = ((cos_q * QK_SCALE_LOG2, sin_q * QK_SCALE_LOG2), tables(HEAD_DIM // 2, False))
    cos64, sin64 = tables(IDX_DIM // 2, False)
    is_key = (lane < IDX_DIM)[None, :]
    low = (lane % IDX_DIM < IDX_DIM // 2)[None, :]
    kw = (jnp.where(is_key, cos64, 1.0), jnp.where(is_key & low, sin64, 0.0),
          jnp.where(is_key & ~low, sin64, 0.0))
    return rope128, tables(IDX_DIM // 2, True), kw


def kernel(x, c, w_ada, b_ada, g_pre_mix, g_post_mix, w_in, w_moba_out, w_dsa_out, w_o,
           g_pre_ffn, g_post_ffn, w_ff1, w_ff2):
    b, t, d = x.shape
    m = b * t
    nb = t // MOBA_BLOCK
    assert t % MOBA_BLOCK == 0 and nb <= MAX_BLOCKS and d % LANES == 0 and b <= 16
    n_sel = max(1, min(MOBA_TOPK, nb - 1))
    topk = min(DSA_TOPK_MAX, t // 4)
    rope128, rope64, rope_kw = _rope_tables(t)
    c_pad = jnp.zeros((16, d), F32).at[:b].set(c)

    for l in range(w_ada.shape[0]):
        mod = _ada(c_pad, w_ada[l], b_ada[l][None, :])[:b]
        sh1, sc1, gt1, sh2, sc2, gt2 = [v[:, None, :] for v in jnp.split(mod, 6, axis=-1)]

        h = _norm_mod(x, g_pre_mix[l][None, :], sc1, sh1).reshape(m, d)
        qt3, kk, vt3, qit3, klo, khi, wit, sg, (wa, wb, wo) = _in_proj(
            h, jnp.swapaxes(w_in, 1, 2), l, t, rope128, rope64, rope_kw,
            (w_moba_out[l], w_dsa_out[l], w_o[l]))

        kk3 = kk.reshape(b, t, -1)
        kmt = _moba_kmean(kk3, nb)
        oa = _moba_attn(qt3, kk3, vt3, kmt, nb, n_sel)
        ob = _dsa_attn(qit3, klo.reshape(b, t, -1), khi.reshape(b, t, -1), wit, qt3, kk3, vt3, topk)

        x, h2, w1, w2 = _mix_out(oa, ob, sg.reshape(b, t, -1), wa, wb, wo, x, gt1,
                                 g_post_mix[l][None, :], g_pre_ffn[l][None, :], sc2, sh2,
                                 w_ff1[l], w_ff2[l])
        x = _ffn(h2, w1, w2, x, gt2, g_post_ffn[l][None, :])
    return x
```

```python
import functools

import jax
import jax.numpy as jnp
from jax import lax
from jax.experimental import pallas as pl
from jax.experimental.pallas import tpu as pltpu

HEAD_DIM = 128
MOBA_HEADS = 8
MOBA_BLOCK = 256
MOBA_TOPK = 3
DSA_HEADS = 8
IDX_HEADS = 16
IDX_DIM = 64
DSA_TOPK_MAX = 256
ROPE_THETA = 10000.0
RMS_EPS = 1e-6

MOBA_W = MOBA_HEADS * HEAD_DIM
DSA_W = DSA_HEADS * HEAD_DIM
IDX_W = IDX_HEADS * IDX_DIM
LANES = 128
MAX_BLOCKS = LANES // MOBA_HEADS
KEY_TILE = 256
SUB_TILE = 128
ROW_CHUNK = 512
LOG2_E = 1.4426950408889634
QK_SCALE_LOG2 = HEAD_DIM ** -0.5 * LOG2_E
MASK_BIAS = -30000.0
NEG_BIG = -1e30
BISECT_FIXED = 18
VMEM_LIMIT = 56 * 1024 * 1024

F32 = jnp.float32
BF16 = jnp.bfloat16
_NT = (((1,), (1,)), ((), ()))


def _params(sem):
    return pltpu.CompilerParams(dimension_semantics=sem, vmem_limit_bytes=VMEM_LIMIT)


def _tile(n, pref):
    if n <= pref:
        return n
    t = pref - pref % LANES
    while t >= LANES:
        if n % t == 0:
            return t
        t -= LANES
    return n


def _side_cast_specs(arrays, n_steps, step_of, row_blocks):
    ins, outs, shapes = [], [], []
    for w, axis in arrays:
        n = w.shape[axis]
        unit = row_blocks if axis == 0 else LANES
        target = max(unit, n // n_steps)
        slab = next(c for c in range(target - target % unit, 0, -unit) if n % c == 0)
        last = n // slab - 1
        block = (slab, w.shape[1]) if axis == 0 else (w.shape[0], slab)

        def imap(*g, axis=axis, last=last):
            idx = jnp.minimum(step_of(*g), last)
            return (idx, 0) if axis == 0 else (0, idx)

        ins.append(pl.BlockSpec(block, imap))
        outs.append(pl.BlockSpec(block, imap))
        shapes.append(jax.ShapeDtypeStruct(w.shape, BF16))
    return ins, outs, shapes


def _rms(x):
    return x * lax.rsqrt(jnp.mean(x * x, axis=-1, keepdims=True) + RMS_EPS)


def _ada_kernel(c_ref, w_ref, b_ref, o_ref):
    c = c_ref[...]
    cs = (c * jax.nn.sigmoid(c)).astype(BF16)
    o_ref[...] = jnp.dot(cs, w_ref[...].astype(BF16), preferred_element_type=F32) + b_ref[...]


def _ada(c_pad, w, b):
    rows, d = c_pad.shape
    n = w.shape[1]
    tn = _tile(n, 1024)
    return pl.pallas_call(
        _ada_kernel,
        grid=(n // tn,),
        in_specs=[pl.BlockSpec((rows, d), lambda j: (0, 0)),
                  pl.BlockSpec((d, tn), lambda j: (0, j)),
                  pl.BlockSpec((1, tn), lambda j: (0, j))],
        out_specs=pl.BlockSpec((rows, tn), lambda j: (0, j)),
        out_shape=jax.ShapeDtypeStruct((rows, n), F32),
        compiler_params=_params(("parallel",)),
        name="ada_mod",
    )(c_pad, w, b)


def _norm_mod_kernel(x_ref, g_ref, sc_ref, sh_ref, o_ref):
    y = _rms(x_ref[...])
    o_ref[...] = ((y * g_ref[...]) * (1.0 + sc_ref[...]) + sh_ref[...]).astype(o_ref.dtype)


def _norm_mod(x, g, sc, sh):
    b, t, d = x.shape
    tt = _tile(t, 1024)
    vec = pl.BlockSpec((None, 1, d), lambda bi, ti: (bi, 0, 0))
    return pl.pallas_call(
        _norm_mod_kernel,
        grid=(b, t // tt),
        in_specs=[pl.BlockSpec((None, tt, d), lambda bi, ti: (bi, ti, 0)),
                  pl.BlockSpec((1, d), lambda bi, ti: (0, 0)), vec, vec],
        out_specs=pl.BlockSpec((None, tt, d), lambda bi, ti: (bi, ti, 0)),
        out_shape=jax.ShapeDtypeStruct((b, t, d), BF16),
        compiler_params=_params(("parallel", "parallel")),
        name="norm_mod",
    )(x, g, sc, sh)


def _first_token_tile():
    return pl.program_id(1) == 0


def _rope64(y, cos, sin_a, sin_b):
    return y * cos + pltpu.roll(y, LANES - 32, 1) * sin_a + pltpu.roll(y, 32, 1) * sin_b


def _row_chunks(ref, step=ROW_CHUNK):
    n = ref.shape[0]
    step = min(n, step)
    return [slice(r, r + step) for r in range(0, n, step)]


def _wp_rope128_kernel(h_ref, w_ref, cos_ref, sin_ref, o_ref, wb_ref):
    @pl.when(_first_token_tile())
    def _():
        wb_ref[...] = w_ref[...].astype(BF16)

    for rows in _row_chunks(h_ref):
        acc = lax.dot_general(h_ref[rows, :], wb_ref[...], _NT, preferred_element_type=F32)
        cos = cos_ref[rows, :]
        sin = sin_ref[rows, :]
        for g in range(acc.shape[1] // LANES):
            y = acc[:, g * LANES:(g + 1) * LANES]
            o_ref[rows, g * LANES:(g + 1) * LANES] = (
                y * cos + pltpu.roll(y, LANES // 2, 1) * sin).astype(o_ref.dtype)


def _swap_row_halves(y, width):
    half = width // 2
    parts = []
    for r in range(0, y.shape[0], width):
        parts += [y[r + half:r + width], y[r:r + half]]
    return jnp.concatenate(parts, axis=0)


def _wp_rope_t_kernel(h_ref, w_ref, cos_ref, sin_ref, o_ref, wb_ref, *, head_dim):
    @pl.when(_first_token_tile())
    def _():
        wb_ref[...] = w_ref[...].astype(BF16)

    for c in range(o_ref.shape[0]):
        tok = slice(c * KEY_TILE, (c + 1) * KEY_TILE)
        r = lax.dot_general(wb_ref[...], h_ref[tok, :], _NT, preferred_element_type=F32)
        for g in range(0, r.shape[0], LANES):
            y = r[g:g + LANES]
            o_ref[c, g:g + LANES, :] = (y * cos_ref[:, tok]
                                        + _swap_row_halves(y, head_dim) * sin_ref[:, tok]).astype(o_ref.dtype)


def _wp_vt_kernel(h_ref, w_ref, o_ref, wb_ref):
    @pl.when(_first_token_tile())
    def _():
        wb_ref[...] = w_ref[...].astype(BF16)

    for c in range(o_ref.shape[0]):
        r = lax.dot_general(wb_ref[...], h_ref[c * KEY_TILE:(c + 1) * KEY_TILE, :], _NT,
                            preferred_element_type=F32)
        o_ref[c] = r.astype(o_ref.dtype)


def _wp_gate_kernel(h_ref, wa_ref, wb_ref, c0_ref, c1_ref, c2_ref, o_ref, d0_ref, d1_ref, d2_ref, wg_ref,
                    *, shift):
    d0_ref[...] = c0_ref[...].astype(BF16)
    d1_ref[...] = c1_ref[...].astype(BF16)
    d2_ref[...] = c2_ref[...].astype(BF16)

    @pl.when(_first_token_tile())
    def _():
        keep = wg_ref.shape[0] - shift
        wg_ref[:keep, :] = wa_ref[shift:, :].astype(BF16)
        wg_ref[keep:, :] = wb_ref[:shift, :].astype(BF16)

    for rows in _row_chunks(h_ref):
        acc = lax.dot_general(h_ref[rows, :], wg_ref[...], _NT, preferred_element_type=F32)
        o_ref[rows, :] = jax.nn.sigmoid(acc).astype(o_ref.dtype)


def _wp_kw_kernel(h_ref, w_ref, cos_ref, sa_ref, sb_ref, klo_ref, khi_ref, wit_ref, wb_ref):
    @pl.when(pl.program_id(0) == 0)
    def _():
        r = lax.broadcasted_iota(jnp.int32, w_ref.shape, 0)
        wb_ref[...] = jnp.where(r < IDX_DIM + IDX_HEADS, w_ref[...], 0.0).astype(BF16)

    h = h_ref[...]
    y = lax.dot_general(h, wb_ref[...], _NT, preferred_element_type=F32)
    r = _rope64(y, cos_ref[...], sa_ref[...], sb_ref[...])
    lane = lax.broadcasted_iota(jnp.int32, y.shape, 1)
    klo = jnp.where(lane < IDX_DIM, r, 0.0)
    klo_ref[...] = klo.astype(BF16)
    khi_ref[...] = pltpu.roll(klo, IDX_DIM, 1).astype(BF16)
    for c in range(wit_ref.shape[0]):
        wit_ref[c] = y[c * KEY_TILE:(c + 1) * KEY_TILE, :].T


def _in_proj(h, wt, l, t, tables128, tables64, tables_kw, out_weights):
    m, k = h.shape
    d = k
    tm = _tile(t, 1024)
    nt = t // tm
    ni = m // tm
    assert tm % KEY_TILE == 0
    tn = 1024
    o_dsa = 3 * MOBA_W
    o_qi = o_dsa + 3 * DSA_W
    o_kw = o_qi + IDX_W
    o_g = o_kw + IDX_DIM + IDX_HEADS
    sem = ("arbitrary", "arbitrary")
    h_spec = pl.BlockSpec((tm, k), lambda j, i: (i, 0))
    tab = pl.BlockSpec((tm, LANES), lambda j, i: (i % nt, 0))
    tabt = pl.BlockSpec((LANES, tm), lambda j, i: (0, i % nt))
    row = pl.BlockSpec((tm, tn), lambda j, i: (i, j))
    out_t = pl.BlockSpec((tm // KEY_TILE, tn, KEY_TILE), lambda j, i: (i, j, 0))

    def wrows(first_blocks, n_first, second_start):
        return pl.BlockSpec((None, tn, k), lambda j, i: (l, jnp.where(j < n_first, first_blocks + j,
                                                                      second_start + j - n_first), 0))

    assert MOBA_W == DSA_W and MOBA_W % tn == 0
    n_m, n_d = MOBA_W // tn, DSA_W // tn

    qt = pl.pallas_call(
        functools.partial(_wp_rope_t_kernel, head_dim=HEAD_DIM),
        grid=(n_m + n_d, ni),
        in_specs=[h_spec, wrows(0, n_m, o_dsa // tn), tabt, tabt],
        out_specs=out_t,
        out_shape=jax.ShapeDtypeStruct((m // KEY_TILE, MOBA_W + DSA_W, KEY_TILE), BF16),
        scratch_shapes=[pltpu.VMEM((tn, k), BF16)],
        compiler_params=_params(sem),
        name="proj_q",
    )(h, wt, *tables128[0])

    kk = pl.pallas_call(
        _wp_rope128_kernel,
        grid=(n_m + n_d, ni),
        in_specs=[h_spec, wrows(MOBA_W // tn, n_m, (o_dsa + DSA_W) // tn), tab, tab],
        out_specs=row,
        out_shape=jax.ShapeDtypeStruct((m, MOBA_W + DSA_W), BF16),
        scratch_shapes=[pltpu.VMEM((tn, k), BF16)],
        compiler_params=_params(sem),
        name="proj_k",
    )(h, wt, *tables128[1])

    vt = pl.pallas_call(
        _wp_vt_kernel,
        grid=(n_m + n_d, ni),
        in_specs=[h_spec, wrows(2 * MOBA_W // tn, n_m, (o_dsa + 2 * DSA_W) // tn)],
        out_specs=out_t,
        out_shape=jax.ShapeDtypeStruct((m // KEY_TILE, MOBA_W + DSA_W, KEY_TILE), BF16),
        scratch_shapes=[pltpu.VMEM((tn, k), BF16)],
        compiler_params=_params(sem),
        name="proj_v",
    )(h, wt)

    n_q = IDX_W // tn
    qit = pl.pallas_call(
        functools.partial(_wp_rope_t_kernel, head_dim=IDX_DIM),
        grid=(n_q, ni),
        in_specs=[h_spec, wrows(o_qi // tn, n_q, 0), tabt, tabt],
        out_specs=out_t,
        out_shape=jax.ShapeDtypeStruct((m // KEY_TILE, IDX_W, KEY_TILE), BF16),
        scratch_shapes=[pltpu.VMEM((tn, k), BF16)],
        compiler_params=_params(sem),
        name="proj_qi",
    )(h, wt, *tables64)

    tab1 = pl.BlockSpec((tm, LANES), lambda i: (i % nt, 0))
    row1 = pl.BlockSpec((tm, LANES), lambda i: (i, 0))
    klo, khi, wit = pl.pallas_call(
        _wp_kw_kernel,
        grid=(ni,),
        in_specs=[pl.BlockSpec((tm, k), lambda i: (i, 0)),
                  pl.BlockSpec((None, LANES, k), lambda i: (l, o_kw // LANES, 0)), tab1, tab1, tab1],
        out_specs=[row1, row1, pl.BlockSpec((tm // KEY_TILE, LANES, KEY_TILE), lambda i: (i, 0, 0))],
        out_shape=[jax.ShapeDtypeStruct((m, LANES), BF16), jax.ShapeDtypeStruct((m, LANES), BF16),
                   jax.ShapeDtypeStruct((m // KEY_TILE, LANES, KEY_TILE), F32)],
        scratch_shapes=[pltpu.VMEM((LANES, k), BF16)],
        compiler_params=_params(("arbitrary",)),
        name="proj_kw",
    )(h, wt, *tables_kw)

    tg = _tile(2 * d, tn)
    g0 = o_kw // tg
    shift = o_g - o_kw
    assert o_kw % tg == 0 and shift <= LANES and tg % LANES == 0
    ng = 2 * d // tg
    c_in, c_out, c_shape = _side_cast_specs([(w, 0) for w in out_weights], ng * ni,
                                            lambda j, i: j * ni + i, 16)
    sg, *out_weights_bf16 = pl.pallas_call(
        functools.partial(_wp_gate_kernel, shift=shift),
        grid=(ng, ni),
        in_specs=[h_spec,
                  pl.BlockSpec((None, tg, k), lambda j, i: (l, g0 + j, 0)),
                  pl.BlockSpec((None, LANES, k), lambda j, i: (l, (g0 + j + 1) * (tg // LANES), 0))] + c_in,
        out_specs=[pl.BlockSpec((tm, tg), lambda j, i: (i, j))] + c_out,
        out_shape=[jax.ShapeDtypeStruct((m, 2 * d), BF16)] + c_shape,
        scratch_shapes=[pltpu.VMEM((tg, k), BF16)],
        compiler_params=_params(sem),
        name="proj_gate",
    )(h, wt, wt, *out_weights)
    return qt, kk, vt, qit, klo, khi, wit, sg, out_weights_bf16


def _moba_kmean_kernel(k_ref, o_ref, km_ref, *, nb):
    km_ref[...] = jnp.zeros(km_ref.shape, F32)
    for j in range(nb):
        blk = k_ref[j * MOBA_BLOCK:(j + 1) * MOBA_BLOCK, :].astype(F32)
        km_ref[j:j + 1, :] = jnp.sum(blk, axis=0, keepdims=True) * (1.0 / MOBA_BLOCK)
    km = km_ref[...]
    tiled = jnp.concatenate([km] * MOBA_HEADS, axis=0)
    r = lax.shift_right_logical(lax.broadcasted_iota(jnp.int32, tiled.shape, 0), 4)
    c = lax.shift_right_logical(lax.broadcasted_iota(jnp.int32, tiled.shape, 1), 7)
    o_ref[...] = jnp.where(r == c, tiled, 0.0).astype(o_ref.dtype)


def _moba_kmean(kk3, nb):
    b, t, _ = kk3.shape
    return pl.pallas_call(
        functools.partial(_moba_kmean_kernel, nb=nb),
        grid=(b,),
        in_specs=[pl.BlockSpec((None, t, MOBA_W), lambda bi: (bi, 0, 0))],
        out_specs=pl.BlockSpec((None, LANES, MOBA_W), lambda bi: (bi, 0, 0)),
        out_shape=jax.ShapeDtypeStruct((b, LANES, MOBA_W), BF16),
        scratch_shapes=[pltpu.VMEM((MAX_BLOCKS, MOBA_W), F32)],
        compiler_params=_params(("parallel",)),
        name="moba_kmean",
    )(kk3)


def _moba_block_bias(qt_ref, kmt_ref, sbt_ref, i, n_sel):
    g = jnp.dot(kmt_ref[...], qt_ref[...], preferred_element_type=F32)
    shape = (MAX_BLOCKS, g.shape[1])
    j = lax.broadcasted_iota(jnp.int32, shape, 0)
    past = j < i
    for h in range(MOBA_HEADS):
        gm = jnp.where(past, g[h * MAX_BLOCKS:(h + 1) * MAX_BLOCKS, :], -jnp.inf)
        rank = jnp.zeros(shape, F32)
        for jo in range(MAX_BLOCKS):
            other = gm[jo:jo + 1, :]
            beats = (other > gm) | ((other == gm) & (jo < j))
            rank = rank + jnp.where(beats, 1.0, 0.0)
        bias = jnp.where(past & (rank < n_sel), 0.0, MASK_BIAS)
        for jb in range(MAX_BLOCKS):
            sbt_ref[h, jb] = jnp.broadcast_to(bias[jb:jb + 1, :], (8, shape[1]))


def _flash_step(st, m, l, acc_t, v_t):
    m_new = jnp.maximum(m, jnp.max(st, axis=0, keepdims=True))
    alpha = jnp.exp2(m - m_new)
    p = jnp.exp2(st - m_new)
    l = alpha * l + jnp.sum(p, axis=0, keepdims=True)
    acc_t = alpha * acc_t + jnp.dot(v_t, p.astype(BF16), preferred_element_type=F32)
    return m_new, l, acc_t


def _flash_step_joint(sts, m, l, acc_t, v_ts):
    m_new = m
    for st in sts:
        m_new = jnp.maximum(m_new, jnp.max(st, axis=0, keepdims=True))
    alpha = jnp.exp2(m - m_new)
    l = alpha * l
    acc_t = alpha * acc_t
    for st, v_t in zip(sts, v_ts):
        p = jnp.exp2(st - m_new)
        l = l + jnp.sum(p, axis=0, keepdims=True)
        acc_t = acc_t + jnp.dot(v_t, p.astype(BF16), preferred_element_type=F32)
    return m_new, l, acc_t


def _flash_finish(o_ref, l_ref, acc_ref, heads):
    for h in range(heads):
        hs = slice(h * HEAD_DIM, (h + 1) * HEAD_DIM)
        o_ref[:, hs] = (acc_ref[h] / l_ref[h:h + 1, :]).T.astype(o_ref.dtype)


def _moba_attn_kernel(qt_ref, k_ref, vt_ref, kmt_ref, o_ref, m_ref, l_ref, acc_ref, sbt_ref, *, n_sel):
    i = pl.program_id(1)
    own = pl.multiple_of(i * MOBA_BLOCK, MOBA_BLOCK)
    sub_shape = (SUB_TILE, MOBA_BLOCK)
    kidx = lax.broadcasted_iota(jnp.int32, sub_shape, 0)
    qidx = lax.broadcasted_iota(jnp.int32, sub_shape, 1)
    _moba_block_bias(qt_ref, kmt_ref, sbt_ref, i, n_sel)
    for h in range(MOBA_HEADS):
        hs = slice(h * HEAD_DIM, (h + 1) * HEAD_DIM)
        m = jnp.full((1, MOBA_BLOCK), NEG_BIG, F32)
        l = jnp.zeros((1, MOBA_BLOCK), F32)
        acc = jnp.zeros((HEAD_DIM, MOBA_BLOCK), F32)
        for u in range(MOBA_BLOCK // SUB_TILE):
            st = jnp.dot(k_ref[pl.ds(own + u * SUB_TILE, SUB_TILE), hs], qt_ref[hs, :],
                         preferred_element_type=F32)
            st = jnp.where(kidx + u * SUB_TILE <= qidx, st, -jnp.inf)
            m, l, acc = _flash_step(st, m, l, acc, vt_ref[i, hs, u * SUB_TILE:(u + 1) * SUB_TILE])
        m_ref[h:h + 1, :] = m
        l_ref[h:h + 1, :] = l
        acc_ref[h] = acc

    def blocks(js):
        for h in range(MOBA_HEADS):
            hs = slice(h * HEAD_DIM, (h + 1) * HEAD_DIM)
            m, l, acc = m_ref[h:h + 1, :], l_ref[h:h + 1, :], acc_ref[h]
            for j in js:
                off = pl.multiple_of(j * MOBA_BLOCK, MOBA_BLOCK)
                bias = sbt_ref[h, j][0:1, :]
                sts, vts = [], []
                for u in range(MOBA_BLOCK // SUB_TILE):
                    sts.append(jnp.dot(k_ref[pl.ds(off + u * SUB_TILE, SUB_TILE), hs], qt_ref[hs, :],
                                       preferred_element_type=F32) + bias)
                    vts.append(vt_ref[j, hs, u * SUB_TILE:(u + 1) * SUB_TILE])
                m, l, acc = _flash_step_joint(sts, m, l, acc, vts)
            m_ref[h:h + 1, :] = m
            l_ref[h:h + 1, :] = l
            acc_ref[h] = acc

    def body(t, carry):
        blocks((2 * t, 2 * t + 1))
        return carry

    lax.fori_loop(0, i // 2, body, 0)

    @pl.when(i % 2 == 1)
    def _():
        blocks((i - 1,))

    _flash_finish(o_ref, l_ref, acc_ref, MOBA_HEADS)


def _moba_attn(qt3, kk3, vt3, kmt, nb, n_sel):
    b, t, _ = kk3.shape
    one = pl.Buffered(1)
    return pl.pallas_call(
        functools.partial(_moba_attn_kernel, n_sel=n_sel),
        grid=(b, nb),
        in_specs=[pl.BlockSpec((None, MOBA_W, MOBA_BLOCK), lambda bi, i: (bi * nb + i, 0, 0)),
                  pl.BlockSpec((None, t, MOBA_W), lambda bi, i: (bi, 0, 0), pipeline_mode=one),
                  pl.BlockSpec((nb, MOBA_W, KEY_TILE), lambda bi, i: (bi, 0, 0), pipeline_mode=one),
                  pl.BlockSpec((None, LANES, MOBA_W), lambda bi, i: (bi, 0, 0), pipeline_mode=one)],
        out_specs=pl.BlockSpec((None, MOBA_BLOCK, MOBA_W), lambda bi, i: (bi, i, 0)),
        out_shape=jax.ShapeDtypeStruct((b, t, MOBA_W), BF16),
        scratch_shapes=[pltpu.VMEM((MOBA_HEADS, MOBA_BLOCK), F32),
                        pltpu.VMEM((MOBA_HEADS, MOBA_BLOCK), F32),
                        pltpu.VMEM((MOBA_HEADS, HEAD_DIM, MOBA_BLOCK), F32),
                        pltpu.VMEM((MOBA_HEADS, MAX_BLOCKS, 8, MOBA_BLOCK), F32)],
        compiler_params=_params(("parallel", "arbitrary")),
        name="moba_attn",
    )(qt3, kk3, vt3, kmt)


def _float_to_key(x):
    b = lax.bitcast_convert_type(x, jnp.int32)
    return jnp.where(b >= 0, b, b ^ 0x7FFFFFFF)


def _key_to_float(t):
    return lax.bitcast_convert_type(jnp.where(t >= 0, t, t ^ 0x7FFFFFFF), F32)


def _dsa_kernel(qit_ref, klo_ref, khi_ref, wt_ref, qt_ref, k_ref, vt_ref, o_ref,
                sc_ref, jl_ref, m_ref, l_ref, acc_ref, *, topk, idx_scale, tq, seq_bits):
    i = pl.program_id(1)
    nk = i + 1
    shape = (KEY_TILE, tq)
    krow = lax.broadcasted_iota(jnp.int32, shape, 0)
    qpos = i * tq + lax.broadcasted_iota(jnp.int32, shape, 1)

    wt = wt_ref[...]

    def score_tiles(kts):
        offs = [pl.multiple_of(kt * KEY_TILE, KEY_TILE) for kt in kts]
        keys = jnp.concatenate(
            [r[pl.ds(off, KEY_TILE), :] for off in offs for r in (klo_ref, khi_ref)], axis=0)
        accs = [jnp.zeros(shape, F32) for _ in kts]
        for p in range(IDX_HEADS // 2):
            s = jnp.dot(keys, qit_ref[p * LANES:(p + 1) * LANES, :], preferred_element_type=F32)
            for n in range(len(kts)):
                lo = s[2 * n * KEY_TILE:(2 * n + 1) * KEY_TILE]
                hi = s[(2 * n + 1) * KEY_TILE:(2 * n + 2) * KEY_TILE]
                accs[n] = accs[n] + (jnp.maximum(lo, 0.0) * wt[2 * p:2 * p + 1, :]
                                     + jnp.maximum(hi, 0.0) * wt[2 * p + 1:2 * p + 2, :])
        for kt, acc in zip(kts, accs):
            kpos = kt * KEY_TILE + krow
            sc_ref[kt] = jnp.where(kpos <= qpos, acc * idx_scale, -jnp.inf)

    def score_body(t, carry):
        score_tiles((2 * t, 2 * t + 1))
        return carry

    lax.fori_loop(0, nk // 2, score_body, 0)

    @pl.when(nk % 2 == 1)
    def _():
        score_tiles((nk - 1,))

    def count(pred):
        def body(kt, acc):
            hit = jnp.where(pred(sc_ref[kt], kt * KEY_TILE + krow), 1.0, 0.0)
            return acc + jnp.sum(hit.reshape(KEY_TILE // 32, 32, tq), axis=0)
        part = lax.fori_loop(0, nk, body, jnp.zeros((32, tq), F32))
        return jnp.sum(part, axis=0, keepdims=True)

    need_select = nk * tq > topk

    @pl.when(jnp.logical_not(need_select))
    def _():
        def body(kt, carry):
            sc_ref[kt] = jnp.where(kt * KEY_TILE + krow <= qpos, 0.0, NEG_BIG)
            return carry
        lax.fori_loop(0, nk, body, 0)

    @pl.when(need_select)
    def _():
        kf = jnp.float32(topk)

        def ext_body(kt, carry):
            s = sc_ref[kt]
            lo = jnp.min(jnp.where(s > -jnp.inf, s, jnp.inf).reshape(KEY_TILE // 32, 32, tq), axis=0)
            hi = jnp.max(s.reshape(KEY_TILE // 32, 32, tq), axis=0)
            return jnp.minimum(carry[0], lo), jnp.maximum(carry[1], hi)

        lo, hi = lax.fori_loop(0, nk, ext_body, (jnp.full((32, tq), jnp.inf, F32),
                                                 jnp.full((32, tq), -jnp.inf, F32)))
        lo = jnp.min(lo, axis=0, keepdims=True)
        hi = _key_to_float(_float_to_key(jnp.max(hi, axis=0, keepdims=True)) + 1)
        n_fin = count(lambda s, kp: s >= lo)
        done = jnp.where(n_fin <= kf, 1.0, 0.0)

        def step(state):
            lo, hi, clo, done = state
            mid = lo * 0.5 + hi * 0.5
            stuck = (mid <= lo) | (mid >= hi)
            c = count(lambda s, kp: s >= mid)
            up = (c >= kf) & (done < 0.5)
            down = (c < kf) & (done < 0.5)
            lo = jnp.where(up, mid, lo)
            clo = jnp.where(up, c, clo)
            hi = jnp.where(down, mid, hi)
            done = jnp.where(stuck | (clo == kf), 1.0, done)
            return lo, hi, clo, done

        state = lax.fori_loop(0, BISECT_FIXED, lambda _, st: step(st), (lo, hi, n_fin, done))
        state = lax.while_loop(lambda st: jnp.min(st[3]) < 0.5, lambda st: step(step(st)), state)
        thr = jnp.where(n_fin < kf, -jnp.inf, state[0])

        def tie_body(kt, acc):
            s = sc_ref[kt]
            gt = jnp.where(s > thr, 1.0, 0.0).reshape(KEY_TILE // 32, 32, tq)
            eq = jnp.where(s == thr, 1.0, 0.0).reshape(KEY_TILE // 32, 32, tq)
            return acc[0] + jnp.sum(gt, axis=0), acc[1] + jnp.sum(eq, axis=0)

        n_gt, n_eq = lax.fori_loop(0, nk, tie_body, (jnp.zeros((32, tq), F32),) * 2)
        need = kf - jnp.sum(n_gt, axis=0, keepdims=True)
        n_eq = jnp.sum(n_eq, axis=0, keepdims=True)
        jl_ref[...] = jnp.full(jl_ref.shape, 2 ** seq_bits, jnp.int32)

        @pl.when(jnp.max(n_eq - need) > 0.0)
        def _():
            def idx_body(b, c):
                cand = c | lax.shift_left(jnp.int32(1), seq_bits - 1 - b)
                g = count(lambda s, kp: (s == thr) & (kp < cand))
                return jnp.where(g < need, cand, c)
            c = lax.fori_loop(0, seq_bits, idx_body, jnp.zeros((1, tq), jnp.int32))
            jl_ref[...] = jnp.broadcast_to(c, jl_ref.shape)

        bound = jnp.minimum(jl_ref[0:1, :], qpos[0:1, :])

        def bias_body(kt, carry):
            s = sc_ref[kt]
            sel = (s > thr) | ((s == thr) & (kt * KEY_TILE + krow <= bound))
            sc_ref[kt] = jnp.where(sel, 0.0, NEG_BIG)
            return carry
        lax.fori_loop(0, nk, bias_body, 0)

    m_ref[...] = jnp.full(m_ref.shape, NEG_BIG, F32)
    l_ref[...] = jnp.zeros(l_ref.shape, F32)
    acc_ref[...] = jnp.zeros(acc_ref.shape, F32)

    def tiles(kts):
        for h in range(DSA_HEADS):
            hs = slice(h * HEAD_DIM, (h + 1) * HEAD_DIM)
            m, l, acc = m_ref[h:h + 1, :], l_ref[h:h + 1, :], acc_ref[h]
            for kt in kts:
                off = pl.multiple_of(kt * KEY_TILE, KEY_TILE)
                for u in range(KEY_TILE // SUB_TILE):
                    us = slice(u * SUB_TILE, (u + 1) * SUB_TILE)
                    st = jnp.dot(k_ref[pl.ds(off + u * SUB_TILE, SUB_TILE), hs], qt_ref[hs, :],
                                 preferred_element_type=F32) + sc_ref[kt, us, :]
                    m, l, acc = _flash_step(st, m, l, acc, vt_ref[kt, hs, us])
            m_ref[h:h + 1, :] = m
            l_ref[h:h + 1, :] = l
            acc_ref[h] = acc

    def attn_body(t, carry):
        tiles((2 * t, 2 * t + 1))
        return carry

    lax.fori_loop(0, nk // 2, attn_body, 0)

    @pl.when(nk % 2 == 1)
    def _():
        tiles((nk - 1,))

    _flash_finish(o_ref, l_ref, acc_ref, DSA_HEADS)


def _dsa_attn(qit3, klo3, khi3, wit, qt3, kk3, vt3, topk):
    b, t, _ = kk3.shape
    tq = KEY_TILE
    nq = t // tq
    seq_bits = max(1, (t - 1).bit_length())
    one = pl.Buffered(1)
    kern = functools.partial(
        _dsa_kernel, topk=topk,
        idx_scale=(IDX_DIM ** -0.5) * (IDX_HEADS ** -0.5), tq=tq, seq_bits=seq_bits)
    return pl.pallas_call(
        kern,
        grid=(b, nq),
        in_specs=[pl.BlockSpec((None, IDX_W, tq), lambda bi, i: (bi * nq + i, 0, 0)),
                  pl.BlockSpec((None, t, LANES), lambda bi, i: (bi, 0, 0), pipeline_mode=one),
                  pl.BlockSpec((None, t, LANES), lambda bi, i: (bi, 0, 0), pipeline_mode=one),
                  pl.BlockSpec((None, IDX_HEADS, tq), lambda bi, i: (bi * nq + i, IDX_DIM // IDX_HEADS, 0)),
                  pl.BlockSpec((None, DSA_W, tq), lambda bi, i: (bi * nq + i, 1, 0)),
                  pl.BlockSpec((None, t, DSA_W), lambda bi, i: (bi, 0, 1), pipeline_mode=one),
                  pl.BlockSpec((nq, DSA_W, KEY_TILE), lambda bi, i: (bi, 1, 0), pipeline_mode=one)],
        out_specs=pl.BlockSpec((None, tq, DSA_W), lambda bi, i: (bi, i, 0)),
        out_shape=jax.ShapeDtypeStruct((b, t, DSA_W), BF16),
        scratch_shapes=[pltpu.VMEM((nq, KEY_TILE, tq), F32),
                        pltpu.VMEM((8, tq), jnp.int32),
                        pltpu.VMEM((DSA_HEADS, tq), F32),
                        pltpu.VMEM((DSA_HEADS, tq), F32),
                        pltpu.VMEM((DSA_HEADS, HEAD_DIM, tq), F32)],
        compiler_params=_params(("parallel", "arbitrary")),
        name="dsa_attn",
    )(qit3, klo3, khi3, wit, qt3, kk3, vt3)


def _mix_out_kernel(oa_ref, ob_ref, sg_ref, wa_ref, wb_ref, wo_ref, x_ref, gt_ref, gpost_ref,
                    gpre_ref, sc_ref, sh_ref, w1f_ref, w2f_ref, x1_ref, h2_ref, w1b_ref, w2b_ref, *, d):
    w1b_ref[...] = w1f_ref[...].astype(BF16)
    w2b_ref[...] = w2f_ref[...].astype(BF16)
    ya = jnp.dot(oa_ref[...], wa_ref[...], preferred_element_type=F32)
    yb = jnp.dot(ob_ref[...], wb_ref[...], preferred_element_type=F32)
    z = sg_ref[:, :d].astype(F32) * ya + sg_ref[:, d:].astype(F32) * yb
    y = jnp.dot(z.astype(BF16), wo_ref[...], preferred_element_type=F32)
    x1 = x_ref[...] + gt_ref[...] * (_rms(y) * gpost_ref[...])
    x1_ref[...] = x1
    h2_ref[...] = ((_rms(x1) * gpre_ref[...]) * (1.0 + sc_ref[...]) + sh_ref[...]).astype(h2_ref.dtype)


def _mix_out(oa3, ob3, sg3, wa, wb, wo, x, gt1, g_post, g_pre2, sc2, sh2, w1f, w2f):
    b, t, d = x.shape
    tm = _tile(t, 256)
    nt = t // tm
    c_in, c_out, c_shape = _side_cast_specs([(w1f, 1), (w2f, 0)], b * nt, lambda bi, i: bi * nt + i, 16)
    one = pl.Buffered(1)
    row = lambda w: pl.BlockSpec((None, tm, w), lambda bi, i: (bi, i, 0))
    full = lambda a: pl.BlockSpec(a.shape, lambda bi, i: (0, 0), pipeline_mode=one)
    vec = pl.BlockSpec((None, 1, d), lambda bi, i: (bi, 0, 0))
    gain = pl.BlockSpec((1, d), lambda bi, i: (0, 0))
    return pl.pallas_call(
        functools.partial(_mix_out_kernel, d=d),
        grid=(b, t // tm),
        in_specs=[row(MOBA_W), row(DSA_W), row(2 * d), full(wa), full(wb), full(wo), row(d),
                  vec, gain, gain, vec, vec] + c_in,
        out_specs=[row(d), row(d)] + c_out,
        out_shape=[jax.ShapeDtypeStruct((b, t, d), F32), jax.ShapeDtypeStruct((b, t, d), BF16)] + c_shape,
        compiler_params=_params(("arbitrary", "arbitrary")),
        name="mix_out",
    )(oa3, ob3, sg3, wa, wb, wo, x, gt1, g_post, g_pre2, sc2, sh2, w1f, w2f)


def _ffn_kernel(h_ref, w1_ref, w2_ref, x1_ref, gt_ref, g_ref, o_ref, acc_ref):
    j = pl.program_id(2)

    @pl.when(j == 0)
    def _():
        acc_ref[...] = jnp.zeros(acc_ref.shape, F32)

    for rows in _row_chunks(h_ref, KEY_TILE):
        u = jnp.dot(h_ref[rows, :], w1_ref[...], preferred_element_type=F32)
        u = jnp.square(jnp.maximum(u, 0.0)).astype(BF16)
        acc_ref[rows, :] += jnp.dot(u, w2_ref[...], preferred_element_type=F32)

    @pl.when(j == pl.num_programs(2) - 1)
    def _():
        o_ref[...] = x1_ref[...] + gt_ref[...] * (_rms(acc_ref[...]) * g_ref[...])


def _ffn(h2, w1, w2, x1, gt2, g_post):
    b, t, d = x1.shape
    ff = w1.shape[1]
    tm = _tile(t, 512)
    tf = _tile(ff, 1024)
    row = lambda: pl.BlockSpec((None, tm, d), lambda bi, i, j: (bi, i, 0))
    return pl.pallas_call(
        _ffn_kernel,
        grid=(b, t // tm, ff // tf),
        in_specs=[row(),
                  pl.BlockSpec((d, tf), lambda bi, i, j: (0, j)),
                  pl.BlockSpec((tf, d), lambda bi, i, j: (j, 0)),
                  row(),
                  pl.BlockSpec((None, 1, d), lambda bi, i, j: (bi, 0, 0)),
                  pl.BlockSpec((1, d), lambda bi, i, j: (0, 0))],
        out_specs=row(),
        out_shape=jax.ShapeDtypeStruct((b, t, d), F32),
        scratch_shapes=[pltpu.VMEM((tm, d), F32)],
        compiler_params=_params(("parallel", "parallel", "arbitrary")),
        name="ffn",
    )(h2, w1, w2, x1, gt2, g_post)


def _rope_tables(t):
    half = HEAD_DIM // 2
    inv_freq = jnp.power(ROPE_THETA, -jnp.arange(half, dtype=F32) / half)
    ang = jnp.arange(t, dtype=F32)[:, None] * inv_freq[None, :]
    cos, sin = jnp.cos(ang), jnp.sin(ang)
    cos_k = jnp.concatenate([cos, cos], axis=1)
    sin_k = jnp.concatenate([-sin, sin], axis=1)
    rope128 = (((cos_k * QK_SCALE_LOG2).T, (sin_k * QK_SCALE_LOG2).T), (cos_k, sin_k))
    step = half // (IDX_DIM // 2)
    cos_i, sin_i = cos[:, ::step], sin[:, ::step]
    cos64 = jnp.tile(cos_i, (1, LANES // cos_i.shape[1]))
    sin64 = jnp.tile(jnp.concatenate([-sin_i, sin_i], axis=1), (1, LANES // IDX_DIM))
    lane = jnp.arange(LANES)[None, :]
    is_key = lane < IDX_DIM
    low = lane % IDX_DIM < IDX_DIM // 2
    kw = (jnp.where(is_key, cos64, 1.0), jnp.where(is_key & low, sin64, 0.0),
          jnp.where(is_key & ~low, sin64, 0.0))
    return rope128, (cos64.T, sin64.T), kw


def kernel(x, c, w_ada, b_ada, g_pre_mix, g_post_mix, w_in, w_moba_out, w_dsa_out, w_o,
           g_pre_ffn, g_post_ffn, w_ff1, w_ff2):
    b, t, d = x.shape
    m = b * t
    nb = t // MOBA_BLOCK
    assert t % MOBA_BLOCK == 0 and nb <= MAX_BLOCKS and d % LANES == 0 and b <= 16
    n_sel = max(1, min(MOBA_TOPK, nb - 1))
    topk = min(DSA_TOPK_MAX, t // 4)
    rope128, rope64, rope_kw = _rope_tables(t)
    c_pad = jnp.zeros((16, d), F32).at[:b].set(c)

    for l in range(w_ada.shape[0]):
        mod = _ada(c_pad, w_ada[l], b_ada[l][None, :])[:b]
        sh1, sc1, gt1, sh2, sc2, gt2 = [v[:, None, :] for v in jnp.split(mod, 6, axis=-1)]

        h = _norm_mod(x, g_pre_mix[l][None, :], sc1, sh1).reshape(m, d)
        qt3, kk, vt3, qit3, klo, khi, wit, sg, (wa, wb, wo) = _in_proj(
            h, jnp.swapaxes(w_in, 1, 2), l, t, rope128, rope64, rope_kw,
            (w_moba_out[l], w_dsa_out[l], w_o[l]))

        kk3 = kk.reshape(b, t, -1)
        kmt = _moba_kmean(kk3, nb)
        oa = _moba_attn(qt3, kk3, vt3, kmt, nb, n_sel)
        ob = _dsa_attn(qit3, klo.reshape(b, t, -1), khi.reshape(b, t, -1), wit, qt3, kk3, vt3, topk)

        x, h2, w1, w2 = _mix_out(oa, ob, sg.reshape(b, t, -1), wa, wb, wo, x, gt1,
                                 g_post_mix[l][None, :], g_pre_ffn[l][None, :], sc2, sh2,
                                 w_ff1[l], w_ff2[l])
        x = _ffn(h2, w1, w2, x, gt2, g_post_ffn[l][None, :])
    return x
```

```python
import functools

import jax
import jax.numpy as jnp
from jax import lax
from jax.experimental import pallas as pl
from jax.experimental.pallas import tpu as pltpu

HEAD_DIM = 128
MOBA_HEADS = 8
MOBA_BLOCK = 256
MOBA_TOPK = 3
DSA_HEADS = 8
IDX_HEADS = 16
IDX_DIM = 64
DSA_TOPK_MAX = 256
ROPE_THETA = 10000.0
RMS_EPS = 1e-6

MOBA_W = MOBA_HEADS * HEAD_DIM
DSA_W = DSA_HEADS * HEAD_DIM
IDX_W = IDX_HEADS * IDX_DIM
LANES = 128
MAX_BLOCKS = LANES // MOBA_HEADS
KEY_TILE = 256
SUB_TILE = 128
ROW_CHUNK = 512
LOG2_E = 1.4426950408889634
QK_SCALE_LOG2 = HEAD_DIM ** -0.5 * LOG2_E
MASK_BIAS = -30000.0
NEG_BIG = -1e30
BISECT_FIXED = 18
VMEM_LIMIT = 56 * 1024 * 1024

F32 = jnp.float32
BF16 = jnp.bfloat16
_NT = (((1,), (1,)), ((), ()))


def _params(sem):
    return pltpu.CompilerParams(dimension_semantics=sem, vmem_limit_bytes=VMEM_LIMIT)


def _tile(n, pref):
    if n <= pref:
        return n
    t = pref - pref % LANES
    while t >= LANES:
        if n % t == 0:
            return t
        t -= LANES
    return n


def _side_cast_specs(arrays, n_steps, step_of, row_blocks):
    ins, outs, shapes = [], [], []
    for w, axis in arrays:
        n = w.shape[axis]
        unit = row_blocks if axis == 0 else LANES
        target = max(unit, n // n_steps)
        slab = next(c for c in range(target - target % unit, 0, -unit) if n % c == 0)
        last = n // slab - 1
        block = (slab, w.shape[1]) if axis == 0 else (w.shape[0], slab)

        def imap(*g, axis=axis, last=last):
            idx = jnp.minimum(step_of(*g), last)
            return (idx, 0) if axis == 0 else (0, idx)

        ins.append(pl.BlockSpec(block, imap))
        outs.append(pl.BlockSpec(block, imap))
        shapes.append(jax.ShapeDtypeStruct(w.shape, BF16))
    return ins, outs, shapes


def _rms(x):
    return x * lax.rsqrt(jnp.mean(x * x, axis=-1, keepdims=True) + RMS_EPS)


def _ada_kernel(c_ref, w_ref, b_ref, o_ref):
    c = c_ref[...]
    cs = (c * jax.nn.sigmoid(c)).astype(BF16)
    o_ref[...] = jnp.dot(cs, w_ref[...].astype(BF16), preferred_element_type=F32) + b_ref[...]


def _ada(c_pad, w, b):
    rows, d = c_pad.shape
    n = w.shape[1]
    tn = _tile(n, 1024)
    return pl.pallas_call(
        _ada_kernel,
        grid=(n // tn,),
        in_specs=[pl.BlockSpec((rows, d), lambda j: (0, 0)),
                  pl.BlockSpec((d, tn), lambda j: (0, j)),
                  pl.BlockSpec((1, tn), lambda j: (0, j))],
        out_specs=pl.BlockSpec((rows, tn), lambda j: (0, j)),
        out_shape=jax.ShapeDtypeStruct((rows, n), F32),
        compiler_params=_params(("parallel",)),
        name="ada_mod",
    )(c_pad, w, b)


def _norm_mod_kernel(x_ref, g_ref, sc_ref, sh_ref, o_ref):
    y = _rms(x_ref[...])
    o_ref[...] = ((y * g_ref[...]) * (1.0 + sc_ref[...]) + sh_ref[...]).astype(o_ref.dtype)


def _norm_mod(x, g, sc, sh):
    b, t, d = x.shape
    tt = _tile(t, 1024)
    vec = pl.BlockSpec((None, 1, d), lambda bi, ti: (bi, 0, 0))
    return pl.pallas_call(
        _norm_mod_kernel,
        grid=(b, t // tt),
        in_specs=[pl.BlockSpec((None, tt, d), lambda bi, ti: (bi, ti, 0)),
                  pl.BlockSpec((1, d), lambda bi, ti: (0, 0)), vec, vec],
        out_specs=pl.BlockSpec((None, tt, d), lambda bi, ti: (bi, ti, 0)),
        out_shape=jax.ShapeDtypeStruct((b, t, d), BF16),
        compiler_params=_params(("parallel", "parallel")),
        name="norm_mod",
    )(x, g, sc, sh)


def _first_token_tile():
    return pl.program_id(1) == 0


def _rope64(y, cos, sin_a, sin_b):
    return y * cos + pltpu.roll(y, LANES - 32, 1) * sin_a + pltpu.roll(y, 32, 1) * sin_b


def _row_chunks(ref, step=ROW_CHUNK):
    n = ref.shape[0]
    step = min(n, step)
    return [slice(r, r + step) for r in range(0, n, step)]


def _wp_rope128_kernel(h_ref, w_ref, cos_ref, sin_ref, o_ref, wb_ref):
    @pl.when(_first_token_tile())
    def _():
        wb_ref[...] = w_ref[...].astype(BF16)

    for rows in _row_chunks(h_ref):
        acc = lax.dot_general(h_ref[rows, :], wb_ref[...], _NT, preferred_element_type=F32)
        cos = cos_ref[rows, :]
        sin = sin_ref[rows, :]
        for g in range(acc.shape[1] // LANES):
            y = acc[:, g * LANES:(g + 1) * LANES]
            o_ref[rows, g * LANES:(g + 1) * LANES] = (
                y * cos + pltpu.roll(y, LANES // 2, 1) * sin).astype(o_ref.dtype)


def _swap_row_halves(y, width):
    half = width // 2
    parts = []
    for r in range(0, y.shape[0], width):
        parts += [y[r + half:r + width], y[r:r + half]]
    return jnp.concatenate(parts, axis=0)


def _wp_rope_t_kernel(h_ref, w_ref, cos_ref, sin_ref, o_ref, wb_ref, *, head_dim):
    @pl.when(_first_token_tile())
    def _():
        wb_ref[...] = w_ref[...].astype(BF16)

    for c in range(o_ref.shape[0]):
        tok = slice(c * KEY_TILE, (c + 1) * KEY_TILE)
        r = lax.dot_general(wb_ref[...], h_ref[tok, :], _NT, preferred_element_type=F32)
        for g in range(0, r.shape[0], LANES):
            y = r[g:g + LANES]
            o_ref[c, g:g + LANES, :] = (y * cos_ref[:, tok]
                                        + _swap_row_halves(y, head_dim) * sin_ref[:, tok]).astype(o_ref.dtype)


def _wp_vt_kernel(h_ref, w_ref, o_ref, wb_ref):
    @pl.when(_first_token_tile())
    def _():
        wb_ref[...] = w_ref[...].astype(BF16)

    for c in range(o_ref.shape[0]):
        r = lax.dot_general(wb_ref[...], h_ref[c * KEY_TILE:(c + 1) * KEY_TILE, :], _NT,
                            preferred_element_type=F32)
        o_ref[c] = r.astype(o_ref.dtype)


def _wp_gate_kernel(h_ref, wa_ref, wb_ref, c0_ref, c1_ref, c2_ref, o_ref, d0_ref, d1_ref, d2_ref, wg_ref,
                    *, shift):
    d0_ref[...] = c0_ref[...].astype(BF16)
    d1_ref[...] = c1_ref[...].astype(BF16)
    d2_ref[...] = c2_ref[...].astype(BF16)

    @pl.when(_first_token_tile())
    def _():
        keep = wg_ref.shape[0] - shift
        wg_ref[:keep, :] = wa_ref[shift:, :].astype(BF16)
        wg_ref[keep:, :] = wb_ref[:shift, :].astype(BF16)

    for rows in _row_chunks(h_ref):
        acc = lax.dot_general(h_ref[rows, :], wg_ref[...], _NT, preferred_element_type=F32)
        o_ref[rows, :] = jax.nn.sigmoid(acc).astype(o_ref.dtype)


def _wp_kw_kernel(h_ref, w_ref, cos_ref, sa_ref, sb_ref, klo_ref, khi_ref, wit_ref, wb_ref):
    @pl.when(pl.program_id(0) == 0)
    def _():
        r = lax.broadcasted_iota(jnp.int32, w_ref.shape, 0)
        wb_ref[...] = jnp.where(r < IDX_DIM + IDX_HEADS, w_ref[...], 0.0).astype(BF16)

    h = h_ref[...]
    y = lax.dot_general(h, wb_ref[...], _NT, preferred_element_type=F32)
    r = _rope64(y, cos_ref[...], sa_ref[...], sb_ref[...])
    lane = lax.broadcasted_iota(jnp.int32, y.shape, 1)
    klo = jnp.where(lane < IDX_DIM, r, 0.0)
    klo_ref[...] = klo.astype(BF16)
    khi_ref[...] = pltpu.roll(klo, IDX_DIM, 1).astype(BF16)
    for c in range(wit_ref.shape[0]):
        wit_ref[c] = y[c * KEY_TILE:(c + 1) * KEY_TILE, :].T


def _in_proj(h, wt, l, t, tables128, tables64, tables_kw, out_weights):
    m, k = h.shape
    d = k
    tm = _tile(t, 1024)
    nt = t // tm
    ni = m // tm
    assert tm % KEY_TILE == 0
    tn = 1024
    o_dsa = 3 * MOBA_W
    o_qi = o_dsa + 3 * DSA_W
    o_kw = o_qi + IDX_W
    o_g = o_kw + IDX_DIM + IDX_HEADS
    sem = ("arbitrary", "arbitrary")
    h_spec = pl.BlockSpec((tm, k), lambda j, i: (i, 0))
    tab = pl.BlockSpec((tm, LANES), lambda j, i: (i % nt, 0))
    tabt = pl.BlockSpec((LANES, tm), lambda j, i: (0, i % nt))
    row = pl.BlockSpec((tm, tn), lambda j, i: (i, j))
    out_t = pl.BlockSpec((tm // KEY_TILE, tn, KEY_TILE), lambda j, i: (i, j, 0))

    def wrows(first_blocks, n_first, second_start):
        return pl.BlockSpec((None, tn, k), lambda j, i: (l, jnp.where(j < n_first, first_blocks + j,
                                                                      second_start + j - n_first), 0))

    assert MOBA_W == DSA_W and MOBA_W % tn == 0
    n_m, n_d = MOBA_W // tn, DSA_W // tn

    qt = pl.pallas_call(
        functools.partial(_wp_rope_t_kernel, head_dim=HEAD_DIM),
        grid=(n_m + n_d, ni),
        in_specs=[h_spec, wrows(0, n_m, o_dsa // tn), tabt, tabt],
        out_specs=out_t,
        out_shape=jax.ShapeDtypeStruct((m // KEY_TILE, MOBA_W + DSA_W, KEY_TILE), BF16),
        scratch_shapes=[pltpu.VMEM((tn, k), BF16)],
        compiler_params=_params(sem),
        name="proj_q",
    )(h, wt, *tables128[0])

    kk = pl.pallas_call(
        _wp_rope128_kernel,
        grid=(n_m + n_d, ni),
        in_specs=[h_spec, wrows(MOBA_W // tn, n_m, (o_dsa + DSA_W) // tn), tab, tab],
        out_specs=row,
        out_shape=jax.ShapeDtypeStruct((m, MOBA_W + DSA_W), BF16),
        scratch_shapes=[pltpu.VMEM((tn, k), BF16)],
        compiler_params=_params(sem),
        name="proj_k",
    )(h, wt, *tables128[1])

    vt = pl.pallas_call(
        _wp_vt_kernel,
        grid=(n_m + n_d, ni),
        in_specs=[h_spec, wrows(2 * MOBA_W // tn, n_m, (o_dsa + 2 * DSA_W) // tn)],
        out_specs=out_t,
        out_shape=jax.ShapeDtypeStruct((m // KEY_TILE, MOBA_W + DSA_W, KEY_TILE), BF16),
        scratch_shapes=[pltpu.VMEM((tn, k), BF16)],
        compiler_params=_params(sem),
        name="proj_v",
    )(h, wt)

    n_q = IDX_W // tn
    qit = pl.pallas_call(
        functools.partial(_wp_rope_t_kernel, head_dim=IDX_DIM),
        grid=(n_q, ni),
        in_specs=[h_spec, wrows(o_qi // tn, n_q, 0), tabt, tabt],
        out_specs=out_t,
        out_shape=jax.ShapeDtypeStruct((m // KEY_TILE, IDX_W, KEY_TILE), BF16),
        scratch_shapes=[pltpu.VMEM((tn, k), BF16)],
        compiler_params=_params(sem),
        name="proj_qi",
    )(h, wt, *tables64)

    tab1 = pl.BlockSpec((tm, LANES), lambda i: (i % nt, 0))
    row1 = pl.BlockSpec((tm, LANES), lambda i: (i, 0))
    klo, khi, wit = pl.pallas_call(
        _wp_kw_kernel,
        grid=(ni,),
        in_specs=[pl.BlockSpec((tm, k), lambda i: (i, 0)),
                  pl.BlockSpec((None, LANES, k), lambda i: (l, o_kw // LANES, 0)), tab1, tab1, tab1],
        out_specs=[row1, row1, pl.BlockSpec((tm // KEY_TILE, LANES, KEY_TILE), lambda i: (i, 0, 0))],
        out_shape=[jax.ShapeDtypeStruct((m, LANES), BF16), jax.ShapeDtypeStruct((m, LANES), BF16),
                   jax.ShapeDtypeStruct((m // KEY_TILE, LANES, KEY_TILE), F32)],
        scratch_shapes=[pltpu.VMEM((LANES, k), BF16)],
        compiler_params=_params(("arbitrary",)),
        name="proj_kw",
    )(h, wt, *tables_kw)

    tg = _tile(2 * d, tn)
    g0 = o_kw // tg
    shift = o_g - o_kw
    assert o_kw % tg == 0 and shift <= LANES and tg % LANES == 0
    ng = 2 * d // tg
    c_in, c_out, c_shape = _side_cast_specs([(w, 0) for w in out_weights], ng * ni,
                                            lambda j, i: j * ni + i, 16)
    sg, *out_weights_bf16 = pl.pallas_call(
        functools.partial(_wp_gate_kernel, shift=shift),
        grid=(ng, ni),
        in_specs=[h_spec,
                  pl.BlockSpec((None, tg, k), lambda j, i: (l, g0 + j, 0)),
                  pl.BlockSpec((None, LANES, k), lambda j, i: (l, (g0 + j + 1) * (tg // LANES), 0))] + c_in,
        out_specs=[pl.BlockSpec((tm, tg), lambda j, i: (i, j))] + c_out,
        out_shape=[jax.ShapeDtypeStruct((m, 2 * d), BF16)] + c_shape,
        scratch_shapes=[pltpu.VMEM((tg, k), BF16)],
        compiler_params=_params(sem),
        name="proj_gate",
    )(h, wt, wt, *out_weights)
    return qt, kk, vt, qit, klo, khi, wit, sg, out_weights_bf16


def _moba_kmean_kernel(k_ref, o_ref, km_ref, *, nb):
    km_ref[...] = jnp.zeros(km_ref.shape, F32)
    for j in range(nb):
        blk = k_ref[j * MOBA_BLOCK:(j + 1) * MOBA_BLOCK, :].astype(F32)
        km_ref[j:j + 1, :] = jnp.sum(blk, axis=0, keepdims=True) * (1.0 / MOBA_BLOCK)
    km = km_ref[...]
    tiled = jnp.concatenate([km] * MOBA_HEADS, axis=0)
    r = lax.shift_right_logical(lax.broadcasted_iota(jnp.int32, tiled.shape, 0), 4)
    c = lax.shift_right_logical(lax.broadcasted_iota(jnp.int32, tiled.shape, 1), 7)
    o_ref[...] = jnp.where(r == c, tiled, 0.0).astype(o_ref.dtype)


def _moba_kmean(kk3, nb):
    b, t, _ = kk3.shape
    return pl.pallas_call(
        functools.partial(_moba_kmean_kernel, nb=nb),
        grid=(b,),
        in_specs=[pl.BlockSpec((None, t, MOBA_W), lambda bi: (bi, 0, 0))],
        out_specs=pl.BlockSpec((None, LANES, MOBA_W), lambda bi: (bi, 0, 0)),
        out_shape=jax.ShapeDtypeStruct((b, LANES, MOBA_W), BF16),
        scratch_shapes=[pltpu.VMEM((MAX_BLOCKS, MOBA_W), F32)],
        compiler_params=_params(("parallel",)),
        name="moba_kmean",
    )(kk3)


def _moba_block_bias(qt_ref, kmt_ref, sbt_ref, i, n_sel):
    g = jnp.dot(kmt_ref[...], qt_ref[...], preferred_element_type=F32)
    shape = (MAX_BLOCKS, g.shape[1])
    j = lax.broadcasted_iota(jnp.int32, shape, 0)
    past = j < i
    for h in range(MOBA_HEADS):
        gm = jnp.where(past, g[h * MAX_BLOCKS:(h + 1) * MAX_BLOCKS, :], -jnp.inf)
        rank = jnp.zeros(shape, F32)
        for jo in range(MAX_BLOCKS):
            other = gm[jo:jo + 1, :]
            beats = (other > gm) | ((other == gm) & (jo < j))
            rank = rank + jnp.where(beats, 1.0, 0.0)
        bias = jnp.where(past & (rank < n_sel), 0.0, MASK_BIAS)
        for jb in range(MAX_BLOCKS):
            sbt_ref[h, jb] = jnp.broadcast_to(bias[jb:jb + 1, :], (8, shape[1]))


def _flash_step(st, m, l, acc_t, v_t):
    m_new = jnp.maximum(m, jnp.max(st, axis=0, keepdims=True))
    alpha = jnp.exp2(m - m_new)
    p = jnp.exp2(st - m_new)
    l = alpha * l + jnp.sum(p, axis=0, keepdims=True)
    acc_t = alpha * acc_t + jnp.dot(v_t, p.astype(BF16), preferred_element_type=F32)
    return m_new, l, acc_t


def _flash_step_joint(sts, m, l, acc_t, v_ts):
    m_new = m
    for st in sts:
        m_new = jnp.maximum(m_new, jnp.max(st, axis=0, keepdims=True))
    alpha = jnp.exp2(m - m_new)
    l = alpha * l
    acc_t = alpha * acc_t
    for st, v_t in zip(sts, v_ts):
        p = jnp.exp2(st - m_new)
        l = l + jnp.sum(p, axis=0, keepdims=True)
        acc_t = acc_t + jnp.dot(v_t, p.astype(BF16), preferred_element_type=F32)
    return m_new, l, acc_t


def _flash_finish(o_ref, l_ref, acc_ref, heads):
    for h in range(heads):
        hs = slice(h * HEAD_DIM, (h + 1) * HEAD_DIM)
        o_ref[:, hs] = (acc_ref[h] / l_ref[h:h + 1, :]).T.astype(o_ref.dtype)


def _moba_attn_kernel(qt_ref, k_ref, vt_ref, kmt_ref, o_ref, m_ref, l_ref, acc_ref, sbt_ref, *, n_sel):
    i = pl.program_id(1)
    own = pl.multiple_of(i * MOBA_BLOCK, MOBA_BLOCK)
    sub_shape = (SUB_TILE, MOBA_BLOCK)
    kidx = lax.broadcasted_iota(jnp.int32, sub_shape, 0)
    qidx = lax.broadcasted_iota(jnp.int32, sub_shape, 1)
    _moba_block_bias(qt_ref, kmt_ref, sbt_ref, i, n_sel)
    for h in range(MOBA_HEADS):
        hs = slice(h * HEAD_DIM, (h + 1) * HEAD_DIM)
        m = jnp.full((1, MOBA_BLOCK), NEG_BIG, F32)
        l = jnp.zeros((1, MOBA_BLOCK), F32)
        acc = jnp.zeros((HEAD_DIM, MOBA_BLOCK), F32)
        for u in range(MOBA_BLOCK // SUB_TILE):
            st = jnp.dot(k_ref[pl.ds(own + u * SUB_TILE, SUB_TILE), hs], qt_ref[hs, :],
                         preferred_element_type=F32)
            st = jnp.where(kidx + u * SUB_TILE <= qidx, st, -jnp.inf)
            m, l, acc = _flash_step(st, m, l, acc, vt_ref[i, hs, u * SUB_TILE:(u + 1) * SUB_TILE])
        m_ref[h:h + 1, :] = m
        l_ref[h:h + 1, :] = l
        acc_ref[h] = acc

    def blocks(js):
        for h in range(MOBA_HEADS):
            hs = slice(h * HEAD_DIM, (h + 1) * HEAD_DIM)
            m, l, acc = m_ref[h:h + 1, :], l_ref[h:h + 1, :], acc_ref[h]
            for j in js:
                off = pl.multiple_of(j * MOBA_BLOCK, MOBA_BLOCK)
                bias = sbt_ref[h, j][0:1, :]
                sts, vts = [], []
                for u in range(MOBA_BLOCK // SUB_TILE):
                    sts.append(jnp.dot(k_ref[pl.ds(off + u * SUB_TILE, SUB_TILE), hs], qt_ref[hs, :],
                                       preferred_element_type=F32) + bias)
                    vts.append(vt_ref[j, hs, u * SUB_TILE:(u + 1) * SUB_TILE])
                m, l, acc = _flash_step_joint(sts, m, l, acc, vts)
            m_ref[h:h + 1, :] = m
            l_ref[h:h + 1, :] = l
            acc_ref[h] = acc

    def body(t, carry):
        blocks((2 * t, 2 * t + 1))
        return carry

    lax.fori_loop(0, i // 2, body, 0)

    @pl.when(i % 2 == 1)
    def _():
        blocks((i - 1,))

    _flash_finish(o_ref, l_ref, acc_ref, MOBA_HEADS)


def _moba_attn(qt3, kk3, vt3, kmt, nb, n_sel):
    b, t, _ = kk3.shape
    one = pl.Buffered(1)
    return pl.pallas_call(
        functools.partial(_moba_attn_kernel, n_sel=n_sel),
        grid=(b, nb),
        in_specs=[pl.BlockSpec((None, MOBA_W, MOBA_BLOCK), lambda bi, i: (bi * nb + i, 0, 0)),
                  pl.BlockSpec((None, t, MOBA_W), lambda bi, i: (bi, 0, 0), pipeline_mode=one),
                  pl.BlockSpec((nb, MOBA_W, KEY_TILE), lambda bi, i: (bi, 0, 0), pipeline_mode=one),
                  pl.BlockSpec((None, LANES, MOBA_W), lambda bi, i: (bi, 0, 0), pipeline_mode=one)],
        out_specs=pl.BlockSpec((None, MOBA_BLOCK, MOBA_W), lambda bi, i: (bi, i, 0)),
        out_shape=jax.ShapeDtypeStruct((b, t, MOBA_W), BF16),
        scratch_shapes=[pltpu.VMEM((MOBA_HEADS, MOBA_BLOCK), F32),
                        pltpu.VMEM((MOBA_HEADS, MOBA_BLOCK), F32),
                        pltpu.VMEM((MOBA_HEADS, HEAD_DIM, MOBA_BLOCK), F32),
                        pltpu.VMEM((MOBA_HEADS, MAX_BLOCKS, 8, MOBA_BLOCK), F32)],
        compiler_params=_params(("parallel", "arbitrary")),
        name="moba_attn",
    )(qt3, kk3, vt3, kmt)


def _float_to_key(x):
    b = lax.bitcast_convert_type(x, jnp.int32)
    return jnp.where(b >= 0, b, b ^ 0x7FFFFFFF)


def _key_to_float(t):
    return lax.bitcast_convert_type(jnp.where(t >= 0, t, t ^ 0x7FFFFFFF), F32)


def _dsa_kernel(qit_ref, klo_ref, khi_ref, wt_ref, qt_ref, k_ref, vt_ref, o_ref,
                sc_ref, jl_ref, m_ref, l_ref, acc_ref, *, topk, idx_scale, tq, seq_bits):
    i = pl.program_id(1)
    nk = i + 1
    shape = (KEY_TILE, tq)
    krow = lax.broadcasted_iota(jnp.int32, shape, 0)
    qpos = i * tq + lax.broadcasted_iota(jnp.int32, shape, 1)

    wt = wt_ref[...]

    def score_tiles(kts):
        offs = [pl.multiple_of(kt * KEY_TILE, KEY_TILE) for kt in kts]
        keys = jnp.concatenate(
            [r[pl.ds(off, KEY_TILE), :] for off in offs for r in (klo_ref, khi_ref)], axis=0)
        accs = [jnp.zeros(shape, F32) for _ in kts]
        for p in range(IDX_HEADS // 2):
            s = jnp.dot(keys, qit_ref[p * LANES:(p + 1) * LANES, :], preferred_element_type=F32)
            for n in range(len(kts)):
                lo = s[2 * n * KEY_TILE:(2 * n + 1) * KEY_TILE]
                hi = s[(2 * n + 1) * KEY_TILE:(2 * n + 2) * KEY_TILE]
                accs[n] = accs[n] + (jnp.maximum(lo, 0.0) * wt[2 * p:2 * p + 1, :]
                                     + jnp.maximum(hi, 0.0) * wt[2 * p + 1:2 * p + 2, :])
        for kt, acc in zip(kts, accs):
            kpos = kt * KEY_TILE + krow
            sc_ref[kt] = jnp.where(kpos <= qpos, acc * idx_scale, -jnp.inf)

    def score_body(t, carry):
        score_tiles((2 * t, 2 * t + 1))
        return carry

    lax.fori_loop(0, nk // 2, score_body, 0)

    @pl.when(nk % 2 == 1)
    def _():
        score_tiles((nk - 1,))

    def count(pred):
        def body(kt, acc):
            hit = jnp.where(pred(sc_ref[kt], kt * KEY_TILE + krow), 1.0, 0.0)
            return acc + jnp.sum(hit.reshape(KEY_TILE // 32, 32, tq), axis=0)
        part = lax.fori_loop(0, nk, body, jnp.zeros((32, tq), F32))
        return jnp.sum(part, axis=0, keepdims=True)

    need_select = nk * tq > topk

    @pl.when(jnp.logical_not(need_select))
    def _():
        def body(kt, carry):
            sc_ref[kt] = jnp.where(kt * KEY_TILE + krow <= qpos, 0.0, NEG_BIG)
            return carry
        lax.fori_loop(0, nk, body, 0)

    @pl.when(need_select)
    def _():
        kf = jnp.float32(topk)

        def ext_body(kt, carry):
            s = sc_ref[kt]
            lo = jnp.min(jnp.where(s > -jnp.inf, s, jnp.inf).reshape(KEY_TILE // 32, 32, tq), axis=0)
            hi = jnp.max(s.reshape(KEY_TILE // 32, 32, tq), axis=0)
            return jnp.minimum(carry[0], lo), jnp.maximum(carry[1], hi)

        lo, hi = lax.fori_loop(0, nk, ext_body, (jnp.full((32, tq), jnp.inf, F32),
                                                 jnp.full((32, tq), -jnp.inf, F32)))
        lo = jnp.min(lo, axis=0, keepdims=True)
        hi = _key_to_float(_float_to_key(jnp.max(hi, axis=0, keepdims=True)) + 1)
        n_fin = count(lambda s, kp: s >= lo)
        done = jnp.where(n_fin <= kf, 1.0, 0.0)

        def step(state):
            lo, hi, clo, done = state
            mid = lo * 0.5 + hi * 0.5
            stuck = (mid <= lo) | (mid >= hi)
            c = count(lambda s, kp: s >= mid)
            up = (c >= kf) & (done < 0.5)
            down = (c < kf) & (done < 0.5)
            lo = jnp.where(up, mid, lo)
            clo = jnp.where(up, c, clo)
            hi = jnp.where(down, mid, hi)
            done = jnp.where(stuck | (clo == kf), 1.0, done)
            return lo, hi, clo, done

        state = lax.fori_loop(0, BISECT_FIXED, lambda _, st: step(st), (lo, hi, n_fin, done))
        state = lax.while_loop(lambda st: jnp.min(st[3]) < 0.5, lambda st: step(step(st)), state)
        thr = jnp.where(n_fin < kf, -jnp.inf, state[0])

        def tie_body(kt, acc):
            s = sc_ref[kt]
            gt = jnp.where(s > thr, 1.0, 0.0).reshape(KEY_TILE // 32, 32, tq)
            eq = jnp.where(s == thr, 1.0, 0.0).reshape(KEY_TILE // 32, 32, tq)
            return acc[0] + jnp.sum(gt, axis=0), acc[1] + jnp.sum(eq, axis=0)

        n_gt, n_eq = lax.fori_loop(0, nk, tie_body, (jnp.zeros((32, tq), F32),) * 2)
        need = kf - jnp.sum(n_gt, axis=0, keepdims=True)
        n_eq = jnp.sum(n_eq, axis=0, keepdims=True)
        jl_ref[...] = jnp.full(jl_ref.shape, 2 ** seq_bits, jnp.int32)

        @pl.when(jnp.max(n_eq - need) > 0.0)
        def _():
            def idx_body(b, c):
                cand = c | lax.shift_left(jnp.int32(1), seq_bits - 1 - b)
                g = count(lambda s, kp: (s == thr) & (kp < cand))
                return jnp.where(g < need, cand, c)
            c = lax.fori_loop(0, seq_bits, idx_body, jnp.zeros((1, tq), jnp.int32))
            jl_ref[...] = jnp.broadcast_to(c, jl_ref.shape)

        bound = jnp.minimum(jl_ref[0:1, :], qpos[0:1, :])

        def bias_body(kt, carry):
            s = sc_ref[kt]
            sel = (s > thr) | ((s == thr) & (kt * KEY_TILE + krow <= bound))
            sc_ref[kt] = jnp.where(sel, 0.0, NEG_BIG)
            return carry
        lax.fori_loop(0, nk, bias_body, 0)

    m_ref[...] = jnp.full(m_ref.shape, NEG_BIG, F32)
    l_ref[...] = jnp.zeros(l_ref.shape, F32)
    acc_ref[...] = jnp.zeros(acc_ref.shape, F32)

    def tiles(kts):
        for h in range(DSA_HEADS):
            hs = slice(h * HEAD_DIM, (h + 1) * HEAD_DIM)
            m, l, acc = m_ref[h:h + 1, :], l_ref[h:h + 1, :], acc_ref[h]
            for kt in kts:
                off = pl.multiple_of(kt * KEY_TILE, KEY_TILE)
                for u in range(KEY_TILE // SUB_TILE):
                    us = slice(u * SUB_TILE, (u + 1) * SUB_TILE)
                    st = jnp.dot(k_ref[pl.ds(off + u * SUB_TILE, SUB_TILE), hs], qt_ref[hs, :],
                                 preferred_element_type=F32) + sc_ref[kt, us, :]
                    m, l, acc = _flash_step(st, m, l, acc, vt_ref[kt, hs, us])
            m_ref[h:h + 1, :] = m
            l_ref[h:h + 1, :] = l
            acc_ref[h] = acc

    def attn_body(t, carry):
        tiles((2 * t, 2 * t + 1))
        return carry

    lax.fori_loop(0, nk // 2, attn_body, 0)

    @pl.when(nk % 2 == 1)
    def _():
        tiles((nk - 1,))

    _flash_finish(o_ref, l_ref, acc_ref, DSA_HEADS)


def _dsa_attn(qit3, klo3, khi3, wit, qt3, kk3, vt3, topk):
    b, t, _ = kk3.shape
    tq = KEY_TILE
    nq = t // tq
    seq_bits = max(1, (t - 1).bit_length())
    one = pl.Buffered(1)
    kern = functools.partial(
        _dsa_kernel, topk=topk,
        idx_scale=(IDX_DIM ** -0.5) * (IDX_HEADS ** -0.5), tq=tq, seq_bits=seq_bits)
    return pl.pallas_call(
        kern,
        grid=(b, nq),
        in_specs=[pl.BlockSpec((None, IDX_W, tq), lambda bi, i: (bi * nq + i, 0, 0)),
                  pl.BlockSpec((None, t, LANES), lambda bi, i: (bi, 0, 0), pipeline_mode=one),
                  pl.BlockSpec((None, t, LANES), lambda bi, i: (bi, 0, 0), pipeline_mode=one),
                  pl.BlockSpec((None, IDX_HEADS, tq), lambda bi, i: (bi * nq + i, IDX_DIM // IDX_HEADS, 0)),
                  pl.BlockSpec((None, DSA_W, tq), lambda bi, i: (bi * nq + i, 1, 0)),
                  pl.BlockSpec((None, t, DSA_W), lambda bi, i: (bi, 0, 1), pipeline_mode=one),
                  pl.BlockSpec((nq, DSA_W, KEY_TILE), lambda bi, i: (bi, 1, 0), pipeline_mode=one)],
        out_specs=pl.BlockSpec((None, tq, DSA_W), lambda bi, i: (bi, i, 0)),
        out_shape=jax.ShapeDtypeStruct((b, t, DSA_W), BF16),
        scratch_shapes=[pltpu.VMEM((nq, KEY_TILE, tq), F32),
                        pltpu.VMEM((8, tq), jnp.int32),
                        pltpu.VMEM((DSA_HEADS, tq), F32),
                        pltpu.VMEM((DSA_HEADS, tq), F32),
                        pltpu.VMEM((DSA_HEADS, HEAD_DIM, tq), F32)],
        compiler_params=_params(("parallel", "arbitrary")),
        name="dsa_attn",
    )(qit3, klo3, khi3, wit, qt3, kk3, vt3)


def _mix_out_kernel(oa_ref, ob_ref, sg_ref, wa_ref, wb_ref, wo_ref, x_ref, gt_ref, gpost_ref,
                    gpre_ref, sc_ref, sh_ref, w1f_ref, w2f_ref, x1_ref, h2_ref, w1b_ref, w2b_ref, *, d):
    w1b_ref[...] = w1f_ref[...].astype(BF16)
    w2b_ref[...] = w2f_ref[...].astype(BF16)
    ya = jnp.dot(oa_ref[...], wa_ref[...], preferred_element_type=F32)
    yb = jnp.dot(ob_ref[...], wb_ref[...], preferred_element_type=F32)
    z = sg_ref[:, :d].astype(F32) * ya + sg_ref[:, d:].astype(F32) * yb
    y = jnp.dot(z.astype(BF16), wo_ref[...], preferred_element_type=F32)
    x1 = x_ref[...] + gt_ref[...] * (_rms(y) * gpost_ref[...])
    x1_ref[...] = x1
    h2_ref[...] = ((_rms(x1) * gpre_ref[...]) * (1.0 + sc_ref[...]) + sh_ref[...]).astype(h2_ref.dtype)


def _mix_out(oa3, ob3, sg3, wa, wb, wo, x, gt1, g_post, g_pre2, sc2, sh2, w1f, w2f):
    b, t, d = x.shape
    tm = _tile(t, 256)
    nt = t // tm
    c_in, c_out, c_shape = _side_cast_specs([(w1f, 1), (w2f, 0)], b * nt, lambda bi, i: bi * nt + i, 16)
    one = pl.Buffered(1)
    row = lambda w: pl.BlockSpec((None, tm, w), lambda bi, i: (bi, i, 0))
    full = lambda a: pl.BlockSpec(a.shape, lambda bi, i: (0, 0), pipeline_mode=one)
    vec = pl.BlockSpec((None, 1, d), lambda bi, i: (bi, 0, 0))
    gain = pl.BlockSpec((1, d), lambda bi, i: (0, 0))
    return pl.pallas_call(
        functools.partial(_mix_out_kernel, d=d),
        grid=(b, t // tm),
        in_specs=[row(MOBA_W), row(DSA_W), row(2 * d), full(wa), full(wb), full(wo), row(d),
                  vec, gain, gain, vec, vec] + c_in,
        out_specs=[row(d), row(d)] + c_out,
        out_shape=[jax.ShapeDtypeStruct((b, t, d), F32), jax.ShapeDtypeStruct((b, t, d), BF16)] + c_shape,
        compiler_params=_params(("arbitrary", "arbitrary")),
        name="mix_out",
    )(oa3, ob3, sg3, wa, wb, wo, x, gt1, g_post, g_pre2, sc2, sh2, w1f, w2f)


def _ffn_kernel(h_ref, w1_ref, w2_ref, x1_ref, gt_ref, g_ref, o_ref, acc_ref):
    j = pl.program_id(2)

    @pl.when(j == 0)
    def _():
        acc_ref[...] = jnp.zeros(acc_ref.shape, F32)

    for rows in _row_chunks(h_ref, KEY_TILE):
        u = jnp.dot(h_ref[rows, :], w1_ref[...], preferred_element_type=F32)
        u = jnp.square(jnp.maximum(u, 0.0)).astype(BF16)
        acc_ref[rows, :] += jnp.dot(u, w2_ref[...], preferred_element_type=F32)

    @pl.when(j == pl.num_programs(2) - 1)
    def _():
        o_ref[...] = x1_ref[...] + gt_ref[...] * (_rms(acc_ref[...]) * g_ref[...])


def _ffn(h2, w1, w2, x1, gt2, g_post):
    b, t, d = x1.shape
    ff = w1.shape[1]
    tm = _tile(t, 512)
    tf = _tile(ff, 1024)
    row = lambda: pl.BlockSpec((None, tm, d), lambda bi, i, j: (bi, i, 0))
    return pl.pallas_call(
        _ffn_kernel,
        grid=(b, t // tm, ff // tf),
        in_specs=[row(),
                  pl.BlockSpec((d, tf), lambda bi, i, j: (0, j)),
                  pl.BlockSpec((tf, d), lambda bi, i, j: (j, 0)),
                  row(),
                  pl.BlockSpec((None, 1, d), lambda bi, i, j: (bi, 0, 0)),
                  pl.BlockSpec((1, d), lambda bi, i, j: (0, 0))],
        out_specs=row(),
        out_shape=jax.ShapeDtypeStruct((b, t, d), F32),
        scratch_shapes=[pltpu.VMEM((tm, d), F32)],
        compiler_params=_params(("parallel", "parallel", "arbitrary")),
        name="ffn",
    )(h2, w1, w2, x1, gt2, g_post)


def _rope_tables(t):
    pos = jnp.arange(t, dtype=F32)[:, None]
    lane = jnp.arange(LANES)[None, :]

    def cos_sin(half):
        inv_freq = jnp.power(ROPE_THETA, -jnp.arange(half, dtype=F32) / half)
        ang = pos * inv_freq[None, :]
        reps = LANES // half
        sign = jnp.where(lane % (2 * half) < half, -1.0, 1.0)
        return jnp.tile(jnp.cos(ang), (1, reps)), jnp.tile(jnp.sin(ang), (1, reps)) * sign

    cos_k, sin_k = cos_sin(HEAD_DIM // 2)
    rope128 = (((cos_k * QK_SCALE_LOG2).T, (sin_k * QK_SCALE_LOG2).T), (cos_k, sin_k))
    cos64, sin64 = cos_sin(IDX_DIM // 2)
    is_key = lane < IDX_DIM
    low = lane % IDX_DIM < IDX_DIM // 2
    kw = (jnp.where(is_key, cos64, 1.0), jnp.where(is_key & low, sin64, 0.0),
          jnp.where(is_key & ~low, sin64, 0.0))
    return rope128, (cos64.T, sin64.T), kw


def kernel(x, c, w_ada, b_ada, g_pre_mix, g_post_mix, w_in, w_moba_out, w_dsa_out, w_o,
           g_pre_ffn, g_post_ffn, w_ff1, w_ff2):
    b, t, d = x.shape
    m = b * t
    nb = t // MOBA_BLOCK
    assert t % MOBA_BLOCK == 0 and nb <= MAX_BLOCKS and d % LANES == 0 and b <= 16
    n_sel = max(1, min(MOBA_TOPK, nb - 1))
    topk = min(DSA_TOPK_MAX, t // 4)
    rope128, rope64, rope_kw = _rope_tables(t)
    c_pad = jnp.zeros((16, d), F32).at[:b].set(c)

    for l in range(w_ada.shape[0]):
        mod = _ada(c_pad, w_ada[l], b_ada[l][None, :])[:b]
        sh1, sc1, gt1, sh2, sc2, gt2 = [v[:, None, :] for v in jnp.split(mod, 6, axis=-1)]

        h = _norm_mod(x, g_pre_mix[l][None, :], sc1, sh1).reshape(m, d)
        qt3, kk, vt3, qit3, klo, khi, wit, sg, (wa, wb, wo) = _in_proj(
            h, jnp.swapaxes(w_in, 1, 2), l, t, rope128, rope64, rope_kw,
            (w_moba_out[l], w_dsa_out[l], w_o[l]))

        kk3 = kk.reshape(b, t, -1)
        kmt = _moba_kmean(kk3, nb)
        oa = _moba_attn(qt3, kk3, vt3, kmt, nb, n_sel)
        ob = _dsa_attn(qit3, klo.reshape(b, t, -1), khi.reshape(b, t, -1), wit, qt3, kk3, vt3, topk)

        x, h2, w1, w2 = _mix_out(oa, ob, sg.reshape(b, t, -1), wa, wb, wo, x, gt1,
                                 g_post_mix[l][None, :], g_pre_ffn[l][None, :], sc2, sh2,
                                 w_ff1[l], w_ff2[l])
        x = _ffn(h2, w1, w2, x, gt2, g_post_ffn[l][None, :])
    return x
```

```python
import functools

import jax
import jax.numpy as jnp
from jax import lax
from jax.experimental import pallas as pl
from jax.experimental.pallas import tpu as pltpu

HEAD_DIM = 128
MOBA_HEADS = 8
MOBA_BLOCK = 256
MOBA_TOPK = 3
DSA_HEADS = 8
IDX_HEADS = 16
IDX_DIM = 64
DSA_TOPK_MAX = 256
ROPE_THETA = 10000.0
RMS_EPS = 1e-6

MOBA_W = MOBA_HEADS * HEAD_DIM
DSA_W = DSA_HEADS * HEAD_DIM
IDX_W = IDX_HEADS * IDX_DIM
LANES = 128
MAX_BLOCKS = LANES // MOBA_HEADS
KEY_TILE = 256
SUB_TILE = 128
ROW_CHUNK = 512
LOG2_E = 1.4426950408889634
QK_SCALE_LOG2 = HEAD_DIM ** -0.5 * LOG2_E
MASK_BIAS = -30000.0
NEG_BIG = -1e30
BISECT_FIXED = 18
VMEM_LIMIT = 56 * 1024 * 1024

F32 = jnp.float32
BF16 = jnp.bfloat16
_NT = (((1,), (1,)), ((), ()))


def _params(sem):
    return pltpu.CompilerParams(dimension_semantics=sem, vmem_limit_bytes=VMEM_LIMIT)


def _tile(n, pref):
    if n <= pref:
        return n
    t = pref - pref % LANES
    while t >= LANES:
        if n % t == 0:
            return t
        t -= LANES
    return n


def _side_cast_specs(arrays, n_steps, step_of, row_blocks):
    ins, outs, shapes = [], [], []
    for w, axis in arrays:
        n = w.shape[axis]
        unit = row_blocks if axis == 0 else LANES
        target = max(unit, n // n_steps)
        slab = next(c for c in range(target - target % unit, 0, -unit) if n % c == 0)
        last = n // slab - 1
        block = (slab, w.shape[1]) if axis == 0 else (w.shape[0], slab)

        def imap(*g, axis=axis, last=last):
            idx = jnp.minimum(step_of(*g), last)
            return (idx, 0) if axis == 0 else (0, idx)

        ins.append(pl.BlockSpec(block, imap))
        outs.append(pl.BlockSpec(block, imap))
        shapes.append(jax.ShapeDtypeStruct(w.shape, BF16))
    return ins, outs, shapes


def _rms(x):
    return x * lax.rsqrt(jnp.mean(x * x, axis=-1, keepdims=True) + RMS_EPS)


def _ada_kernel(c_ref, w_ref, b_ref, o_ref):
    c = c_ref[...]
    cs = (c * jax.nn.sigmoid(c)).astype(BF16)
    o_ref[...] = jnp.dot(cs, w_ref[...].astype(BF16), preferred_element_type=F32) + b_ref[...]


def _ada(c_pad, w, b):
    rows, d = c_pad.shape
    n = w.shape[1]
    tn = _tile(n, 1024)
    return pl.pallas_call(
        _ada_kernel,
        grid=(n // tn,),
        in_specs=[pl.BlockSpec((rows, d), lambda j: (0, 0)),
                  pl.BlockSpec((d, tn), lambda j: (0, j)),
                  pl.BlockSpec((1, tn), lambda j: (0, j))],
        out_specs=pl.BlockSpec((rows, tn), lambda j: (0, j)),
        out_shape=jax.ShapeDtypeStruct((rows, n), F32),
        compiler_params=_params(("parallel",)),
        name="ada_mod",
    )(c_pad, w, b)


def _norm_mod_kernel(x_ref, g_ref, sc_ref, sh_ref, o_ref):
    y = _rms(x_ref[...])
    o_ref[...] = ((y * g_ref[...]) * (1.0 + sc_ref[...]) + sh_ref[...]).astype(o_ref.dtype)


def _norm_mod(x, g, sc, sh):
    b, t, d = x.shape
    tt = _tile(t, 1024)
    vec = pl.BlockSpec((None, 1, d), lambda bi, ti: (bi, 0, 0))
    return pl.pallas_call(
        _norm_mod_kernel,
        grid=(b, t // tt),
        in_specs=[pl.BlockSpec((None, tt, d), lambda bi, ti: (bi, ti, 0)),
                  pl.BlockSpec((1, d), lambda bi, ti: (0, 0)), vec, vec],
        out_specs=pl.BlockSpec((None, tt, d), lambda bi, ti: (bi, ti, 0)),
        out_shape=jax.ShapeDtypeStruct((b, t, d), BF16),
        compiler_params=_params(("parallel", "parallel")),
        name="norm_mod",
    )(x, g, sc, sh)


def _first_token_tile():
    return pl.program_id(1) == 0


def _rope64(y, cos, sin_a, sin_b):
    return y * cos + pltpu.roll(y, LANES - 32, 1) * sin_a + pltpu.roll(y, 32, 1) * sin_b


def _row_chunks(ref, step=ROW_CHUNK):
    n = ref.shape[0]
    step = min(n, step)
    return [slice(r, r + step) for r in range(0, n, step)]


def _wp_rope128_kernel(h_ref, w_ref, cos_ref, sin_ref, o_ref, wb_ref):
    @pl.when(_first_token_tile())
    def _():
        wb_ref[...] = w_ref[...].astype(BF16)

    for rows in _row_chunks(h_ref):
        acc = lax.dot_general(h_ref[rows, :], wb_ref[...], _NT, preferred_element_type=F32)
        cos = cos_ref[rows, :]
        sin = sin_ref[rows, :]
        for g in range(acc.shape[1] // LANES):
            y = acc[:, g * LANES:(g + 1) * LANES]
            o_ref[rows, g * LANES:(g + 1) * LANES] = (
                y * cos + pltpu.roll(y, LANES // 2, 1) * sin).astype(o_ref.dtype)


def _swap_row_halves(y, width):
    half = width // 2
    parts = []
    for r in range(0, y.shape[0], width):
        parts += [y[r + half:r + width], y[r:r + half]]
    return jnp.concatenate(parts, axis=0)


def _wp_rope_t_kernel(h_ref, w_ref, cos_ref, sin_ref, o_ref, wb_ref, *, head_dim):
    @pl.when(_first_token_tile())
    def _():
        wb_ref[...] = w_ref[...].astype(BF16)

    for c in range(o_ref.shape[0]):
        tok = slice(c * KEY_TILE, (c + 1) * KEY_TILE)
        r = lax.dot_general(wb_ref[...], h_ref[tok, :], _NT, preferred_element_type=F32)
        for g in range(0, r.shape[0], LANES):
            y = r[g:g + LANES]
            o_ref[c, g:g + LANES, :] = (y * cos_ref[:, tok]
                                        + _swap_row_halves(y, head_dim) * sin_ref[:, tok]).astype(o_ref.dtype)


def _wp_vt_kernel(h_ref, w_ref, o_ref, wb_ref):
    @pl.when(_first_token_tile())
    def _():
        wb_ref[...] = w_ref[...].astype(BF16)

    for c in range(o_ref.shape[0]):
        r = lax.dot_general(wb_ref[...], h_ref[c * KEY_TILE:(c + 1) * KEY_TILE, :], _NT,
                            preferred_element_type=F32)
        o_ref[c] = r.astype(o_ref.dtype)


def _wp_gate_kernel(h_ref, wa_ref, wb_ref, c0_ref, c1_ref, c2_ref, o_ref, d0_ref, d1_ref, d2_ref, wg_ref,
                    *, shift):
    d0_ref[...] = c0_ref[...].astype(BF16)
    d1_ref[...] = c1_ref[...].astype(BF16)
    d2_ref[...] = c2_ref[...].astype(BF16)

    @pl.when(_first_token_tile())
    def _():
        keep = wg_ref.shape[0] - shift
        wg_ref[:keep, :] = wa_ref[shift:, :].astype(BF16)
        wg_ref[keep:, :] = wb_ref[:shift, :].astype(BF16)

    for rows in _row_chunks(h_ref):
        acc = lax.dot_general(h_ref[rows, :], wg_ref[...], _NT, preferred_element_type=F32)
        o_ref[rows, :] = jax.nn.sigmoid(acc).astype(o_ref.dtype)


def _wp_kw_kernel(h_ref, w_ref, cos_ref, sa_ref, sb_ref, klo_ref, khi_ref, wit_ref, wb_ref):
    @pl.when(pl.program_id(0) == 0)
    def _():
        r = lax.broadcasted_iota(jnp.int32, w_ref.shape, 0)
        wb_ref[...] = jnp.where(r < IDX_DIM + IDX_HEADS, w_ref[...], 0.0).astype(BF16)

    h = h_ref[...]
    y = lax.dot_general(h, wb_ref[...], _NT, preferred_element_type=F32)
    r = _rope64(y, cos_ref[...], sa_ref[...], sb_ref[...])
    lane = lax.broadcasted_iota(jnp.int32, y.shape, 1)
    klo = jnp.where(lane < IDX_DIM, r, 0.0)
    klo_ref[...] = klo.astype(BF16)
    khi_ref[...] = pltpu.roll(klo, IDX_DIM, 1).astype(BF16)
    for c in range(wit_ref.shape[0]):
        wit_ref[c] = y[c * KEY_TILE:(c + 1) * KEY_TILE, :].T


def _in_proj(h, wt, l, t, tables128, tables64, tables_kw, out_weights):
    m, k = h.shape
    d = k
    tm = _tile(t, 1024)
    nt = t // tm
    ni = m // tm
    assert tm % KEY_TILE == 0
    tn = 1024
    o_dsa = 3 * MOBA_W
    o_qi = o_dsa + 3 * DSA_W
    o_kw = o_qi + IDX_W
    o_g = o_kw + IDX_DIM + IDX_HEADS
    sem = ("arbitrary", "arbitrary")
    h_spec = pl.BlockSpec((tm, k), lambda j, i: (i, 0))
    tab = pl.BlockSpec((tm, LANES), lambda j, i: (i % nt, 0))
    tabt = pl.BlockSpec((LANES, tm), lambda j, i: (0, i % nt))
    row = pl.BlockSpec((tm, tn), lambda j, i: (i, j))
    out_t = pl.BlockSpec((tm // KEY_TILE, tn, KEY_TILE), lambda j, i: (i, j, 0))

    def wrows(first_blocks, n_first, second_start):
        return pl.BlockSpec((None, tn, k), lambda j, i: (l, jnp.where(j < n_first, first_blocks + j,
                                                                      second_start + j - n_first), 0))

    assert MOBA_W == DSA_W and MOBA_W % tn == 0
    n_m, n_d = MOBA_W // tn, DSA_W // tn

    qt = pl.pallas_call(
        functools.partial(_wp_rope_t_kernel, head_dim=HEAD_DIM),
        grid=(n_m + n_d, ni),
        in_specs=[h_spec, wrows(0, n_m, o_dsa // tn), tabt, tabt],
        out_specs=out_t,
        out_shape=jax.ShapeDtypeStruct((m // KEY_TILE, MOBA_W + DSA_W, KEY_TILE), BF16),
        scratch_shapes=[pltpu.VMEM((tn, k), BF16)],
        compiler_params=_params(sem),
        name="proj_q",
    )(h, wt, *tables128[0])

    kk = pl.pallas_call(
        _wp_rope128_kernel,
        grid=(n_m + n_d, ni),
        in_specs=[h_spec, wrows(MOBA_W // tn, n_m, (o_dsa + DSA_W) // tn), tab, tab],
        out_specs=row,
        out_shape=jax.ShapeDtypeStruct((m, MOBA_W + DSA_W), BF16),
        scratch_shapes=[pltpu.VMEM((tn, k), BF16)],
        compiler_params=_params(sem),
        name="proj_k",
    )(h, wt, *tables128[1])

    vt = pl.pallas_call(
        _wp_vt_kernel,
        grid=(n_m + n_d, ni),
        in_specs=[h_spec, wrows(2 * MOBA_W // tn, n_m, (o_dsa + 2 * DSA_W) // tn)],
        out_specs=out_t,
        out_shape=jax.ShapeDtypeStruct((m // KEY_TILE, MOBA_W + DSA_W, KEY_TILE), BF16),
        scratch_shapes=[pltpu.VMEM((tn, k), BF16)],
        compiler_params=_params(sem),
        name="proj_v",
    )(h, wt)

    n_q = IDX_W // tn
    qit = pl.pallas_call(
        functools.partial(_wp_rope_t_kernel, head_dim=IDX_DIM),
        grid=(n_q, ni),
        in_specs=[h_spec, wrows(o_qi // tn, n_q, 0), tabt, tabt],
        out_specs=out_t,
        out_shape=jax.ShapeDtypeStruct((m // KEY_TILE, IDX_W, KEY_TILE), BF16),
        scratch_shapes=[pltpu.VMEM((tn, k), BF16)],
        compiler_params=_params(sem),
        name="proj_qi",
    )(h, wt, *tables64)

    tab1 = pl.BlockSpec((tm, LANES), lambda i: (i % nt, 0))
    row1 = pl.BlockSpec((tm, LANES), lambda i: (i, 0))
    klo, khi, wit = pl.pallas_call(
        _wp_kw_kernel,
        grid=(ni,),
        in_specs=[pl.BlockSpec((tm, k), lambda i: (i, 0)),
                  pl.BlockSpec((None, LANES, k), lambda i: (l, o_kw // LANES, 0)), tab1, tab1, tab1],
        out_specs=[row1, row1, pl.BlockSpec((tm // KEY_TILE, LANES, KEY_TILE), lambda i: (i, 0, 0))],
        out_shape=[jax.ShapeDtypeStruct((m, LANES), BF16), jax.ShapeDtypeStruct((m, LANES), BF16),
                   jax.ShapeDtypeStruct((m // KEY_TILE, LANES, KEY_TILE), F32)],
        scratch_shapes=[pltpu.VMEM((LANES, k), BF16)],
        compiler_params=_params(("arbitrary",)),
        name="proj_kw",
    )(h, wt, *tables_kw)

    tg = _tile(2 * d, tn)
    g0 = o_kw // tg
    shift = o_g - o_kw
    assert o_kw % tg == 0 and shift <= LANES and tg % LANES == 0
    ng = 2 * d // tg
    c_in, c_out, c_shape = _side_cast_specs([(w, 0) for w in out_weights], ng * ni,
                                            lambda j, i: j * ni + i, 16)
    sg, *out_weights_bf16 = pl.pallas_call(
        functools.partial(_wp_gate_kernel, shift=shift),
        grid=(ng, ni),
        in_specs=[h_spec,
                  pl.BlockSpec((None, tg, k), lambda j, i: (l, g0 + j, 0)),
                  pl.BlockSpec((None, LANES, k), lambda j, i: (l, (g0 + j + 1) * (tg // LANES), 0))] + c_in,
        out_specs=[pl.BlockSpec((tm, tg), lambda j, i: (i, j))] + c_out,
        out_shape=[jax.ShapeDtypeStruct((m, 2 * d), BF16)] + c_shape,
        scratch_shapes=[pltpu.VMEM((tg, k), BF16)],
        compiler_params=_params(sem),
        name="proj_gate",
    )(h, wt, wt, *out_weights)
    return qt, kk, vt, qit, klo, khi, wit, sg, out_weights_bf16


def _moba_kmean_kernel(k_ref, o_ref, km_ref, *, nb):
    km_ref[...] = jnp.zeros(km_ref.shape, F32)
    for j in range(nb):
        blk = k_ref[j * MOBA_BLOCK:(j + 1) * MOBA_BLOCK, :].astype(F32)
        km_ref[j:j + 1, :] = jnp.sum(blk, axis=0, keepdims=True) * (1.0 / MOBA_BLOCK)
    km = km_ref[...]
    tiled = jnp.concatenate([km] * MOBA_HEADS, axis=0)
    r = lax.shift_right_logical(lax.broadcasted_iota(jnp.int32, tiled.shape, 0), MAX_BLOCKS.bit_length() - 1)
    c = lax.shift_right_logical(lax.broadcasted_iota(jnp.int32, tiled.shape, 1), HEAD_DIM.bit_length() - 1)
    o_ref[...] = jnp.where(r == c, tiled, 0.0).astype(o_ref.dtype)


def _moba_kmean(kk3, nb):
    b, t, _ = kk3.shape
    return pl.pallas_call(
        functools.partial(_moba_kmean_kernel, nb=nb),
        grid=(b,),
        in_specs=[pl.BlockSpec((None, t, MOBA_W), lambda bi: (bi, 0, 0))],
        out_specs=pl.BlockSpec((None, LANES, MOBA_W), lambda bi: (bi, 0, 0)),
        out_shape=jax.ShapeDtypeStruct((b, LANES, MOBA_W), BF16),
        scratch_shapes=[pltpu.VMEM((MAX_BLOCKS, MOBA_W), F32)],
        compiler_params=_params(("parallel",)),
        name="moba_kmean",
    )(kk3)


def _moba_block_bias(qt_ref, kmt_ref, sbt_ref, i, n_sel):
    g = jnp.dot(kmt_ref[...], qt_ref[...], preferred_element_type=F32)
    shape = (MAX_BLOCKS, g.shape[1])
    j = lax.broadcasted_iota(jnp.int32, shape, 0)
    past = j < i
    for h in range(MOBA_HEADS):
        gm = jnp.where(past, g[h * MAX_BLOCKS:(h + 1) * MAX_BLOCKS, :], -jnp.inf)
        rank = jnp.zeros(shape, F32)
        for jo in range(MAX_BLOCKS):
            other = gm[jo:jo + 1, :]
            beats = (other > gm) | ((other == gm) & (jo < j))
            rank = rank + jnp.where(beats, 1.0, 0.0)
        bias = jnp.where(past & (rank < n_sel), 0.0, MASK_BIAS)
        for jb in range(MAX_BLOCKS):
            sbt_ref[h, jb] = jnp.broadcast_to(bias[jb:jb + 1, :], (8, shape[1]))


def _flash_step(st, m, l, acc_t, v_t):
    m_new = jnp.maximum(m, jnp.max(st, axis=0, keepdims=True))
    alpha = jnp.exp2(m - m_new)
    p = jnp.exp2(st - m_new)
    l = alpha * l + jnp.sum(p, axis=0, keepdims=True)
    acc_t = alpha * acc_t + jnp.dot(v_t, p.astype(BF16), preferred_element_type=F32)
    return m_new, l, acc_t


def _flash_step_joint(sts, m, l, acc_t, v_ts):
    m_new = m
    for st in sts:
        m_new = jnp.maximum(m_new, jnp.max(st, axis=0, keepdims=True))
    alpha = jnp.exp2(m - m_new)
    l = alpha * l
    acc_t = alpha * acc_t
    for st, v_t in zip(sts, v_ts):
        p = jnp.exp2(st - m_new)
        l = l + jnp.sum(p, axis=0, keepdims=True)
        acc_t = acc_t + jnp.dot(v_t, p.astype(BF16), preferred_element_type=F32)
    return m_new, l, acc_t


def _flash_finish(o_ref, l_ref, acc_ref, heads):
    for h in range(heads):
        hs = slice(h * HEAD_DIM, (h + 1) * HEAD_DIM)
        o_ref[:, hs] = (acc_ref[h] / l_ref[h:h + 1, :]).T.astype(o_ref.dtype)


def _moba_attn_kernel(qt_ref, k_ref, vt_ref, kmt_ref, o_ref, m_ref, l_ref, acc_ref, sbt_ref, *, n_sel):
    i = pl.program_id(1)
    own = pl.multiple_of(i * MOBA_BLOCK, MOBA_BLOCK)
    sub_shape = (SUB_TILE, MOBA_BLOCK)
    kidx = lax.broadcasted_iota(jnp.int32, sub_shape, 0)
    qidx = lax.broadcasted_iota(jnp.int32, sub_shape, 1)
    _moba_block_bias(qt_ref, kmt_ref, sbt_ref, i, n_sel)
    for h in range(MOBA_HEADS):
        hs = slice(h * HEAD_DIM, (h + 1) * HEAD_DIM)
        m = jnp.full((1, MOBA_BLOCK), NEG_BIG, F32)
        l = jnp.zeros((1, MOBA_BLOCK), F32)
        acc = jnp.zeros((HEAD_DIM, MOBA_BLOCK), F32)
        for u in range(MOBA_BLOCK // SUB_TILE):
            st = jnp.dot(k_ref[pl.ds(own + u * SUB_TILE, SUB_TILE), hs], qt_ref[hs, :],
                         preferred_element_type=F32)
            st = jnp.where(kidx + u * SUB_TILE <= qidx, st, -jnp.inf)
            m, l, acc = _flash_step(st, m, l, acc, vt_ref[i, hs, u * SUB_TILE:(u + 1) * SUB_TILE])
        m_ref[h:h + 1, :] = m
        l_ref[h:h + 1, :] = l
        acc_ref[h] = acc

    def blocks(js):
        for h in range(MOBA_HEADS):
            hs = slice(h * HEAD_DIM, (h + 1) * HEAD_DIM)
            m, l, acc = m_ref[h:h + 1, :], l_ref[h:h + 1, :], acc_ref[h]
            for j in js:
                off = pl.multiple_of(j * MOBA_BLOCK, MOBA_BLOCK)
                bias = sbt_ref[h, j][0:1, :]
                sts, vts = [], []
                for u in range(MOBA_BLOCK // SUB_TILE):
                    sts.append(jnp.dot(k_ref[pl.ds(off + u * SUB_TILE, SUB_TILE), hs], qt_ref[hs, :],
                                       preferred_element_type=F32) + bias)
                    vts.append(vt_ref[j, hs, u * SUB_TILE:(u + 1) * SUB_TILE])
                m, l, acc = _flash_step_joint(sts, m, l, acc, vts)
            m_ref[h:h + 1, :] = m
            l_ref[h:h + 1, :] = l
            acc_ref[h] = acc

    def body(t, carry):
        blocks((2 * t, 2 * t + 1))
        return carry

    lax.fori_loop(0, i // 2, body, 0)

    @pl.when(i % 2 == 1)
    def _():
        blocks((i - 1,))

    _flash_finish(o_ref, l_ref, acc_ref, MOBA_HEADS)


def _moba_attn(qt3, kk3, vt3, kmt, nb, n_sel):
    b, t, _ = kk3.shape
    one = pl.Buffered(1)
    return pl.pallas_call(
        functools.partial(_moba_attn_kernel, n_sel=n_sel),
        grid=(b, nb),
        in_specs=[pl.BlockSpec((None, MOBA_W, MOBA_BLOCK), lambda bi, i: (bi * nb + i, 0, 0)),
                  pl.BlockSpec((None, t, MOBA_W), lambda bi, i: (bi, 0, 0), pipeline_mode=one),
                  pl.BlockSpec((nb, MOBA_W, KEY_TILE), lambda bi, i: (bi, 0, 0), pipeline_mode=one),
                  pl.BlockSpec((None, LANES, MOBA_W), lambda bi, i: (bi, 0, 0), pipeline_mode=one)],
        out_specs=pl.BlockSpec((None, MOBA_BLOCK, MOBA_W), lambda bi, i: (bi, i, 0)),
        out_shape=jax.ShapeDtypeStruct((b, t, MOBA_W), BF16),
        scratch_shapes=[pltpu.VMEM((MOBA_HEADS, MOBA_BLOCK), F32),
                        pltpu.VMEM((MOBA_HEADS, MOBA_BLOCK), F32),
                        pltpu.VMEM((MOBA_HEADS, HEAD_DIM, MOBA_BLOCK), F32),
                        pltpu.VMEM((MOBA_HEADS, MAX_BLOCKS, 8, MOBA_BLOCK), F32)],
        compiler_params=_params(("parallel", "arbitrary")),
        name="moba_attn",
    )(qt3, kk3, vt3, kmt)


def _float_to_key(x):
    b = lax.bitcast_convert_type(x, jnp.int32)
    return jnp.where(b >= 0, b, b ^ 0x7FFFFFFF)


def _key_to_float(t):
    return lax.bitcast_convert_type(jnp.where(t >= 0, t, t ^ 0x7FFFFFFF), F32)


def _dsa_kernel(qit_ref, klo_ref, khi_ref, wt_ref, qt_ref, k_ref, vt_ref, o_ref,
                sc_ref, jl_ref, m_ref, l_ref, acc_ref, *, topk, idx_scale, tq, seq_bits):
    i = pl.program_id(1)
    nk = i + 1
    shape = (KEY_TILE, tq)
    krow = lax.broadcasted_iota(jnp.int32, shape, 0)
    qpos = i * tq + lax.broadcasted_iota(jnp.int32, shape, 1)

    wt = wt_ref[...]

    def score_tiles(kts):
        offs = [pl.multiple_of(kt * KEY_TILE, KEY_TILE) for kt in kts]
        keys = jnp.concatenate(
            [r[pl.ds(off, KEY_TILE), :] for off in offs for r in (klo_ref, khi_ref)], axis=0)
        accs = [jnp.zeros(shape, F32) for _ in kts]
        for p in range(IDX_HEADS // 2):
            s = jnp.dot(keys, qit_ref[p * LANES:(p + 1) * LANES, :], preferred_element_type=F32)
            for n in range(len(kts)):
                lo = s[2 * n * KEY_TILE:(2 * n + 1) * KEY_TILE]
                hi = s[(2 * n + 1) * KEY_TILE:(2 * n + 2) * KEY_TILE]
                accs[n] = accs[n] + (jnp.maximum(lo, 0.0) * wt[2 * p:2 * p + 1, :]
                                     + jnp.maximum(hi, 0.0) * wt[2 * p + 1:2 * p + 2, :])
        for kt, acc in zip(kts, accs):
            kpos = kt * KEY_TILE + krow
            sc_ref[kt] = jnp.where(kpos <= qpos, acc * idx_scale, -jnp.inf)

    def score_body(t, carry):
        score_tiles((2 * t, 2 * t + 1))
        return carry

    lax.fori_loop(0, nk // 2, score_body, 0)

    @pl.when(nk % 2 == 1)
    def _():
        score_tiles((nk - 1,))

    def count(pred):
        def body(kt, acc):
            hit = jnp.where(pred(sc_ref[kt], kt * KEY_TILE + krow), 1.0, 0.0)
            return acc + jnp.sum(hit.reshape(KEY_TILE // 32, 32, tq), axis=0)
        part = lax.fori_loop(0, nk, body, jnp.zeros((32, tq), F32))
        return jnp.sum(part, axis=0, keepdims=True)

    need_select = nk * tq > topk

    @pl.when(jnp.logical_not(need_select))
    def _():
        def body(kt, carry):
            sc_ref[kt] = jnp.where(kt * KEY_TILE + krow <= qpos, 0.0, NEG_BIG)
            return carry
        lax.fori_loop(0, nk, body, 0)

    @pl.when(need_select)
    def _():
        kf = jnp.float32(topk)

        def ext_body(kt, carry):
            s = sc_ref[kt]
            lo = jnp.min(jnp.where(s > -jnp.inf, s, jnp.inf).reshape(KEY_TILE // 32, 32, tq), axis=0)
            hi = jnp.max(s.reshape(KEY_TILE // 32, 32, tq), axis=0)
            return jnp.minimum(carry[0], lo), jnp.maximum(carry[1], hi)

        lo, hi = lax.fori_loop(0, nk, ext_body, (jnp.full((32, tq), jnp.inf, F32),
                                                 jnp.full((32, tq), -jnp.inf, F32)))
        lo = jnp.min(lo, axis=0, keepdims=True)
        hi = _key_to_float(_float_to_key(jnp.max(hi, axis=0, keepdims=True)) + 1)
        n_fin = count(lambda s, kp: s >= lo)
        done = jnp.where(n_fin <= kf, 1.0, 0.0)

        def step(state):
            lo, hi, clo, done = state
            mid = lo * 0.5 + hi * 0.5
            stuck = (mid <= lo) | (mid >= hi)
            c = count(lambda s, kp: s >= mid)
            up = (c >= kf) & (done < 0.5)
            down = (c < kf) & (done < 0.5)
            lo = jnp.where(up, mid, lo)
            clo = jnp.where(up, c, clo)
            hi = jnp.where(down, mid, hi)
            done = jnp.where(stuck | (clo == kf), 1.0, done)
            return lo, hi, clo, done

        state = lax.fori_loop(0, BISECT_FIXED, lambda _, st: step(st), (lo, hi, n_fin, done))
        state = lax.while_loop(lambda st: jnp.min(st[3]) < 0.5, step, state)
        thr = jnp.where(n_fin < kf, -jnp.inf, state[0])

        def tie_body(kt, acc):
            s = sc_ref[kt]
            gt = jnp.where(s > thr, 1.0, 0.0).reshape(KEY_TILE // 32, 32, tq)
            eq = jnp.where(s == thr, 1.0, 0.0).reshape(KEY_TILE // 32, 32, tq)
            return acc[0] + jnp.sum(gt, axis=0), acc[1] + jnp.sum(eq, axis=0)

        n_gt, n_eq = lax.fori_loop(0, nk, tie_body, (jnp.zeros((32, tq), F32),) * 2)
        need = kf - jnp.sum(n_gt, axis=0, keepdims=True)
        n_eq = jnp.sum(n_eq, axis=0, keepdims=True)
        jl_ref[...] = jnp.full(jl_ref.shape, 2 ** seq_bits, jnp.int32)

        @pl.when(jnp.max(n_eq - need) > 0.0)
        def _():
            def idx_body(b, c):
                cand = c | lax.shift_left(jnp.int32(1), seq_bits - 1 - b)
                g = count(lambda s, kp: (s == thr) & (kp < cand))
                return jnp.where(g < need, cand, c)
            c = lax.fori_loop(0, seq_bits, idx_body, jnp.zeros((1, tq), jnp.int32))
            jl_ref[...] = jnp.broadcast_to(c, jl_ref.shape)

        bound = jnp.minimum(jl_ref[0:1, :], qpos[0:1, :])

        def bias_body(kt, carry):
            s = sc_ref[kt]
            sel = (s > thr) | ((s == thr) & (kt * KEY_TILE + krow <= bound))
            sc_ref[kt] = jnp.where(sel, 0.0, NEG_BIG)
            return carry
        lax.fori_loop(0, nk, bias_body, 0)

    m_ref[...] = jnp.full(m_ref.shape, NEG_BIG, F32)
    l_ref[...] = jnp.zeros(l_ref.shape, F32)
    acc_ref[...] = jnp.zeros(acc_ref.shape, F32)

    def tiles(kts):
        for h in range(DSA_HEADS):
            hs = slice(h * HEAD_DIM, (h + 1) * HEAD_DIM)
            m, l, acc = m_ref[h:h + 1, :], l_ref[h:h + 1, :], acc_ref[h]
            for kt in kts:
                off = pl.multiple_of(kt * KEY_TILE, KEY_TILE)
                for u in range(KEY_TILE // SUB_TILE):
                    us = slice(u * SUB_TILE, (u + 1) * SUB_TILE)
                    st = jnp.dot(k_ref[pl.ds(off + u * SUB_TILE, SUB_TILE), hs], qt_ref[hs, :],
                                 preferred_element_type=F32) + sc_ref[kt, us, :]
                    m, l, acc = _flash_step(st, m, l, acc, vt_ref[kt, hs, us])
            m_ref[h:h + 1, :] = m
            l_ref[h:h + 1, :] = l
            acc_ref[h] = acc

    def attn_body(t, carry):
        tiles((2 * t, 2 * t + 1))
        return carry

    lax.fori_loop(0, nk // 2, attn_body, 0)

    @pl.when(nk % 2 == 1)
    def _():
        tiles((nk - 1,))

    _flash_finish(o_ref, l_ref, acc_ref, DSA_HEADS)


def _dsa_attn(qit3, klo3, khi3, wit, qt3, kk3, vt3, topk):
    b, t, _ = kk3.shape
    tq = KEY_TILE
    nq = t // tq
    seq_bits = max(1, (t - 1).bit_length())
    one = pl.Buffered(1)
    kern = functools.partial(
        _dsa_kernel, topk=topk,
        idx_scale=(IDX_DIM ** -0.5) * (IDX_HEADS ** -0.5), tq=tq, seq_bits=seq_bits)
    return pl.pallas_call(
        kern,
        grid=(b, nq),
        in_specs=[pl.BlockSpec((None, IDX_W, tq), lambda bi, i: (bi * nq + i, 0, 0)),
                  pl.BlockSpec((None, t, LANES), lambda bi, i: (bi, 0, 0), pipeline_mode=one),
                  pl.BlockSpec((None, t, LANES), lambda bi, i: (bi, 0, 0), pipeline_mode=one),
                  pl.BlockSpec((None, IDX_HEADS, tq), lambda bi, i: (bi * nq + i, IDX_DIM // IDX_HEADS, 0)),
                  pl.BlockSpec((None, DSA_W, tq), lambda bi, i: (bi * nq + i, 1, 0)),
                  pl.BlockSpec((None, t, DSA_W), lambda bi, i: (bi, 0, 1), pipeline_mode=one),
                  pl.BlockSpec((nq, DSA_W, KEY_TILE), lambda bi, i: (bi, 1, 0), pipeline_mode=one)],
        out_specs=pl.BlockSpec((None, tq, DSA_W), lambda bi, i: (bi, i, 0)),
        out_shape=jax.ShapeDtypeStruct((b, t, DSA_W), BF16),
        scratch_shapes=[pltpu.VMEM((nq, KEY_TILE, tq), F32),
                        pltpu.VMEM((8, tq), jnp.int32),
                        pltpu.VMEM((DSA_HEADS, tq), F32),
                        pltpu.VMEM((DSA_HEADS, tq), F32),
                        pltpu.VMEM((DSA_HEADS, HEAD_DIM, tq), F32)],
        compiler_params=_params(("parallel", "arbitrary")),
        name="dsa_attn",
    )(qit3, klo3, khi3, wit, qt3, kk3, vt3)


def _mix_out_kernel(oa_ref, ob_ref, sg_ref, wa_ref, wb_ref, wo_ref, x_ref, gt_ref, gpost_ref,
                    gpre_ref, sc_ref, sh_ref, w1f_ref, w2f_ref, x1_ref, h2_ref, w1b_ref, w2b_ref, *, d):
    w1b_ref[...] = w1f_ref[...].astype(BF16)
    w2b_ref[...] = w2f_ref[...].astype(BF16)
    ya = jnp.dot(oa_ref[...], wa_ref[...], preferred_element_type=F32)
    yb = jnp.dot(ob_ref[...], wb_ref[...], preferred_element_type=F32)
    z = sg_ref[:, :d].astype(F32) * ya + sg_ref[:, d:].astype(F32) * yb
    y = jnp.dot(z.astype(BF16), wo_ref[...], preferred_element_type=F32)
    x1 = x_ref[...] + gt_ref[...] * (_rms(y) * gpost_ref[...])
    x1_ref[...] = x1
    h2_ref[...] = ((_rms(x1) * gpre_ref[...]) * (1.0 + sc_ref[...]) + sh_ref[...]).astype(h2_ref.dtype)


def _mix_out(oa3, ob3, sg3, wa, wb, wo, x, gt1, g_post, g_pre2, sc2, sh2, w1f, w2f):
    b, t, d = x.shape
    tm = _tile(t, 256)
    nt = t // tm
    c_in, c_out, c_shape = _side_cast_specs([(w1f, 1), (w2f, 0)], b * nt, lambda bi, i: bi * nt + i, 16)
    one = pl.Buffered(1)
    row = lambda w: pl.BlockSpec((None, tm, w), lambda bi, i: (bi, i, 0))
    full = lambda a: pl.BlockSpec(a.shape, lambda bi, i: (0, 0), pipeline_mode=one)
    vec = pl.BlockSpec((None, 1, d), lambda bi, i: (bi, 0, 0))
    gain = pl.BlockSpec((1, d), lambda bi, i: (0, 0))
    return pl.pallas_call(
        functools.partial(_mix_out_kernel, d=d),
        grid=(b, t // tm),
        in_specs=[row(MOBA_W), row(DSA_W), row(2 * d), full(wa), full(wb), full(wo), row(d),
                  vec, gain, gain, vec, vec] + c_in,
        out_specs=[row(d), row(d)] + c_out,
        out_shape=[jax.ShapeDtypeStruct((b, t, d), F32), jax.ShapeDtypeStruct((b, t, d), BF16)] + c_shape,
        compiler_params=_params(("arbitrary", "arbitrary")),
        name="mix_out",
    )(oa3, ob3, sg3, wa, wb, wo, x, gt1, g_post, g_pre2, sc2, sh2, w1f, w2f)


def _ffn_kernel(h_ref, w1_ref, w2_ref, x1_ref, gt_ref, g_ref, o_ref, acc_ref):
    j = pl.program_id(2)

    @pl.when(j == 0)
    def _():
        acc_ref[...] = jnp.zeros(acc_ref.shape, F32)

    for rows in _row_chunks(h_ref, KEY_TILE):
        u = jnp.dot(h_ref[rows, :], w1_ref[...], preferred_element_type=F32)
        u = jnp.square(jnp.maximum(u, 0.0)).astype(BF16)
        acc_ref[rows, :] += jnp.dot(u, w2_ref[...], preferred_element_type=F32)

    @pl.when(j == pl.num_programs(2) - 1)
    def _():
        o_ref[...] = x1_ref[...] + gt_ref[...] * (_rms(acc_ref[...]) * g_ref[...])


def _ffn(h2, w1, w2, x1, gt2, g_post):
    b, t, d = x1.shape
    ff = w1.shape[1]
    tm = _tile(t, 512)
    tf = _tile(ff, 1024)
    row = lambda: pl.BlockSpec((None, tm, d), lambda bi, i, j: (bi, i, 0))
    return pl.pallas_call(
        _ffn_kernel,
        grid=(b, t // tm, ff // tf),
        in_specs=[row(),
                  pl.BlockSpec((d, tf), lambda bi, i, j: (0, j)),
                  pl.BlockSpec((tf, d), lambda bi, i, j: (j, 0)),
                  row(),
                  pl.BlockSpec((None, 1, d), lambda bi, i, j: (bi, 0, 0)),
                  pl.BlockSpec((1, d), lambda bi, i, j: (0, 0))],
        out_specs=row(),
        out_shape=jax.ShapeDtypeStruct((b, t, d), F32),
        scratch_shapes=[pltpu.VMEM((tm, d), F32)],
        compiler_params=_params(("parallel", "parallel", "arbitrary")),
        name="ffn",
    )(h2, w1, w2, x1, gt2, g_post)


def _rope_tables(t):
    pos = jnp.arange(t, dtype=F32)[:, None]
    lane = jnp.arange(LANES)[None, :]

    def cos_sin(half):
        inv_freq = jnp.power(ROPE_THETA, -jnp.arange(half, dtype=F32) / half)
        ang = pos * inv_freq[None, :]
        reps = LANES // half
        sign = jnp.where(lane % (2 * half) < half, -1.0, 1.0)
        return jnp.tile(jnp.cos(ang), (1, reps)), jnp.tile(jnp.sin(ang), (1, reps)) * sign

    cos_k, sin_k = cos_sin(HEAD_DIM // 2)
    rope128 = (((cos_k * QK_SCALE_LOG2).T, (sin_k * QK_SCALE_LOG2).T), (cos_k, sin_k))
    cos64, sin64 = cos_sin(IDX_DIM // 2)
    is_key = lane < IDX_DIM
    low = lane % IDX_DIM < IDX_DIM // 2
    kw = (jnp.where(is_key, cos64, 1.0), jnp.where(is_key & low, sin64, 0.0),
          jnp.where(is_key & ~low, sin64, 0.0))
    return rope128, (cos64.T, sin64.T), kw


def kernel(x, c, w_ada, b_ada, g_pre_mix, g_post_mix, w_in, w_moba_out, w_dsa_out, w_o,
           g_pre_ffn, g_post_ffn, w_ff1, w_ff2):
    b, t, d = x.shape
    m = b * t
    nb = t // MOBA_BLOCK
    assert t % MOBA_BLOCK == 0 and nb <= MAX_BLOCKS and d % LANES == 0 and b <= 16
    n_sel = max(1, min(MOBA_TOPK, nb - 1))
    topk = min(DSA_TOPK_MAX, t // 4)
    rope128, rope64, rope_kw = _rope_tables(t)
    c_pad = jnp.zeros((16, d), F32).at[:b].set(c)

    for l in range(w_ada.shape[0]):
        mod = _ada(c_pad, w_ada[l], b_ada[l][None, :])[:b]
        sh1, sc1, gt1, sh2, sc2, gt2 = [v[:, None, :] for v in jnp.split(mod, 6, axis=-1)]

        h = _norm_mod(x, g_pre_mix[l][None, :], sc1, sh1).reshape(m, d)
        qt3, kk, vt3, qit3, klo, khi, wit, sg, (wa, wb, wo) = _in_proj(
            h, jnp.swapaxes(w_in, 1, 2), l, t, rope128, rope64, rope_kw,
            (w_moba_out[l], w_dsa_out[l], w_o[l]))

        kk3 = kk.reshape(b, t, -1)
        kmt = _moba_kmean(kk3, nb)
        oa = _moba_attn(qt3, kk3, vt3, kmt, nb, n_sel)
        ob = _dsa_attn(qit3, klo.reshape(b, t, -1), khi.reshape(b, t, -1), wit, qt3, kk3, vt3, topk)

        x, h2, w1, w2 = _mix_out(oa, ob, sg.reshape(b, t, -1), wa, wb, wo, x, gt1,
                                 g_post_mix[l][None, :], g_pre_ffn[l][None, :], sc2, sh2,
                                 w_ff1[l], w_ff2[l])
        x = _ffn(h2, w1, w2, x, gt2, g_post_ffn[l][None, :])
    return x
```

```python
import functools

import jax
import jax.numpy as jnp
from jax import lax
from jax.experimental import pallas as pl
from jax.experimental.pallas import tpu as pltpu

HEAD_DIM = 128
MOBA_HEADS = 8
MOBA_BLOCK = 256
MOBA_TOPK = 3
DSA_HEADS = 8
IDX_HEADS = 16
IDX_DIM = 64
DSA_TOPK_MAX = 256
ROPE_THETA = 10000.0
RMS_EPS = 1e-6

MOBA_W = MOBA_HEADS * HEAD_DIM
DSA_W = DSA_HEADS * HEAD_DIM
IDX_W = IDX_HEADS * IDX_DIM
LANES = 128
MAX_BLOCKS = LANES // MOBA_HEADS
KEY_TILE = 256
SUB_TILE = 128
ROW_CHUNK = 256
LOG2_E = 1.4426950408889634
QK_SCALE_LOG2 = HEAD_DIM ** -0.5 * LOG2_E
MASK_BIAS = -30000.0
NEG_BIG = -1e30
BISECT_FIXED = 18
VMEM_LIMIT = 56 * 1024 * 1024

F32 = jnp.float32
BF16 = jnp.bfloat16
_NT = (((1,), (1,)), ((), ()))


def _params(sem):
    return pltpu.CompilerParams(dimension_semantics=sem, vmem_limit_bytes=VMEM_LIMIT)


def _tile(n, pref):
    if n <= pref:
        return n
    t = pref - pref % LANES
    while t >= LANES:
        if n % t == 0:
            return t
        t -= LANES
    return n


def _side_cast_specs(arrays, n_steps, step_of, row_blocks):
    ins, outs, shapes = [], [], []
    for w, axis in arrays:
        n = w.shape[axis]
        unit = row_blocks if axis == 0 else LANES
        target = max(unit, n // n_steps)
        slab = next(c for c in range(target - target % unit, 0, -unit) if n % c == 0)
        last = n // slab - 1
        block = (slab, w.shape[1]) if axis == 0 else (w.shape[0], slab)

        def imap(*g, axis=axis, last=last):
            idx = jnp.minimum(step_of(*g), last)
            return (idx, 0) if axis == 0 else (0, idx)

        ins.append(pl.BlockSpec(block, imap))
        outs.append(pl.BlockSpec(block, imap))
        shapes.append(jax.ShapeDtypeStruct(w.shape, BF16))
    return ins, outs, shapes


def _rms(x):
    return x * lax.rsqrt(jnp.mean(x * x, axis=-1, keepdims=True) + RMS_EPS)


def _ada_kernel(c_ref, w_ref, b_ref, o_ref):
    c = c_ref[...]
    cs = (c * jax.nn.sigmoid(c)).astype(BF16)
    o_ref[...] = jnp.dot(cs, w_ref[...].astype(BF16), preferred_element_type=F32) + b_ref[...]


def _ada(c_pad, w, b):
    rows, d = c_pad.shape
    n = w.shape[1]
    tn = _tile(n, 1024)
    return pl.pallas_call(
        _ada_kernel,
        grid=(n // tn,),
        in_specs=[pl.BlockSpec((rows, d), lambda j: (0, 0)),
                  pl.BlockSpec((d, tn), lambda j: (0, j)),
                  pl.BlockSpec((1, tn), lambda j: (0, j))],
        out_specs=pl.BlockSpec((rows, tn), lambda j: (0, j)),
        out_shape=jax.ShapeDtypeStruct((rows, n), F32),
        compiler_params=_params(("parallel",)),
        name="ada_mod",
    )(c_pad, w, b)


def _norm_mod_kernel(x_ref, g_ref, sc_ref, sh_ref, o_ref):
    y = _rms(x_ref[...])
    o_ref[...] = ((y * g_ref[...]) * (1.0 + sc_ref[...]) + sh_ref[...]).astype(o_ref.dtype)


def _norm_mod(x, g, sc, sh):
    b, t, d = x.shape
    tt = _tile(t, 1024)
    vec = pl.BlockSpec((None, 1, d), lambda bi, ti: (bi, 0, 0))
    return pl.pallas_call(
        _norm_mod_kernel,
        grid=(b, t // tt),
        in_specs=[pl.BlockSpec((None, tt, d), lambda bi, ti: (bi, ti, 0)),
                  pl.BlockSpec((1, d), lambda bi, ti: (0, 0)), vec, vec],
        out_specs=pl.BlockSpec((None, tt, d), lambda bi, ti: (bi, ti, 0)),
        out_shape=jax.ShapeDtypeStruct((b, t, d), BF16),
        compiler_params=_params(("parallel", "parallel")),
        name="norm_mod",
    )(x, g, sc, sh)


def _first_token_tile():
    return pl.program_id(1) == 0


def _rope64(y, cos, sin_a, sin_b):
    return y * cos + pltpu.roll(y, LANES - 32, 1) * sin_a + pltpu.roll(y, 32, 1) * sin_b


def _row_chunks(ref, step=ROW_CHUNK):
    n = ref.shape[0]
    step = min(n, step)
    return [slice(r, r + step) for r in range(0, n, step)]


def _wp_rope128_kernel(h_ref, w_ref, cos_ref, sin_ref, o_ref, wb_ref):
    @pl.when(_first_token_tile())
    def _():
        wb_ref[...] = w_ref[...].astype(BF16)

    for rows in _row_chunks(h_ref):
        acc = lax.dot_general(h_ref[rows, :], wb_ref[...], _NT, preferred_element_type=F32)
        cos = cos_ref[rows, :]
        sin = sin_ref[rows, :]
        for g in range(acc.shape[1] // LANES):
            y = acc[:, g * LANES:(g + 1) * LANES]
            o_ref[rows, g * LANES:(g + 1) * LANES] = (
                y * cos + pltpu.roll(y, LANES // 2, 1) * sin).astype(o_ref.dtype)


def _swap_row_halves(y, width):
    half = width // 2
    parts = []
    for r in range(0, y.shape[0], width):
        parts += [y[r + half:r + width], y[r:r + half]]
    return jnp.concatenate(parts, axis=0)


def _wp_rope_t_kernel(h_ref, w_ref, cos_ref, sin_ref, o_ref, wb_ref, *, head_dim):
    @pl.when(_first_token_tile())
    def _():
        wb_ref[...] = w_ref[...].astype(BF16)

    for c in range(o_ref.shape[0]):
        tok = slice(c * KEY_TILE, (c + 1) * KEY_TILE)
        r = lax.dot_general(wb_ref[...], h_ref[tok, :], _NT, preferred_element_type=F32)
        for g in range(0, r.shape[0], LANES):
            y = r[g:g + LANES]
            o_ref[c, g:g + LANES, :] = (y * cos_ref[:, tok]
                                        + _swap_row_halves(y, head_dim) * sin_ref[:, tok]).astype(o_ref.dtype)


def _wp_vt_kernel(h_ref, w_ref, o_ref, wb_ref):
    @pl.when(_first_token_tile())
    def _():
        wb_ref[...] = w_ref[...].astype(BF16)

    for c in range(o_ref.shape[0]):
        r = lax.dot_general(wb_ref[...], h_ref[c * KEY_TILE:(c + 1) * KEY_TILE, :], _NT,
                            preferred_element_type=F32)
        o_ref[c] = r.astype(o_ref.dtype)


def _wp_gate_kernel(h_ref, wa_ref, wb_ref, c0_ref, c1_ref, c2_ref, o_ref, d0_ref, d1_ref, d2_ref, wg_ref,
                    *, shift):
    d0_ref[...] = c0_ref[...].astype(BF16)
    d1_ref[...] = c1_ref[...].astype(BF16)
    d2_ref[...] = c2_ref[...].astype(BF16)

    @pl.when(_first_token_tile())
    def _():
        keep = wg_ref.shape[0] - shift
        wg_ref[:keep, :] = wa_ref[shift:, :].astype(BF16)
        wg_ref[keep:, :] = wb_ref[:shift, :].astype(BF16)

    for rows in _row_chunks(h_ref):
        acc = lax.dot_general(h_ref[rows, :], wg_ref[...], _NT, preferred_element_type=F32)
        o_ref[rows, :] = jax.nn.sigmoid(acc).astype(o_ref.dtype)


def _wp_kw_kernel(h_ref, w_ref, cos_ref, sa_ref, sb_ref, klo_ref, khi_ref, wit_ref, wb_ref):
    @pl.when(pl.program_id(0) == 0)
    def _():
        r = lax.broadcasted_iota(jnp.int32, w_ref.shape, 0)
        wb_ref[...] = jnp.where(r < IDX_DIM + IDX_HEADS, w_ref[...], 0.0).astype(BF16)

    h = h_ref[...]
    y = lax.dot_general(h, wb_ref[...], _NT, preferred_element_type=F32)
    r = _rope64(y, cos_ref[...], sa_ref[...], sb_ref[...])
    lane = lax.broadcasted_iota(jnp.int32, y.shape, 1)
    klo = jnp.where(lane < IDX_DIM, r, 0.0)
    klo_ref[...] = klo.astype(BF16)
    khi_ref[...] = pltpu.roll(klo, IDX_DIM, 1).astype(BF16)
    for c in range(wit_ref.shape[0]):
        wit_ref[c] = y[c * KEY_TILE:(c + 1) * KEY_TILE, :].T


def _in_proj(h, wt, l, t, tables128, tables64, tables_kw, out_weights):
    m, k = h.shape
    d = k
    tm = _tile(t, 1024)
    nt = t // tm
    ni = m // tm
    assert tm % KEY_TILE == 0
    tn = 1024
    o_dsa = 3 * MOBA_W
    o_qi = o_dsa + 3 * DSA_W
    o_kw = o_qi + IDX_W
    o_g = o_kw + IDX_DIM + IDX_HEADS
    sem = ("arbitrary", "arbitrary")
    h_spec = pl.BlockSpec((tm, k), lambda j, i: (i, 0))
    tab = pl.BlockSpec((tm, LANES), lambda j, i: (i % nt, 0))
    tabt = pl.BlockSpec((LANES, tm), lambda j, i: (0, i % nt))
    row = pl.BlockSpec((tm, tn), lambda j, i: (i, j))
    out_t = pl.BlockSpec((tm // KEY_TILE, tn, KEY_TILE), lambda j, i: (i, j, 0))

    def wrows(first_blocks, n_first, second_start):
        return pl.BlockSpec((None, tn, k), lambda j, i: (l, jnp.where(j < n_first, first_blocks + j,
                                                                      second_start + j - n_first), 0))

    assert MOBA_W == DSA_W and MOBA_W % tn == 0
    n_m, n_d = MOBA_W // tn, DSA_W // tn

    qt = pl.pallas_call(
        functools.partial(_wp_rope_t_kernel, head_dim=HEAD_DIM),
        grid=(n_m + n_d, ni),
        in_specs=[h_spec, wrows(0, n_m, o_dsa // tn), tabt, tabt],
        out_specs=out_t,
        out_shape=jax.ShapeDtypeStruct((m // KEY_TILE, MOBA_W + DSA_W, KEY_TILE), BF16),
        scratch_shapes=[pltpu.VMEM((tn, k), BF16)],
        compiler_params=_params(sem),
        name="proj_q",
    )(h, wt, *tables128[0])

    kk = pl.pallas_call(
        _wp_rope128_kernel,
        grid=(n_m + n_d, ni),
        in_specs=[h_spec, wrows(MOBA_W // tn, n_m, (o_dsa + DSA_W) // tn), tab, tab],
        out_specs=row,
        out_shape=jax.ShapeDtypeStruct((m, MOBA_W + DSA_W), BF16),
        scratch_shapes=[pltpu.VMEM((tn, k), BF16)],
        compiler_params=_params(sem),
        name="proj_k",
    )(h, wt, *tables128[1])

    vt = pl.pallas_call(
        _wp_vt_kernel,
        grid=(n_m + n_d, ni),
        in_specs=[h_spec, wrows(2 * MOBA_W // tn, n_m, (o_dsa + 2 * DSA_W) // tn)],
        out_specs=out_t,
        out_shape=jax.ShapeDtypeStruct((m // KEY_TILE, MOBA_W + DSA_W, KEY_TILE), BF16),
        scratch_shapes=[pltpu.VMEM((tn, k), BF16)],
        compiler_params=_params(sem),
        name="proj_v",
    )(h, wt)

    n_q = IDX_W // tn
    qit = pl.pallas_call(
        functools.partial(_wp_rope_t_kernel, head_dim=IDX_DIM),
        grid=(n_q, ni),
        in_specs=[h_spec, wrows(o_qi // tn, n_q, 0), tabt, tabt],
        out_specs=out_t,
        out_shape=jax.ShapeDtypeStruct((m // KEY_TILE, IDX_W, KEY_TILE), BF16),
        scratch_shapes=[pltpu.VMEM((tn, k), BF16)],
        compiler_params=_params(sem),
        name="proj_qi",
    )(h, wt, *tables64)

    tab1 = pl.BlockSpec((tm, LANES), lambda i: (i % nt, 0))
    row1 = pl.BlockSpec((tm, LANES), lambda i: (i, 0))
    klo, khi, wit = pl.pallas_call(
        _wp_kw_kernel,
        grid=(ni,),
        in_specs=[pl.BlockSpec((tm, k), lambda i: (i, 0)),
                  pl.BlockSpec((None, LANES, k), lambda i: (l, o_kw // LANES, 0)), tab1, tab1, tab1],
        out_specs=[row1, row1, pl.BlockSpec((tm // KEY_TILE, LANES, KEY_TILE), lambda i: (i, 0, 0))],
        out_shape=[jax.ShapeDtypeStruct((m, LANES), BF16), jax.ShapeDtypeStruct((m, LANES), BF16),
                   jax.ShapeDtypeStruct((m // KEY_TILE, LANES, KEY_TILE), F32)],
        scratch_shapes=[pltpu.VMEM((LANES, k), BF16)],
        compiler_params=_params(("arbitrary",)),
        name="proj_kw",
    )(h, wt, *tables_kw)

    tg = _tile(2 * d, tn)
    g0 = o_kw // tg
    shift = o_g - o_kw
    assert o_kw % tg == 0 and shift <= LANES and tg % LANES == 0
    ng = 2 * d // tg
    c_in, c_out, c_shape = _side_cast_specs([(w, 0) for w in out_weights], ng * ni,
                                            lambda j, i: j * ni + i, 16)
    sg, *out_weights_bf16 = pl.pallas_call(
        functools.partial(_wp_gate_kernel, shift=shift),
        grid=(ng, ni),
        in_specs=[h_spec,
                  pl.BlockSpec((None, tg, k), lambda j, i: (l, g0 + j, 0)),
                  pl.BlockSpec((None, LANES, k), lambda j, i: (l, (g0 + j + 1) * (tg // LANES), 0))] + c_in,
        out_specs=[pl.BlockSpec((tm, tg), lambda j, i: (i, j))] + c_out,
        out_shape=[jax.ShapeDtypeStruct((m, 2 * d), BF16)] + c_shape,
        scratch_shapes=[pltpu.VMEM((tg, k), BF16)],
        compiler_params=_params(sem),
        name="proj_gate",
    )(h, wt, wt, *out_weights)
    return qt, kk, vt, qit, klo, khi, wit, sg, out_weights_bf16


def _moba_kmean_kernel(k_ref, o_ref, km_ref, *, nb):
    km_ref[...] = jnp.zeros(km_ref.shape, F32)
    for j in range(nb):
        blk = k_ref[j * MOBA_BLOCK:(j + 1) * MOBA_BLOCK, :].astype(F32)
        km_ref[j:j + 1, :] = jnp.sum(blk, axis=0, keepdims=True) * (1.0 / MOBA_BLOCK)
    km = km_ref[...]
    tiled = jnp.concatenate([km] * MOBA_HEADS, axis=0)
    r = lax.shift_right_logical(lax.broadcasted_iota(jnp.int32, tiled.shape, 0), MAX_BLOCKS.bit_length() - 1)
    c = lax.shift_right_logical(lax.broadcasted_iota(jnp.int32, tiled.shape, 1), HEAD_DIM.bit_length() - 1)
    o_ref[...] = jnp.where(r == c, tiled, 0.0).astype(o_ref.dtype)


def _moba_kmean(kk3, nb):
    b, t, _ = kk3.shape
    return pl.pallas_call(
        functools.partial(_moba_kmean_kernel, nb=nb),
        grid=(b,),
        in_specs=[pl.BlockSpec((None, t, MOBA_W), lambda bi: (bi, 0, 0))],
        out_specs=pl.BlockSpec((None, LANES, MOBA_W), lambda bi: (bi, 0, 0)),
        out_shape=jax.ShapeDtypeStruct((b, LANES, MOBA_W), BF16),
        scratch_shapes=[pltpu.VMEM((MAX_BLOCKS, MOBA_W), F32)],
        compiler_params=_params(("parallel",)),
        name="moba_kmean",
    )(kk3)


def _moba_block_bias(qt_ref, kmt_ref, sbt_ref, i, n_sel):
    g = jnp.dot(kmt_ref[...], qt_ref[...], preferred_element_type=F32)
    shape = (MAX_BLOCKS, g.shape[1])
    j = lax.broadcasted_iota(jnp.int32, shape, 0)
    past = j < i
    for h in range(MOBA_HEADS):
        gm = jnp.where(past, g[h * MAX_BLOCKS:(h + 1) * MAX_BLOCKS, :], -jnp.inf)
        rank = jnp.zeros(shape, F32)
        for jo in range(MAX_BLOCKS):
            other = gm[jo:jo + 1, :]
            beats = (other > gm) | ((other == gm) & (jo < j))
            rank = rank + jnp.where(beats, 1.0, 0.0)
        bias = jnp.where(past & (rank < n_sel), 0.0, MASK_BIAS)
        for jb in range(MAX_BLOCKS):
            sbt_ref[h, jb] = jnp.broadcast_to(bias[jb:jb + 1, :], (8, shape[1]))


def _flash_step(st, m, l, acc_t, v_t):
    m_new = jnp.maximum(m, jnp.max(st, axis=0, keepdims=True))
    alpha = jnp.exp2(m - m_new)
    p = jnp.exp2(st - m_new)
    l = alpha * l + jnp.sum(p, axis=0, keepdims=True)
    acc_t = alpha * acc_t + jnp.dot(v_t, p.astype(BF16), preferred_element_type=F32)
    return m_new, l, acc_t


def _flash_step_joint(sts, m, l, acc_t, v_ts):
    m_new = m
    for st in sts:
        m_new = jnp.maximum(m_new, jnp.max(st, axis=0, keepdims=True))
    alpha = jnp.exp2(m - m_new)
    l = alpha * l
    acc_t = alpha * acc_t
    for st, v_t in zip(sts, v_ts):
        p = jnp.exp2(st - m_new)
        l = l + jnp.sum(p, axis=0, keepdims=True)
        acc_t = acc_t + jnp.dot(v_t, p.astype(BF16), preferred_element_type=F32)
    return m_new, l, acc_t


def _flash_finish(o_ref, l_ref, acc_ref, heads):
    for h in range(heads):
        hs = slice(h * HEAD_DIM, (h + 1) * HEAD_DIM)
        o_ref[:, hs] = (acc_ref[h] / l_ref[h:h + 1, :]).T.astype(o_ref.dtype)


def _moba_attn_kernel(qt_ref, k_ref, vt_ref, kmt_ref, o_ref, m_ref, l_ref, acc_ref, sbt_ref, *, n_sel):
    i = pl.program_id(1)
    own = pl.multiple_of(i * MOBA_BLOCK, MOBA_BLOCK)
    sub_shape = (SUB_TILE, MOBA_BLOCK)
    kidx = lax.broadcasted_iota(jnp.int32, sub_shape, 0)
    qidx = lax.broadcasted_iota(jnp.int32, sub_shape, 1)
    _moba_block_bias(qt_ref, kmt_ref, sbt_ref, i, n_sel)
    for h in range(MOBA_HEADS):
        hs = slice(h * HEAD_DIM, (h + 1) * HEAD_DIM)
        m = jnp.full((1, MOBA_BLOCK), NEG_BIG, F32)
        l = jnp.zeros((1, MOBA_BLOCK), F32)
        acc = jnp.zeros((HEAD_DIM, MOBA_BLOCK), F32)
        for u in range(MOBA_BLOCK // SUB_TILE):
            st = jnp.dot(k_ref[pl.ds(own + u * SUB_TILE, SUB_TILE), hs], qt_ref[hs, :],
                         preferred_element_type=F32)
            st = jnp.where(kidx + u * SUB_TILE <= qidx, st, -jnp.inf)
            m, l, acc = _flash_step(st, m, l, acc, vt_ref[i, hs, u * SUB_TILE:(u + 1) * SUB_TILE])
        m_ref[h:h + 1, :] = m
        l_ref[h:h + 1, :] = l
        acc_ref[h] = acc

    def blocks(js):
        for h in range(MOBA_HEADS):
            hs = slice(h * HEAD_DIM, (h + 1) * HEAD_DIM)
            m, l, acc = m_ref[h:h + 1, :], l_ref[h:h + 1, :], acc_ref[h]
            for j in js:
                off = pl.multiple_of(j * MOBA_BLOCK, MOBA_BLOCK)
                bias = sbt_ref[h, j][0:1, :]
                sts, vts = [], []
                for u in range(MOBA_BLOCK // SUB_TILE):
                    sts.append(jnp.dot(k_ref[pl.ds(off + u * SUB_TILE, SUB_TILE), hs], qt_ref[hs, :],
                                       preferred_element_type=F32) + bias)
                    vts.append(vt_ref[j, hs, u * SUB_TILE:(u + 1) * SUB_TILE])
                m, l, acc = _flash_step_joint(sts, m, l, acc, vts)
            m_ref[h:h + 1, :] = m
            l_ref[h:h + 1, :] = l
            acc_ref[h] = acc

    def body(t, carry):
        blocks((2 * t, 2 * t + 1))
        return carry

    lax.fori_loop(0, i // 2, body, 0)

    @pl.when(i % 2 == 1)
    def _():
        blocks((i - 1,))

    _flash_finish(o_ref, l_ref, acc_ref, MOBA_HEADS)


def _moba_attn(qt3, kk3, vt3, kmt, nb, n_sel):
    b, t, _ = kk3.shape
    one = pl.Buffered(1)
    return pl.pallas_call(
        functools.partial(_moba_attn_kernel, n_sel=n_sel),
        grid=(b, nb),
        in_specs=[pl.BlockSpec((None, MOBA_W, MOBA_BLOCK), lambda bi, i: (bi * nb + i, 0, 0)),
                  pl.BlockSpec((None, t, MOBA_W), lambda bi, i: (bi, 0, 0), pipeline_mode=one),
                  pl.BlockSpec((nb, MOBA_W, KEY_TILE), lambda bi, i: (bi, 0, 0), pipeline_mode=one),
                  pl.BlockSpec((None, LANES, MOBA_W), lambda bi, i: (bi, 0, 0), pipeline_mode=one)],
        out_specs=pl.BlockSpec((None, MOBA_BLOCK, MOBA_W), lambda bi, i: (bi, i, 0)),
        out_shape=jax.ShapeDtypeStruct((b, t, MOBA_W), BF16),
        scratch_shapes=[pltpu.VMEM((MOBA_HEADS, MOBA_BLOCK), F32),
                        pltpu.VMEM((MOBA_HEADS, MOBA_BLOCK), F32),
                        pltpu.VMEM((MOBA_HEADS, HEAD_DIM, MOBA_BLOCK), F32),
                        pltpu.VMEM((MOBA_HEADS, MAX_BLOCKS, 8, MOBA_BLOCK), F32)],
        compiler_params=_params(("parallel", "arbitrary")),
        name="moba_attn",
    )(qt3, kk3, vt3, kmt)


def _float_to_key(x):
    b = lax.bitcast_convert_type(x, jnp.int32)
    return jnp.where(b >= 0, b, b ^ 0x7FFFFFFF)


def _key_to_float(t):
    return lax.bitcast_convert_type(jnp.where(t >= 0, t, t ^ 0x7FFFFFFF), F32)


def _dsa_kernel(qit_ref, klo_ref, khi_ref, wt_ref, qt_ref, k_ref, vt_ref, o_ref,
                sc_ref, jl_ref, m_ref, l_ref, acc_ref, *, topk, idx_scale, tq, seq_bits):
    i = pl.program_id(1)
    nk = i + 1
    shape = (KEY_TILE, tq)
    krow = lax.broadcasted_iota(jnp.int32, shape, 0)
    qpos = i * tq + lax.broadcasted_iota(jnp.int32, shape, 1)

    wt = wt_ref[...]

    def score_tiles(kts):
        offs = [pl.multiple_of(kt * KEY_TILE, KEY_TILE) for kt in kts]
        keys = jnp.concatenate(
            [r[pl.ds(off, KEY_TILE), :] for off in offs for r in (klo_ref, khi_ref)], axis=0)
        accs = [jnp.zeros(shape, F32) for _ in kts]
        for p in range(IDX_HEADS // 2):
            s = jnp.dot(keys, qit_ref[p * LANES:(p + 1) * LANES, :], preferred_element_type=F32)
            for n in range(len(kts)):
                lo = s[2 * n * KEY_TILE:(2 * n + 1) * KEY_TILE]
                hi = s[(2 * n + 1) * KEY_TILE:(2 * n + 2) * KEY_TILE]
                accs[n] = accs[n] + (jnp.maximum(lo, 0.0) * wt[2 * p:2 * p + 1, :]
                                     + jnp.maximum(hi, 0.0) * wt[2 * p + 1:2 * p + 2, :])
        for kt, acc in zip(kts, accs):
            kpos = kt * KEY_TILE + krow
            sc_ref[kt] = jnp.where(kpos <= qpos, acc * idx_scale, -jnp.inf)

    def score_body(t, carry):
        score_tiles((2 * t, 2 * t + 1))
        return carry

    lax.fori_loop(0, nk // 2, score_body, 0)

    @pl.when(nk % 2 == 1)
    def _():
        score_tiles((nk - 1,))

    def count(pred):
        def body(kt, acc):
            hit = jnp.where(pred(sc_ref[kt], kt * KEY_TILE + krow), 1.0, 0.0)
            return acc + jnp.sum(hit.reshape(KEY_TILE // 32, 32, tq), axis=0)
        part = lax.fori_loop(0, nk, body, jnp.zeros((32, tq), F32))
        return jnp.sum(part, axis=0, keepdims=True)

    need_select = nk * tq > topk

    @pl.when(jnp.logical_not(need_select))
    def _():
        def body(kt, carry):
            sc_ref[kt] = jnp.where(kt * KEY_TILE + krow <= qpos, 0.0, NEG_BIG)
            return carry
        lax.fori_loop(0, nk, body, 0)

    @pl.when(need_select)
    def _():
        kf = jnp.float32(topk)

        def ext_body(kt, carry):
            s = sc_ref[kt]
            lo = jnp.min(jnp.where(s > -jnp.inf, s, jnp.inf).reshape(KEY_TILE // 32, 32, tq), axis=0)
            hi = jnp.max(s.reshape(KEY_TILE // 32, 32, tq), axis=0)
            return jnp.minimum(carry[0], lo), jnp.maximum(carry[1], hi)

        lo, hi = lax.fori_loop(0, nk, ext_body, (jnp.full((32, tq), jnp.inf, F32),
                                                 jnp.full((32, tq), -jnp.inf, F32)))
        lo = jnp.min(lo, axis=0, keepdims=True)
        hi = _key_to_float(_float_to_key(jnp.max(hi, axis=0, keepdims=True)) + 1)
        n_fin = count(lambda s, kp: s >= lo)
        done = jnp.where(n_fin <= kf, 1.0, 0.0)

        def step(state):
            lo, hi, clo, done = state
            mid = lo * 0.5 + hi * 0.5
            stuck = (mid <= lo) | (mid >= hi)
            c = count(lambda s, kp: s >= mid)
            up = (c >= kf) & (done < 0.5)
            down = (c < kf) & (done < 0.5)
            lo = jnp.where(up, mid, lo)
            clo = jnp.where(up, c, clo)
            hi = jnp.where(down, mid, hi)
            done = jnp.where(stuck | (clo == kf), 1.0, done)
            return lo, hi, clo, done

        state = lax.fori_loop(0, BISECT_FIXED, lambda _, st: step(st), (lo, hi, n_fin, done))
        state = lax.while_loop(lambda st: jnp.min(st[3]) < 0.5, step, state)
        thr = jnp.where(n_fin < kf, -jnp.inf, state[0])

        def tie_body(kt, acc):
            s = sc_ref[kt]
            gt = jnp.where(s > thr, 1.0, 0.0).reshape(KEY_TILE // 32, 32, tq)
            eq = jnp.where(s == thr, 1.0, 0.0).reshape(KEY_TILE // 32, 32, tq)
            return acc[0] + jnp.sum(gt, axis=0), acc[1] + jnp.sum(eq, axis=0)

        n_gt, n_eq = lax.fori_loop(0, nk, tie_body, (jnp.zeros((32, tq), F32),) * 2)
        need = kf - jnp.sum(n_gt, axis=0, keepdims=True)
        n_eq = jnp.sum(n_eq, axis=0, keepdims=True)
        jl_ref[...] = jnp.full(jl_ref.shape, 2 ** seq_bits, jnp.int32)

        @pl.when(jnp.max(n_eq - need) > 0.0)
        def _():
            def idx_body(b, c):
                cand = c | lax.shift_left(jnp.int32(1), seq_bits - 1 - b)
                g = count(lambda s, kp: (s == thr) & (kp < cand))
                return jnp.where(g < need, cand, c)
            c = lax.fori_loop(0, seq_bits, idx_body, jnp.zeros((1, tq), jnp.int32))
            jl_ref[...] = jnp.broadcast_to(c, jl_ref.shape)

        bound = jnp.minimum(jl_ref[0:1, :], qpos[0:1, :])

        def bias_body(kt, carry):
            s = sc_ref[kt]
            sel = (s > thr) | ((s == thr) & (kt * KEY_TILE + krow <= bound))
            sc_ref[kt] = jnp.where(sel, 0.0, NEG_BIG)
            return carry
        lax.fori_loop(0, nk, bias_body, 0)

    m_ref[...] = jnp.full(m_ref.shape, NEG_BIG, F32)
    l_ref[...] = jnp.zeros(l_ref.shape, F32)
    acc_ref[...] = jnp.zeros(acc_ref.shape, F32)

    def tiles(kts):
        for h in range(DSA_HEADS):
            hs = slice(h * HEAD_DIM, (h + 1) * HEAD_DIM)
            m, l, acc = m_ref[h:h + 1, :], l_ref[h:h + 1, :], acc_ref[h]
            for kt in kts:
                off = pl.multiple_of(kt * KEY_TILE, KEY_TILE)
                for u in range(KEY_TILE // SUB_TILE):
                    us = slice(u * SUB_TILE, (u + 1) * SUB_TILE)
                    st = jnp.dot(k_ref[pl.ds(off + u * SUB_TILE, SUB_TILE), hs], qt_ref[hs, :],
                                 preferred_element_type=F32) + sc_ref[kt, us, :]
                    m, l, acc = _flash_step(st, m, l, acc, vt_ref[kt, hs, us])
            m_ref[h:h + 1, :] = m
            l_ref[h:h + 1, :] = l
            acc_ref[h] = acc

    def attn_body(t, carry):
        tiles((2 * t, 2 * t + 1))
        return carry

    lax.fori_loop(0, nk // 2, attn_body, 0)

    @pl.when(nk % 2 == 1)
    def _():
        tiles((nk - 1,))

    _flash_finish(o_ref, l_ref, acc_ref, DSA_HEADS)


def _dsa_attn(qit3, klo3, khi3, wit, qt3, kk3, vt3, topk):
    b, t, _ = kk3.shape
    tq = KEY_TILE
    nq = t // tq
    seq_bits = max(1, (t - 1).bit_length())
    one = pl.Buffered(1)
    kern = functools.partial(
        _dsa_kernel, topk=topk,
        idx_scale=(IDX_DIM ** -0.5) * (IDX_HEADS ** -0.5), tq=tq, seq_bits=seq_bits)
    return pl.pallas_call(
        kern,
        grid=(b, nq),
        in_specs=[pl.BlockSpec((None, IDX_W, tq), lambda bi, i: (bi * nq + i, 0, 0)),
                  pl.BlockSpec((None, t, LANES), lambda bi, i: (bi, 0, 0), pipeline_mode=one),
                  pl.BlockSpec((None, t, LANES), lambda bi, i: (bi, 0, 0), pipeline_mode=one),
                  pl.BlockSpec((None, IDX_HEADS, tq), lambda bi, i: (bi * nq + i, IDX_DIM // IDX_HEADS, 0)),
                  pl.BlockSpec((None, DSA_W, tq), lambda bi, i: (bi * nq + i, 1, 0)),
                  pl.BlockSpec((None, t, DSA_W), lambda bi, i: (bi, 0, 1), pipeline_mode=one),
                  pl.BlockSpec((nq, DSA_W, KEY_TILE), lambda bi, i: (bi, 1, 0), pipeline_mode=one)],
        out_specs=pl.BlockSpec((None, tq, DSA_W), lambda bi, i: (bi, i, 0)),
        out_shape=jax.ShapeDtypeStruct((b, t, DSA_W), BF16),
        scratch_shapes=[pltpu.VMEM((nq, KEY_TILE, tq), F32),
                        pltpu.VMEM((8, tq), jnp.int32),
                        pltpu.VMEM((DSA_HEADS, tq), F32),
                        pltpu.VMEM((DSA_HEADS, tq), F32),
                        pltpu.VMEM((DSA_HEADS, HEAD_DIM, tq), F32)],
        compiler_params=_params(("parallel", "arbitrary")),
        name="dsa_attn",
    )(qit3, klo3, khi3, wit, qt3, kk3, vt3)


def _mix_out_kernel(oa_ref, ob_ref, sg_ref, wa_ref, wb_ref, wo_ref, x_ref, gt_ref, gpost_ref,
                    gpre_ref, sc_ref, sh_ref, w1f_ref, w2f_ref, x1_ref, h2_ref, w1b_ref, w2b_ref, *, d):
    w1b_ref[...] = w1f_ref[...].astype(BF16)
    w2b_ref[...] = w2f_ref[...].astype(BF16)
    ya = jnp.dot(oa_ref[...], wa_ref[...], preferred_element_type=F32)
    yb = jnp.dot(ob_ref[...], wb_ref[...], preferred_element_type=F32)
    z = sg_ref[:, :d].astype(F32) * ya + sg_ref[:, d:].astype(F32) * yb
    y = jnp.dot(z.astype(BF16), wo_ref[...], preferred_element_type=F32)
    x1 = x_ref[...] + gt_ref[...] * (_rms(y) * gpost_ref[...])
    x1_ref[...] = x1
    h2_ref[...] = ((_rms(x1) * gpre_ref[...]) * (1.0 + sc_ref[...]) + sh_ref[...]).astype(h2_ref.dtype)


def _mix_out(oa3, ob3, sg3, wa, wb, wo, x, gt1, g_post, g_pre2, sc2, sh2, w1f, w2f):
    b, t, d = x.shape
    tm = _tile(t, 256)
    nt = t // tm
    c_in, c_out, c_shape = _side_cast_specs([(w1f, 1), (w2f, 0)], b * nt, lambda bi, i: bi * nt + i, 16)
    one = pl.Buffered(1)
    row = lambda w: pl.BlockSpec((None, tm, w), lambda bi, i: (bi, i, 0))
    full = lambda a: pl.BlockSpec(a.shape, lambda bi, i: (0, 0), pipeline_mode=one)
    vec = pl.BlockSpec((None, 1, d), lambda bi, i: (bi, 0, 0))
    gain = pl.BlockSpec((1, d), lambda bi, i: (0, 0))
    return pl.pallas_call(
        functools.partial(_mix_out_kernel, d=d),
        grid=(b, t // tm),
        in_specs=[row(MOBA_W), row(DSA_W), row(2 * d), full(wa), full(wb), full(wo), row(d),
                  vec, gain, gain, vec, vec] + c_in,
        out_specs=[row(d), row(d)] + c_out,
        out_shape=[jax.ShapeDtypeStruct((b, t, d), F32), jax.ShapeDtypeStruct((b, t, d), BF16)] + c_shape,
        compiler_params=_params(("arbitrary", "arbitrary")),
        name="mix_out",
    )(oa3, ob3, sg3, wa, wb, wo, x, gt1, g_post, g_pre2, sc2, sh2, w1f, w2f)


def _ffn_kernel(h_ref, w1_ref, w2_ref, x1_ref, gt_ref, g_ref, o_ref, acc_ref):
    j = pl.program_id(2)

    @pl.when(j == 0)
    def _():
        acc_ref[...] = jnp.zeros(acc_ref.shape, F32)

    for rows in _row_chunks(h_ref, KEY_TILE):
        u = jnp.dot(h_ref[rows, :], w1_ref[...], preferred_element_type=F32)
        u = jnp.square(jnp.maximum(u, 0.0)).astype(BF16)
        acc_ref[rows, :] += jnp.dot(u, w2_ref[...], preferred_element_type=F32)

    @pl.when(j == pl.num_programs(2) - 1)
    def _():
        o_ref[...] = x1_ref[...] + gt_ref[...] * (_rms(acc_ref[...]) * g_ref[...])


def _ffn(h2, w1, w2, x1, gt2, g_post):
    b, t, d = x1.shape
    ff = w1.shape[1]
    tm = _tile(t, 512)
    tf = _tile(ff, 1024)
    row = lambda: pl.BlockSpec((None, tm, d), lambda bi, i, j: (bi, i, 0))
    return pl.pallas_call(
        _ffn_kernel,
        grid=(b, t // tm, ff // tf),
        in_specs=[row(),
                  pl.BlockSpec((d, tf), lambda bi, i, j: (0, j)),
                  pl.BlockSpec((tf, d), lambda bi, i, j: (j, 0)),
                  row(),
                  pl.BlockSpec((None, 1, d), lambda bi, i, j: (bi, 0, 0)),
                  pl.BlockSpec((1, d), lambda bi, i, j: (0, 0))],
        out_specs=row(),
        out_shape=jax.ShapeDtypeStruct((b, t, d), F32),
        scratch_shapes=[pltpu.VMEM((tm, d), F32)],
        compiler_params=_params(("parallel", "parallel", "arbitrary")),
        name="ffn",
    )(h2, w1, w2, x1, gt2, g_post)


def _rope_tables(t):
    pos = jnp.arange(t, dtype=F32)[:, None]
    lane = jnp.arange(LANES)[None, :]

    def cos_sin(half):
        inv_freq = jnp.power(ROPE_THETA, -jnp.arange(half, dtype=F32) / half)
        ang = pos * inv_freq[None, :]
        reps = LANES // half
        sign = jnp.where(lane % (2 * half) < half, -1.0, 1.0)
        return jnp.tile(jnp.cos(ang), (1, reps)), jnp.tile(jnp.sin(ang), (1, reps)) * sign

    cos_k, sin_k = cos_sin(HEAD_DIM // 2)
    rope128 = (((cos_k * QK_SCALE_LOG2).T, (sin_k * QK_SCALE_LOG2).T), (cos_k, sin_k))
    cos64, sin64 = cos_sin(IDX_DIM // 2)
    is_key = lane < IDX_DIM
    low = lane % IDX_DIM < IDX_DIM // 2
    kw = (jnp.where(is_key, cos64, 1.0), jnp.where(is_key & low, sin64, 0.0),
          jnp.where(is_key & ~low, sin64, 0.0))
    return rope128, (cos64.T, sin64.T), kw


def kernel(x, c, w_ada, b_ada, g_pre_mix, g_post_mix, w_in, w_moba_out, w_dsa_out, w_o,
           g_pre_ffn, g_post_ffn, w_ff1, w_ff2):
    b, t, d = x.shape
    m = b * t
    nb = t // MOBA_BLOCK
    assert t % MOBA_BLOCK == 0 and nb <= MAX_BLOCKS and d % LANES == 0 and b <= 16
    n_sel = max(1, min(MOBA_TOPK, nb - 1))
    topk = min(DSA_TOPK_MAX, t // 4)
    rope128, rope64, rope_kw = _rope_tables(t)
    c_pad = jnp.zeros((16, d), F32).at[:b].set(c)

    for l in range(w_ada.shape[0]):
        mod = _ada(c_pad, w_ada[l], b_ada[l][None, :])[:b]
        sh1, sc1, gt1, sh2, sc2, gt2 = [v[:, None, :] for v in jnp.split(mod, 6, axis=-1)]

        h = _norm_mod(x, g_pre_mix[l][None, :], sc1, sh1).reshape(m, d)
        qt3, kk, vt3, qit3, klo, khi, wit, sg, (wa, wb, wo) = _in_proj(
            h, jnp.swapaxes(w_in, 1, 2), l, t, rope128, rope64, rope_kw,
            (w_moba_out[l], w_dsa_out[l], w_o[l]))

        kk3 = kk.reshape(b, t, -1)
        kmt = _moba_kmean(kk3, nb)
        oa = _moba_attn(qt3, kk3, vt3, kmt, nb, n_sel)
        ob = _dsa_attn(qit3, klo.reshape(b, t, -1), khi.reshape(b, t, -1), wit, qt3, kk3, vt3, topk)

        x, h2, w1, w2 = _mix_out(oa, ob, sg.reshape(b, t, -1), wa, wb, wo, x, gt1,
                                 g_post_mix[l][None, :], g_pre_ffn[l][None, :], sc2, sh2,
                                 w_ff1[l], w_ff2[l])
        x = _ffn(h2, w1, w2, x, gt2, g_post_ffn[l][None, :])
    return x
```

```python
import functools

import jax
import jax.numpy as jnp
from jax import lax
from jax.experimental import pallas as pl
from jax.experimental.pallas import tpu as pltpu

HEAD_DIM = 128
MOBA_HEADS = 8
MOBA_BLOCK = 256
MOBA_TOPK = 3
DSA_HEADS = 8
IDX_HEADS = 16
IDX_DIM = 64
DSA_TOPK_MAX = 256
ROPE_THETA = 10000.0
RMS_EPS = 1e-6

MOBA_W = MOBA_HEADS * HEAD_DIM
DSA_W = DSA_HEADS * HEAD_DIM
IDX_W = IDX_HEADS * IDX_DIM
LANES = 128
MAX_BLOCKS = LANES // MOBA_HEADS
KEY_TILE = 256
SUB_TILE = 128
ROW_CHUNK = 256
LOG2_E = 1.4426950408889634
QK_SCALE_LOG2 = HEAD_DIM ** -0.5 * LOG2_E
MASK_BIAS = -30000.0
NEG_BIG = -1e30
BISECT_FIXED = 18
VMEM_LIMIT = 56 * 1024 * 1024

F32 = jnp.float32
BF16 = jnp.bfloat16
_NT = (((1,), (1,)), ((), ()))


def _params(sem):
    return pltpu.CompilerParams(dimension_semantics=sem, vmem_limit_bytes=VMEM_LIMIT)


def _tile(n, pref):
    if n <= pref:
        return n
    t = pref - pref % LANES
    while t >= LANES:
        if n % t == 0:
            return t
        t -= LANES
    return n


def _side_cast_specs(arrays, n_steps, step_of, row_blocks):
    ins, outs, shapes = [], [], []
    for w, axis in arrays:
        n = w.shape[axis]
        unit = row_blocks if axis == 0 else LANES
        target = max(unit, n // n_steps)
        slab = next(c for c in range(target - target % unit, 0, -unit) if n % c == 0)
        last = n // slab - 1
        block = (slab, w.shape[1]) if axis == 0 else (w.shape[0], slab)

        def imap(*g, axis=axis, last=last):
            idx = jnp.minimum(step_of(*g), last)
            return (idx, 0) if axis == 0 else (0, idx)

        ins.append(pl.BlockSpec(block, imap))
        outs.append(pl.BlockSpec(block, imap))
        shapes.append(jax.ShapeDtypeStruct(w.shape, BF16))
    return ins, outs, shapes


def _rms(x):
    return x * lax.rsqrt(jnp.mean(x * x, axis=-1, keepdims=True) + RMS_EPS)


def _ada_kernel(c_ref, w_ref, b_ref, o_ref):
    c = c_ref[...]
    cs = (c * jax.nn.sigmoid(c)).astype(BF16)
    o_ref[...] = jnp.dot(cs, w_ref[...].astype(BF16), preferred_element_type=F32) + b_ref[...]


def _ada(c_pad, w, b):
    rows, d = c_pad.shape
    n = w.shape[1]
    tn = _tile(n, 1024)
    return pl.pallas_call(
        _ada_kernel,
        grid=(n // tn,),
        in_specs=[pl.BlockSpec((rows, d), lambda j: (0, 0)),
                  pl.BlockSpec((d, tn), lambda j: (0, j)),
                  pl.BlockSpec((1, tn), lambda j: (0, j))],
        out_specs=pl.BlockSpec((rows, tn), lambda j: (0, j)),
        out_shape=jax.ShapeDtypeStruct((rows, n), F32),
        compiler_params=_params(("parallel",)),
        name="ada_mod",
    )(c_pad, w, b)


def _norm_mod_kernel(x_ref, g_ref, sc_ref, sh_ref, o_ref):
    y = _rms(x_ref[...])
    o_ref[...] = ((y * g_ref[...]) * (1.0 + sc_ref[...]) + sh_ref[...]).astype(o_ref.dtype)


def _norm_mod(x, g, sc, sh):
    b, t, d = x.shape
    tt = _tile(t, 1024)
    vec = pl.BlockSpec((None, 1, d), lambda bi, ti: (bi, 0, 0))
    return pl.pallas_call(
        _norm_mod_kernel,
        grid=(b, t // tt),
        in_specs=[pl.BlockSpec((None, tt, d), lambda bi, ti: (bi, ti, 0)),
                  pl.BlockSpec((1, d), lambda bi, ti: (0, 0)), vec, vec],
        out_specs=pl.BlockSpec((None, tt, d), lambda bi, ti: (bi, ti, 0)),
        out_shape=jax.ShapeDtypeStruct((b, t, d), BF16),
        compiler_params=_params(("parallel", "parallel")),
        name="norm_mod",
    )(x, g, sc, sh)


def _first_token_tile():
    return pl.program_id(1) == 0


def _rope64(y, cos, sin_a, sin_b):
    return y * cos + pltpu.roll(y, LANES - 32, 1) * sin_a + pltpu.roll(y, 32, 1) * sin_b


def _row_chunks(ref, step=ROW_CHUNK):
    n = ref.shape[0]
    step = min(n, step)
    return [slice(r, r + step) for r in range(0, n, step)]


def _wp_rope128_kernel(h_ref, w_ref, cos_ref, sin_ref, o_ref, wb_ref):
    @pl.when(_first_token_tile())
    def _():
        wb_ref[...] = w_ref[...].astype(BF16)

    for rows in _row_chunks(h_ref):
        acc = lax.dot_general(h_ref[rows, :], wb_ref[...], _NT, preferred_element_type=F32)
        cos = cos_ref[rows, :]
        sin = sin_ref[rows, :]
        for g in range(acc.shape[1] // LANES):
            y = acc[:, g * LANES:(g + 1) * LANES]
            o_ref[rows, g * LANES:(g + 1) * LANES] = (
                y * cos + pltpu.roll(y, LANES // 2, 1) * sin).astype(o_ref.dtype)


def _swap_row_halves(y, width):
    half = width // 2
    parts = []
    for r in range(0, y.shape[0], width):
        parts += [y[r + half:r + width], y[r:r + half]]
    return jnp.concatenate(parts, axis=0)


def _wp_rope_t_kernel(h_ref, w_ref, cos_ref, sin_ref, o_ref, wb_ref, *, head_dim):
    @pl.when(_first_token_tile())
    def _():
        wb_ref[...] = w_ref[...].astype(BF16)

    for c in range(o_ref.shape[0]):
        tok = slice(c * KEY_TILE, (c + 1) * KEY_TILE)
        r = lax.dot_general(wb_ref[...], h_ref[tok, :], _NT, preferred_element_type=F32)
        for g in range(0, r.shape[0], LANES):
            y = r[g:g + LANES]
            o_ref[c, g:g + LANES, :] = (y * cos_ref[:, tok]
                                        + _swap_row_halves(y, head_dim) * sin_ref[:, tok]).astype(o_ref.dtype)


def _wp_vt_kernel(h_ref, w_ref, o_ref, wb_ref):
    @pl.when(_first_token_tile())
    def _():
        wb_ref[...] = w_ref[...].astype(BF16)

    for c in range(o_ref.shape[0]):
        r = lax.dot_general(wb_ref[...], h_ref[c * KEY_TILE:(c + 1) * KEY_TILE, :], _NT,
                            preferred_element_type=F32)
        o_ref[c] = r.astype(o_ref.dtype)


def _wp_gate_kernel(h_ref, wa_ref, wb_ref, c0_ref, c1_ref, c2_ref, o_ref, d0_ref, d1_ref, d2_ref, wg_ref,
                    *, shift):
    d0_ref[...] = c0_ref[...].astype(BF16)
    d1_ref[...] = c1_ref[...].astype(BF16)
    d2_ref[...] = c2_ref[...].astype(BF16)

    @pl.when(_first_token_tile())
    def _():
        keep = wg_ref.shape[0] - shift
        wg_ref[:keep, :] = wa_ref[shift:, :].astype(BF16)
        wg_ref[keep:, :] = wb_ref[:shift, :].astype(BF16)

    for rows in _row_chunks(h_ref):
        acc = lax.dot_general(h_ref[rows, :], wg_ref[...], _NT, preferred_element_type=F32)
        o_ref[rows, :] = jax.nn.sigmoid(acc).astype(o_ref.dtype)


def _wp_kw_kernel(h_ref, w_ref, cos_ref, sa_ref, sb_ref, klo_ref, khi_ref, wit_ref, wb_ref):
    @pl.when(pl.program_id(0) == 0)
    def _():
        r = lax.broadcasted_iota(jnp.int32, w_ref.shape, 0)
        wb_ref[...] = jnp.where(r < IDX_DIM + IDX_HEADS, w_ref[...], 0.0).astype(BF16)

    h = h_ref[...]
    y = lax.dot_general(h, wb_ref[...], _NT, preferred_element_type=F32)
    r = _rope64(y, cos_ref[...], sa_ref[...], sb_ref[...])
    lane = lax.broadcasted_iota(jnp.int32, y.shape, 1)
    klo = jnp.where(lane < IDX_DIM, r, 0.0)
    klo_ref[...] = klo.astype(BF16)
    khi_ref[...] = pltpu.roll(klo, IDX_DIM, 1).astype(BF16)
    for c in range(wit_ref.shape[0]):
        wit_ref[c] = y[c * KEY_TILE:(c + 1) * KEY_TILE, :].T


def _in_proj(h, wt, l, t, tables128, tables64, tables_kw, out_weights):
    m, k = h.shape
    d = k
    tm = _tile(t, 1024)
    nt = t // tm
    ni = m // tm
    assert tm % KEY_TILE == 0
    tn = 1024
    o_dsa = 3 * MOBA_W
    o_qi = o_dsa + 3 * DSA_W
    o_kw = o_qi + IDX_W
    o_g = o_kw + IDX_DIM + IDX_HEADS
    sem = ("arbitrary", "arbitrary")
    h_spec = pl.BlockSpec((tm, k), lambda j, i: (i, 0))
    tab = pl.BlockSpec((tm, LANES), lambda j, i: (i % nt, 0))
    tabt = pl.BlockSpec((LANES, tm), lambda j, i: (0, i % nt))
    row = pl.BlockSpec((tm, tn), lambda j, i: (i, j))
    out_t = pl.BlockSpec((tm // KEY_TILE, tn, KEY_TILE), lambda j, i: (i, j, 0))

    def wrows(first_blocks, n_first, second_start):
        return pl.BlockSpec((None, tn, k), lambda j, i: (l, jnp.where(j < n_first, first_blocks + j,
                                                                      second_start + j - n_first), 0))

    assert MOBA_W == DSA_W and MOBA_W % tn == 0
    n_m, n_d = MOBA_W // tn, DSA_W // tn

    qt = pl.pallas_call(
        functools.partial(_wp_rope_t_kernel, head_dim=HEAD_DIM),
        grid=(n_m + n_d, ni),
        in_specs=[h_spec, wrows(0, n_m, o_dsa // tn), tabt, tabt],
        out_specs=out_t,
        out_shape=jax.ShapeDtypeStruct((m // KEY_TILE, MOBA_W + DSA_W, KEY_TILE), BF16),
        scratch_shapes=[pltpu.VMEM((tn, k), BF16)],
        compiler_params=_params(sem),
        name="proj_q",
    )(h, wt, *tables128[0])

    kk = pl.pallas_call(
        _wp_rope128_kernel,
        grid=(n_m + n_d, ni),
        in_specs=[h_spec, wrows(MOBA_W // tn, n_m, (o_dsa + DSA_W) // tn), tab, tab],
        out_specs=row,
        out_shape=jax.ShapeDtypeStruct((m, MOBA_W + DSA_W), BF16),
        scratch_shapes=[pltpu.VMEM((tn, k), BF16)],
        compiler_params=_params(sem),
        name="proj_k",
    )(h, wt, *tables128[1])

    vt = pl.pallas_call(
        _wp_vt_kernel,
        grid=(n_m + n_d, ni),
        in_specs=[h_spec, wrows(2 * MOBA_W // tn, n_m, (o_dsa + 2 * DSA_W) // tn)],
        out_specs=out_t,
        out_shape=jax.ShapeDtypeStruct((m // KEY_TILE, MOBA_W + DSA_W, KEY_TILE), BF16),
        scratch_shapes=[pltpu.VMEM((tn, k), BF16)],
        compiler_params=_params(sem),
        name="proj_v",
    )(h, wt)

    n_q = IDX_W // tn
    qit = pl.pallas_call(
        functools.partial(_wp_rope_t_kernel, head_dim=IDX_DIM),
        grid=(n_q, ni),
        in_specs=[h_spec, wrows(o_qi // tn, n_q, 0), tabt, tabt],
        out_specs=out_t,
        out_shape=jax.ShapeDtypeStruct((m // KEY_TILE, IDX_W, KEY_TILE), BF16),
        scratch_shapes=[pltpu.VMEM((tn, k), BF16)],
        compiler_params=_params(sem),
        name="proj_qi",
    )(h, wt, *tables64)

    tab1 = pl.BlockSpec((tm, LANES), lambda i: (i % nt, 0))
    row1 = pl.BlockSpec((tm, LANES), lambda i: (i, 0))
    klo, khi, wit = pl.pallas_call(
        _wp_kw_kernel,
        grid=(ni,),
        in_specs=[pl.BlockSpec((tm, k), lambda i: (i, 0)),
                  pl.BlockSpec((None, LANES, k), lambda i: (l, o_kw // LANES, 0)), tab1, tab1, tab1],
        out_specs=[row1, row1, pl.BlockSpec((tm // KEY_TILE, LANES, KEY_TILE), lambda i: (i, 0, 0))],
        out_shape=[jax.ShapeDtypeStruct((m, LANES), BF16), jax.ShapeDtypeStruct((m, LANES), BF16),
                   jax.ShapeDtypeStruct((m // KEY_TILE, LANES, KEY_TILE), F32)],
        scratch_shapes=[pltpu.VMEM((LANES, k), BF16)],
        compiler_params=_params(("arbitrary",)),
        name="proj_kw",
    )(h, wt, *tables_kw)

    tg = _tile(2 * d, tn)
    g0 = o_kw // tg
    shift = o_g - o_kw
    assert o_kw % tg == 0 and shift <= LANES and tg % LANES == 0
    ng = 2 * d // tg
    c_in, c_out, c_shape = _side_cast_specs([(w, 0) for w in out_weights], ng * ni,
                                            lambda j, i: j * ni + i, 16)
    sg, *out_weights_bf16 = pl.pallas_call(
        functools.partial(_wp_gate_kernel, shift=shift),
        grid=(ng, ni),
        in_specs=[h_spec,
                  pl.BlockSpec((None, tg, k), lambda j, i: (l, g0 + j, 0)),
                  pl.BlockSpec((None, LANES, k), lambda j, i: (l, (g0 + j + 1) * (tg // LANES), 0))] + c_in,
        out_specs=[pl.BlockSpec((tm, tg), lambda j, i: (i, j))] + c_out,
        out_shape=[jax.ShapeDtypeStruct((m, 2 * d), BF16)] + c_shape,
        scratch_shapes=[pltpu.VMEM((tg, k), BF16)],
        compiler_params=_params(sem),
        name="proj_gate",
    )(h, wt, wt, *out_weights)
    return qt, kk, vt, qit, klo, khi, wit, sg, out_weights_bf16


def _moba_kmean_kernel(k_ref, o_ref, km_ref, *, nb):
    km_ref[...] = jnp.zeros(km_ref.shape, F32)
    for j in range(nb):
        blk = k_ref[j * MOBA_BLOCK:(j + 1) * MOBA_BLOCK, :].astype(F32)
        km_ref[j:j + 1, :] = jnp.sum(blk, axis=0, keepdims=True) * (1.0 / MOBA_BLOCK)
    km = km_ref[...]
    tiled = jnp.concatenate([km] * MOBA_HEADS, axis=0)
    r = lax.shift_right_logical(lax.broadcasted_iota(jnp.int32, tiled.shape, 0), MAX_BLOCKS.bit_length() - 1)
    c = lax.shift_right_logical(lax.broadcasted_iota(jnp.int32, tiled.shape, 1), HEAD_DIM.bit_length() - 1)
    o_ref[...] = jnp.where(r == c, tiled, 0.0).astype(o_ref.dtype)


def _moba_kmean(kk3, nb):
    b, t, _ = kk3.shape
    return pl.pallas_call(
        functools.partial(_moba_kmean_kernel, nb=nb),
        grid=(b,),
        in_specs=[pl.BlockSpec((None, t, MOBA_W), lambda bi: (bi, 0, 0))],
        out_specs=pl.BlockSpec((None, LANES, MOBA_W), lambda bi: (bi, 0, 0)),
        out_shape=jax.ShapeDtypeStruct((b, LANES, MOBA_W), BF16),
        scratch_shapes=[pltpu.VMEM((MAX_BLOCKS, MOBA_W), F32)],
        compiler_params=_params(("parallel",)),
        name="moba_kmean",
    )(kk3)


def _moba_block_bias(qt_ref, kmt_ref, sbt_ref, i, n_sel):
    g = jnp.dot(kmt_ref[...], qt_ref[...], preferred_element_type=F32)
    shape = (MAX_BLOCKS, g.shape[1])
    j = lax.broadcasted_iota(jnp.int32, shape, 0)
    past = j < i
    for h in range(MOBA_HEADS):
        gm = jnp.where(past, g[h * MAX_BLOCKS:(h + 1) * MAX_BLOCKS, :], -jnp.inf)
        rank = jnp.zeros(shape, F32)
        for jo in range(MAX_BLOCKS):
            other = gm[jo:jo + 1, :]
            beats = (other > gm) | ((other == gm) & (jo < j))
            rank = rank + jnp.where(beats, 1.0, 0.0)
        bias = jnp.where(past & (rank < n_sel), 0.0, MASK_BIAS)
        for jb in range(MAX_BLOCKS):
            sbt_ref[h, jb] = jnp.broadcast_to(bias[jb:jb + 1, :], (8, shape[1]))


def _flash_step(st, m, l, acc_t, v_t):
    m_new = jnp.maximum(m, jnp.max(st, axis=0, keepdims=True))
    alpha = jnp.exp2(m - m_new)
    p = jnp.exp2(st - m_new)
    l = alpha * l + jnp.sum(p, axis=0, keepdims=True)
    acc_t = alpha * acc_t + jnp.dot(v_t, p.astype(BF16), preferred_element_type=F32)
    return m_new, l, acc_t


def _flash_step_joint(sts, m, l, acc_t, v_ts):
    m_new = m
    for st in sts:
        m_new = jnp.maximum(m_new, jnp.max(st, axis=0, keepdims=True))
    alpha = jnp.exp2(m - m_new)
    l = alpha * l
    acc_t = alpha * acc_t
    for st, v_t in zip(sts, v_ts):
        p = jnp.exp2(st - m_new)
        l = l + jnp.sum(p, axis=0, keepdims=True)
        acc_t = acc_t + jnp.dot(v_t, p.astype(BF16), preferred_element_type=F32)
    return m_new, l, acc_t


def _flash_finish(o_ref, l_ref, acc_ref, heads):
    for h in range(heads):
        hs = slice(h * HEAD_DIM, (h + 1) * HEAD_DIM)
        o_ref[:, hs] = (acc_ref[h] / l_ref[h:h + 1, :]).T.astype(o_ref.dtype)


def _moba_attn_kernel(qt_ref, k_ref, vt_ref, kmt_ref, o_ref, m_ref, l_ref, acc_ref, sbt_ref, *, n_sel):
    i = pl.program_id(1)
    own = pl.multiple_of(i * MOBA_BLOCK, MOBA_BLOCK)
    sub_shape = (SUB_TILE, MOBA_BLOCK)
    kidx = lax.broadcasted_iota(jnp.int32, sub_shape, 0)
    qidx = lax.broadcasted_iota(jnp.int32, sub_shape, 1)
    _moba_block_bias(qt_ref, kmt_ref, sbt_ref, i, n_sel)
    for h in range(MOBA_HEADS):
        hs = slice(h * HEAD_DIM, (h + 1) * HEAD_DIM)
        m = jnp.full((1, MOBA_BLOCK), NEG_BIG, F32)
        l = jnp.zeros((1, MOBA_BLOCK), F32)
        acc = jnp.zeros((HEAD_DIM, MOBA_BLOCK), F32)
        for u in range(MOBA_BLOCK // SUB_TILE):
            st = jnp.dot(k_ref[pl.ds(own + u * SUB_TILE, SUB_TILE), hs], qt_ref[hs, :],
                         preferred_element_type=F32)
            st = jnp.where(kidx + u * SUB_TILE <= qidx, st, -jnp.inf)
            m, l, acc = _flash_step(st, m, l, acc, vt_ref[i, hs, u * SUB_TILE:(u + 1) * SUB_TILE])
        m_ref[h:h + 1, :] = m
        l_ref[h:h + 1, :] = l
        acc_ref[h] = acc

    def blocks(js):
        for h in range(MOBA_HEADS):
            hs = slice(h * HEAD_DIM, (h + 1) * HEAD_DIM)
            m, l, acc = m_ref[h:h + 1, :], l_ref[h:h + 1, :], acc_ref[h]
            for j in js:
                off = pl.multiple_of(j * MOBA_BLOCK, MOBA_BLOCK)
                bias = sbt_ref[h, j][0:1, :]
                sts, vts = [], []
                for u in range(MOBA_BLOCK // SUB_TILE):
                    sts.append(jnp.dot(k_ref[pl.ds(off + u * SUB_TILE, SUB_TILE), hs], qt_ref[hs, :],
                                       preferred_element_type=F32) + bias)
                    vts.append(vt_ref[j, hs, u * SUB_TILE:(u + 1) * SUB_TILE])
                m, l, acc = _flash_step_joint(sts, m, l, acc, vts)
            m_ref[h:h + 1, :] = m
            l_ref[h:h + 1, :] = l
            acc_ref[h] = acc

    def body(t, carry):
        blocks((2 * t, 2 * t + 1))
        return carry

    lax.fori_loop(0, i // 2, body, 0)

    @pl.when(i % 2 == 1)
    def _():
        blocks((i - 1,))

    _flash_finish(o_ref, l_ref, acc_ref, MOBA_HEADS)


def _moba_attn(qt3, kk3, vt3, kmt, nb, n_sel):
    b, t, _ = kk3.shape
    return pl.pallas_call(
        functools.partial(_moba_attn_kernel, n_sel=n_sel),
        grid=(b, nb),
        in_specs=[pl.BlockSpec((None, MOBA_W, MOBA_BLOCK), lambda bi, i: (bi * nb + i, 0, 0)),
                  pl.BlockSpec((None, t, MOBA_W), lambda bi, i: (bi, 0, 0)),
                  pl.BlockSpec((nb, MOBA_W, KEY_TILE), lambda bi, i: (bi, 0, 0)),
                  pl.BlockSpec((None, LANES, MOBA_W), lambda bi, i: (bi, 0, 0))],
        out_specs=pl.BlockSpec((None, MOBA_BLOCK, MOBA_W), lambda bi, i: (bi, i, 0)),
        out_shape=jax.ShapeDtypeStruct((b, t, MOBA_W), BF16),
        scratch_shapes=[pltpu.VMEM((MOBA_HEADS, MOBA_BLOCK), F32),
                        pltpu.VMEM((MOBA_HEADS, MOBA_BLOCK), F32),
                        pltpu.VMEM((MOBA_HEADS, HEAD_DIM, MOBA_BLOCK), F32),
                        pltpu.VMEM((MOBA_HEADS, MAX_BLOCKS, 8, MOBA_BLOCK), F32)],
        compiler_params=_params(("parallel", "arbitrary")),
        name="moba_attn",
    )(qt3, kk3, vt3, kmt)


def _float_to_key(x):
    b = lax.bitcast_convert_type(x, jnp.int32)
    return jnp.where(b >= 0, b, b ^ 0x7FFFFFFF)


def _key_to_float(t):
    return lax.bitcast_convert_type(jnp.where(t >= 0, t, t ^ 0x7FFFFFFF), F32)


def _dsa_kernel(qit_ref, klo_ref, khi_ref, wt_ref, qt_ref, k_ref, vt_ref, o_ref,
                sc_ref, jl_ref, m_ref, l_ref, acc_ref, *, topk, idx_scale, tq, seq_bits):
    i = pl.program_id(1)
    nk = i + 1
    shape = (KEY_TILE, tq)
    krow = lax.broadcasted_iota(jnp.int32, shape, 0)
    qpos = i * tq + lax.broadcasted_iota(jnp.int32, shape, 1)

    wt = wt_ref[...]

    def score_tiles(kts):
        offs = [pl.multiple_of(kt * KEY_TILE, KEY_TILE) for kt in kts]
        keys = jnp.concatenate(
            [r[pl.ds(off, KEY_TILE), :] for off in offs for r in (klo_ref, khi_ref)], axis=0)
        accs = [jnp.zeros(shape, F32) for _ in kts]
        for p in range(IDX_HEADS // 2):
            s = jnp.dot(keys, qit_ref[p * LANES:(p + 1) * LANES, :], preferred_element_type=F32)
            for n in range(len(kts)):
                lo = s[2 * n * KEY_TILE:(2 * n + 1) * KEY_TILE]
                hi = s[(2 * n + 1) * KEY_TILE:(2 * n + 2) * KEY_TILE]
                accs[n] = accs[n] + (jnp.maximum(lo, 0.0) * wt[2 * p:2 * p + 1, :]
                                     + jnp.maximum(hi, 0.0) * wt[2 * p + 1:2 * p + 2, :])
        for kt, acc in zip(kts, accs):
            kpos = kt * KEY_TILE + krow
            sc_ref[kt] = jnp.where(kpos <= qpos, acc * idx_scale, -jnp.inf)

    def score_body(t, carry):
        score_tiles((2 * t, 2 * t + 1))
        return carry

    lax.fori_loop(0, nk // 2, score_body, 0)

    @pl.when(nk % 2 == 1)
    def _():
        score_tiles((nk - 1,))

    def count(pred):
        def body(kt, acc):
            hit = jnp.where(pred(sc_ref[kt], kt * KEY_TILE + krow), 1.0, 0.0)
            return acc + jnp.sum(hit.reshape(KEY_TILE // 32, 32, tq), axis=0)
        part = lax.fori_loop(0, nk, body, jnp.zeros((32, tq), F32))
        return jnp.sum(part, axis=0, keepdims=True)

    need_select = nk * tq > topk

    @pl.when(jnp.logical_not(need_select))
    def _():
        def body(kt, carry):
            sc_ref[kt] = jnp.where(kt * KEY_TILE + krow <= qpos, 0.0, NEG_BIG)
            return carry
        lax.fori_loop(0, nk, body, 0)

    @pl.when(need_select)
    def _():
        kf = jnp.float32(topk)

        def ext_body(kt, carry):
            s = sc_ref[kt]
            lo = jnp.min(jnp.where(s > -jnp.inf, s, jnp.inf).reshape(KEY_TILE // 32, 32, tq), axis=0)
            hi = jnp.max(s.reshape(KEY_TILE // 32, 32, tq), axis=0)
            return jnp.minimum(carry[0], lo), jnp.maximum(carry[1], hi)

        lo, hi = lax.fori_loop(0, nk, ext_body, (jnp.full((32, tq), jnp.inf, F32),
                                                 jnp.full((32, tq), -jnp.inf, F32)))
        lo = jnp.min(lo, axis=0, keepdims=True)
        hi = _key_to_float(_float_to_key(jnp.max(hi, axis=0, keepdims=True)) + 1)
        n_fin = count(lambda s, kp: s >= lo)
        done = jnp.where(n_fin <= kf, 1.0, 0.0)

        def step(state):
            lo, hi, clo, done = state
            mid = lo * 0.5 + hi * 0.5
            stuck = (mid <= lo) | (mid >= hi)
            c = count(lambda s, kp: s >= mid)
            up = (c >= kf) & (done < 0.5)
            down = (c < kf) & (done < 0.5)
            lo = jnp.where(up, mid, lo)
            clo = jnp.where(up, c, clo)
            hi = jnp.where(down, mid, hi)
            done = jnp.where(stuck | (clo == kf), 1.0, done)
            return lo, hi, clo, done

        state = lax.fori_loop(0, BISECT_FIXED, lambda _, st: step(st), (lo, hi, n_fin, done))
        state = lax.while_loop(lambda st: jnp.min(st[3]) < 0.5, step, state)
        thr = jnp.where(n_fin < kf, -jnp.inf, state[0])

        def tie_body(kt, acc):
            s = sc_ref[kt]
            gt = jnp.where(s > thr, 1.0, 0.0).reshape(KEY_TILE // 32, 32, tq)
            eq = jnp.where(s == thr, 1.0, 0.0).reshape(KEY_TILE // 32, 32, tq)
            return acc[0] + jnp.sum(gt, axis=0), acc[1] + jnp.sum(eq, axis=0)

        n_gt, n_eq = lax.fori_loop(0, nk, tie_body, (jnp.zeros((32, tq), F32),) * 2)
        need = kf - jnp.sum(n_gt, axis=0, keepdims=True)
        n_eq = jnp.sum(n_eq, axis=0, keepdims=True)
        jl_ref[...] = jnp.full(jl_ref.shape, 2 ** seq_bits, jnp.int32)

        @pl.when(jnp.max(n_eq - need) > 0.0)
        def _():
            def idx_body(b, c):
                cand = c | lax.shift_left(jnp.int32(1), seq_bits - 1 - b)
                g = count(lambda s, kp: (s == thr) & (kp < cand))
                return jnp.where(g < need, cand, c)
            c = lax.fori_loop(0, seq_bits, idx_body, jnp.zeros((1, tq), jnp.int32))
            jl_ref[...] = jnp.broadcast_to(c, jl_ref.shape)

        bound = jnp.minimum(jl_ref[0:1, :], qpos[0:1, :])

        def bias_body(kt, carry):
            s = sc_ref[kt]
            sel = (s > thr) | ((s == thr) & (kt * KEY_TILE + krow <= bound))
            sc_ref[kt] = jnp.where(sel, 0.0, NEG_BIG)
            return carry
        lax.fori_loop(0, nk, bias_body, 0)

    m_ref[...] = jnp.full(m_ref.shape, NEG_BIG, F32)
    l_ref[...] = jnp.zeros(l_ref.shape, F32)
    acc_ref[...] = jnp.zeros(acc_ref.shape, F32)

    def tiles(kts):
        for h in range(DSA_HEADS):
            hs = slice(h * HEAD_DIM, (h + 1) * HEAD_DIM)
            m, l, acc = m_ref[h:h + 1, :], l_ref[h:h + 1, :], acc_ref[h]
            for kt in kts:
                off = pl.multiple_of(kt * KEY_TILE, KEY_TILE)
                for u in range(KEY_TILE // SUB_TILE):
                    us = slice(u * SUB_TILE, (u + 1) * SUB_TILE)
                    st = jnp.dot(k_ref[pl.ds(off + u * SUB_TILE, SUB_TILE), hs], qt_ref[hs, :],
                                 preferred_element_type=F32) + sc_ref[kt, us, :]
                    m, l, acc = _flash_step(st, m, l, acc, vt_ref[kt, hs, us])
            m_ref[h:h + 1, :] = m
            l_ref[h:h + 1, :] = l
            acc_ref[h] = acc

    def attn_body(t, carry):
        tiles((2 * t, 2 * t + 1))
        return carry

    lax.fori_loop(0, nk // 2, attn_body, 0)

    @pl.when(nk % 2 == 1)
    def _():
        tiles((nk - 1,))

    _flash_finish(o_ref, l_ref, acc_ref, DSA_HEADS)


def _dsa_attn(qit3, klo3, khi3, wit, qt3, kk3, vt3, topk):
    b, t, _ = kk3.shape
    tq = KEY_TILE
    nq = t // tq
    seq_bits = max(1, (t - 1).bit_length())
    kern = functools.partial(
        _dsa_kernel, topk=topk,
        idx_scale=(IDX_DIM ** -0.5) * (IDX_HEADS ** -0.5), tq=tq, seq_bits=seq_bits)
    return pl.pallas_call(
        kern,
        grid=(b, nq),
        in_specs=[pl.BlockSpec((None, IDX_W, tq), lambda bi, i: (bi * nq + i, 0, 0)),
                  pl.BlockSpec((None, t, LANES), lambda bi, i: (bi, 0, 0)),
                  pl.BlockSpec((None, t, LANES), lambda bi, i: (bi, 0, 0)),
                  pl.BlockSpec((None, IDX_HEADS, tq), lambda bi, i: (bi * nq + i, IDX_DIM // IDX_HEADS, 0)),
                  pl.BlockSpec((None, DSA_W, tq), lambda bi, i: (bi * nq + i, 1, 0)),
                  pl.BlockSpec((None, t, DSA_W), lambda bi, i: (bi, 0, 1)),
                  pl.BlockSpec((nq, DSA_W, KEY_TILE), lambda bi, i: (bi, 1, 0))],
        out_specs=pl.BlockSpec((None, tq, DSA_W), lambda bi, i: (bi, i, 0)),
        out_shape=jax.ShapeDtypeStruct((b, t, DSA_W), BF16),
        scratch_shapes=[pltpu.VMEM((nq, KEY_TILE, tq), F32),
                        pltpu.VMEM((8, tq), jnp.int32),
                        pltpu.VMEM((DSA_HEADS, tq), F32),
                        pltpu.VMEM((DSA_HEADS, tq), F32),
                        pltpu.VMEM((DSA_HEADS, HEAD_DIM, tq), F32)],
        compiler_params=_params(("parallel", "arbitrary")),
        name="dsa_attn",
    )(qit3, klo3, khi3, wit, qt3, kk3, vt3)


def _mix_out_kernel(oa_ref, ob_ref, sg_ref, wa_ref, wb_ref, wo_ref, x_ref, gt_ref, gpost_ref,
                    gpre_ref, sc_ref, sh_ref, w1f_ref, w2f_ref, x1_ref, h2_ref, w1b_ref, w2b_ref, *, d):
    w1b_ref[...] = w1f_ref[...].astype(BF16)
    w2b_ref[...] = w2f_ref[...].astype(BF16)
    ya = jnp.dot(oa_ref[...], wa_ref[...], preferred_element_type=F32)
    yb = jnp.dot(ob_ref[...], wb_ref[...], preferred_element_type=F32)
    z = sg_ref[:, :d].astype(F32) * ya + sg_ref[:, d:].astype(F32) * yb
    y = jnp.dot(z.astype(BF16), wo_ref[...], preferred_element_type=F32)
    x1 = x_ref[...] + gt_ref[...] * (_rms(y) * gpost_ref[...])
    x1_ref[...] = x1
    h2_ref[...] = ((_rms(x1) * gpre_ref[...]) * (1.0 + sc_ref[...]) + sh_ref[...]).astype(h2_ref.dtype)


def _mix_out(oa3, ob3, sg3, wa, wb, wo, x, gt1, g_post, g_pre2, sc2, sh2, w1f, w2f):
    b, t, d = x.shape
    tm = _tile(t, 256)
    nt = t // tm
    c_in, c_out, c_shape = _side_cast_specs([(w1f, 1), (w2f, 0)], b * nt, lambda bi, i: bi * nt + i, 16)
    one = pl.Buffered(1)
    row = lambda w: pl.BlockSpec((None, tm, w), lambda bi, i: (bi, i, 0))
    full = lambda a: pl.BlockSpec(a.shape, lambda bi, i: (0, 0), pipeline_mode=one)
    vec = pl.BlockSpec((None, 1, d), lambda bi, i: (bi, 0, 0))
    gain = pl.BlockSpec((1, d), lambda bi, i: (0, 0))
    return pl.pallas_call(
        functools.partial(_mix_out_kernel, d=d),
        grid=(b, t // tm),
        in_specs=[row(MOBA_W), row(DSA_W), row(2 * d), full(wa), full(wb), full(wo), row(d),
                  vec, gain, gain, vec, vec] + c_in,
        out_specs=[row(d), row(d)] + c_out,
        out_shape=[jax.ShapeDtypeStruct((b, t, d), F32), jax.ShapeDtypeStruct((b, t, d), BF16)] + c_shape,
        compiler_params=_params(("arbitrary", "arbitrary")),
        name="mix_out",
    )(oa3, ob3, sg3, wa, wb, wo, x, gt1, g_post, g_pre2, sc2, sh2, w1f, w2f)


def _ffn_kernel(h_ref, w1_ref, w2_ref, x1_ref, gt_ref, g_ref, o_ref, acc_ref):
    j = pl.program_id(2)

    @pl.when(j == 0)
    def _():
        acc_ref[...] = jnp.zeros(acc_ref.shape, F32)

    for rows in _row_chunks(h_ref, KEY_TILE):
        u = jnp.dot(h_ref[rows, :], w1_ref[...], preferred_element_type=F32)
        u = jnp.square(jnp.maximum(u, 0.0)).astype(BF16)
        acc_ref[rows, :] += jnp.dot(u, w2_ref[...], preferred_element_type=F32)

    @pl.when(j == pl.num_programs(2) - 1)
    def _():
        o_ref[...] = x1_ref[...] + gt_ref[...] * (_rms(acc_ref[...]) * g_ref[...])


def _ffn(h2, w1, w2, x1, gt2, g_post):
    b, t, d = x1.shape
    ff = w1.shape[1]
    tm = _tile(t, 512)
    tf = _tile(ff, 1024)
    row = lambda: pl.BlockSpec((None, tm, d), lambda bi, i, j: (bi, i, 0))
    return pl.pallas_call(
        _ffn_kernel,
        grid=(b, t // tm, ff // tf),
        in_specs=[row(),
                  pl.BlockSpec((d, tf), lambda bi, i, j: (0, j)),
                  pl.BlockSpec((tf, d), lambda bi, i, j: (j, 0)),
                  row(),
                  pl.BlockSpec((None, 1, d), lambda bi, i, j: (bi, 0, 0)),
                  pl.BlockSpec((1, d), lambda bi, i, j: (0, 0))],
        out_specs=row(),
        out_shape=jax.ShapeDtypeStruct((b, t, d), F32),
        scratch_shapes=[pltpu.VMEM((tm, d), F32)],
        compiler_params=_params(("parallel", "parallel", "arbitrary")),
        name="ffn",
    )(h2, w1, w2, x1, gt2, g_post)


def _rope_tables(t):
    pos = jnp.arange(t, dtype=F32)[:, None]
    lane = jnp.arange(LANES)[None, :]

    def cos_sin(half):
        inv_freq = jnp.power(ROPE_THETA, -jnp.arange(half, dtype=F32) / half)
        ang = pos * inv_freq[None, :]
        reps = LANES // half
        sign = jnp.where(lane % (2 * half) < half, -1.0, 1.0)
        return jnp.tile(jnp.cos(ang), (1, reps)), jnp.tile(jnp.sin(ang), (1, reps)) * sign

    cos_k, sin_k = cos_sin(HEAD_DIM // 2)
    rope128 = (((cos_k * QK_SCALE_LOG2).T, (sin_k * QK_SCALE_LOG2).T), (cos_k, sin_k))
    cos64, sin64 = cos_sin(IDX_DIM // 2)
    is_key = lane < IDX_DIM
    low = lane % IDX_DIM < IDX_DIM // 2
    kw = (jnp.where(is_key, cos64, 1.0), jnp.where(is_key & low, sin64, 0.0),
          jnp.where(is_key & ~low, sin64, 0.0))
    return rope128, (cos64.T, sin64.T), kw


def kernel(x, c, w_ada, b_ada, g_pre_mix, g_post_mix, w_in, w_moba_out, w_dsa_out, w_o,
           g_pre_ffn, g_post_ffn, w_ff1, w_ff2):
    b, t, d = x.shape
    m = b * t
    nb = t // MOBA_BLOCK
    assert t % MOBA_BLOCK == 0 and nb <= MAX_BLOCKS and d % LANES == 0 and b <= 16
    n_sel = max(1, min(MOBA_TOPK, nb - 1))
    topk = min(DSA_TOPK_MAX, t // 4)
    rope128, rope64, rope_kw = _rope_tables(t)
    c_pad = jnp.zeros((16, d), F32).at[:b].set(c)

    for l in range(w_ada.shape[0]):
        mod = _ada(c_pad, w_ada[l], b_ada[l][None, :])[:b]
        sh1, sc1, gt1, sh2, sc2, gt2 = [v[:, None, :] for v in jnp.split(mod, 6, axis=-1)]

        h = _norm_mod(x, g_pre_mix[l][None, :], sc1, sh1).reshape(m, d)
        qt3, kk, vt3, qit3, klo, khi, wit, sg, (wa, wb, wo) = _in_proj(
            h, jnp.swapaxes(w_in, 1, 2), l, t, rope128, rope64, rope_kw,
            (w_moba_out[l], w_dsa_out[l], w_o[l]))

        kk3 = kk.reshape(b, t, -1)
        kmt = _moba_kmean(kk3, nb)
        oa = _moba_attn(qt3, kk3, vt3, kmt, nb, n_sel)
        ob = _dsa_attn(qit3, klo.reshape(b, t, -1), khi.reshape(b, t, -1), wit, qt3, kk3, vt3, topk)

        x, h2, w1, w2 = _mix_out(oa, ob, sg.reshape(b, t, -1), wa, wb, wo, x, gt1,
                                 g_post_mix[l][None, :], g_pre_ffn[l][None, :], sc2, sh2,
                                 w_ff1[l], w_ff2[l])
        x = _ffn(h2, w1, w2, x, gt2, g_post_ffn[l][None, :])
    return x
```

```python
import functools

import jax
import jax.numpy as jnp
from jax import lax
from jax.experimental import pallas as pl
from jax.experimental.pallas import tpu as pltpu

HEAD_DIM = 128
MOBA_HEADS = 8
MOBA_BLOCK = 256
MOBA_TOPK = 3
DSA_HEADS = 8
IDX_HEADS = 16
IDX_DIM = 64
DSA_TOPK_MAX = 256
ROPE_THETA = 10000.0
RMS_EPS = 1e-6

MOBA_W = MOBA_HEADS * HEAD_DIM
DSA_W = DSA_HEADS * HEAD_DIM
IDX_W = IDX_HEADS * IDX_DIM
LANES = 128
MAX_BLOCKS = LANES // MOBA_HEADS
KEY_TILE = 256
SUB_TILE = 128
ROW_CHUNK = 256
LOG2_E = 1.4426950408889634
QK_SCALE_LOG2 = HEAD_DIM ** -0.5 * LOG2_E
MASK_BIAS = -30000.0
NEG_BIG = -1e30
BISECT_FIXED = 19
VMEM_LIMIT = 56 * 1024 * 1024

F32 = jnp.float32
BF16 = jnp.bfloat16
_NT = (((1,), (1,)), ((), ()))


def _params(sem):
    return pltpu.CompilerParams(dimension_semantics=sem, vmem_limit_bytes=VMEM_LIMIT)


def _tile(n, pref):
    if n <= pref:
        return n
    t = pref - pref % LANES
    while t >= LANES:
        if n % t == 0:
            return t
        t -= LANES
    return n


def _side_cast_specs(arrays, n_steps, step_of, row_blocks):
    ins, outs, shapes = [], [], []
    for w, axis in arrays:
        n = w.shape[axis]
        unit = row_blocks if axis == 0 else LANES
        target = max(unit, n // n_steps)
        slab = next(c for c in range(target - target % unit, 0, -unit) if n % c == 0)
        last = n // slab - 1
        block = (slab, w.shape[1]) if axis == 0 else (w.shape[0], slab)

        def imap(*g, axis=axis, last=last):
            idx = jnp.minimum(step_of(*g), last)
            return (idx, 0) if axis == 0 else (0, idx)

        ins.append(pl.BlockSpec(block, imap))
        outs.append(pl.BlockSpec(block, imap))
        shapes.append(jax.ShapeDtypeStruct(w.shape, BF16))
    return ins, outs, shapes


def _rms(x):
    return x * lax.rsqrt(jnp.mean(x * x, axis=-1, keepdims=True) + RMS_EPS)


def _ada_kernel(c_ref, w_ref, b_ref, o_ref):
    c = c_ref[...]
    cs = (c * jax.nn.sigmoid(c)).astype(BF16)
    o_ref[...] = jnp.dot(cs, w_ref[...].astype(BF16), preferred_element_type=F32) + b_ref[...]


def _ada(c_pad, w, b):
    rows, d = c_pad.shape
    n = w.shape[1]
    tn = _tile(n, 1024)
    return pl.pallas_call(
        _ada_kernel,
        grid=(n // tn,),
        in_specs=[pl.BlockSpec((rows, d), lambda j: (0, 0)),
                  pl.BlockSpec((d, tn), lambda j: (0, j)),
                  pl.BlockSpec((1, tn), lambda j: (0, j))],
        out_specs=pl.BlockSpec((rows, tn), lambda j: (0, j)),
        out_shape=jax.ShapeDtypeStruct((rows, n), F32),
        compiler_params=_params(("parallel",)),
        name="ada_mod",
    )(c_pad, w, b)


def _norm_mod_kernel(x_ref, g_ref, sc_ref, sh_ref, o_ref):
    y = _rms(x_ref[...])
    o_ref[...] = ((y * g_ref[...]) * (1.0 + sc_ref[...]) + sh_ref[...]).astype(o_ref.dtype)


def _norm_mod(x, g, sc, sh):
    b, t, d = x.shape
    tt = _tile(t, 1024)
    vec = pl.BlockSpec((None, 1, d), lambda bi, ti: (bi, 0, 0))
    return pl.pallas_call(
        _norm_mod_kernel,
        grid=(b, t // tt),
        in_specs=[pl.BlockSpec((None, tt, d), lambda bi, ti: (bi, ti, 0)),
                  pl.BlockSpec((1, d), lambda bi, ti: (0, 0)), vec, vec],
        out_specs=pl.BlockSpec((None, tt, d), lambda bi, ti: (bi, ti, 0)),
        out_shape=jax.ShapeDtypeStruct((b, t, d), BF16),
        compiler_params=_params(("parallel", "parallel")),
        name="norm_mod",
    )(x, g, sc, sh)


def _first_token_tile():
    return pl.program_id(1) == 0


def _rope64(y, cos, sin_a, sin_b):
    return y * cos + pltpu.roll(y, LANES - 32, 1) * sin_a + pltpu.roll(y, 32, 1) * sin_b


def _row_chunks(ref, step=ROW_CHUNK):
    n = ref.shape[0]
    step = min(n, step)
    return [slice(r, r + step) for r in range(0, n, step)]


def _wp_rope128_kernel(h_ref, w_ref, cos_ref, sin_ref, o_ref, wb_ref):
    @pl.when(_first_token_tile())
    def _():
        wb_ref[...] = w_ref[...].astype(BF16)

    for rows in _row_chunks(h_ref):
        acc = lax.dot_general(h_ref[rows, :], wb_ref[...], _NT, preferred_element_type=F32)
        cos = cos_ref[rows, :]
        sin = sin_ref[rows, :]
        for g in range(acc.shape[1] // LANES):
            y = acc[:, g * LANES:(g + 1) * LANES]
            o_ref[rows, g * LANES:(g + 1) * LANES] = (
                y * cos + pltpu.roll(y, LANES // 2, 1) * sin).astype(o_ref.dtype)


def _swap_row_halves(y, width):
    half = width // 2
    parts = []
    for r in range(0, y.shape[0], width):
        parts += [y[r + half:r + width], y[r:r + half]]
    return jnp.concatenate(parts, axis=0)


def _wp_rope_t_kernel(h_ref, w_ref, cos_ref, sin_ref, o_ref, wb_ref, *, head_dim):
    @pl.when(_first_token_tile())
    def _():
        wb_ref[...] = w_ref[...].astype(BF16)

    for c in range(o_ref.shape[0]):
        tok = slice(c * KEY_TILE, (c + 1) * KEY_TILE)
        r = lax.dot_general(wb_ref[...], h_ref[tok, :], _NT, preferred_element_type=F32)
        for g in range(0, r.shape[0], LANES):
            y = r[g:g + LANES]
            o_ref[c, g:g + LANES, :] = (y * cos_ref[:, tok]
                                        + _swap_row_halves(y, head_dim) * sin_ref[:, tok]).astype(o_ref.dtype)


def _wp_vt_kernel(h_ref, w_ref, o_ref, wb_ref):
    @pl.when(_first_token_tile())
    def _():
        wb_ref[...] = w_ref[...].astype(BF16)

    for c in range(o_ref.shape[0]):
        r = lax.dot_general(wb_ref[...], h_ref[c * KEY_TILE:(c + 1) * KEY_TILE, :], _NT,
                            preferred_element_type=F32)
        o_ref[c] = r.astype(o_ref.dtype)


def _wp_gate_kernel(h_ref, wa_ref, wb_ref, c0_ref, c1_ref, c2_ref, o_ref, d0_ref, d1_ref, d2_ref, wg_ref,
                    *, shift):
    d0_ref[...] = c0_ref[...].astype(BF16)
    d1_ref[...] = c1_ref[...].astype(BF16)
    d2_ref[...] = c2_ref[...].astype(BF16)

    @pl.when(_first_token_tile())
    def _():
        keep = wg_ref.shape[0] - shift
        wg_ref[:keep, :] = wa_ref[shift:, :].astype(BF16)
        wg_ref[keep:, :] = wb_ref[:shift, :].astype(BF16)

    for rows in _row_chunks(h_ref):
        acc = lax.dot_general(h_ref[rows, :], wg_ref[...], _NT, preferred_element_type=F32)
        o_ref[rows, :] = jax.nn.sigmoid(acc).astype(o_ref.dtype)


def _wp_kw_kernel(h_ref, w_ref, cos_ref, sa_ref, sb_ref, klo_ref, khi_ref, wit_ref, wb_ref):
    @pl.when(pl.program_id(0) == 0)
    def _():
        r = lax.broadcasted_iota(jnp.int32, w_ref.shape, 0)
        wb_ref[...] = jnp.where(r < IDX_DIM + IDX_HEADS, w_ref[...], 0.0).astype(BF16)

    h = h_ref[...]
    y = lax.dot_general(h, wb_ref[...], _NT, preferred_element_type=F32)
    r = _rope64(y, cos_ref[...], sa_ref[...], sb_ref[...])
    lane = lax.broadcasted_iota(jnp.int32, y.shape, 1)
    klo = jnp.where(lane < IDX_DIM, r, 0.0)
    klo_ref[...] = klo.astype(BF16)
    khi_ref[...] = pltpu.roll(klo, IDX_DIM, 1).astype(BF16)
    for c in range(wit_ref.shape[0]):
        wit_ref[c] = y[c * KEY_TILE:(c + 1) * KEY_TILE, :].T


def _in_proj(h, wt, l, t, tables128, tables64, tables_kw, out_weights):
    m, k = h.shape
    d = k
    tm = _tile(t, 1024)
    nt = t // tm
    ni = m // tm
    assert tm % KEY_TILE == 0
    tn = 1024
    o_dsa = 3 * MOBA_W
    o_qi = o_dsa + 3 * DSA_W
    o_kw = o_qi + IDX_W
    o_g = o_kw + IDX_DIM + IDX_HEADS
    sem = ("arbitrary", "arbitrary")
    h_spec = pl.BlockSpec((tm, k), lambda j, i: (i, 0))
    tab = pl.BlockSpec((tm, LANES), lambda j, i: (i % nt, 0))
    tabt = pl.BlockSpec((LANES, tm), lambda j, i: (0, i % nt))
    row = pl.BlockSpec((tm, tn), lambda j, i: (i, j))
    out_t = pl.BlockSpec((tm // KEY_TILE, tn, KEY_TILE), lambda j, i: (i, j, 0))

    def wrows(first_blocks, n_first, second_start):
        return pl.BlockSpec((None, tn, k), lambda j, i: (l, jnp.where(j < n_first, first_blocks + j,
                                                                      second_start + j - n_first), 0))

    assert MOBA_W == DSA_W and MOBA_W % tn == 0
    n_m, n_d = MOBA_W // tn, DSA_W // tn

    qt = pl.pallas_call(
        functools.partial(_wp_rope_t_kernel, head_dim=HEAD_DIM),
        grid=(n_m + n_d, ni),
        in_specs=[h_spec, wrows(0, n_m, o_dsa // tn), tabt, tabt],
        out_specs=out_t,
        out_shape=jax.ShapeDtypeStruct((m // KEY_TILE, MOBA_W + DSA_W, KEY_TILE), BF16),
        scratch_shapes=[pltpu.VMEM((tn, k), BF16)],
        compiler_params=_params(sem),
        name="proj_q",
    )(h, wt, *tables128[0])

    kk = pl.pallas_call(
        _wp_rope128_kernel,
        grid=(n_m + n_d, ni),
        in_specs=[h_spec, wrows(MOBA_W // tn, n_m, (o_dsa + DSA_W) // tn), tab, tab],
        out_specs=row,
        out_shape=jax.ShapeDtypeStruct((m, MOBA_W + DSA_W), BF16),
        scratch_shapes=[pltpu.VMEM((tn, k), BF16)],
        compiler_params=_params(sem),
        name="proj_k",
    )(h, wt, *tables128[1])

    vt = pl.pallas_call(
        _wp_vt_kernel,
        grid=(n_m + n_d, ni),
        in_specs=[h_spec, wrows(2 * MOBA_W // tn, n_m, (o_dsa + 2 * DSA_W) // tn)],
        out_specs=out_t,
        out_shape=jax.ShapeDtypeStruct((m // KEY_TILE, MOBA_W + DSA_W, KEY_TILE), BF16),
        scratch_shapes=[pltpu.VMEM((tn, k), BF16)],
        compiler_params=_params(sem),
        name="proj_v",
    )(h, wt)

    n_q = IDX_W // tn
    qit = pl.pallas_call(
        functools.partial(_wp_rope_t_kernel, head_dim=IDX_DIM),
        grid=(n_q, ni),
        in_specs=[h_spec, wrows(o_qi // tn, n_q, 0), tabt, tabt],
        out_specs=out_t,
        out_shape=jax.ShapeDtypeStruct((m // KEY_TILE, IDX_W, KEY_TILE), BF16),
        scratch_shapes=[pltpu.VMEM((tn, k), BF16)],
        compiler_params=_params(sem),
        name="proj_qi",
    )(h, wt, *tables64)

    tab1 = pl.BlockSpec((tm, LANES), lambda i: (i % nt, 0))
    row1 = pl.BlockSpec((tm, LANES), lambda i: (i, 0))
    klo, khi, wit = pl.pallas_call(
        _wp_kw_kernel,
        grid=(ni,),
        in_specs=[pl.BlockSpec((tm, k), lambda i: (i, 0)),
                  pl.BlockSpec((None, LANES, k), lambda i: (l, o_kw // LANES, 0)), tab1, tab1, tab1],
        out_specs=[row1, row1, pl.BlockSpec((tm // KEY_TILE, LANES, KEY_TILE), lambda i: (i, 0, 0))],
        out_shape=[jax.ShapeDtypeStruct((m, LANES), BF16), jax.ShapeDtypeStruct((m, LANES), BF16),
                   jax.ShapeDtypeStruct((m // KEY_TILE, LANES, KEY_TILE), F32)],
        scratch_shapes=[pltpu.VMEM((LANES, k), BF16)],
        compiler_params=_params(("arbitrary",)),
        name="proj_kw",
    )(h, wt, *tables_kw)

    tg = _tile(2 * d, tn)
    g0 = o_kw // tg
    shift = o_g - o_kw
    assert o_kw % tg == 0 and shift <= LANES and tg % LANES == 0
    ng = 2 * d // tg
    c_in, c_out, c_shape = _side_cast_specs([(w, 0) for w in out_weights], ng * ni,
                                            lambda j, i: j * ni + i, 16)
    sg, *out_weights_bf16 = pl.pallas_call(
        functools.partial(_wp_gate_kernel, shift=shift),
        grid=(ng, ni),
        in_specs=[h_spec,
                  pl.BlockSpec((None, tg, k), lambda j, i: (l, g0 + j, 0)),
                  pl.BlockSpec((None, LANES, k), lambda j, i: (l, (g0 + j + 1) * (tg // LANES), 0))] + c_in,
        out_specs=[pl.BlockSpec((tm, tg), lambda j, i: (i, j))] + c_out,
        out_shape=[jax.ShapeDtypeStruct((m, 2 * d), BF16)] + c_shape,
        scratch_shapes=[pltpu.VMEM((tg, k), BF16)],
        compiler_params=_params(sem),
        name="proj_gate",
    )(h, wt, wt, *out_weights)
    return qt, kk, vt, qit, klo, khi, wit, sg, out_weights_bf16


def _moba_kmean_kernel(k_ref, o_ref, km_ref, *, nb):
    km_ref[...] = jnp.zeros(km_ref.shape, F32)
    for j in range(nb):
        blk = k_ref[j * MOBA_BLOCK:(j + 1) * MOBA_BLOCK, :].astype(F32)
        km_ref[j:j + 1, :] = jnp.sum(blk, axis=0, keepdims=True) * (1.0 / MOBA_BLOCK)
    km = km_ref[...]
    tiled = jnp.concatenate([km] * MOBA_HEADS, axis=0)
    r = lax.shift_right_logical(lax.broadcasted_iota(jnp.int32, tiled.shape, 0), MAX_BLOCKS.bit_length() - 1)
    c = lax.shift_right_logical(lax.broadcasted_iota(jnp.int32, tiled.shape, 1), HEAD_DIM.bit_length() - 1)
    o_ref[...] = jnp.where(r == c, tiled, 0.0).astype(o_ref.dtype)


def _moba_kmean(kk3, nb):
    b, t, _ = kk3.shape
    return pl.pallas_call(
        functools.partial(_moba_kmean_kernel, nb=nb),
        grid=(b,),
        in_specs=[pl.BlockSpec((None, t, MOBA_W), lambda bi: (bi, 0, 0))],
        out_specs=pl.BlockSpec((None, LANES, MOBA_W), lambda bi: (bi, 0, 0)),
        out_shape=jax.ShapeDtypeStruct((b, LANES, MOBA_W), BF16),
        scratch_shapes=[pltpu.VMEM((MAX_BLOCKS, MOBA_W), F32)],
        compiler_params=_params(("parallel",)),
        name="moba_kmean",
    )(kk3)


def _moba_block_bias(qt_ref, kmt_ref, sbt_ref, i, n_sel):
    g = jnp.dot(kmt_ref[...], qt_ref[...], preferred_element_type=F32)
    shape = (MAX_BLOCKS, g.shape[1])
    j = lax.broadcasted_iota(jnp.int32, shape, 0)
    past = j < i
    for h in range(MOBA_HEADS):
        gm = jnp.where(past, g[h * MAX_BLOCKS:(h + 1) * MAX_BLOCKS, :], -jnp.inf)
        rank = jnp.zeros(shape, F32)
        for jo in range(MAX_BLOCKS):
            other = gm[jo:jo + 1, :]
            beats = (other > gm) | ((other == gm) & (jo < j))
            rank = rank + jnp.where(beats, 1.0, 0.0)
        bias = jnp.where(past & (rank < n_sel), 0.0, MASK_BIAS)
        for jb in range(MAX_BLOCKS):
            sbt_ref[h, jb] = jnp.broadcast_to(bias[jb:jb + 1, :], (8, shape[1]))


def _flash_step(st, m, l, acc_t, v_t):
    m_new = jnp.maximum(m, jnp.max(st, axis=0, keepdims=True))
    alpha = jnp.exp2(m - m_new)
    p = jnp.exp2(st - m_new)
    l = alpha * l + jnp.sum(p, axis=0, keepdims=True)
    acc_t = alpha * acc_t + jnp.dot(v_t, p.astype(BF16), preferred_element_type=F32)
    return m_new, l, acc_t


def _flash_step_joint(sts, m, l, acc_t, v_ts):
    m_new = m
    for st in sts:
        m_new = jnp.maximum(m_new, jnp.max(st, axis=0, keepdims=True))
    alpha = jnp.exp2(m - m_new)
    l = alpha * l
    acc_t = alpha * acc_t
    for st, v_t in zip(sts, v_ts):
        p = jnp.exp2(st - m_new)
        l = l + jnp.sum(p, axis=0, keepdims=True)
        acc_t = acc_t + jnp.dot(v_t, p.astype(BF16), preferred_element_type=F32)
    return m_new, l, acc_t


def _flash_finish(o_ref, l_ref, acc_ref, heads):
    for h in range(heads):
        hs = slice(h * HEAD_DIM, (h + 1) * HEAD_DIM)
        o_ref[:, hs] = (acc_ref[h] / l_ref[h:h + 1, :]).T.astype(o_ref.dtype)


def _moba_attn_kernel(qt_ref, k_ref, vt_ref, kmt_ref, o_ref, m_ref, l_ref, acc_ref, sbt_ref, *, n_sel):
    i = pl.program_id(1)
    own = pl.multiple_of(i * MOBA_BLOCK, MOBA_BLOCK)
    sub_shape = (SUB_TILE, MOBA_BLOCK)
    kidx = lax.broadcasted_iota(jnp.int32, sub_shape, 0)
    qidx = lax.broadcasted_iota(jnp.int32, sub_shape, 1)
    _moba_block_bias(qt_ref, kmt_ref, sbt_ref, i, n_sel)
    for h in range(MOBA_HEADS):
        hs = slice(h * HEAD_DIM, (h + 1) * HEAD_DIM)
        m = jnp.full((1, MOBA_BLOCK), NEG_BIG, F32)
        l = jnp.zeros((1, MOBA_BLOCK), F32)
        acc = jnp.zeros((HEAD_DIM, MOBA_BLOCK), F32)
        for u in range(MOBA_BLOCK // SUB_TILE):
            st = jnp.dot(k_ref[pl.ds(own + u * SUB_TILE, SUB_TILE), hs], qt_ref[hs, :],
                         preferred_element_type=F32)
            st = jnp.where(kidx + u * SUB_TILE <= qidx, st, -jnp.inf)
            m, l, acc = _flash_step(st, m, l, acc, vt_ref[i, hs, u * SUB_TILE:(u + 1) * SUB_TILE])
        m_ref[h:h + 1, :] = m
        l_ref[h:h + 1, :] = l
        acc_ref[h] = acc

    def blocks(js):
        for h in range(MOBA_HEADS):
            hs = slice(h * HEAD_DIM, (h + 1) * HEAD_DIM)
            m, l, acc = m_ref[h:h + 1, :], l_ref[h:h + 1, :], acc_ref[h]
            for j in js:
                off = pl.multiple_of(j * MOBA_BLOCK, MOBA_BLOCK)
                bias = sbt_ref[h, j][0:1, :]
                sts, vts = [], []
                for u in range(MOBA_BLOCK // SUB_TILE):
                    sts.append(jnp.dot(k_ref[pl.ds(off + u * SUB_TILE, SUB_TILE), hs], qt_ref[hs, :],
                                       preferred_element_type=F32) + bias)
                    vts.append(vt_ref[j, hs, u * SUB_TILE:(u + 1) * SUB_TILE])
                m, l, acc = _flash_step_joint(sts, m, l, acc, vts)
            m_ref[h:h + 1, :] = m
            l_ref[h:h + 1, :] = l
            acc_ref[h] = acc

    def body(t, carry):
        blocks((2 * t, 2 * t + 1))
        return carry

    lax.fori_loop(0, i // 2, body, 0)

    @pl.when(i % 2 == 1)
    def _():
        blocks((i - 1,))

    _flash_finish(o_ref, l_ref, acc_ref, MOBA_HEADS)


def _moba_attn(qt3, kk3, vt3, kmt, nb, n_sel):
    b, t, _ = kk3.shape
    return pl.pallas_call(
        functools.partial(_moba_attn_kernel, n_sel=n_sel),
        grid=(b, nb),
        in_specs=[pl.BlockSpec((None, MOBA_W, MOBA_BLOCK), lambda bi, i: (bi * nb + i, 0, 0)),
                  pl.BlockSpec((None, t, MOBA_W), lambda bi, i: (bi, 0, 0)),
                  pl.BlockSpec((nb, MOBA_W, KEY_TILE), lambda bi, i: (bi, 0, 0)),
                  pl.BlockSpec((None, LANES, MOBA_W), lambda bi, i: (bi, 0, 0))],
        out_specs=pl.BlockSpec((None, MOBA_BLOCK, MOBA_W), lambda bi, i: (bi, i, 0)),
        out_shape=jax.ShapeDtypeStruct((b, t, MOBA_W), BF16),
        scratch_shapes=[pltpu.VMEM((MOBA_HEADS, MOBA_BLOCK), F32),
                        pltpu.VMEM((MOBA_HEADS, MOBA_BLOCK), F32),
                        pltpu.VMEM((MOBA_HEADS, HEAD_DIM, MOBA_BLOCK), F32),
                        pltpu.VMEM((MOBA_HEADS, MAX_BLOCKS, 8, MOBA_BLOCK), F32)],
        compiler_params=_params(("parallel", "arbitrary")),
        name="moba_attn",
    )(qt3, kk3, vt3, kmt)


def _float_to_key(x):
    b = lax.bitcast_convert_type(x, jnp.int32)
    return jnp.where(b >= 0, b, b ^ 0x7FFFFFFF)


def _key_to_float(t):
    return lax.bitcast_convert_type(jnp.where(t >= 0, t, t ^ 0x7FFFFFFF), F32)


def _dsa_kernel(qit_ref, klo_ref, khi_ref, wt_ref, qt_ref, k_ref, vt_ref, o_ref,
                sc_ref, jl_ref, m_ref, l_ref, acc_ref, *, topk, idx_scale, tq, seq_bits):
    i = pl.program_id(1)
    nk = i + 1
    shape = (KEY_TILE, tq)
    krow = lax.broadcasted_iota(jnp.int32, shape, 0)
    qpos = i * tq + lax.broadcasted_iota(jnp.int32, shape, 1)

    wt = wt_ref[...]

    def score_tiles(kts):
        offs = [pl.multiple_of(kt * KEY_TILE, KEY_TILE) for kt in kts]
        keys = jnp.concatenate(
            [r[pl.ds(off, KEY_TILE), :] for off in offs for r in (klo_ref, khi_ref)], axis=0)
        accs = [jnp.zeros(shape, F32) for _ in kts]
        for p in range(IDX_HEADS // 2):
            s = jnp.dot(keys, qit_ref[p * LANES:(p + 1) * LANES, :], preferred_element_type=F32)
            for n in range(len(kts)):
                lo = s[2 * n * KEY_TILE:(2 * n + 1) * KEY_TILE]
                hi = s[(2 * n + 1) * KEY_TILE:(2 * n + 2) * KEY_TILE]
                accs[n] = accs[n] + (jnp.maximum(lo, 0.0) * wt[2 * p:2 * p + 1, :]
                                     + jnp.maximum(hi, 0.0) * wt[2 * p + 1:2 * p + 2, :])
        for kt, acc in zip(kts, accs):
            kpos = kt * KEY_TILE + krow
            sc_ref[kt] = jnp.where(kpos <= qpos, acc * idx_scale, -jnp.inf)

    def score_body(t, carry):
        score_tiles((2 * t, 2 * t + 1))
        return carry

    lax.fori_loop(0, nk // 2, score_body, 0)

    @pl.when(nk % 2 == 1)
    def _():
        score_tiles((nk - 1,))

    def count(pred):
        def body(kt, acc):
            hit = jnp.where(pred(sc_ref[kt], kt * KEY_TILE + krow), 1.0, 0.0)
            return acc + jnp.sum(hit.reshape(KEY_TILE // 32, 32, tq), axis=0)
        part = lax.fori_loop(0, nk, body, jnp.zeros((32, tq), F32))
        return jnp.sum(part, axis=0, keepdims=True)

    need_select = nk * tq > topk

    @pl.when(jnp.logical_not(need_select))
    def _():
        def body(kt, carry):
            sc_ref[kt] = jnp.where(kt * KEY_TILE + krow <= qpos, 0.0, NEG_BIG)
            return carry
        lax.fori_loop(0, nk, body, 0)

    @pl.when(need_select)
    def _():
        kf = jnp.float32(topk)

        def ext_body(kt, carry):
            s = sc_ref[kt]
            lo = jnp.min(jnp.where(s > -jnp.inf, s, jnp.inf).reshape(KEY_TILE // 32, 32, tq), axis=0)
            hi = jnp.max(s.reshape(KEY_TILE // 32, 32, tq), axis=0)
            return jnp.minimum(carry[0], lo), jnp.maximum(carry[1], hi)

        lo, hi = lax.fori_loop(0, nk, ext_body, (jnp.full((32, tq), jnp.inf, F32),
                                                 jnp.full((32, tq), -jnp.inf, F32)))
        lo = jnp.min(lo, axis=0, keepdims=True)
        hi = _key_to_float(_float_to_key(jnp.max(hi, axis=0, keepdims=True)) + 1)
        n_fin = count(lambda s, kp: s >= lo)
        done = jnp.where(n_fin <= kf, 1.0, 0.0)

        def step(state):
            lo, hi, clo, done = state
            mid = lo * 0.5 + hi * 0.5
            stuck = (mid <= lo) | (mid >= hi)
            c = count(lambda s, kp: s >= mid)
            up = (c >= kf) & (done < 0.5)
            down = (c < kf) & (done < 0.5)
            lo = jnp.where(up, mid, lo)
            clo = jnp.where(up, c, clo)
            hi = jnp.where(down, mid, hi)
            done = jnp.where(stuck | (clo == kf), 1.0, done)
            return lo, hi, clo, done

        state = lax.fori_loop(0, BISECT_FIXED, lambda _, st: step(st), (lo, hi, n_fin, done))
        state = lax.while_loop(lambda st: jnp.min(st[3]) < 0.5, step, state)
        thr = jnp.where(n_fin < kf, -jnp.inf, state[0])

        def tie_body(kt, acc):
            s = sc_ref[kt]
            gt = jnp.where(s > thr, 1.0, 0.0).reshape(KEY_TILE // 32, 32, tq)
            eq = jnp.where(s == thr, 1.0, 0.0).reshape(KEY_TILE // 32, 32, tq)
            return acc[0] + jnp.sum(gt, axis=0), acc[1] + jnp.sum(eq, axis=0)

        n_gt, n_eq = lax.fori_loop(0, nk, tie_body, (jnp.zeros((32, tq), F32),) * 2)
        need = kf - jnp.sum(n_gt, axis=0, keepdims=True)
        n_eq = jnp.sum(n_eq, axis=0, keepdims=True)
        jl_ref[...] = jnp.full(jl_ref.shape, 2 ** seq_bits, jnp.int32)

        @pl.when(jnp.max(n_eq - need) > 0.0)
        def _():
            def idx_body(b, c):
                cand = c | lax.shift_left(jnp.int32(1), seq_bits - 1 - b)
                g = count(lambda s, kp: (s == thr) & (kp < cand))
                return jnp.where(g < need, cand, c)
            c = lax.fori_loop(0, seq_bits, idx_body, jnp.zeros((1, tq), jnp.int32))
            jl_ref[...] = jnp.broadcast_to(c, jl_ref.shape)

        bound = jnp.minimum(jl_ref[0:1, :], qpos[0:1, :])

        def bias_body(kt, carry):
            s = sc_ref[kt]
            sel = (s > thr) | ((s == thr) & (kt * KEY_TILE + krow <= bound))
            sc_ref[kt] = jnp.where(sel, 0.0, NEG_BIG)
            return carry
        lax.fori_loop(0, nk, bias_body, 0)

    m_ref[...] = jnp.full(m_ref.shape, NEG_BIG, F32)
    l_ref[...] = jnp.zeros(l_ref.shape, F32)
    acc_ref[...] = jnp.zeros(acc_ref.shape, F32)

    def tiles(kts):
        for h in range(DSA_HEADS):
            hs = slice(h * HEAD_DIM, (h + 1) * HEAD_DIM)
            m, l, acc = m_ref[h:h + 1, :], l_ref[h:h + 1, :], acc_ref[h]
            for kt in kts:
                off = pl.multiple_of(kt * KEY_TILE, KEY_TILE)
                for u in range(KEY_TILE // SUB_TILE):
                    us = slice(u * SUB_TILE, (u + 1) * SUB_TILE)
                    st = jnp.dot(k_ref[pl.ds(off + u * SUB_TILE, SUB_TILE), hs], qt_ref[hs, :],
                                 preferred_element_type=F32) + sc_ref[kt, us, :]
                    m, l, acc = _flash_step(st, m, l, acc, vt_ref[kt, hs, us])
            m_ref[h:h + 1, :] = m
            l_ref[h:h + 1, :] = l
            acc_ref[h] = acc

    def attn_body(t, carry):
        tiles((2 * t, 2 * t + 1))
        return carry

    lax.fori_loop(0, nk // 2, attn_body, 0)

    @pl.when(nk % 2 == 1)
    def _():
        tiles((nk - 1,))

    _flash_finish(o_ref, l_ref, acc_ref, DSA_HEADS)


def _dsa_attn(qit3, klo3, khi3, wit, qt3, kk3, vt3, topk):
    b, t, _ = kk3.shape
    tq = KEY_TILE
    nq = t // tq
    seq_bits = max(1, (t - 1).bit_length())
    kern = functools.partial(
        _dsa_kernel, topk=topk,
        idx_scale=(IDX_DIM ** -0.5) * (IDX_HEADS ** -0.5), tq=tq, seq_bits=seq_bits)
    return pl.pallas_call(
        kern,
        grid=(b, nq),
        in_specs=[pl.BlockSpec((None, IDX_W, tq), lambda bi, i: (bi * nq + i, 0, 0)),
                  pl.BlockSpec((None, t, LANES), lambda bi, i: (bi, 0, 0)),
                  pl.BlockSpec((None, t, LANES), lambda bi, i: (bi, 0, 0)),
                  pl.BlockSpec((None, IDX_HEADS, tq), lambda bi, i: (bi * nq + i, IDX_DIM // IDX_HEADS, 0)),
                  pl.BlockSpec((None, DSA_W, tq), lambda bi, i: (bi * nq + i, 1, 0)),
                  pl.BlockSpec((None, t, DSA_W), lambda bi, i: (bi, 0, 1)),
                  pl.BlockSpec((nq, DSA_W, KEY_TILE), lambda bi, i: (bi, 1, 0))],
        out_specs=pl.BlockSpec((None, tq, DSA_W), lambda bi, i: (bi, i, 0)),
        out_shape=jax.ShapeDtypeStruct((b, t, DSA_W), BF16),
        scratch_shapes=[pltpu.VMEM((nq, KEY_TILE, tq), F32),
                        pltpu.VMEM((8, tq), jnp.int32),
                        pltpu.VMEM((DSA_HEADS, tq), F32),
                        pltpu.VMEM((DSA_HEADS, tq), F32),
                        pltpu.VMEM((DSA_HEADS, HEAD_DIM, tq), F32)],
        compiler_params=_params(("parallel", "arbitrary")),
        name="dsa_attn",
    )(qit3, klo3, khi3, wit, qt3, kk3, vt3)


def _mix_out_kernel(oa_ref, ob_ref, sg_ref, wa_ref, wb_ref, wo_ref, x_ref, gt_ref, gpost_ref,
                    gpre_ref, sc_ref, sh_ref, w1f_ref, w2f_ref, x1_ref, h2_ref, w1b_ref, w2b_ref, *, d):
    w1b_ref[...] = w1f_ref[...].astype(BF16)
    w2b_ref[...] = w2f_ref[...].astype(BF16)
    ya = jnp.dot(oa_ref[...], wa_ref[...], preferred_element_type=F32)
    yb = jnp.dot(ob_ref[...], wb_ref[...], preferred_element_type=F32)
    z = sg_ref[:, :d].astype(F32) * ya + sg_ref[:, d:].astype(F32) * yb
    y = jnp.dot(z.astype(BF16), wo_ref[...], preferred_element_type=F32)
    x1 = x_ref[...] + gt_ref[...] * (_rms(y) * gpost_ref[...])
    x1_ref[...] = x1
    h2_ref[...] = ((_rms(x1) * gpre_ref[...]) * (1.0 + sc_ref[...]) + sh_ref[...]).astype(h2_ref.dtype)


def _mix_out(oa3, ob3, sg3, wa, wb, wo, x, gt1, g_post, g_pre2, sc2, sh2, w1f, w2f):
    b, t, d = x.shape
    tm = _tile(t, 256)
    nt = t // tm
    c_in, c_out, c_shape = _side_cast_specs([(w1f, 1), (w2f, 0)], b * nt, lambda bi, i: bi * nt + i, 16)
    one = pl.Buffered(1)
    row = lambda w: pl.BlockSpec((None, tm, w), lambda bi, i: (bi, i, 0))
    full = lambda a: pl.BlockSpec(a.shape, lambda bi, i: (0, 0), pipeline_mode=one)
    vec = pl.BlockSpec((None, 1, d), lambda bi, i: (bi, 0, 0))
    gain = pl.BlockSpec((1, d), lambda bi, i: (0, 0))
    return pl.pallas_call(
        functools.partial(_mix_out_kernel, d=d),
        grid=(b, t // tm),
        in_specs=[row(MOBA_W), row(DSA_W), row(2 * d), full(wa), full(wb), full(wo), row(d),
                  vec, gain, gain, vec, vec] + c_in,
        out_specs=[row(d), row(d)] + c_out,
        out_shape=[jax.ShapeDtypeStruct((b, t, d), F32), jax.ShapeDtypeStruct((b, t, d), BF16)] + c_shape,
        compiler_params=_params(("arbitrary", "arbitrary")),
        name="mix_out",
    )(oa3, ob3, sg3, wa, wb, wo, x, gt1, g_post, g_pre2, sc2, sh2, w1f, w2f)


def _ffn_kernel(h_ref, w1_ref, w2_ref, x1_ref, gt_ref, g_ref, o_ref, acc_ref):
    j = pl.program_id(2)

    @pl.when(j == 0)
    def _():
        acc_ref[...] = jnp.zeros(acc_ref.shape, F32)

    for rows in _row_chunks(h_ref, KEY_TILE):
        u = jnp.dot(h_ref[rows, :], w1_ref[...], preferred_element_type=F32)
        u = jnp.square(jnp.maximum(u, 0.0)).astype(BF16)
        acc_ref[rows, :] += jnp.dot(u, w2_ref[...], preferred_element_type=F32)

    @pl.when(j == pl.num_programs(2) - 1)
    def _():
        o_ref[...] = x1_ref[...] + gt_ref[...] * (_rms(acc_ref[...]) * g_ref[...])


def _ffn(h2, w1, w2, x1, gt2, g_post):
    b, t, d = x1.shape
    ff = w1.shape[1]
    tm = _tile(t, 512)
    tf = _tile(ff, 1024)
    row = lambda: pl.BlockSpec((None, tm, d), lambda bi, i, j: (bi, i, 0))
    return pl.pallas_call(
        _ffn_kernel,
        grid=(b, t // tm, ff // tf),
        in_specs=[row(),
                  pl.BlockSpec((d, tf), lambda bi, i, j: (0, j)),
                  pl.BlockSpec((tf, d), lambda bi, i, j: (j, 0)),
                  row(),
                  pl.BlockSpec((None, 1, d), lambda bi, i, j: (bi, 0, 0)),
                  pl.BlockSpec((1, d), lambda bi, i, j: (0, 0))],
        out_specs=row(),
        out_shape=jax.ShapeDtypeStruct((b, t, d), F32),
        scratch_shapes=[pltpu.VMEM((tm, d), F32)],
        compiler_params=_params(("parallel", "parallel", "arbitrary")),
        name="ffn",
    )(h2, w1, w2, x1, gt2, g_post)


def _rope_tables(t):
    pos = jnp.arange(t, dtype=F32)[:, None]
    lane = jnp.arange(LANES)[None, :]

    def cos_sin(half):
        inv_freq = jnp.power(ROPE_THETA, -jnp.arange(half, dtype=F32) / half)
        ang = pos * inv_freq[None, :]
        reps = LANES // half
        sign = jnp.where(lane % (2 * half) < half, -1.0, 1.0)
        return jnp.tile(jnp.cos(ang), (1, reps)), jnp.tile(jnp.sin(ang), (1, reps)) * sign

    cos_k, sin_k = cos_sin(HEAD_DIM // 2)
    rope128 = (((cos_k * QK_SCALE_LOG2).T, (sin_k * QK_SCALE_LOG2).T), (cos_k, sin_k))
    cos64, sin64 = cos_sin(IDX_DIM // 2)
    is_key = lane < IDX_DIM
    low = lane % IDX_DIM < IDX_DIM // 2
    kw = (jnp.where(is_key, cos64, 1.0), jnp.where(is_key & low, sin64, 0.0),
          jnp.where(is_key & ~low, sin64, 0.0))
    return rope128, (cos64.T, sin64.T), kw


def kernel(x, c, w_ada, b_ada, g_pre_mix, g_post_mix, w_in, w_moba_out, w_dsa_out, w_o,
           g_pre_ffn, g_post_ffn, w_ff1, w_ff2):
    b, t, d = x.shape
    m = b * t
    nb = t // MOBA_BLOCK
    assert t % MOBA_BLOCK == 0 and nb <= MAX_BLOCKS and d % LANES == 0 and b <= 16
    n_sel = max(1, min(MOBA_TOPK, nb - 1))
    topk = min(DSA_TOPK_MAX, t // 4)
    rope128, rope64, rope_kw = _rope_tables(t)
    c_pad = jnp.zeros((16, d), F32).at[:b].set(c)

    for l in range(w_ada.shape[0]):
        mod = _ada(c_pad, w_ada[l], b_ada[l][None, :])[:b]
        sh1, sc1, gt1, sh2, sc2, gt2 = [v[:, None, :] for v in jnp.split(mod, 6, axis=-1)]

        h = _norm_mod(x, g_pre_mix[l][None, :], sc1, sh1).reshape(m, d)
        qt3, kk, vt3, qit3, klo, khi, wit, sg, (wa, wb, wo) = _in_proj(
            h, jnp.swapaxes(w_in, 1, 2), l, t, rope128, rope64, rope_kw,
            (w_moba_out[l], w_dsa_out[l], w_o[l]))

        kk3 = kk.reshape(b, t, -1)
        kmt = _moba_kmean(kk3, nb)
        oa = _moba_attn(qt3, kk3, vt3, kmt, nb, n_sel)
        ob = _dsa_attn(qit3, klo.reshape(b, t, -1), khi.reshape(b, t, -1), wit, qt3, kk3, vt3, topk)

        x, h2, w1, w2 = _mix_out(oa, ob, sg.reshape(b, t, -1), wa, wb, wo, x, gt1,
                                 g_post_mix[l][None, :], g_pre_ffn[l][None, :], sc2, sh2,
                                 w_ff1[l], w_ff2[l])
        x = _ffn(h2, w1, w2, x, gt2, g_post_ffn[l][None, :])
    return x
```

```python
import functools

import jax
import jax.numpy as jnp
from jax import lax
from jax.experimental import pallas as pl
from jax.experimental.pallas import tpu as pltpu

HEAD_DIM = 128
MOBA_HEADS = 8
MOBA_BLOCK = 256
MOBA_TOPK = 3
DSA_HEADS = 8
IDX_HEADS = 16
IDX_DIM = 64
DSA_TOPK_MAX = 256
ROPE_THETA = 10000.0
RMS_EPS = 1e-6

MOBA_W = MOBA_HEADS * HEAD_DIM
DSA_W = DSA_HEADS * HEAD_DIM
IDX_W = IDX_HEADS * IDX_DIM
IN_KW_START = 3 * MOBA_W + 3 * DSA_W + IDX_W
LANES = 128
MAX_BLOCKS = LANES // MOBA_HEADS
KEY_TILE = 256
SUB_TILE = 128
ROW_CHUNK = 256
LOG2_E = 1.4426950408889634
QK_SCALE_LOG2 = HEAD_DIM ** -0.5 * LOG2_E
MASK_BIAS = -30000.0
NEG_BIG = -1e30
BISECT_FIXED = 19
VMEM_LIMIT = 56 * 1024 * 1024

F32 = jnp.float32
BF16 = jnp.bfloat16
_NT = (((1,), (1,)), ((), ()))


def _params(sem):
    return pltpu.CompilerParams(dimension_semantics=sem, vmem_limit_bytes=VMEM_LIMIT)


def _tile(n, pref):
    if n <= pref:
        return n
    t = pref - pref % LANES
    while t >= LANES:
        if n % t == 0:
            return t
        t -= LANES
    return n


def _side_cast_specs(arrays, n_steps, step_of, row_blocks):
    ins, outs, shapes = [], [], []
    for w, axis in arrays:
        n = w.shape[axis]
        unit = row_blocks if axis == 0 else LANES
        target = max(unit, n // n_steps)
        slab = next(c for c in range(target - target % unit, 0, -unit) if n % c == 0)
        last = n // slab - 1
        block = (slab, w.shape[1]) if axis == 0 else (w.shape[0], slab)

        def imap(*g, axis=axis, last=last):
            idx = jnp.minimum(step_of(*g), last)
            return (idx, 0) if axis == 0 else (0, idx)

        ins.append(pl.BlockSpec(block, imap))
        outs.append(pl.BlockSpec(block, imap))
        shapes.append(jax.ShapeDtypeStruct(w.shape, BF16))
    return ins, outs, shapes


def _rms(x):
    return x * lax.rsqrt(jnp.mean(x * x, axis=-1, keepdims=True) + RMS_EPS)


def _ada_kernel(c_ref, w_ref, b_ref, o_ref):
    c = c_ref[...]
    cs = (c * jax.nn.sigmoid(c)).astype(BF16)
    o_ref[...] = jnp.dot(cs, w_ref[...].astype(BF16), preferred_element_type=F32) + b_ref[...]


def _ada(c_pad, w, b):
    rows, d = c_pad.shape
    n = w.shape[1]
    tn = _tile(n, 1024)
    return pl.pallas_call(
        _ada_kernel,
        grid=(n // tn,),
        in_specs=[pl.BlockSpec((rows, d), lambda j: (0, 0)),
                  pl.BlockSpec((d, tn), lambda j: (0, j)),
                  pl.BlockSpec((1, tn), lambda j: (0, j))],
        out_specs=pl.BlockSpec((rows, tn), lambda j: (0, j)),
        out_shape=jax.ShapeDtypeStruct((rows, n), F32),
        compiler_params=_params(("parallel",)),
        name="ada_mod",
    )(c_pad, w, b)


def _rope64(y, cos, sin_a, sin_b):
    return y * cos + pltpu.roll(y, LANES - 32, 1) * sin_a + pltpu.roll(y, 32, 1) * sin_b


def _norm_mod_kernel(x_ref, g_ref, sc_ref, sh_ref, w_ref, cos_ref, sa_ref, sb_ref,
                     o_ref, klo_ref, khi_ref, wit_ref):
    y = _rms(x_ref[...])
    h = ((y * g_ref[...]) * (1.0 + sc_ref[...]) + sh_ref[...]).astype(o_ref.dtype)
    o_ref[...] = h

    r = lax.broadcasted_iota(jnp.int32, w_ref.shape, 0)
    w = jnp.where(r < IDX_DIM + IDX_HEADS, w_ref[...], 0.0).astype(BF16)
    y = lax.dot_general(h, w, _NT, preferred_element_type=F32)
    r = _rope64(y, cos_ref[...], sa_ref[...], sb_ref[...])
    lane = lax.broadcasted_iota(jnp.int32, y.shape, 1)
    klo = jnp.where(lane < IDX_DIM, r, 0.0)
    klo_ref[...] = klo.astype(BF16)
    khi_ref[...] = pltpu.roll(klo, IDX_DIM, 1).astype(BF16)
    for c in range(wit_ref.shape[0]):
        wit_ref[c] = y[c * KEY_TILE:(c + 1) * KEY_TILE, :].T


def _norm_mod(x, g, sc, sh, wt, l, w_row_block, tables_kw):
    b, t, d = x.shape
    tt = _tile(t, 1024)
    nt = t // tt
    m = b * t
    assert tt % KEY_TILE == 0
    vec = pl.BlockSpec((None, 1, d), lambda bi, ti: (bi, 0, 0))
    tab = pl.BlockSpec((tt, LANES), lambda bi, ti: (ti, 0))
    row = pl.BlockSpec((tt, LANES), lambda bi, ti: (bi * nt + ti, 0))
    return pl.pallas_call(
        _norm_mod_kernel,
        grid=(b, nt),
        in_specs=[pl.BlockSpec((None, tt, d), lambda bi, ti: (bi, ti, 0)),
                  pl.BlockSpec((1, d), lambda bi, ti: (0, 0)), vec, vec,
                  pl.BlockSpec((None, LANES, d), lambda bi, ti: (l, w_row_block, 0)), tab, tab, tab],
        out_specs=[pl.BlockSpec((None, tt, d), lambda bi, ti: (bi, ti, 0)), row, row,
                   pl.BlockSpec((tt // KEY_TILE, LANES, KEY_TILE), lambda bi, ti: (bi * nt + ti, 0, 0))],
        out_shape=[jax.ShapeDtypeStruct((b, t, d), BF16),
                   jax.ShapeDtypeStruct((m, LANES), BF16), jax.ShapeDtypeStruct((m, LANES), BF16),
                   jax.ShapeDtypeStruct((m // KEY_TILE, LANES, KEY_TILE), F32)],
        compiler_params=_params(("parallel", "parallel")),
        name="norm_mod",
    )(x, g, sc, sh, wt, *tables_kw)


def _first_token_tile():
    return pl.program_id(1) == 0


def _row_chunks(ref, step=ROW_CHUNK):
    n = ref.shape[0]
    step = min(n, step)
    return [slice(r, r + step) for r in range(0, n, step)]


def _wp_rope128_kernel(h_ref, w_ref, cos_ref, sin_ref, o_ref, wb_ref):
    @pl.when(_first_token_tile())
    def _():
        wb_ref[...] = w_ref[...].astype(BF16)

    for rows in _row_chunks(h_ref):
        acc = lax.dot_general(h_ref[rows, :], wb_ref[...], _NT, preferred_element_type=F32)
        cos = cos_ref[rows, :]
        sin = sin_ref[rows, :]
        for g in range(acc.shape[1] // LANES):
            y = acc[:, g * LANES:(g + 1) * LANES]
            o_ref[rows, g * LANES:(g + 1) * LANES] = (
                y * cos + pltpu.roll(y, LANES // 2, 1) * sin).astype(o_ref.dtype)


def _swap_row_halves(y, width):
    half = width // 2
    parts = []
    for r in range(0, y.shape[0], width):
        parts += [y[r + half:r + width], y[r:r + half]]
    return jnp.concatenate(parts, axis=0)


def _wp_rope_t_kernel(h_ref, w_ref, cos_ref, sin_ref, o_ref, wb_ref, *, head_dim):
    @pl.when(_first_token_tile())
    def _():
        wb_ref[...] = w_ref[...].astype(BF16)

    for c in range(o_ref.shape[0]):
        tok = slice(c * KEY_TILE, (c + 1) * KEY_TILE)
        r = lax.dot_general(wb_ref[...], h_ref[tok, :], _NT, preferred_element_type=F32)
        for g in range(0, r.shape[0], LANES):
            y = r[g:g + LANES]
            o_ref[c, g:g + LANES, :] = (y * cos_ref[:, tok]
                                        + _swap_row_halves(y, head_dim) * sin_ref[:, tok]).astype(o_ref.dtype)


def _wp_vt_kernel(h_ref, w_ref, o_ref, wb_ref):
    @pl.when(_first_token_tile())
    def _():
        wb_ref[...] = w_ref[...].astype(BF16)

    for c in range(o_ref.shape[0]):
        r = lax.dot_general(wb_ref[...], h_ref[c * KEY_TILE:(c + 1) * KEY_TILE, :], _NT,
                            preferred_element_type=F32)
        o_ref[c] = r.astype(o_ref.dtype)


def _wp_gate_kernel(h_ref, wa_ref, wb_ref, c0_ref, c1_ref, c2_ref, o_ref, d0_ref, d1_ref, d2_ref, wg_ref,
                    *, shift):
    d0_ref[...] = c0_ref[...].astype(BF16)
    d1_ref[...] = c1_ref[...].astype(BF16)
    d2_ref[...] = c2_ref[...].astype(BF16)

    @pl.when(_first_token_tile())
    def _():
        keep = wg_ref.shape[0] - shift
        wg_ref[:keep, :] = wa_ref[shift:, :].astype(BF16)
        wg_ref[keep:, :] = wb_ref[:shift, :].astype(BF16)

    for rows in _row_chunks(h_ref):
        acc = lax.dot_general(h_ref[rows, :], wg_ref[...], _NT, preferred_element_type=F32)
        o_ref[rows, :] = jax.nn.sigmoid(acc).astype(o_ref.dtype)


def _in_proj(h, wt, l, t, tables128, tables64, out_weights):
    m, k = h.shape
    d = k
    tm = _tile(t, 1024)
    nt = t // tm
    ni = m // tm
    assert tm % KEY_TILE == 0
    tn = 1024
    o_dsa = 3 * MOBA_W
    o_qi = o_dsa + 3 * DSA_W
    sem = ("arbitrary", "arbitrary")
    h_spec = pl.BlockSpec((tm, k), lambda j, i: (i, 0))
    tab = pl.BlockSpec((tm, LANES), lambda j, i: (i % nt, 0))
    tabt = pl.BlockSpec((LANES, tm), lambda j, i: (0, i % nt))
    row = pl.BlockSpec((tm, tn), lambda j, i: (i, j))
    out_t = pl.BlockSpec((tm // KEY_TILE, tn, KEY_TILE), lambda j, i: (i, j, 0))

    def wrows(first_blocks, n_first, second_start):
        return pl.BlockSpec((None, tn, k), lambda j, i: (l, jnp.where(j < n_first, first_blocks + j,
                                                                      second_start + j - n_first), 0))

    assert MOBA_W == DSA_W and MOBA_W % tn == 0
    n_m, n_d = MOBA_W // tn, DSA_W // tn

    qt = pl.pallas_call(
        functools.partial(_wp_rope_t_kernel, head_dim=HEAD_DIM),
        grid=(n_m + n_d, ni),
        in_specs=[h_spec, wrows(0, n_m, o_dsa // tn), tabt, tabt],
        out_specs=out_t,
        out_shape=jax.ShapeDtypeStruct((m // KEY_TILE, MOBA_W + DSA_W, KEY_TILE), BF16),
        scratch_shapes=[pltpu.VMEM((tn, k), BF16)],
        compiler_params=_params(sem),
        name="proj_q",
    )(h, wt, *tables128[0])

    kk = pl.pallas_call(
        _wp_rope128_kernel,
        grid=(n_m + n_d, ni),
        in_specs=[h_spec, wrows(MOBA_W // tn, n_m, (o_dsa + DSA_W) // tn), tab, tab],
        out_specs=row,
        out_shape=jax.ShapeDtypeStruct((m, MOBA_W + DSA_W), BF16),
        scratch_shapes=[pltpu.VMEM((tn, k), BF16)],
        compiler_params=_params(sem),
        name="proj_k",
    )(h, wt, *tables128[1])

    vt = pl.pallas_call(
        _wp_vt_kernel,
        grid=(n_m + n_d, ni),
        in_specs=[h_spec, wrows(2 * MOBA_W // tn, n_m, (o_dsa + 2 * DSA_W) // tn)],
        out_specs=out_t,
        out_shape=jax.ShapeDtypeStruct((m // KEY_TILE, MOBA_W + DSA_W, KEY_TILE), BF16),
        scratch_shapes=[pltpu.VMEM((tn, k), BF16)],
        compiler_params=_params(sem),
        name="proj_v",
    )(h, wt)

    n_q = IDX_W // tn
    qit = pl.pallas_call(
        functools.partial(_wp_rope_t_kernel, head_dim=IDX_DIM),
        grid=(n_q, ni),
        in_specs=[h_spec, wrows(o_qi // tn, n_q, 0), tabt, tabt],
        out_specs=out_t,
        out_shape=jax.ShapeDtypeStruct((m // KEY_TILE, IDX_W, KEY_TILE), BF16),
        scratch_shapes=[pltpu.VMEM((tn, k), BF16)],
        compiler_params=_params(sem),
        name="proj_qi",
    )(h, wt, *tables64)

    tg = _tile(2 * d, tn)
    g0 = IN_KW_START // tg
    shift = IDX_DIM + IDX_HEADS
    assert IN_KW_START % tg == 0 and shift <= LANES and tg % LANES == 0
    ng = 2 * d // tg
    c_in, c_out, c_shape = _side_cast_specs([(w, 0) for w in out_weights], ng * ni,
                                            lambda j, i: j * ni + i, 16)
    sg, *out_weights_bf16 = pl.pallas_call(
        functools.partial(_wp_gate_kernel, shift=shift),
        grid=(ng, ni),
        in_specs=[h_spec,
                  pl.BlockSpec((None, tg, k), lambda j, i: (l, g0 + j, 0)),
                  pl.BlockSpec((None, LANES, k), lambda j, i: (l, (g0 + j + 1) * (tg // LANES), 0))] + c_in,
        out_specs=[pl.BlockSpec((tm, tg), lambda j, i: (i, j))] + c_out,
        out_shape=[jax.ShapeDtypeStruct((m, 2 * d), BF16)] + c_shape,
        scratch_shapes=[pltpu.VMEM((tg, k), BF16)],
        compiler_params=_params(sem),
        name="proj_gate",
    )(h, wt, wt, *out_weights)
    return qt, kk, vt, qit, sg, out_weights_bf16


def _moba_kmean_kernel(k_ref, o_ref, km_ref, *, nb):
    km_ref[...] = jnp.zeros(km_ref.shape, F32)
    for j in range(nb):
        blk = k_ref[j * MOBA_BLOCK:(j + 1) * MOBA_BLOCK, :].astype(F32)
        km_ref[j:j + 1, :] = jnp.sum(blk, axis=0, keepdims=True) * (1.0 / MOBA_BLOCK)
    km = km_ref[...]
    tiled = jnp.concatenate([km] * MOBA_HEADS, axis=0)
    r = lax.shift_right_logical(lax.broadcasted_iota(jnp.int32, tiled.shape, 0), MAX_BLOCKS.bit_length() - 1)
    c = lax.shift_right_logical(lax.broadcasted_iota(jnp.int32, tiled.shape, 1), HEAD_DIM.bit_length() - 1)
    o_ref[...] = jnp.where(r == c, tiled, 0.0).astype(o_ref.dtype)


def _moba_kmean(kk3, nb):
    b, t, _ = kk3.shape
    return pl.pallas_call(
        functools.partial(_moba_kmean_kernel, nb=nb),
        grid=(b,),
        in_specs=[pl.BlockSpec((None, t, MOBA_W), lambda bi: (bi, 0, 0))],
        out_specs=pl.BlockSpec((None, LANES, MOBA_W), lambda bi: (bi, 0, 0)),
        out_shape=jax.ShapeDtypeStruct((b, LANES, MOBA_W), BF16),
        scratch_shapes=[pltpu.VMEM((MAX_BLOCKS, MOBA_W), F32)],
        compiler_params=_params(("parallel",)),
        name="moba_kmean",
    )(kk3)


def _moba_block_bias(qt_ref, kmt_ref, sbt_ref, i, n_sel):
    g = jnp.dot(kmt_ref[...], qt_ref[...], preferred_element_type=F32)
    shape = (MAX_BLOCKS, g.shape[1])
    j = lax.broadcasted_iota(jnp.int32, shape, 0)
    past = j < i
    for h in range(MOBA_HEADS):
        gm = jnp.where(past, g[h * MAX_BLOCKS:(h + 1) * MAX_BLOCKS, :], -jnp.inf)
        rank = jnp.zeros(shape, F32)
        for jo in range(MAX_BLOCKS):
            other = gm[jo:jo + 1, :]
            beats = (other > gm) | ((other == gm) & (jo < j))
            rank = rank + jnp.where(beats, 1.0, 0.0)
        bias = jnp.where(past & (rank < n_sel), 0.0, MASK_BIAS)
        for jb in range(MAX_BLOCKS):
            sbt_ref[h, jb] = jnp.broadcast_to(bias[jb:jb + 1, :], (8, shape[1]))


def _flash_step(st, m, l, acc_t, v_t):
    m_new = jnp.maximum(m, jnp.max(st, axis=0, keepdims=True))
    alpha = jnp.exp2(m - m_new)
    p = jnp.exp2(st - m_new)
    l = alpha * l + jnp.sum(p, axis=0, keepdims=True)
    acc_t = alpha * acc_t + jnp.dot(v_t, p.astype(BF16), preferred_element_type=F32)
    return m_new, l, acc_t


def _flash_step_joint(sts, m, l, acc_t, v_ts):
    m_new = m
    for st in sts:
        m_new = jnp.maximum(m_new, jnp.max(st, axis=0, keepdims=True))
    alpha = jnp.exp2(m - m_new)
    l = alpha * l
    acc_t = alpha * acc_t
    for st, v_t in zip(sts, v_ts):
        p = jnp.exp2(st - m_new)
        l = l + jnp.sum(p, axis=0, keepdims=True)
        acc_t = acc_t + jnp.dot(v_t, p.astype(BF16), preferred_element_type=F32)
    return m_new, l, acc_t


def _flash_finish(o_ref, l_ref, acc_ref, heads):
    for h in range(heads):
        hs = slice(h * HEAD_DIM, (h + 1) * HEAD_DIM)
        o_ref[:, hs] = (acc_ref[h] / l_ref[h:h + 1, :]).T.astype(o_ref.dtype)


def _moba_attn_kernel(qt_ref, k_ref, vt_ref, kmt_ref, o_ref, m_ref, l_ref, acc_ref, sbt_ref, *, n_sel):
    i = pl.program_id(1)
    own = pl.multiple_of(i * MOBA_BLOCK, MOBA_BLOCK)
    sub_shape = (SUB_TILE, MOBA_BLOCK)
    kidx = lax.broadcasted_iota(jnp.int32, sub_shape, 0)
    qidx = lax.broadcasted_iota(jnp.int32, sub_shape, 1)
    _moba_block_bias(qt_ref, kmt_ref, sbt_ref, i, n_sel)
    for h in range(MOBA_HEADS):
        hs = slice(h * HEAD_DIM, (h + 1) * HEAD_DIM)
        m = jnp.full((1, MOBA_BLOCK), NEG_BIG, F32)
        l = jnp.zeros((1, MOBA_BLOCK), F32)
        acc = jnp.zeros((HEAD_DIM, MOBA_BLOCK), F32)
        for u in range(MOBA_BLOCK // SUB_TILE):
            st = jnp.dot(k_ref[pl.ds(own + u * SUB_TILE, SUB_TILE), hs], qt_ref[hs, :],
                         preferred_element_type=F32)
            st = jnp.where(kidx + u * SUB_TILE <= qidx, st, -jnp.inf)
            m, l, acc = _flash_step(st, m, l, acc, vt_ref[i, hs, u * SUB_TILE:(u + 1) * SUB_TILE])
        m_ref[h:h + 1, :] = m
        l_ref[h:h + 1, :] = l
        acc_ref[h] = acc

    def blocks(js):
        for h in range(MOBA_HEADS):
            hs = slice(h * HEAD_DIM, (h + 1) * HEAD_DIM)
            m, l, acc = m_ref[h:h + 1, :], l_ref[h:h + 1, :], acc_ref[h]
            for j in js:
                off = pl.multiple_of(j * MOBA_BLOCK, MOBA_BLOCK)
                bias = sbt_ref[h, j][0:1, :]
                sts, vts = [], []
                for u in range(MOBA_BLOCK // SUB_TILE):
                    sts.append(jnp.dot(k_ref[pl.ds(off + u * SUB_TILE, SUB_TILE), hs], qt_ref[hs, :],
                                       preferred_element_type=F32) + bias)
                    vts.append(vt_ref[j, hs, u * SUB_TILE:(u + 1) * SUB_TILE])
                m, l, acc = _flash_step_joint(sts, m, l, acc, vts)
            m_ref[h:h + 1, :] = m
            l_ref[h:h + 1, :] = l
            acc_ref[h] = acc

    def body(t, carry):
        blocks((2 * t, 2 * t + 1))
        return carry

    lax.fori_loop(0, i // 2, body, 0)

    @pl.when(i % 2 == 1)
    def _():
        blocks((i - 1,))

    _flash_finish(o_ref, l_ref, acc_ref, MOBA_HEADS)


def _moba_attn(qt3, kk3, vt3, kmt, nb, n_sel):
    b, t, _ = kk3.shape
    return pl.pallas_call(
        functools.partial(_moba_attn_kernel, n_sel=n_sel),
        grid=(b, nb),
        in_specs=[pl.BlockSpec((None, MOBA_W, MOBA_BLOCK), lambda bi, i: (bi * nb + i, 0, 0)),
                  pl.BlockSpec((None, t, MOBA_W), lambda bi, i: (bi, 0, 0)),
                  pl.BlockSpec((nb, MOBA_W, KEY_TILE), lambda bi, i: (bi, 0, 0)),
                  pl.BlockSpec((None, LANES, MOBA_W), lambda bi, i: (bi, 0, 0))],
        out_specs=pl.BlockSpec((None, MOBA_BLOCK, MOBA_W), lambda bi, i: (bi, i, 0)),
        out_shape=jax.ShapeDtypeStruct((b, t, MOBA_W), BF16),
        scratch_shapes=[pltpu.VMEM((MOBA_HEADS, MOBA_BLOCK), F32),
                        pltpu.VMEM((MOBA_HEADS, MOBA_BLOCK), F32),
                        pltpu.VMEM((MOBA_HEADS, HEAD_DIM, MOBA_BLOCK), F32),
                        pltpu.VMEM((MOBA_HEADS, MAX_BLOCKS, 8, MOBA_BLOCK), F32)],
        compiler_params=_params(("parallel", "arbitrary")),
        name="moba_attn",
    )(qt3, kk3, vt3, kmt)


def _float_to_key(x):
    b = lax.bitcast_convert_type(x, jnp.int32)
    return jnp.where(b >= 0, b, b ^ 0x7FFFFFFF)


def _key_to_float(t):
    return lax.bitcast_convert_type(jnp.where(t >= 0, t, t ^ 0x7FFFFFFF), F32)


def _dsa_kernel(qit_ref, klo_ref, khi_ref, wt_ref, qt_ref, k_ref, vt_ref, o_ref,
                sc_ref, jl_ref, m_ref, l_ref, acc_ref, *, topk, idx_scale, tq, seq_bits):
    i = pl.program_id(1)
    nk = i + 1
    shape = (KEY_TILE, tq)
    krow = lax.broadcasted_iota(jnp.int32, shape, 0)
    qpos = i * tq + lax.broadcasted_iota(jnp.int32, shape, 1)

    wt = wt_ref[...]

    def score_tiles(kts):
        offs = [pl.multiple_of(kt * KEY_TILE, KEY_TILE) for kt in kts]
        keys = jnp.concatenate(
            [r[pl.ds(off, KEY_TILE), :] for off in offs for r in (klo_ref, khi_ref)], axis=0)
        accs = [jnp.zeros(shape, F32) for _ in kts]
        for p in range(IDX_HEADS // 2):
            s = jnp.dot(keys, qit_ref[p * LANES:(p + 1) * LANES, :], preferred_element_type=F32)
            for n in range(len(kts)):
                lo = s[2 * n * KEY_TILE:(2 * n + 1) * KEY_TILE]
                hi = s[(2 * n + 1) * KEY_TILE:(2 * n + 2) * KEY_TILE]
                accs[n] = accs[n] + (jnp.maximum(lo, 0.0) * wt[2 * p:2 * p + 1, :]
                                     + jnp.maximum(hi, 0.0) * wt[2 * p + 1:2 * p + 2, :])
        for kt, acc in zip(kts, accs):
            kpos = kt * KEY_TILE + krow
            sc_ref[kt] = jnp.where(kpos <= qpos, acc * idx_scale, -jnp.inf)

    def score_body(t, carry):
        score_tiles((2 * t, 2 * t + 1))
        return carry

    lax.fori_loop(0, nk // 2, score_body, 0)

    @pl.when(nk % 2 == 1)
    def _():
        score_tiles((nk - 1,))

    def count(pred):
        def body(kt, acc):
            hit = jnp.where(pred(sc_ref[kt], kt * KEY_TILE + krow), 1.0, 0.0)
            return acc + jnp.sum(hit.reshape(KEY_TILE // 32, 32, tq), axis=0)
        part = lax.fori_loop(0, nk, body, jnp.zeros((32, tq), F32))
        return jnp.sum(part, axis=0, keepdims=True)

    need_select = nk * tq > topk

    @pl.when(jnp.logical_not(need_select))
    def _():
        def body(kt, carry):
            sc_ref[kt] = jnp.where(kt * KEY_TILE + krow <= qpos, 0.0, NEG_BIG)
            return carry
        lax.fori_loop(0, nk, body, 0)

    @pl.when(need_select)
    def _():
        kf = jnp.float32(topk)

        def ext_body(kt, carry):
            s = sc_ref[kt]
            lo = jnp.min(jnp.where(s > -jnp.inf, s, jnp.inf).reshape(KEY_TILE // 32, 32, tq), axis=0)
            hi = jnp.max(s.reshape(KEY_TILE // 32, 32, tq), axis=0)
            return jnp.minimum(carry[0], lo), jnp.maximum(carry[1], hi)

        lo, hi = lax.fori_loop(0, nk, ext_body, (jnp.full((32, tq), jnp.inf, F32),
                                                 jnp.full((32, tq), -jnp.inf, F32)))
        lo = jnp.min(lo, axis=0, keepdims=True)
        hi = _key_to_float(_float_to_key(jnp.max(hi, axis=0, keepdims=True)) + 1)
        n_fin = count(lambda s, kp: s >= lo)
        done = jnp.where(n_fin <= kf, 1.0, 0.0)

        def step(state):
            lo, hi, clo, done = state
            mid = lo * 0.5 + hi * 0.5
            stuck = (mid <= lo) | (mid >= hi)
            c = count(lambda s, kp: s >= mid)
            up = (c >= kf) & (done < 0.5)
            down = (c < kf) & (done < 0.5)
            lo = jnp.where(up, mid, lo)
            clo = jnp.where(up, c, clo)
            hi = jnp.where(down, mid, hi)
            done = jnp.where(stuck | (clo == kf), 1.0, done)
            return lo, hi, clo, done

        state = lax.fori_loop(0, BISECT_FIXED, lambda _, st: step(st), (lo, hi, n_fin, done))
        state = lax.while_loop(lambda st: jnp.min(st[3]) < 0.5, step, state)
        thr = jnp.where(n_fin < kf, -jnp.inf, state[0])

        def tie_body(kt, acc):
            s = sc_ref[kt]
            gt = jnp.where(s > thr, 1.0, 0.0).reshape(KEY_TILE // 32, 32, tq)
            eq = jnp.where(s == thr, 1.0, 0.0).reshape(KEY_TILE // 32, 32, tq)
            return acc[0] + jnp.sum(gt, axis=0), acc[1] + jnp.sum(eq, axis=0)

        n_gt, n_eq = lax.fori_loop(0, nk, tie_body, (jnp.zeros((32, tq), F32),) * 2)
        need = kf - jnp.sum(n_gt, axis=0, keepdims=True)
        n_eq = jnp.sum(n_eq, axis=0, keepdims=True)
        jl_ref[...] = jnp.full(jl_ref.shape, 2 ** seq_bits, jnp.int32)

        @pl.when(jnp.max(n_eq - need) > 0.0)
        def _():
            def idx_body(b, c):
                cand = c | lax.shift_left(jnp.int32(1), seq_bits - 1 - b)
                g = count(lambda s, kp: (s == thr) & (kp < cand))
                return jnp.where(g < need, cand, c)
            c = lax.fori_loop(0, seq_bits, idx_body, jnp.zeros((1, tq), jnp.int32))
            jl_ref[...] = jnp.broadcast_to(c, jl_ref.shape)

        bound = jnp.minimum(jl_ref[0:1, :], qpos[0:1, :])

        def bias_body(kt, carry):
            s = sc_ref[kt]
            sel = (s > thr) | ((s == thr) & (kt * KEY_TILE + krow <= bound))
            sc_ref[kt] = jnp.where(sel, 0.0, NEG_BIG)
            return carry
        lax.fori_loop(0, nk, bias_body, 0)

    m_ref[...] = jnp.full(m_ref.shape, NEG_BIG, F32)
    l_ref[...] = jnp.zeros(l_ref.shape, F32)
    acc_ref[...] = jnp.zeros(acc_ref.shape, F32)

    def tiles(kts):
        for h in range(DSA_HEADS):
            hs = slice(h * HEAD_DIM, (h + 1) * HEAD_DIM)
            m, l, acc = m_ref[h:h + 1, :], l_ref[h:h + 1, :], acc_ref[h]
            for kt in kts:
                off = pl.multiple_of(kt * KEY_TILE, KEY_TILE)
                for u in range(KEY_TILE // SUB_TILE):
                    us = slice(u * SUB_TILE, (u + 1) * SUB_TILE)
                    st = jnp.dot(k_ref[pl.ds(off + u * SUB_TILE, SUB_TILE), hs], qt_ref[hs, :],
                                 preferred_element_type=F32) + sc_ref[kt, us, :]
                    m, l, acc = _flash_step(st, m, l, acc, vt_ref[kt, hs, us])
            m_ref[h:h + 1, :] = m
            l_ref[h:h + 1, :] = l
            acc_ref[h] = acc

    def attn_body(t, carry):
        tiles((2 * t, 2 * t + 1))
        return carry

    lax.fori_loop(0, nk // 2, attn_body, 0)

    @pl.when(nk % 2 == 1)
    def _():
        tiles((nk - 1,))

    _flash_finish(o_ref, l_ref, acc_ref, DSA_HEADS)


def _dsa_attn(qit3, klo3, khi3, wit, qt3, kk3, vt3, topk):
    b, t, _ = kk3.shape
    tq = KEY_TILE
    nq = t // tq
    seq_bits = max(1, (t - 1).bit_length())
    kern = functools.partial(
        _dsa_kernel, topk=topk,
        idx_scale=(IDX_DIM ** -0.5) * (IDX_HEADS ** -0.5), tq=tq, seq_bits=seq_bits)
    return pl.pallas_call(
        kern,
        grid=(b, nq),
        in_specs=[pl.BlockSpec((None, IDX_W, tq), lambda bi, i: (bi * nq + i, 0, 0)),
                  pl.BlockSpec((None, t, LANES), lambda bi, i: (bi, 0, 0)),
                  pl.BlockSpec((None, t, LANES), lambda bi, i: (bi, 0, 0)),
                  pl.BlockSpec((None, IDX_HEADS, tq), lambda bi, i: (bi * nq + i, IDX_DIM // IDX_HEADS, 0)),
                  pl.BlockSpec((None, DSA_W, tq), lambda bi, i: (bi * nq + i, 1, 0)),
                  pl.BlockSpec((None, t, DSA_W), lambda bi, i: (bi, 0, 1)),
                  pl.BlockSpec((nq, DSA_W, KEY_TILE), lambda bi, i: (bi, 1, 0))],
        out_specs=pl.BlockSpec((None, tq, DSA_W), lambda bi, i: (bi, i, 0)),
        out_shape=jax.ShapeDtypeStruct((b, t, DSA_W), BF16),
        scratch_shapes=[pltpu.VMEM((nq, KEY_TILE, tq), F32),
                        pltpu.VMEM((8, tq), jnp.int32),
                        pltpu.VMEM((DSA_HEADS, tq), F32),
                        pltpu.VMEM((DSA_HEADS, tq), F32),
                        pltpu.VMEM((DSA_HEADS, HEAD_DIM, tq), F32)],
        compiler_params=_params(("parallel", "arbitrary")),
        name="dsa_attn",
    )(qit3, klo3, khi3, wit, qt3, kk3, vt3)


def _mix_out_kernel(oa_ref, ob_ref, sg_ref, wa_ref, wb_ref, wo_ref, x_ref, gt_ref, gpost_ref,
                    gpre_ref, sc_ref, sh_ref, w1f_ref, w2f_ref, x1_ref, h2_ref, w1b_ref, w2b_ref, *, d):
    w1b_ref[...] = w1f_ref[...].astype(BF16)
    w2b_ref[...] = w2f_ref[...].astype(BF16)
    ya = jnp.dot(oa_ref[...], wa_ref[...], preferred_element_type=F32)
    yb = jnp.dot(ob_ref[...], wb_ref[...], preferred_element_type=F32)
    z = sg_ref[:, :d].astype(F32) * ya + sg_ref[:, d:].astype(F32) * yb
    y = jnp.dot(z.astype(BF16), wo_ref[...], preferred_element_type=F32)
    x1 = x_ref[...] + gt_ref[...] * (_rms(y) * gpost_ref[...])
    x1_ref[...] = x1
    h2_ref[...] = ((_rms(x1) * gpre_ref[...]) * (1.0 + sc_ref[...]) + sh_ref[...]).astype(h2_ref.dtype)


def _mix_out(oa3, ob3, sg3, wa, wb, wo, x, gt1, g_post, g_pre2, sc2, sh2, w1f, w2f):
    b, t, d = x.shape
    tm = _tile(t, 256)
    nt = t // tm
    c_in, c_out, c_shape = _side_cast_specs([(w1f, 1), (w2f, 0)], b * nt, lambda bi, i: bi * nt + i, 16)
    one = pl.Buffered(1)
    row = lambda w: pl.BlockSpec((None, tm, w), lambda bi, i: (bi, i, 0))
    full = lambda a: pl.BlockSpec(a.shape, lambda bi, i: (0, 0), pipeline_mode=one)
    vec = pl.BlockSpec((None, 1, d), lambda bi, i: (bi, 0, 0))
    gain = pl.BlockSpec((1, d), lambda bi, i: (0, 0))
    return pl.pallas_call(
        functools.partial(_mix_out_kernel, d=d),
        grid=(b, t // tm),
        in_specs=[row(MOBA_W), row(DSA_W), row(2 * d), full(wa), full(wb), full(wo), row(d),
                  vec, gain, gain, vec, vec] + c_in,
        out_specs=[row(d), row(d)] + c_out,
        out_shape=[jax.ShapeDtypeStruct((b, t, d), F32), jax.ShapeDtypeStruct((b, t, d), BF16)] + c_shape,
        compiler_params=_params(("arbitrary", "arbitrary")),
        name="mix_out",
    )(oa3, ob3, sg3, wa, wb, wo, x, gt1, g_post, g_pre2, sc2, sh2, w1f, w2f)


def _ffn_kernel(h_ref, w1_ref, w2_ref, x1_ref, gt_ref, g_ref, o_ref, acc_ref):
    j = pl.program_id(2)

    @pl.when(j == 0)
    def _():
        acc_ref[...] = jnp.zeros(acc_ref.shape, F32)

    for rows in _row_chunks(h_ref, KEY_TILE):
        u = jnp.dot(h_ref[rows, :], w1_ref[...], preferred_element_type=F32)
        u = jnp.square(jnp.maximum(u, 0.0)).astype(BF16)
        acc_ref[rows, :] += jnp.dot(u, w2_ref[...], preferred_element_type=F32)

    @pl.when(j == pl.num_programs(2) - 1)
    def _():
        o_ref[...] = x1_ref[...] + gt_ref[...] * (_rms(acc_ref[...]) * g_ref[...])


def _ffn(h2, w1, w2, x1, gt2, g_post):
    b, t, d = x1.shape
    ff = w1.shape[1]
    tm = _tile(t, 512)
    tf = _tile(ff, 1024)
    row = lambda: pl.BlockSpec((None, tm, d), lambda bi, i, j: (bi, i, 0))
    return pl.pallas_call(
        _ffn_kernel,
        grid=(b, t // tm, ff // tf),
        in_specs=[row(),
                  pl.BlockSpec((d, tf), lambda bi, i, j: (0, j)),
                  pl.BlockSpec((tf, d), lambda bi, i, j: (j, 0)),
                  row(),
                  pl.BlockSpec((None, 1, d), lambda bi, i, j: (bi, 0, 0)),
                  pl.BlockSpec((1, d), lambda bi, i, j: (0, 0))],
        out_specs=row(),
        out_shape=jax.ShapeDtypeStruct((b, t, d), F32),
        scratch_shapes=[pltpu.VMEM((tm, d), F32)],
        compiler_params=_params(("parallel", "parallel", "arbitrary")),
        name="ffn",
    )(h2, w1, w2, x1, gt2, g_post)


def _rope_tables(t):
    pos = jnp.arange(t, dtype=F32)[:, None]
    lane = jnp.arange(LANES)[None, :]

    def cos_sin(half):
        inv_freq = jnp.power(ROPE_THETA, -jnp.arange(half, dtype=F32) / half)
        ang = pos * inv_freq[None, :]
        reps = LANES // half
        sign = jnp.where(lane % (2 * half) < half, -1.0, 1.0)
        return jnp.tile(jnp.cos(ang), (1, reps)), jnp.tile(jnp.sin(ang), (1, reps)) * sign

    cos_k, sin_k = cos_sin(HEAD_DIM // 2)
    rope128 = (((cos_k * QK_SCALE_LOG2).T, (sin_k * QK_SCALE_LOG2).T), (cos_k, sin_k))
    cos64, sin64 = cos_sin(IDX_DIM // 2)
    is_key = lane < IDX_DIM
    low = lane % IDX_DIM < IDX_DIM // 2
    kw = (jnp.where(is_key, cos64, 1.0), jnp.where(is_key & low, sin64, 0.0),
          jnp.where(is_key & ~low, sin64, 0.0))
    return rope128, (cos64.T, sin64.T), kw


def kernel(x, c, w_ada, b_ada, g_pre_mix, g_post_mix, w_in, w_moba_out, w_dsa_out, w_o,
           g_pre_ffn, g_post_ffn, w_ff1, w_ff2):
    b, t, d = x.shape
    m = b * t
    nb = t // MOBA_BLOCK
    assert t % MOBA_BLOCK == 0 and nb <= MAX_BLOCKS and d % LANES == 0 and b <= 16
    n_sel = max(1, min(MOBA_TOPK, nb - 1))
    topk = min(DSA_TOPK_MAX, t // 4)
    rope128, rope64, rope_kw = _rope_tables(t)
    c_pad = jnp.zeros((16, d), F32).at[:b].set(c)

    for l in range(w_ada.shape[0]):
        mod = _ada(c_pad, w_ada[l], b_ada[l][None, :])[:b]
        sh1, sc1, gt1, sh2, sc2, gt2 = [v[:, None, :] for v in jnp.split(mod, 6, axis=-1)]

        wt = jnp.swapaxes(w_in, 1, 2)
        h, klo, khi, wit = _norm_mod(x, g_pre_mix[l][None, :], sc1, sh1, wt, l,
                                     IN_KW_START // LANES, rope_kw)
        qt3, kk, vt3, qit3, sg, (wa, wb, wo) = _in_proj(
            h.reshape(m, d), wt, l, t, rope128, rope64, (w_moba_out[l], w_dsa_out[l], w_o[l]))

        kk3 = kk.reshape(b, t, -1)
        kmt = _moba_kmean(kk3, nb)
        oa = _moba_attn(qt3, kk3, vt3, kmt, nb, n_sel)
        ob = _dsa_attn(qit3, klo.reshape(b, t, -1), khi.reshape(b, t, -1), wit, qt3, kk3, vt3, topk)

        x, h2, w1, w2 = _mix_out(oa, ob, sg.reshape(b, t, -1), wa, wb, wo, x, gt1,
                                 g_post_mix[l][None, :], g_pre_ffn[l][None, :], sc2, sh2,
                                 w_ff1[l], w_ff2[l])
        x = _ffn(h2, w1, w2, x, gt2, g_post_ffn[l][None, :])
    return x
```

```python
import functools

import jax
import jax.numpy as jnp
from jax import lax
from jax.experimental import pallas as pl
from jax.experimental.pallas import tpu as pltpu

HEAD_DIM = 128
MOBA_HEADS = 8
MOBA_BLOCK = 256
MOBA_TOPK = 3
DSA_HEADS = 8
IDX_HEADS = 16
IDX_DIM = 64
DSA_TOPK_MAX = 256
ROPE_THETA = 10000.0
RMS_EPS = 1e-6

MOBA_W = MOBA_HEADS * HEAD_DIM
DSA_W = DSA_HEADS * HEAD_DIM
IDX_W = IDX_HEADS * IDX_DIM
IN_QI_START = 3 * MOBA_W + 3 * DSA_W
IN_KW_START = IN_QI_START + IDX_W
LANES = 128
MAX_BLOCKS = LANES // MOBA_HEADS
KEY_TILE = 256
SUB_TILE = 128
ROW_CHUNK = 256
LOG2_E = 1.4426950408889634
QK_SCALE_LOG2 = HEAD_DIM ** -0.5 * LOG2_E
MASK_BIAS = -30000.0
NEG_BIG = -1e30
BISECT_FIXED = 19
VMEM_LIMIT = 56 * 1024 * 1024

F32 = jnp.float32
BF16 = jnp.bfloat16
_NT = (((1,), (1,)), ((), ()))


def _params(sem):
    return pltpu.CompilerParams(dimension_semantics=sem, vmem_limit_bytes=VMEM_LIMIT)


def _tile(n, pref):
    if n <= pref:
        return n
    t = pref - pref % LANES
    while t >= LANES:
        if n % t == 0:
            return t
        t -= LANES
    return n


def _side_cast_specs(arrays, n_steps, step_of, row_blocks):
    ins, outs, shapes = [], [], []
    for w, axis in arrays:
        n = w.shape[axis]
        unit = row_blocks if axis == 0 else LANES
        target = max(unit, n // n_steps)
        slab = next(c for c in range(target - target % unit, 0, -unit) if n % c == 0)
        last = n // slab - 1
        block = (slab, w.shape[1]) if axis == 0 else (w.shape[0], slab)

        def imap(*g, axis=axis, last=last):
            idx = jnp.minimum(step_of(*g), last)
            return (idx, 0) if axis == 0 else (0, idx)

        ins.append(pl.BlockSpec(block, imap))
        outs.append(pl.BlockSpec(block, imap))
        shapes.append(jax.ShapeDtypeStruct(w.shape, BF16))
    return ins, outs, shapes


def _rms(x):
    return x * lax.rsqrt(jnp.mean(x * x, axis=-1, keepdims=True) + RMS_EPS)


def _ada_kernel(c_ref, w_ref, b_ref, o_ref):
    c = c_ref[...]
    cs = (c * jax.nn.sigmoid(c)).astype(BF16)
    o_ref[...] = jnp.dot(cs, w_ref[...].astype(BF16), preferred_element_type=F32) + b_ref[...]


def _ada(c_pad, w, b):
    rows, d = c_pad.shape
    n = w.shape[1]
    tn = _tile(n, 1024)
    return pl.pallas_call(
        _ada_kernel,
        grid=(n // tn,),
        in_specs=[pl.BlockSpec((rows, d), lambda j: (0, 0)),
                  pl.BlockSpec((d, tn), lambda j: (0, j)),
                  pl.BlockSpec((1, tn), lambda j: (0, j))],
        out_specs=pl.BlockSpec((rows, tn), lambda j: (0, j)),
        out_shape=jax.ShapeDtypeStruct((rows, n), F32),
        compiler_params=_params(("parallel",)),
        name="ada_mod",
    )(c_pad, w, b)


def _rope64(y, cos, sin_a, sin_b):
    return y * cos + pltpu.roll(y, LANES - 32, 1) * sin_a + pltpu.roll(y, 32, 1) * sin_b


def _norm_mod_kernel(x_ref, g_ref, sc_ref, sh_ref, w_ref, cos_ref, sa_ref, sb_ref,
                     wq_ref, cosq_ref, sinq_ref, o_ref, klo_ref, khi_ref, wit_ref, qit_ref, wqb_ref):
    @pl.when(_first_token_tile())
    def _():
        wqb_ref[...] = wq_ref[...].astype(BF16)

    r = lax.broadcasted_iota(jnp.int32, w_ref.shape, 0)
    w = jnp.where(r < IDX_DIM + IDX_HEADS, w_ref[...], 0.0).astype(BF16)
    gain, scale, shift = g_ref[...], 1.0 + sc_ref[...], sh_ref[...]
    for c in range(qit_ref.shape[0]):
        tok = slice(c * KEY_TILE, (c + 1) * KEY_TILE)
        h = ((_rms(x_ref[tok, :]) * gain) * scale + shift).astype(o_ref.dtype)
        o_ref[tok, :] = h

        q = lax.dot_general(wqb_ref[...], h, _NT, preferred_element_type=F32)
        for g in range(0, q.shape[0], LANES):
            y = q[g:g + LANES]
            qit_ref[c, g:g + LANES, :] = (y * cosq_ref[:, tok]
                                          + _swap_row_halves(y, IDX_DIM) * sinq_ref[:, tok]).astype(qit_ref.dtype)

        y = lax.dot_general(h, w, _NT, preferred_element_type=F32)
        rot = _rope64(y, cos_ref[tok, :], sa_ref[tok, :], sb_ref[tok, :])
        lane = lax.broadcasted_iota(jnp.int32, y.shape, 1)
        klo = jnp.where(lane < IDX_DIM, rot, 0.0)
        klo_ref[tok, :] = klo.astype(BF16)
        khi_ref[tok, :] = pltpu.roll(klo, IDX_DIM, 1).astype(BF16)
        wit_ref[c] = y.T


def _norm_mod(x, g, sc, sh, wt, l, tables_kw, tables64):
    b, t, d = x.shape
    tt = _tile(t, 512)
    nt = t // tt
    m = b * t
    assert tt % KEY_TILE == 0
    vec = pl.BlockSpec((None, 1, d), lambda bi, ti: (bi, 0, 0))
    tab = pl.BlockSpec((tt, LANES), lambda bi, ti: (ti, 0))
    row = pl.BlockSpec((tt, LANES), lambda bi, ti: (bi * nt + ti, 0))
    tabt = pl.BlockSpec((LANES, tt), lambda bi, ti: (0, ti))
    assert IN_QI_START % IDX_W == 0 and IN_KW_START % LANES == 0
    return pl.pallas_call(
        _norm_mod_kernel,
        grid=(b, nt),
        in_specs=[pl.BlockSpec((None, tt, d), lambda bi, ti: (bi, ti, 0)),
                  pl.BlockSpec((1, d), lambda bi, ti: (0, 0)), vec, vec,
                  pl.BlockSpec((None, LANES, d), lambda bi, ti: (l, IN_KW_START // LANES, 0)), tab, tab, tab,
                  pl.BlockSpec((None, IDX_W, d), lambda bi, ti: (l, IN_QI_START // IDX_W, 0),
                               pipeline_mode=pl.Buffered(1)), tabt, tabt],
        out_specs=[pl.BlockSpec((None, tt, d), lambda bi, ti: (bi, ti, 0)), row, row,
                   pl.BlockSpec((tt // KEY_TILE, LANES, KEY_TILE), lambda bi, ti: (bi * nt + ti, 0, 0)),
                   pl.BlockSpec((tt // KEY_TILE, IDX_W, KEY_TILE), lambda bi, ti: (bi * nt + ti, 0, 0))],
        out_shape=[jax.ShapeDtypeStruct((b, t, d), BF16),
                   jax.ShapeDtypeStruct((m, LANES), BF16), jax.ShapeDtypeStruct((m, LANES), BF16),
                   jax.ShapeDtypeStruct((m // KEY_TILE, LANES, KEY_TILE), F32),
                   jax.ShapeDtypeStruct((m // KEY_TILE, IDX_W, KEY_TILE), BF16)],
        scratch_shapes=[pltpu.VMEM((IDX_W, d), BF16)],
        compiler_params=_params(("arbitrary", "arbitrary")),
        name="norm_mod",
    )(x, g, sc, sh, wt, *tables_kw, wt, *tables64)


def _first_token_tile():
    return pl.program_id(1) == 0


def _row_chunks(ref, step=ROW_CHUNK):
    n = ref.shape[0]
    step = min(n, step)
    return [slice(r, r + step) for r in range(0, n, step)]


def _wp_rope128_kernel(h_ref, w_ref, cos_ref, sin_ref, o_ref, wb_ref):
    @pl.when(_first_token_tile())
    def _():
        wb_ref[...] = w_ref[...].astype(BF16)

    for rows in _row_chunks(h_ref):
        acc = lax.dot_general(h_ref[rows, :], wb_ref[...], _NT, preferred_element_type=F32)
        cos = cos_ref[rows, :]
        sin = sin_ref[rows, :]
        for g in range(acc.shape[1] // LANES):
            y = acc[:, g * LANES:(g + 1) * LANES]
            o_ref[rows, g * LANES:(g + 1) * LANES] = (
                y * cos + pltpu.roll(y, LANES // 2, 1) * sin).astype(o_ref.dtype)


def _swap_row_halves(y, width):
    half = width // 2
    parts = []
    for r in range(0, y.shape[0], width):
        parts += [y[r + half:r + width], y[r:r + half]]
    return jnp.concatenate(parts, axis=0)


def _wp_rope_t_kernel(h_ref, w_ref, cos_ref, sin_ref, o_ref, wb_ref, *, head_dim):
    @pl.when(_first_token_tile())
    def _():
        wb_ref[...] = w_ref[...].astype(BF16)

    for c in range(o_ref.shape[0]):
        tok = slice(c * KEY_TILE, (c + 1) * KEY_TILE)
        r = lax.dot_general(wb_ref[...], h_ref[tok, :], _NT, preferred_element_type=F32)
        for g in range(0, r.shape[0], LANES):
            y = r[g:g + LANES]
            o_ref[c, g:g + LANES, :] = (y * cos_ref[:, tok]
                                        + _swap_row_halves(y, head_dim) * sin_ref[:, tok]).astype(o_ref.dtype)


def _wp_vt_kernel(h_ref, w_ref, o_ref, wb_ref):
    @pl.when(_first_token_tile())
    def _():
        wb_ref[...] = w_ref[...].astype(BF16)

    for c in range(o_ref.shape[0]):
        r = lax.dot_general(wb_ref[...], h_ref[c * KEY_TILE:(c + 1) * KEY_TILE, :], _NT,
                            preferred_element_type=F32)
        o_ref[c] = r.astype(o_ref.dtype)


def _wp_gate_kernel(h_ref, wa_ref, wb_ref, c0_ref, c1_ref, c2_ref, o_ref, d0_ref, d1_ref, d2_ref, wg_ref,
                    *, shift):
    d0_ref[...] = c0_ref[...].astype(BF16)
    d1_ref[...] = c1_ref[...].astype(BF16)
    d2_ref[...] = c2_ref[...].astype(BF16)

    @pl.when(_first_token_tile())
    def _():
        keep = wg_ref.shape[0] - shift
        wg_ref[:keep, :] = wa_ref[shift:, :].astype(BF16)
        wg_ref[keep:, :] = wb_ref[:shift, :].astype(BF16)

    for rows in _row_chunks(h_ref):
        acc = lax.dot_general(h_ref[rows, :], wg_ref[...], _NT, preferred_element_type=F32)
        o_ref[rows, :] = jax.nn.sigmoid(acc).astype(o_ref.dtype)


def _in_proj(h, wt, l, t, tables128, out_weights):
    m, k = h.shape
    d = k
    tm = _tile(t, 1024)
    nt = t // tm
    ni = m // tm
    assert tm % KEY_TILE == 0
    tn = 1024
    o_dsa = 3 * MOBA_W
    sem = ("arbitrary", "arbitrary")
    h_spec = pl.BlockSpec((tm, k), lambda j, i: (i, 0))
    tab = pl.BlockSpec((tm, LANES), lambda j, i: (i % nt, 0))
    tabt = pl.BlockSpec((LANES, tm), lambda j, i: (0, i % nt))
    row = pl.BlockSpec((tm, tn), lambda j, i: (i, j))
    out_t = pl.BlockSpec((tm // KEY_TILE, tn, KEY_TILE), lambda j, i: (i, j, 0))

    def wrows(first_blocks, n_first, second_start):
        return pl.BlockSpec((None, tn, k), lambda j, i: (l, jnp.where(j < n_first, first_blocks + j,
                                                                      second_start + j - n_first), 0))

    assert MOBA_W == DSA_W and MOBA_W % tn == 0
    n_m, n_d = MOBA_W // tn, DSA_W // tn

    qt = pl.pallas_call(
        functools.partial(_wp_rope_t_kernel, head_dim=HEAD_DIM),
        grid=(n_m + n_d, ni),
        in_specs=[h_spec, wrows(0, n_m, o_dsa // tn), tabt, tabt],
        out_specs=out_t,
        out_shape=jax.ShapeDtypeStruct((m // KEY_TILE, MOBA_W + DSA_W, KEY_TILE), BF16),
        scratch_shapes=[pltpu.VMEM((tn, k), BF16)],
        compiler_params=_params(sem),
        name="proj_q",
    )(h, wt, *tables128[0])

    kk = pl.pallas_call(
        _wp_rope128_kernel,
        grid=(n_m + n_d, ni),
        in_specs=[h_spec, wrows(MOBA_W // tn, n_m, (o_dsa + DSA_W) // tn), tab, tab],
        out_specs=row,
        out_shape=jax.ShapeDtypeStruct((m, MOBA_W + DSA_W), BF16),
        scratch_shapes=[pltpu.VMEM((tn, k), BF16)],
        compiler_params=_params(sem),
        name="proj_k",
    )(h, wt, *tables128[1])

    vt = pl.pallas_call(
        _wp_vt_kernel,
        grid=(n_m + n_d, ni),
        in_specs=[h_spec, wrows(2 * MOBA_W // tn, n_m, (o_dsa + 2 * DSA_W) // tn)],
        out_specs=out_t,
        out_shape=jax.ShapeDtypeStruct((m // KEY_TILE, MOBA_W + DSA_W, KEY_TILE), BF16),
        scratch_shapes=[pltpu.VMEM((tn, k), BF16)],
        compiler_params=_params(sem),
        name="proj_v",
    )(h, wt)

    tg = _tile(2 * d, tn)
    g0 = IN_KW_START // tg
    shift = IDX_DIM + IDX_HEADS
    assert IN_KW_START % tg == 0 and shift <= LANES and tg % LANES == 0
    ng = 2 * d // tg
    c_in, c_out, c_shape = _side_cast_specs([(w, 0) for w in out_weights], ng * ni,
                                            lambda j, i: j * ni + i, 16)
    sg, *out_weights_bf16 = pl.pallas_call(
        functools.partial(_wp_gate_kernel, shift=shift),
        grid=(ng, ni),
        in_specs=[h_spec,
                  pl.BlockSpec((None, tg, k), lambda j, i: (l, g0 + j, 0)),
                  pl.BlockSpec((None, LANES, k), lambda j, i: (l, (g0 + j + 1) * (tg // LANES), 0))] + c_in,
        out_specs=[pl.BlockSpec((tm, tg), lambda j, i: (i, j))] + c_out,
        out_shape=[jax.ShapeDtypeStruct((m, 2 * d), BF16)] + c_shape,
        scratch_shapes=[pltpu.VMEM((tg, k), BF16)],
        compiler_params=_params(sem),
        name="proj_gate",
    )(h, wt, wt, *out_weights)
    return qt, kk, vt, sg, out_weights_bf16


def _moba_kmean_kernel(k_ref, o_ref, km_ref, *, nb):
    km_ref[...] = jnp.zeros(km_ref.shape, F32)
    for j in range(nb):
        blk = k_ref[j * MOBA_BLOCK:(j + 1) * MOBA_BLOCK, :].astype(F32)
        km_ref[j:j + 1, :] = jnp.sum(blk, axis=0, keepdims=True) * (1.0 / MOBA_BLOCK)
    km = km_ref[...]
    tiled = jnp.concatenate([km] * MOBA_HEADS, axis=0)
    r = lax.shift_right_logical(lax.broadcasted_iota(jnp.int32, tiled.shape, 0), MAX_BLOCKS.bit_length() - 1)
    c = lax.shift_right_logical(lax.broadcasted_iota(jnp.int32, tiled.shape, 1), HEAD_DIM.bit_length() - 1)
    o_ref[...] = jnp.where(r == c, tiled, 0.0).astype(o_ref.dtype)


def _moba_kmean(kk3, nb):
    b, t, _ = kk3.shape
    return pl.pallas_call(
        functools.partial(_moba_kmean_kernel, nb=nb),
        grid=(b,),
        in_specs=[pl.BlockSpec((None, t, MOBA_W), lambda bi: (bi, 0, 0))],
        out_specs=pl.BlockSpec((None, LANES, MOBA_W), lambda bi: (bi, 0, 0)),
        out_shape=jax.ShapeDtypeStruct((b, LANES, MOBA_W), BF16),
        scratch_shapes=[pltpu.VMEM((MAX_BLOCKS, MOBA_W), F32)],
        compiler_params=_params(("parallel",)),
        name="moba_kmean",
    )(kk3)


def _moba_block_bias(qt_ref, kmt_ref, sbt_ref, i, n_sel):
    g = jnp.dot(kmt_ref[...], qt_ref[...], preferred_element_type=F32)
    shape = (MAX_BLOCKS, g.shape[1])
    j = lax.broadcasted_iota(jnp.int32, shape, 0)
    past = j < i
    for h in range(MOBA_HEADS):
        gm = jnp.where(past, g[h * MAX_BLOCKS:(h + 1) * MAX_BLOCKS, :], -jnp.inf)
        rank = jnp.zeros(shape, F32)
        for jo in range(MAX_BLOCKS):
            other = gm[jo:jo + 1, :]
            beats = (other > gm) | ((other == gm) & (jo < j))
            rank = rank + jnp.where(beats, 1.0, 0.0)
        bias = jnp.where(past & (rank < n_sel), 0.0, MASK_BIAS)
        for jb in range(MAX_BLOCKS):
            sbt_ref[h, jb] = jnp.broadcast_to(bias[jb:jb + 1, :], (8, shape[1]))


def _flash_step(st, m, l, acc_t, v_t):
    m_new = jnp.maximum(m, jnp.max(st, axis=0, keepdims=True))
    alpha = jnp.exp2(m - m_new)
    p = jnp.exp2(st - m_new)
    l = alpha * l + jnp.sum(p, axis=0, keepdims=True)
    acc_t = alpha * acc_t + jnp.dot(v_t, p.astype(BF16), preferred_element_type=F32)
    return m_new, l, acc_t


def _flash_step_joint(sts, m, l, acc_t, v_ts):
    m_new = m
    for st in sts:
        m_new = jnp.maximum(m_new, jnp.max(st, axis=0, keepdims=True))
    alpha = jnp.exp2(m - m_new)
    l = alpha * l
    acc_t = alpha * acc_t
    for st, v_t in zip(sts, v_ts):
        p = jnp.exp2(st - m_new)
        l = l + jnp.sum(p, axis=0, keepdims=True)
        acc_t = acc_t + jnp.dot(v_t, p.astype(BF16), preferred_element_type=F32)
    return m_new, l, acc_t


def _flash_finish(o_ref, l_ref, acc_ref, heads):
    for h in range(heads):
        hs = slice(h * HEAD_DIM, (h + 1) * HEAD_DIM)
        o_ref[:, hs] = (acc_ref[h] / l_ref[h:h + 1, :]).T.astype(o_ref.dtype)


def _moba_attn_kernel(qt_ref, k_ref, vt_ref, kmt_ref, o_ref, m_ref, l_ref, acc_ref, sbt_ref, *, n_sel):
    i = pl.program_id(1)
    own = pl.multiple_of(i * MOBA_BLOCK, MOBA_BLOCK)
    sub_shape = (SUB_TILE, MOBA_BLOCK)
    kidx = lax.broadcasted_iota(jnp.int32, sub_shape, 0)
    qidx = lax.broadcasted_iota(jnp.int32, sub_shape, 1)
    _moba_block_bias(qt_ref, kmt_ref, sbt_ref, i, n_sel)
    for h in range(MOBA_HEADS):
        hs = slice(h * HEAD_DIM, (h + 1) * HEAD_DIM)
        m = jnp.full((1, MOBA_BLOCK), NEG_BIG, F32)
        l = jnp.zeros((1, MOBA_BLOCK), F32)
        acc = jnp.zeros((HEAD_DIM, MOBA_BLOCK), F32)
        for u in range(MOBA_BLOCK // SUB_TILE):
            st = jnp.dot(k_ref[pl.ds(own + u * SUB_TILE, SUB_TILE), hs], qt_ref[hs, :],
                         preferred_element_type=F32)
            st = jnp.where(kidx + u * SUB_TILE <= qidx, st, -jnp.inf)
            m, l, acc = _flash_step(st, m, l, acc, vt_ref[i, hs, u * SUB_TILE:(u + 1) * SUB_TILE])
        m_ref[h:h + 1, :] = m
        l_ref[h:h + 1, :] = l
        acc_ref[h] = acc

    def blocks(js):
        for h in range(MOBA_HEADS):
            hs = slice(h * HEAD_DIM, (h + 1) * HEAD_DIM)
            m, l, acc = m_ref[h:h + 1, :], l_ref[h:h + 1, :], acc_ref[h]
            for j in js:
                off = pl.multiple_of(j * MOBA_BLOCK, MOBA_BLOCK)
                bias = sbt_ref[h, j][0:1, :]
                sts, vts = [], []
                for u in range(MOBA_BLOCK // SUB_TILE):
                    sts.append(jnp.dot(k_ref[pl.ds(off + u * SUB_TILE, SUB_TILE), hs], qt_ref[hs, :],
                                       preferred_element_type=F32) + bias)
                    vts.append(vt_ref[j, hs, u * SUB_TILE:(u + 1) * SUB_TILE])
                m, l, acc = _flash_step_joint(sts, m, l, acc, vts)
            m_ref[h:h + 1, :] = m
            l_ref[h:h + 1, :] = l
            acc_ref[h] = acc

    def body(t, carry):
        blocks((2 * t, 2 * t + 1))
        return carry

    lax.fori_loop(0, i // 2, body, 0)

    @pl.when(i % 2 == 1)
    def _():
        blocks((i - 1,))

    _flash_finish(o_ref, l_ref, acc_ref, MOBA_HEADS)


def _moba_attn(qt3, kk3, vt3, kmt, nb, n_sel):
    b, t, _ = kk3.shape
    return pl.pallas_call(
        functools.partial(_moba_attn_kernel, n_sel=n_sel),
        grid=(b, nb),
        in_specs=[pl.BlockSpec((None, MOBA_W, MOBA_BLOCK), lambda bi, i: (bi * nb + i, 0, 0)),
                  pl.BlockSpec((None, t, MOBA_W), lambda bi, i: (bi, 0, 0)),
                  pl.BlockSpec((nb, MOBA_W, KEY_TILE), lambda bi, i: (bi, 0, 0)),
                  pl.BlockSpec((None, LANES, MOBA_W), lambda bi, i: (bi, 0, 0))],
        out_specs=pl.BlockSpec((None, MOBA_BLOCK, MOBA_W), lambda bi, i: (bi, i, 0)),
        out_shape=jax.ShapeDtypeStruct((b, t, MOBA_W), BF16),
        scratch_shapes=[pltpu.VMEM((MOBA_HEADS, MOBA_BLOCK), F32),
                        pltpu.VMEM((MOBA_HEADS, MOBA_BLOCK), F32),
                        pltpu.VMEM((MOBA_HEADS, HEAD_DIM, MOBA_BLOCK), F32),
                        pltpu.VMEM((MOBA_HEADS, MAX_BLOCKS, 8, MOBA_BLOCK), F32)],
        compiler_params=_params(("parallel", "arbitrary")),
        name="moba_attn",
    )(qt3, kk3, vt3, kmt)


def _float_to_key(x):
    b = lax.bitcast_convert_type(x, jnp.int32)
    return jnp.where(b >= 0, b, b ^ 0x7FFFFFFF)


def _key_to_float(t):
    return lax.bitcast_convert_type(jnp.where(t >= 0, t, t ^ 0x7FFFFFFF), F32)


def _dsa_kernel(qit_ref, klo_ref, khi_ref, wt_ref, qt_ref, k_ref, vt_ref, o_ref,
                sc_ref, jl_ref, m_ref, l_ref, acc_ref, *, topk, idx_scale, tq, seq_bits):
    i = pl.program_id(1)
    nk = i + 1
    shape = (KEY_TILE, tq)
    krow = lax.broadcasted_iota(jnp.int32, shape, 0)
    qpos = i * tq + lax.broadcasted_iota(jnp.int32, shape, 1)

    wt = wt_ref[...]

    def score_tiles(kts):
        offs = [pl.multiple_of(kt * KEY_TILE, KEY_TILE) for kt in kts]
        keys = jnp.concatenate(
            [r[pl.ds(off, KEY_TILE), :] for off in offs for r in (klo_ref, khi_ref)], axis=0)
        accs = [jnp.zeros(shape, F32) for _ in kts]
        for p in range(IDX_HEADS // 2):
            s = jnp.dot(keys, qit_ref[p * LANES:(p + 1) * LANES, :], preferred_element_type=F32)
            for n in range(len(kts)):
                lo = s[2 * n * KEY_TILE:(2 * n + 1) * KEY_TILE]
                hi = s[(2 * n + 1) * KEY_TILE:(2 * n + 2) * KEY_TILE]
                accs[n] = accs[n] + (jnp.maximum(lo, 0.0) * wt[2 * p:2 * p + 1, :]
                                     + jnp.maximum(hi, 0.0) * wt[2 * p + 1:2 * p + 2, :])
        for kt, acc in zip(kts, accs):
            kpos = kt * KEY_TILE + krow
            sc_ref[kt] = jnp.where(kpos <= qpos, acc * idx_scale, -jnp.inf)

    def score_body(t, carry):
        score_tiles((2 * t, 2 * t + 1))
        return carry

    lax.fori_loop(0, nk // 2, score_body, 0)

    @pl.when(nk % 2 == 1)
    def _():
        score_tiles((nk - 1,))

    def count(pred):
        def body(kt, acc):
            hit = jnp.where(pred(sc_ref[kt], kt * KEY_TILE + krow), 1.0, 0.0)
            return acc + jnp.sum(hit.reshape(KEY_TILE // 32, 32, tq), axis=0)
        part = lax.fori_loop(0, nk, body, jnp.zeros((32, tq), F32))
        return jnp.sum(part, axis=0, keepdims=True)

    need_select = nk * tq > topk

    @pl.when(jnp.logical_not(need_select))
    def _():
        def body(kt, carry):
            sc_ref[kt] = jnp.where(kt * KEY_TILE + krow <= qpos, 0.0, NEG_BIG)
            return carry
        lax.fori_loop(0, nk, body, 0)

    @pl.when(need_select)
    def _():
        kf = jnp.float32(topk)

        def ext_body(kt, carry):
            s = sc_ref[kt]
            lo = jnp.min(jnp.where(s > -jnp.inf, s, jnp.inf).reshape(KEY_TILE // 32, 32, tq), axis=0)
            hi = jnp.max(s.reshape(KEY_TILE // 32, 32, tq), axis=0)
            return jnp.minimum(carry[0], lo), jnp.maximum(carry[1], hi)

        lo, hi = lax.fori_loop(0, nk, ext_body, (jnp.full((32, tq), jnp.inf, F32),
                                                 jnp.full((32, tq), -jnp.inf, F32)))
        lo = jnp.min(lo, axis=0, keepdims=True)
        hi = _key_to_float(_float_to_key(jnp.max(hi, axis=0, keepdims=True)) + 1)
        n_fin = count(lambda s, kp: s >= lo)
        done = jnp.where(n_fin <= kf, 1.0, 0.0)

        def step(state):
            lo, hi, clo, done = state
            mid = lo * 0.5 + hi * 0.5
            stuck = (mid <= lo) | (mid >= hi)
            c = count(lambda s, kp: s >= mid)
            up = (c >= kf) & (done < 0.5)
            down = (c < kf) & (done < 0.5)
            lo = jnp.where(up, mid, lo)
            clo = jnp.where(up, c, clo)
            hi = jnp.where(down, mid, hi)
            done = jnp.where(stuck | (clo == kf), 1.0, done)
            return lo, hi, clo, done

        state = lax.fori_loop(0, BISECT_FIXED, lambda _, st: step(st), (lo, hi, n_fin, done))
        state = lax.while_loop(lambda st: jnp.min(st[3]) < 0.5, step, state)
        thr = jnp.where(n_fin < kf, -jnp.inf, state[0])

        def tie_body(kt, acc):
            s = sc_ref[kt]
            gt = jnp.where(s > thr, 1.0, 0.0).reshape(KEY_TILE // 32, 32, tq)
            eq = jnp.where(s == thr, 1.0, 0.0).reshape(KEY_TILE // 32, 32, tq)
            return acc[0] + jnp.sum(gt, axis=0), acc[1] + jnp.sum(eq, axis=0)

        n_gt, n_eq = lax.fori_loop(0, nk, tie_body, (jnp.zeros((32, tq), F32),) * 2)
        need = kf - jnp.sum(n_gt, axis=0, keepdims=True)
        n_eq = jnp.sum(n_eq, axis=0, keepdims=True)
        jl_ref[...] = jnp.full(jl_ref.shape, 2 ** seq_bits, jnp.int32)

        @pl.when(jnp.max(n_eq - need) > 0.0)
        def _():
            def idx_body(b, c):
                cand = c | lax.shift_left(jnp.int32(1), seq_bits - 1 - b)
                g = count(lambda s, kp: (s == thr) & (kp < cand))
                return jnp.where(g < need, cand, c)
            c = lax.fori_loop(0, seq_bits, idx_body, jnp.zeros((1, tq), jnp.int32))
            jl_ref[...] = jnp.broadcast_to(c, jl_ref.shape)

        bound = jnp.minimum(jl_ref[0:1, :], qpos[0:1, :])

        def bias_body(kt, carry):
            s = sc_ref[kt]
            sel = (s > thr) | ((s == thr) & (kt * KEY_TILE + krow <= bound))
            sc_ref[kt] = jnp.where(sel, 0.0, NEG_BIG)
            return carry
        lax.fori_loop(0, nk, bias_body, 0)

    m_ref[...] = jnp.full(m_ref.shape, NEG_BIG, F32)
    l_ref[...] = jnp.zeros(l_ref.shape, F32)
    acc_ref[...] = jnp.zeros(acc_ref.shape, F32)

    def tiles(kts):
        for h in range(DSA_HEADS):
            hs = slice(h * HEAD_DIM, (h + 1) * HEAD_DIM)
            m, l, acc = m_ref[h:h + 1, :], l_ref[h:h + 1, :], acc_ref[h]
            for kt in kts:
                off = pl.multiple_of(kt * KEY_TILE, KEY_TILE)
                for u in range(KEY_TILE // SUB_TILE):
                    us = slice(u * SUB_TILE, (u + 1) * SUB_TILE)
                    st = jnp.dot(k_ref[pl.ds(off + u * SUB_TILE, SUB_TILE), hs], qt_ref[hs, :],
                                 preferred_element_type=F32) + sc_ref[kt, us, :]
                    m, l, acc = _flash_step(st, m, l, acc, vt_ref[kt, hs, us])
            m_ref[h:h + 1, :] = m
            l_ref[h:h + 1, :] = l
            acc_ref[h] = acc

    def attn_body(t, carry):
        tiles((2 * t, 2 * t + 1))
        return carry

    lax.fori_loop(0, nk // 2, attn_body, 0)

    @pl.when(nk % 2 == 1)
    def _():
        tiles((nk - 1,))

    _flash_finish(o_ref, l_ref, acc_ref, DSA_HEADS)


def _dsa_attn(qit3, klo3, khi3, wit, qt3, kk3, vt3, topk):
    b, t, _ = kk3.shape
    tq = KEY_TILE
    nq = t // tq
    seq_bits = max(1, (t - 1).bit_length())
    kern = functools.partial(
        _dsa_kernel, topk=topk,
        idx_scale=(IDX_DIM ** -0.5) * (IDX_HEADS ** -0.5), tq=tq, seq_bits=seq_bits)
    return pl.pallas_call(
        kern,
        grid=(b, nq),
        in_specs=[pl.BlockSpec((None, IDX_W, tq), lambda bi, i: (bi * nq + i, 0, 0)),
                  pl.BlockSpec((None, t, LANES), lambda bi, i: (bi, 0, 0)),
                  pl.BlockSpec((None, t, LANES), lambda bi, i: (bi, 0, 0)),
                  pl.BlockSpec((None, IDX_HEADS, tq), lambda bi, i: (bi * nq + i, IDX_DIM // IDX_HEADS, 0)),
                  pl.BlockSpec((None, DSA_W, tq), lambda bi, i: (bi * nq + i, 1, 0)),
                  pl.BlockSpec((None, t, DSA_W), lambda bi, i: (bi, 0, 1)),
                  pl.BlockSpec((nq, DSA_W, KEY_TILE), lambda bi, i: (bi, 1, 0))],
        out_specs=pl.BlockSpec((None, tq, DSA_W), lambda bi, i: (bi, i, 0)),
        out_shape=jax.ShapeDtypeStruct((b, t, DSA_W), BF16),
        scratch_shapes=[pltpu.VMEM((nq, KEY_TILE, tq), F32),
                        pltpu.VMEM((8, tq), jnp.int32),
                        pltpu.VMEM((DSA_HEADS, tq), F32),
                        pltpu.VMEM((DSA_HEADS, tq), F32),
                        pltpu.VMEM((DSA_HEADS, HEAD_DIM, tq), F32)],
        compiler_params=_params(("parallel", "arbitrary")),
        name="dsa_attn",
    )(qit3, klo3, khi3, wit, qt3, kk3, vt3)


def _mix_out_kernel(oa_ref, ob_ref, sg_ref, wa_ref, wb_ref, wo_ref, x_ref, gt_ref, gpost_ref,
                    gpre_ref, sc_ref, sh_ref, w1f_ref, w2f_ref, x1_ref, h2_ref, w1b_ref, w2b_ref, *, d):
    w1b_ref[...] = w1f_ref[...].astype(BF16)
    w2b_ref[...] = w2f_ref[...].astype(BF16)
    ya = jnp.dot(oa_ref[...], wa_ref[...], preferred_element_type=F32)
    yb = jnp.dot(ob_ref[...], wb_ref[...], preferred_element_type=F32)
    z = sg_ref[:, :d].astype(F32) * ya + sg_ref[:, d:].astype(F32) * yb
    y = jnp.dot(z.astype(BF16), wo_ref[...], preferred_element_type=F32)
    x1 = x_ref[...] + gt_ref[...] * (_rms(y) * gpost_ref[...])
    x1_ref[...] = x1
    h2_ref[...] = ((_rms(x1) * gpre_ref[...]) * (1.0 + sc_ref[...]) + sh_ref[...]).astype(h2_ref.dtype)


def _mix_out(oa3, ob3, sg3, wa, wb, wo, x, gt1, g_post, g_pre2, sc2, sh2, w1f, w2f):
    b, t, d = x.shape
    tm = _tile(t, 256)
    nt = t // tm
    c_in, c_out, c_shape = _side_cast_specs([(w1f, 1), (w2f, 0)], b * nt, lambda bi, i: bi * nt + i, 16)
    one = pl.Buffered(1)
    row = lambda w: pl.BlockSpec((None, tm, w), lambda bi, i: (bi, i, 0))
    full = lambda a: pl.BlockSpec(a.shape, lambda bi, i: (0, 0), pipeline_mode=one)
    vec = pl.BlockSpec((None, 1, d), lambda bi, i: (bi, 0, 0))
    gain = pl.BlockSpec((1, d), lambda bi, i: (0, 0))
    return pl.pallas_call(
        functools.partial(_mix_out_kernel, d=d),
        grid=(b, t // tm),
        in_specs=[row(MOBA_W), row(DSA_W), row(2 * d), full(wa), full(wb), full(wo), row(d),
                  vec, gain, gain, vec, vec] + c_in,
        out_specs=[row(d), row(d)] + c_out,
        out_shape=[jax.ShapeDtypeStruct((b, t, d), F32), jax.ShapeDtypeStruct((b, t, d), BF16)] + c_shape,
        compiler_params=_params(("arbitrary", "arbitrary")),
        name="mix_out",
    )(oa3, ob3, sg3, wa, wb, wo, x, gt1, g_post, g_pre2, sc2, sh2, w1f, w2f)


def _ffn_kernel(h_ref, w1_ref, w2_ref, x1_ref, gt_ref, g_ref, o_ref, acc_ref):
    j = pl.program_id(2)

    @pl.when(j == 0)
    def _():
        acc_ref[...] = jnp.zeros(acc_ref.shape, F32)

    for rows in _row_chunks(h_ref, KEY_TILE):
        u = jnp.dot(h_ref[rows, :], w1_ref[...], preferred_element_type=F32)
        u = jnp.square(jnp.maximum(u, 0.0)).astype(BF16)
        acc_ref[rows, :] += jnp.dot(u, w2_ref[...], preferred_element_type=F32)

    @pl.when(j == pl.num_programs(2) - 1)
    def _():
        o_ref[...] = x1_ref[...] + gt_ref[...] * (_rms(acc_ref[...]) * g_ref[...])


def _ffn(h2, w1, w2, x1, gt2, g_post):
    b, t, d = x1.shape
    ff = w1.shape[1]
    tm = _tile(t, 512)
    tf = _tile(ff, 1024)
    row = lambda: pl.BlockSpec((None, tm, d), lambda bi, i, j: (bi, i, 0))
    return pl.pallas_call(
        _ffn_kernel,
        grid=(b, t // tm, ff // tf),
        in_specs=[row(),
                  pl.BlockSpec((d, tf), lambda bi, i, j: (0, j)),
                  pl.BlockSpec((tf, d), lambda bi, i, j: (j, 0)),
                  row(),
                  pl.BlockSpec((None, 1, d), lambda bi, i, j: (bi, 0, 0)),
                  pl.BlockSpec((1, d), lambda bi, i, j: (0, 0))],
        out_specs=row(),
        out_shape=jax.ShapeDtypeStruct((b, t, d), F32),
        scratch_shapes=[pltpu.VMEM((tm, d), F32)],
        compiler_params=_params(("parallel", "parallel", "arbitrary")),
        name="ffn",
    )(h2, w1, w2, x1, gt2, g_post)


def _rope_tables(t):
    pos = jnp.arange(t, dtype=F32)[:, None]
    lane = jnp.arange(LANES)[None, :]

    def cos_sin(half):
        inv_freq = jnp.power(ROPE_THETA, -jnp.arange(half, dtype=F32) / half)
        ang = pos * inv_freq[None, :]
        reps = LANES // half
        sign = jnp.where(lane % (2 * half) < half, -1.0, 1.0)
        return jnp.tile(jnp.cos(ang), (1, reps)), jnp.tile(jnp.sin(ang), (1, reps)) * sign

    cos_k, sin_k = cos_sin(HEAD_DIM // 2)
    rope128 = (((cos_k * QK_SCALE_LOG2).T, (sin_k * QK_SCALE_LOG2).T), (cos_k, sin_k))
    cos64, sin64 = cos_sin(IDX_DIM // 2)
    is_key = lane < IDX_DIM
    low = lane % IDX_DIM < IDX_DIM // 2
    kw = (jnp.where(is_key, cos64, 1.0), jnp.where(is_key & low, sin64, 0.0),
          jnp.where(is_key & ~low, sin64, 0.0))
    return rope128, (cos64.T, sin64.T), kw


def kernel(x, c, w_ada, b_ada, g_pre_mix, g_post_mix, w_in, w_moba_out, w_dsa_out, w_o,
           g_pre_ffn, g_post_ffn, w_ff1, w_ff2):
    b, t, d = x.shape
    m = b * t
    nb = t // MOBA_BLOCK
    assert t % MOBA_BLOCK == 0 and nb <= MAX_BLOCKS and d % LANES == 0 and b <= 16
    n_sel = max(1, min(MOBA_TOPK, nb - 1))
    topk = min(DSA_TOPK_MAX, t // 4)
    rope128, rope64, rope_kw = _rope_tables(t)
    c_pad = jnp.zeros((16, d), F32).at[:b].set(c)

    for l in range(w_ada.shape[0]):
        mod = _ada(c_pad, w_ada[l], b_ada[l][None, :])[:b]
        sh1, sc1, gt1, sh2, sc2, gt2 = [v[:, None, :] for v in jnp.split(mod, 6, axis=-1)]

        wt = jnp.swapaxes(w_in, 1, 2)
        h, klo, khi, wit, qit3 = _norm_mod(x, g_pre_mix[l][None, :], sc1, sh1, wt, l, rope_kw, rope64)
        qt3, kk, vt3, sg, (wa, wb, wo) = _in_proj(
            h.reshape(m, d), wt, l, t, rope128, (w_moba_out[l], w_dsa_out[l], w_o[l]))

        kk3 = kk.reshape(b, t, -1)
        kmt = _moba_kmean(kk3, nb)
        oa = _moba_attn(qt3, kk3, vt3, kmt, nb, n_sel)
        ob = _dsa_attn(qit3, klo.reshape(b, t, -1), khi.reshape(b, t, -1), wit, qt3, kk3, vt3, topk)

        x, h2, w1, w2 = _mix_out(oa, ob, sg.reshape(b, t, -1), wa, wb, wo, x, gt1,
                                 g_post_mix[l][None, :], g_pre_ffn[l][None, :], sc2, sh2,
                                 w_ff1[l], w_ff2[l])
        x = _ffn(h2, w1, w2, x, gt2, g_post_ffn[l][None, :])
    return x
```
